```python
import math
import jax, jax.numpy as jnp
from jax import lax
import numpy as np

D_MODEL = 1024
BATCH = 8
SEQ = 8192
DEPTH = 2

HEAD_DIM = 64
Q_HEADS = 8
KV_HEADS = 2
GROUP = Q_HEADS // KV_HEADS
ATTN_WIDTH = Q_HEADS * HEAD_DIM
KV_WIDTH = KV_HEADS * HEAD_DIM
WINDOW = 128
BLOCK = 128
POOL_WIDTH = D_MODEL - ATTN_WIDTH
POOL_WINDOWS = (2, 4, 8, 16)
POOL_GROUPS = len(POOL_WINDOWS)
POOL_GC = POOL_WIDTH // POOL_GROUPS
EVEN_IN = ATTN_WIDTH + 2 * KV_WIDTH + ATTN_WIDTH + POOL_WIDTH + POOL_WIDTH
EVEN_MIX = ATTN_WIDTH + POOL_WIDTH
CONV_WIDTH = D_MODEL
CONV_K = 31
ODD_IN = 3 * CONV_WIDTH
EPS = 1e-6
NEG = -1e30
N_EVEN = (DEPTH + 1) // 2
N_ODD = DEPTH // 2

kernel_name = "hybrid_swa_pool_conformer_sandwich"


def rms_norm(x, g):
    xf = x.astype(jnp.float32)
    y = xf * lax.rsqrt(jnp.mean(xf * xf, axis=-1, keepdims=True) + EPS)
    return (y * g.astype(jnp.float32)).astype(x.dtype)


def alibi_slopes(n):
    return jnp.exp2(-8.0 * jnp.arange(1, n + 1, dtype=jnp.float32) / n)


def sliding_window_attention(q, k, v, sinks):
    B, S, _ = q.shape
    nb = S // BLOCK
    q = q.reshape(B, nb, BLOCK, KV_HEADS, GROUP, HEAD_DIM)
    k = k.reshape(B, nb, BLOCK, KV_HEADS, HEAD_DIM)
    v = v.reshape(B, nb, BLOCK, KV_HEADS, HEAD_DIM)
    kpad = jnp.zeros_like(k[:, :1])
    vpad = jnp.zeros_like(v[:, :1])
    kk = jnp.concatenate([jnp.concatenate([kpad, k[:, :-1]], axis=1), k], axis=2)
    vv = jnp.concatenate([jnp.concatenate([vpad, v[:, :-1]], axis=1), v], axis=2)
    scores = jnp.einsum('bnqkgd,bnskd->bnkgqs', q, kk).astype(jnp.float32) * (HEAD_DIM ** -0.5)
    qi = jnp.arange(BLOCK)[:, None] + BLOCK
    sj = jnp.arange(2 * BLOCK)[None, :]
    dist = qi - sj
    key_pos = jnp.arange(nb)[:, None, None] * BLOCK + sj[None] - BLOCK
    valid = (dist >= 0)[None] & (dist < WINDOW)[None] & (key_pos >= 0)
    slopes = alibi_slopes(Q_HEADS).reshape(KV_HEADS, GROUP)
    bias = -slopes[:, :, None, None] * dist.astype(jnp.float32)
    scores = jnp.where(valid[None, :, None, None], scores + bias, NEG)
    sink = sinks.astype(jnp.float32).reshape(KV_HEADS, GROUP)[None, None, :, :, None, None]
    mx = jnp.maximum(jnp.max(scores, axis=-1, keepdims=True), sink)
    p = jnp.exp(scores - mx)
    p = p / (jnp.sum(p, axis=-1, keepdims=True) + jnp.exp(sink - mx))
    out = jnp.einsum('bnkgqs,bnskd->bnqkgd', p.astype(vv.dtype), vv)
    return out.reshape(B, S, ATTN_WIDTH)


def multiscale_pool(u, pool_w, pool_scale):
    B, S, _ = u.shape
    uf = u.astype(jnp.float32).reshape(B, S, POOL_GROUPS, POOL_GC)
    cs = jnp.concatenate([jnp.zeros_like(uf[:, :1]), jnp.cumsum(uf, axis=1)], axis=1)
    t = jnp.arange(S)[:, None]
    win = jnp.array(POOL_WINDOWS, dtype=jnp.int32)[None, :]
    lo = jnp.maximum(t + 1 - win, 0)
    cnt = (t + 1 - lo).astype(jnp.float32)
    lower = cs[:, lo, jnp.arange(POOL_GROUPS)[None, :]]
    pooled = (cs[:, 1:] - lower) / cnt[None, :, :, None] - uf
    y = jnp.einsum('bsgc,gcd->bsgd', pooled, pool_w.astype(jnp.float32))
    y = y * pool_scale.astype(jnp.float32).reshape(POOL_GROUPS, POOL_GC)
    return y.reshape(B, S, POOL_WIDTH).astype(u.dtype)


def even_mixer(h, w_in, sinks, pool_w, pool_scale, w_out):
    proj = h @ w_in
    splits = np.cumsum([ATTN_WIDTH, KV_WIDTH, KV_WIDTH, ATTN_WIDTH, POOL_WIDTH]).tolist()
    q, k, v, ga, u, gb = jnp.split(proj, splits, axis=-1)
    ya = sliding_window_attention(q, k, v, sinks).astype(h.dtype) * jax.nn.silu(ga)
    yb = multiscale_pool(u, pool_w, pool_scale) * jax.nn.silu(gb)
    return jnp.concatenate([ya, yb], axis=-1) @ w_out


def odd_mixer(h, w_in, dw_w, dw_b, ln_g, ln_b, w_out):
    proj = h @ w_in
    a, b, gate = jnp.split(proj, [CONV_WIDTH, 2 * CONV_WIDTH], axis=-1)
    glu = a * jax.nn.sigmoid(b)
    conv = lax.conv_general_dilated(
        glu, dw_w.astype(glu.dtype), window_strides=(1,), padding=[(CONV_K - 1, 0)],
        dimension_numbers=('NWC', 'WIO', 'NWC'), feature_group_count=CONV_WIDTH)
    cf = conv.astype(jnp.float32) + dw_b.astype(jnp.float32)
    mu = jnp.mean(cf, axis=-1, keepdims=True)
    var = jnp.mean(jnp.square(cf - mu), axis=-1, keepdims=True)
    cn = (cf - mu) * lax.rsqrt(var + EPS) * ln_g.astype(jnp.float32) + ln_b.astype(jnp.float32)
    y = jax.nn.silu(cn).astype(h.dtype) * jax.nn.silu(gate)
    return y @ w_out


def _fwd_setup_inputs(seed: int = 0) -> dict:
    key = jax.random.key(seed)
    ks = jax.random.split(key, 16)
    f32 = jnp.float32
    nrm = lambda k, shape, s: jax.random.normal(k, shape, f32) * s
    return {
        'x': nrm(ks[0], (BATCH, SEQ, D_MODEL), 1.0),
        'pre_norm': 1.0 + nrm(ks[1], (DEPTH, D_MODEL), 0.05),
        'post_norm': 1.0 + nrm(ks[2], (DEPTH, D_MODEL), 0.05),
        'a_w_in': nrm(ks[3], (N_EVEN, D_MODEL, EVEN_IN), D_MODEL ** -0.5),
        'a_sinks': nrm(ks[4], (N_EVEN, Q_HEADS), 0.5),
        'b_pool_w': nrm(ks[5], (N_EVEN, POOL_GROUPS, POOL_GC, POOL_GC), POOL_GC ** -0.5),
        'b_pool_scale': 1.0 + nrm(ks[6], (N_EVEN, POOL_WIDTH), 0.1),
        'ab_w_out': nrm(ks[7], (N_EVEN, EVEN_MIX, D_MODEL), EVEN_MIX ** -0.5),
        'c_w_in': nrm(ks[8], (N_ODD, D_MODEL, ODD_IN), D_MODEL ** -0.5),
        'c_dw_w': nrm(ks[9], (N_ODD, CONV_K, 1, CONV_WIDTH), CONV_K ** -0.5),
        'c_dw_b': nrm(ks[10], (N_ODD, CONV_WIDTH), 0.02),
        'c_ln_g': 1.0 + nrm(ks[11], (N_ODD, CONV_WIDTH), 0.05),
        'c_ln_b': nrm(ks[12], (N_ODD, CONV_WIDTH), 0.02),
        'c_w_out': nrm(ks[13], (N_ODD, CONV_WIDTH, D_MODEL), CONV_WIDTH ** -0.5),
    }


def _fwd_reference(x, pre_norm, post_norm, a_w_in, a_sinks, b_pool_w, b_pool_scale, ab_w_out,
              c_w_in, c_dw_w, c_dw_b, c_ln_g, c_ln_b, c_w_out):
    for layer in range(DEPTH):
        h = rms_norm(x, pre_norm[layer])
        if layer % 2 == 0:
            i = layer // 2
            y = even_mixer(h, a_w_in[i], a_sinks[i], b_pool_w[i], b_pool_scale[i], ab_w_out[i])
        else:
            i = layer // 2
            y = odd_mixer(h, c_w_in[i], c_dw_w[i], c_dw_b[i], c_ln_g[i], c_ln_b[i], c_w_out[i])
        x = x + rms_norm(y, post_norm[layer])
    return x


import jax as _jax
import jax.numpy as _jnp

TWIN_FORMAT = 'train_step'
FWD_PARAMS = ['x', 'pre_norm', 'post_norm', 'a_w_in', 'a_sinks', 'b_pool_w', 'b_pool_scale', 'ab_w_out', 'c_w_in', 'c_dw_w', 'c_dw_b', 'c_ln_g', 'c_ln_b', 'c_w_out']
TWIN_WEIGHTS = ['pre_norm', 'post_norm', 'a_w_in', 'a_sinks', 'b_pool_w', 'b_pool_scale', 'ab_w_out', 'c_w_in', 'c_dw_w', 'c_dw_b', 'c_ln_g', 'c_ln_b', 'c_w_out']
TWIN_DIFF_INPUT = 'x'
TWIN_INPUTS = ['x', 'pre_norm', 'post_norm', 'a_w_in', 'a_sinks', 'b_pool_w', 'b_pool_scale', 'ab_w_out', 'c_w_in', 'c_dw_w', 'c_dw_b', 'c_ln_g', 'c_ln_b', 'c_w_out', 'loss_target', 'm_pre_norm', 'm_post_norm', 'm_a_w_in', 'm_a_sinks', 'm_b_pool_w', 'm_b_pool_scale', 'm_ab_w_out', 'm_c_w_in', 'm_c_dw_w', 'm_c_dw_b', 'm_c_ln_g', 'm_c_ln_b', 'm_c_w_out', 'v_pre_norm', 'v_post_norm', 'v_a_w_in', 'v_a_sinks', 'v_b_pool_w', 'v_b_pool_scale', 'v_ab_w_out', 'v_c_w_in', 'v_c_dw_w', 'v_c_dw_b', 'v_c_ln_g', 'v_c_ln_b', 'v_c_w_out']
TWIN_OUTPUTS = ['loss', 'grad_x', 'grad_pre_norm', 'grad_post_norm', 'grad_a_w_in', 'grad_a_sinks', 'grad_b_pool_w', 'grad_b_pool_scale', 'grad_ab_w_out', 'grad_c_w_in', 'grad_c_dw_w', 'grad_c_dw_b', 'grad_c_ln_g', 'grad_c_ln_b', 'grad_c_w_out', 'delta_pre_norm', 'delta_post_norm', 'delta_a_w_in', 'delta_a_sinks', 'delta_b_pool_w', 'delta_b_pool_scale', 'delta_ab_w_out', 'delta_c_w_in', 'delta_c_dw_w', 'delta_c_dw_b', 'delta_c_ln_g', 'delta_c_ln_b', 'delta_c_w_out', 'new_m_pre_norm', 'new_m_post_norm', 'new_m_a_w_in', 'new_m_a_sinks', 'new_m_b_pool_w', 'new_m_b_pool_scale', 'new_m_ab_w_out', 'new_m_c_w_in', 'new_m_c_dw_w', 'new_m_c_dw_b', 'new_m_c_ln_g', 'new_m_c_ln_b', 'new_m_c_w_out', 'new_v_pre_norm', 'new_v_post_norm', 'new_v_a_w_in', 'new_v_a_sinks', 'new_v_b_pool_w', 'new_v_b_pool_scale', 'new_v_ab_w_out', 'new_v_c_w_in', 'new_v_c_dw_w', 'new_v_c_dw_b', 'new_v_c_ln_g', 'new_v_c_ln_b', 'new_v_c_w_out']
TWIN_LEAF_KINDS = {'loss': 'loss', 'grad_x': 'grad_x', 'grad_pre_norm': 'grad_w', 'grad_post_norm': 'grad_w', 'grad_a_w_in': 'grad_w', 'grad_a_sinks': 'grad_w', 'grad_b_pool_w': 'grad_w', 'grad_b_pool_scale': 'grad_w', 'grad_ab_w_out': 'grad_w', 'grad_c_w_in': 'grad_w', 'grad_c_dw_w': 'grad_w', 'grad_c_dw_b': 'grad_w', 'grad_c_ln_g': 'grad_w', 'grad_c_ln_b': 'grad_w', 'grad_c_w_out': 'grad_w', 'delta_pre_norm': 'delta_w', 'delta_post_norm': 'delta_w', 'delta_a_w_in': 'delta_w', 'delta_a_sinks': 'delta_w', 'delta_b_pool_w': 'delta_w', 'delta_b_pool_scale': 'delta_w', 'delta_ab_w_out': 'delta_w', 'delta_c_w_in': 'delta_w', 'delta_c_dw_w': 'delta_w', 'delta_c_dw_b': 'delta_w', 'delta_c_ln_g': 'delta_w', 'delta_c_ln_b': 'delta_w', 'delta_c_w_out': 'delta_w', 'new_m_pre_norm': 'new_m', 'new_m_post_norm': 'new_m', 'new_m_a_w_in': 'new_m', 'new_m_a_sinks': 'new_m', 'new_m_b_pool_w': 'new_m', 'new_m_b_pool_scale': 'new_m', 'new_m_ab_w_out': 'new_m', 'new_m_c_w_in': 'new_m', 'new_m_c_dw_w': 'new_m', 'new_m_c_dw_b': 'new_m', 'new_m_c_ln_g': 'new_m', 'new_m_c_ln_b': 'new_m', 'new_m_c_w_out': 'new_m', 'new_v_pre_norm': 'new_v', 'new_v_post_norm': 'new_v', 'new_v_a_w_in': 'new_v', 'new_v_a_sinks': 'new_v', 'new_v_b_pool_w': 'new_v', 'new_v_b_pool_scale': 'new_v', 'new_v_ab_w_out': 'new_v', 'new_v_c_w_in': 'new_v', 'new_v_c_dw_w': 'new_v', 'new_v_c_dw_b': 'new_v', 'new_v_c_ln_g': 'new_v', 'new_v_c_ln_b': 'new_v', 'new_v_c_w_out': 'new_v'}


def _forward(args):
    return _fwd_reference(*[args[k] for k in FWD_PARAMS])


def _output_shape():
    def fwd():
        inp = _fwd_setup_inputs(0)
        return _fwd_reference(*[inp[k] for k in FWD_PARAMS])
    out = _jax.eval_shape(fwd)
    return out.shape, out.dtype

N_MICROBATCH = 1
ADAM_LR = 0.001
ADAM_B1 = 0.9
ADAM_B2 = 0.999
ADAM_EPS = 1e-08
ADAM_WD = 0.01
ADAM_STEP = 10
PER_EXAMPLE_BATCH_AXIS = {'x': 0, 'loss_target': 0}
SHARED_INPUTS = []
_WEIGHT_DTYPES = {'pre_norm': _jnp.float32, 'post_norm': _jnp.float32, 'a_w_in': _jnp.float32, 'a_sinks': _jnp.float32, 'b_pool_w': _jnp.float32, 'b_pool_scale': _jnp.float32, 'ab_w_out': _jnp.float32, 'c_w_in': _jnp.float32, 'c_dw_w': _jnp.float32, 'c_dw_b': _jnp.float32, 'c_ln_g': _jnp.float32, 'c_ln_b': _jnp.float32, 'c_w_out': _jnp.float32}
MOMENT_SCALE = {'pre_norm': 9.750580e-01, 'post_norm': 6.411096e+01, 'a_w_in': 6.790567e-01, 'a_sinks': 3.596695e-01, 'b_pool_w': 9.966613e-01, 'b_pool_scale': 1.296887e+00, 'ab_w_out': 8.788402e-01, 'c_w_in': 5.091705e-01, 'c_dw_w': 4.916737e-01, 'c_dw_b': 3.317294e+00, 'c_ln_g': 1.287479e+00, 'c_ln_b': 1.987374e+00, 'c_w_out': 7.784762e-01}


def _to_microbatches(a, axis):
    t = _jnp.moveaxis(a, axis, 0)
    t = t.reshape((N_MICROBATCH, t.shape[0] // N_MICROBATCH) + t.shape[1:])
    return _jnp.moveaxis(t, 1, axis + 1)


def setup_inputs(seed: int = 0) -> dict:
    inp = _fwd_setup_inputs(seed)
    key = _jax.random.fold_in(_jax.random.key(seed), 7919)
    shape, _ = _output_shape()
    out = dict(inp)
    out["loss_target"] = _jax.random.normal(_jax.random.fold_in(key, 0), shape, _jnp.float32)
    for i, name in enumerate(TWIN_WEIGHTS):
        w = inp[name].astype(_jnp.float32)
        if MOMENT_SCALE is None:
            s = _jnp.sqrt(_jnp.mean(_jnp.square(w)) + 1e-30)
        else:
            s = MOMENT_SCALE[name]
        km, kv = _jax.random.split(_jax.random.fold_in(key, i + 1))
        out[name] = w
        out["m_" + name] = s * _jax.random.normal(km, w.shape, _jnp.float32)
        out["v_" + name] = (s * s) * _jax.random.uniform(kv, w.shape, _jnp.float32, 0.5, 1.5)
    if N_MICROBATCH > 1:
        for name, axis in PER_EXAMPLE_BATCH_AXIS.items():
            out[name] = _to_microbatches(out[name], axis)
    return {'x': out['x'], 'pre_norm': out['pre_norm'], 'post_norm': out['post_norm'], 'a_w_in': out['a_w_in'], 'a_sinks': out['a_sinks'], 'b_pool_w': out['b_pool_w'], 'b_pool_scale': out['b_pool_scale'], 'ab_w_out': out['ab_w_out'], 'c_w_in': out['c_w_in'], 'c_dw_w': out['c_dw_w'], 'c_dw_b': out['c_dw_b'], 'c_ln_g': out['c_ln_g'], 'c_ln_b': out['c_ln_b'], 'c_w_out': out['c_w_out'], 'loss_target': out['loss_target'], 'm_pre_norm': out['m_pre_norm'], 'm_post_norm': out['m_post_norm'], 'm_a_w_in': out['m_a_w_in'], 'm_a_sinks': out['m_a_sinks'], 'm_b_pool_w': out['m_b_pool_w'], 'm_b_pool_scale': out['m_b_pool_scale'], 'm_ab_w_out': out['m_ab_w_out'], 'm_c_w_in': out['m_c_w_in'], 'm_c_dw_w': out['m_c_dw_w'], 'm_c_dw_b': out['m_c_dw_b'], 'm_c_ln_g': out['m_c_ln_g'], 'm_c_ln_b': out['m_c_ln_b'], 'm_c_w_out': out['m_c_w_out'], 'v_pre_norm': out['v_pre_norm'], 'v_post_norm': out['v_post_norm'], 'v_a_w_in': out['v_a_w_in'], 'v_a_sinks': out['v_a_sinks'], 'v_b_pool_w': out['v_b_pool_w'], 'v_b_pool_scale': out['v_b_pool_scale'], 'v_ab_w_out': out['v_ab_w_out'], 'v_c_w_in': out['v_c_w_in'], 'v_c_dw_w': out['v_c_dw_w'], 'v_c_dw_b': out['v_c_dw_b'], 'v_c_ln_g': out['v_c_ln_g'], 'v_c_ln_b': out['v_c_ln_b'], 'v_c_w_out': out['v_c_w_out']}


def _loss(weights, diff, rest, loss_target):
    with _jax.named_scope("forward"):
        args = {**rest, TWIN_DIFF_INPUT: diff, **{k: w.astype(_WEIGHT_DTYPES[k]) for k, w in weights.items()}}
        y = _forward(args)
    with _jax.named_scope("loss_head"):
        err = _jnp.square(y.astype(_jnp.float32) - loss_target)
        return 0.5 * _jnp.sum(_jnp.mean(err, axis=-1)) if err.ndim else 0.5 * err


def _adamw(w, g, m, v):
    m = ADAM_B1 * m + (1.0 - ADAM_B1) * g
    v = ADAM_B2 * v + (1.0 - ADAM_B2) * _jnp.square(g)
    m_hat = m / (1.0 - ADAM_B1 ** ADAM_STEP)
    v_hat = v / (1.0 - ADAM_B2 ** ADAM_STEP)
    delta = -ADAM_LR * (m_hat / (_jnp.sqrt(v_hat) + ADAM_EPS) + ADAM_WD * w)
    return delta, m, v


def reference(x, pre_norm, post_norm, a_w_in, a_sinks, b_pool_w, b_pool_scale, ab_w_out, c_w_in, c_dw_w, c_dw_b, c_ln_g, c_ln_b, c_w_out, loss_target, m_pre_norm, m_post_norm, m_a_w_in, m_a_sinks, m_b_pool_w, m_b_pool_scale, m_ab_w_out, m_c_w_in, m_c_dw_w, m_c_dw_b, m_c_ln_g, m_c_ln_b, m_c_w_out, v_pre_norm, v_post_norm, v_a_w_in, v_a_sinks, v_b_pool_w, v_b_pool_scale, v_ab_w_out, v_c_w_in, v_c_dw_w, v_c_dw_b, v_c_ln_g, v_c_ln_b, v_c_w_out):
    given = dict(x=x, pre_norm=pre_norm, post_norm=post_norm, a_w_in=a_w_in, a_sinks=a_sinks, b_pool_w=b_pool_w, b_pool_scale=b_pool_scale, ab_w_out=ab_w_out, c_w_in=c_w_in, c_dw_w=c_dw_w, c_dw_b=c_dw_b, c_ln_g=c_ln_g, c_ln_b=c_ln_b, c_w_out=c_w_out, loss_target=loss_target, m_pre_norm=m_pre_norm, m_post_norm=m_post_norm, m_a_w_in=m_a_w_in, m_a_sinks=m_a_sinks, m_b_pool_w=m_b_pool_w, m_b_pool_scale=m_b_pool_scale, m_ab_w_out=m_ab_w_out, m_c_w_in=m_c_w_in, m_c_dw_w=m_c_dw_w, m_c_dw_b=m_c_dw_b, m_c_ln_g=m_c_ln_g, m_c_ln_b=m_c_ln_b, m_c_w_out=m_c_w_out, v_pre_norm=v_pre_norm, v_post_norm=v_post_norm, v_a_w_in=v_a_w_in, v_a_sinks=v_a_sinks, v_b_pool_w=v_b_pool_w, v_b_pool_scale=v_b_pool_scale, v_ab_w_out=v_ab_w_out, v_c_w_in=v_c_w_in, v_c_dw_w=v_c_dw_w, v_c_dw_b=v_c_dw_b, v_c_ln_g=v_c_ln_g, v_c_ln_b=v_c_ln_b, v_c_w_out=v_c_w_out)
    weights = {n: given[n] for n in TWIN_WEIGHTS}
    shared = {n: given[n] for n in SHARED_INPUTS}
    per_example = {n: given[n] for n in ['x']}
    grad_fn = _jax.value_and_grad(_loss, argnums=(0, 1))

    def one_microbatch(ex, loss_target):
        ex = dict(ex)
        diff = ex.pop(TWIN_DIFF_INPUT)
        return grad_fn(weights, diff, {**shared, **ex}, loss_target)

    if N_MICROBATCH == 1:
        loss, (grad_w, grad_x) = one_microbatch(per_example, given["loss_target"])
    else:
        def body(carry, xs):
            loss_sum, grad_sum = carry
            l_k, (gw_k, gx_k) = one_microbatch(xs[0], xs[1])
            with _jax.named_scope("update"):
                return (loss_sum + l_k, _jax.tree.map(_jnp.add, grad_sum, gw_k)), gx_k

        init = (_jnp.zeros((), _jnp.float32), _jax.tree.map(_jnp.zeros_like, weights))
        (loss, grad_w), grad_x = _jax.lax.scan(body, init, (per_example, given["loss_target"]))
    with _jax.named_scope("update"):
        delta_w, new_m, new_v = {}, {}, {}
        for n in TWIN_WEIGHTS:
            delta_w[n], new_m[n], new_v[n] = _adamw(weights[n], grad_w[n], given["m_" + n], given["v_" + n])
    return (loss, grad_x, *[grad_w[n] for n in TWIN_WEIGHTS], *[delta_w[n] for n in TWIN_WEIGHTS],
            *[new_m[n] for n in TWIN_WEIGHTS], *[new_v[n] for n in TWIN_WEIGHTS])
```

```python
import functools

import jax
import jax.numpy as jnp
from jax import lax
from jax.experimental import pallas as pl
from jax.experimental.pallas import tpu as pltpu

F32 = jnp.float32
BF16 = jnp.bfloat16
MESH = pl.DeviceIdType.MESH
AXES = ("x", "y", "c")
N_DEV = 8

D_MODEL = 1024
HEAD_DIM = 64
Q_HEADS = 8
GROUP = 4
ATTN_WIDTH = 512
KV_WIDTH = 128
BLOCK = 128
POOL_WIDTH = 512
POOL_WINDOWS = (2, 4, 8, 16)
POOL_GC = 128
POOL_HALO = 16
EVEN_IN = 2304
CONV_K = 31
CONV_HALO = 32
ODD_IN = 3072
EPS = 1e-6
NEG = -1e30
SLOPES = tuple(2.0 ** (-8.0 * (h + 1) / Q_HEADS) for h in range(Q_HEADS))

ADAM_LR = 0.001
ADAM_B1 = 0.9
ADAM_B2 = 0.999
ADAM_EPS = 1e-08
ADAM_WD = 0.01
ADAM_STEP = 10

SUBLANES = 8
LANES = 128
VMEM_LIMIT = 56 * 1024 * 1024

NT = (((1,), (1,)), ((), ()))
TN = (((0,), (0,)), ((), ()))


def _params(**kw):
    return pltpu.CompilerParams(dimension_semantics=("arbitrary",), vmem_limit_bytes=VMEM_LIMIT, **kw)


def _dot(a, b):
    return jnp.dot(a, b, preferred_element_type=F32)


def _dot_nt(a, b):
    return lax.dot_general(a, b, NT, preferred_element_type=F32)


def _dot_tn(a, b):
    return lax.dot_general(a, b, TN, preferred_element_type=F32)


def _sigmoid(v):
    return 1.0 / (1.0 + jnp.exp(-v))


def _rows8(v):
    r, c = v.shape
    return jnp.sum(v.reshape(r // SUBLANES, SUBLANES, c), axis=0)


def _rms_fwd(v, g):
    r = lax.rsqrt(jnp.mean(v * v, axis=-1, keepdims=True) + EPS)
    return v * r * g, r


def _rms_bwd(v, r, g, dout):
    gd = dout * g
    dv = r * gd - v * (r * r * r) * jnp.mean(v * gd, axis=-1, keepdims=True)
    return dv, dout * (v * r)


def _full(shape):
    return pl.BlockSpec(shape, lambda i: (0,) * len(shape))


def _tile(ts, cols, col_block=0):
    return pl.BlockSpec((ts, cols), lambda i: (i, col_block))


def _position():
    return lax.axis_index("x"), lax.axis_index("y"), lax.axis_index("c")


def _all_gather(block, name):
    rows, cols = block.shape

    def body(x_ref, out_ref, send_sems, recv_sems, local_sem):
        x, y, c = _position()
        me, sibling = (x, y, c), (x, y, 1 - c)
        chips = [(1 - x, y), (x, 1 - y), (1 - x, 1 - y)]

        def slab(px, py, pc):
            return out_ref.at[4 * px + 2 * py + pc]

        def copy(k, owner, to, src=None):
            return pltpu.make_async_remote_copy(
                src_ref=slab(*owner) if src is None else src, dst_ref=slab(*owner),
                send_sem=send_sems.at[k], recv_sem=recv_sems.at[k], device_id=to, device_id_type=MESH)

        mine = pltpu.make_async_copy(x_ref, slab(*me), local_sem)
        mine.start()
        first = [copy(0, me, sibling, src=x_ref)]
        first += [copy(1 + j, me, (*chip, c), src=x_ref) for j, chip in enumerate(chips)]
        for cp in first:
            cp.start()
        passed = [copy(4 + j, (*chip, c), sibling) for j, chip in enumerate(chips)]
        for j, chip in enumerate(chips):
            copy(1 + j, (*chip, c), me).wait_recv()
            passed[j].start()
        copy(0, sibling, me).wait_recv()
        for j, chip in enumerate(chips):
            copy(4 + j, (*chip, 1 - c), me).wait_recv()
        for cp in first + passed:
            cp.wait_send()
        mine.wait()

    return pl.pallas_call(
        body, name=name,
        out_shape=jax.ShapeDtypeStruct((N_DEV, rows, cols), block.dtype),
        in_specs=[pl.BlockSpec(memory_space=pltpu.VMEM)],
        out_specs=pl.BlockSpec(memory_space=pltpu.VMEM),
        scratch_shapes=[pltpu.SemaphoreType.DMA((7,)), pltpu.SemaphoreType.DMA((7,)), pltpu.SemaphoreType.DMA],
        compiler_params=pltpu.CompilerParams(vmem_limit_bytes=VMEM_LIMIT),
    )(block)


def _all_reduce(block, name):
    gathered = _all_gather(block, name + "_gather")
    rows, cols = block.shape

    def body(g_ref, out_ref):
        acc = g_ref[0]
        for d in range(1, N_DEV):
            acc = acc + g_ref[d]
        out_ref[...] = acc

    return pl.pallas_call(body, name=name + "_sum", out_shape=jax.ShapeDtypeStruct((rows, cols), F32))(gathered)


def _peer(group, k):
    x, y, c = _position()
    if group == "c":
        return (x, y, c ^ k), c
    if group == "xy":
        return (x ^ (k >> 1), y ^ (k & 1), c), 2 * x + y
    return (x ^ (k >> 2), y ^ ((k >> 1) & 1), c ^ (k & 1)), 4 * x + 2 * y + c


def _exchange_sum(parts, group, name):
    n_peers, rows, cols = parts.shape

    def body(parts_ref, out_ref, recv_ref, send_sems, recv_sems, local_sem):
        _, me = _peer(group, 0)
        mine = pltpu.make_async_copy(parts_ref.at[me], out_ref, local_sem)
        mine.start()
        copies = []
        for k in range(1, n_peers):
            to, _ = _peer(group, k)
            cp = pltpu.make_async_remote_copy(
                src_ref=parts_ref.at[me ^ k], dst_ref=recv_ref.at[k - 1],
                send_sem=send_sems.at[k - 1], recv_sem=recv_sems.at[k - 1], device_id=to, device_id_type=MESH)
            cp.start()
            copies.append(cp)
        mine.wait()
        for k, cp in enumerate(copies):
            cp.wait_recv()
            out_ref[...] = out_ref[...] + recv_ref[k]
        for cp in copies:
            cp.wait_send()

    return pl.pallas_call(
        body, name=name,
        out_shape=jax.ShapeDtypeStruct((rows, cols), F32),
        in_specs=[pl.BlockSpec(memory_space=pl.ANY)],
        out_specs=pl.BlockSpec(memory_space=pltpu.VMEM),
        scratch_shapes=[pltpu.VMEM((n_peers - 1, rows, cols), F32), pltpu.SemaphoreType.DMA((n_peers - 1,)),
                        pltpu.SemaphoreType.DMA((n_peers - 1,)), pltpu.SemaphoreType.DMA],
        compiler_params=pltpu.CompilerParams(vmem_limit_bytes=VMEM_LIMIT),
    )(parts)


def _fwd_in0(x, g, w, ts):
    seq = x.shape[0]

    def body(x_ref, g_ref, w_ref, h_ref, q_ref, kv_ref, gug_ref):
        h, _ = _rms_fwd(x_ref[...], g_ref[...])
        h = h.astype(BF16)
        h_ref[...] = h
        proj = _dot(h, w_ref[...])
        q_ref[...] = proj[:, :ATTN_WIDTH].astype(BF16)
        kv_ref[...] = proj[:, ATTN_WIDTH:ATTN_WIDTH + 2 * KV_WIDTH].astype(BF16)
        gug_ref[...] = proj[:, ATTN_WIDTH + 2 * KV_WIDTH:]

    return pl.pallas_call(
        body, name="fwd_in0", grid=(seq // ts,),
        in_specs=[_tile(ts, D_MODEL), _full((1, D_MODEL)), _full((D_MODEL, EVEN_IN))],
        out_specs=[_tile(ts, D_MODEL), _tile(ts, ATTN_WIDTH), _tile(ts, 2 * KV_WIDTH), _tile(ts, 3 * POOL_WIDTH)],
        out_shape=[jax.ShapeDtypeStruct((seq, D_MODEL), BF16), jax.ShapeDtypeStruct((seq, ATTN_WIDTH), BF16),
                   jax.ShapeDtypeStruct((seq, 2 * KV_WIDTH), BF16), jax.ShapeDtypeStruct((seq, 3 * POOL_WIDTH), F32)],
        compiler_params=_params(),
    )(x, g, w)


def _attn_mask(first_block):
    row = lax.broadcasted_iota(jnp.int32, (BLOCK, 2 * BLOCK), 0)
    col = lax.broadcasted_iota(jnp.int32, (BLOCK, 2 * BLOCK), 1)
    dist = row + BLOCK - col
    valid = (dist >= 0) & (dist < BLOCK) & ((col >= BLOCK) | jnp.logical_not(first_block))
    return valid, dist.astype(F32)


def _attn_probs(qh, kh, sink, slope, valid, distf):
    s = _dot_nt(qh, kh) * (HEAD_DIM ** -0.5)
    s = jnp.where(valid, s - slope * distf, NEG)
    mx = jnp.maximum(jnp.max(s, axis=-1, keepdims=True), sink)
    e = jnp.exp(s - mx)
    es = jnp.exp(sink - mx)
    den = jnp.sum(e, axis=-1, keepdims=True) + es
    return e / den, es / den


def _attn_fwd(q, kv, sinks):
    seq = q.shape[0]
    nb = seq // BLOCK

    def body(sink_ref, q_ref, kvc_ref, kvp_ref, o_ref):
        n = pl.program_id(0)
        valid, distf = _attn_mask(n == 0)
        qb = q_ref[...]
        kk = jnp.concatenate([kvp_ref[...], kvc_ref[...]], axis=0)
        for h in range(Q_HEADS):
            kvh = h // GROUP
            qh = qb[:, HEAD_DIM * h:HEAD_DIM * (h + 1)]
            kh = kk[:, HEAD_DIM * kvh:HEAD_DIM * (kvh + 1)]
            vh = kk[:, KV_WIDTH + HEAD_DIM * kvh:KV_WIDTH + HEAD_DIM * (kvh + 1)]
            p, _ = _attn_probs(qh, kh, sink_ref[h], SLOPES[h], valid, distf)
            o_ref[:, HEAD_DIM * h:HEAD_DIM * (h + 1)] = _dot(p.astype(BF16), vh)

    return pl.pallas_call(
        body, name="attn_fwd", grid=(nb,),
        in_specs=[pl.BlockSpec(memory_space=pltpu.SMEM),
                  pl.BlockSpec((BLOCK, ATTN_WIDTH), lambda n: (n, 0)),
                  pl.BlockSpec((BLOCK, 2 * KV_WIDTH), lambda n: (n, 0)),
                  pl.BlockSpec((BLOCK, 2 * KV_WIDTH), lambda n: (jnp.maximum(n - 1, 0), 0))],
        out_specs=pl.BlockSpec((BLOCK, ATTN_WIDTH), lambda n: (n, 0)),
        out_shape=jax.ShapeDtypeStruct((seq, ATTN_WIDTH), F32),
        compiler_params=_params(),
    )(sinks, q, kv, kv)


def _pool_counts(first_row, rows, window):
    t = first_row + lax.broadcasted_iota(jnp.int32, (rows, 1), 0)
    return jnp.minimum(t + 1, window).astype(F32)


def _fwd_out0(gug, o, pool_w, pool_scale, w_out, g_post, x, ts):
    seq = x.shape[0]
    hb = ts // POOL_HALO

    def body(gug_ref, halo_ref, o_ref, pw_ref, ps_ref, w_ref, g_ref, x_ref, mix_ref, pooled_ref, y_ref, x1_ref, ubuf):
        i = pl.program_id(0)
        ga = gug_ref[:, :ATTN_WIDTH]
        u = gug_ref[:, ATTN_WIDTH:ATTN_WIDTH + POOL_WIDTH]
        gb = gug_ref[:, ATTN_WIDTH + POOL_WIDTH:]
        mix_ref[:, :ATTN_WIDTH] = (o_ref[...] * (ga * _sigmoid(ga))).astype(BF16)
        ubuf[:POOL_HALO, :] = jnp.where(i > 0, halo_ref[...], 0.0)
        ubuf[POOL_HALO:, :] = u
        silu_gb = gb * _sigmoid(gb)
        for g, window in enumerate(POOL_WINDOWS):
            lanes = slice(POOL_GC * g, POOL_GC * (g + 1))
            acc = ubuf[pl.ds(POOL_HALO, ts), lanes]
            for k in range(1, window):
                acc = acc + ubuf[pl.ds(POOL_HALO - k, ts), lanes]
            pooled = (acc / _pool_counts(i * ts, ts, window) - u[:, lanes]).astype(BF16)
            pooled_ref[:, lanes] = pooled
            ypool = _dot(pooled, pw_ref[g].astype(BF16)) * ps_ref[:, lanes]
            mix_ref[:, ATTN_WIDTH + POOL_GC * g:ATTN_WIDTH + POOL_GC * (g + 1)] = (ypool * silu_gb[:, lanes]).astype(BF16)
        y = _dot(mix_ref[...], w_ref[...])
        y_ref[...] = y
        yn, _ = _rms_fwd(y, g_ref[...])
        x1_ref[...] = x_ref[...] + yn

    return pl.pallas_call(
        body, name="fwd_out0", grid=(seq // ts,),
        in_specs=[_tile(ts, 3 * POOL_WIDTH),
                  pl.BlockSpec((POOL_HALO, POOL_WIDTH), lambda i: (jnp.maximum(i * hb - 1, 0), 1)),
                  _tile(ts, ATTN_WIDTH), _full((4, POOL_GC, POOL_GC)), _full((1, POOL_WIDTH)),
                  _full((D_MODEL, D_MODEL)), _full((1, D_MODEL)), _tile(ts, D_MODEL)],
        out_specs=[_tile(ts, D_MODEL), _tile(ts, POOL_WIDTH), _tile(ts, D_MODEL), _tile(ts, D_MODEL)],
        out_shape=[jax.ShapeDtypeStruct((seq, D_MODEL), BF16), jax.ShapeDtypeStruct((seq, POOL_WIDTH), BF16),
                   jax.ShapeDtypeStruct((seq, D_MODEL), F32), jax.ShapeDtypeStruct((seq, D_MODEL), F32)],
        scratch_shapes=[pltpu.VMEM((ts + POOL_HALO, POOL_WIDTH), F32)],
        compiler_params=_params(),
    )(gug, gug, o, pool_w, pool_scale, w_out, g_post, x)


def _fwd_in1(x1, g, w, ts):
    seq = x1.shape[0]

    def body(x_ref, g_ref, w_ref, h_ref, proj_ref, glu_ref):
        h, _ = _rms_fwd(x_ref[...], g_ref[...])
        h = h.astype(BF16)
        h_ref[...] = h
        proj = _dot(h, w_ref[...])
        proj_ref[...] = proj
        glu_ref[...] = proj[:, :D_MODEL] * _sigmoid(proj[:, D_MODEL:2 * D_MODEL])

    return pl.pallas_call(
        body, name="fwd_in1", grid=(seq // ts,),
        in_specs=[_tile(ts, D_MODEL), _full((1, D_MODEL)), _full((D_MODEL, ODD_IN))],
        out_specs=[_tile(ts, D_MODEL), _tile(ts, ODD_IN), _tile(ts, D_MODEL)],
        out_shape=[jax.ShapeDtypeStruct((seq, D_MODEL), BF16), jax.ShapeDtypeStruct((seq, ODD_IN), F32),
                   jax.ShapeDtypeStruct((seq, D_MODEL), F32)],
        compiler_params=_params(),
    )(x1, g, w)


ACC_LOSS, ACC_POST, ACC_LN_G, ACC_LN_B, ACC_DW_B = range(5)


def _fwd_out1(glu, proj, dw_w, dw_b, ln_g, ln_b, w_out, g_post, x1, target, ts):
    seq = x1.shape[0]
    hb = ts // CONV_HALO

    def body(glu_ref, halo_ref, gate_ref, dww_ref, dwb_ref, lng_ref, lnb_ref, w_ref, g_ref, x1_ref, t_ref,
             ymix_ref, dy_ref, dx2_ref, dcf_ref, dgate_ref, acc_ref, gbuf):
        i = pl.program_id(0)

        @pl.when(i == 0)
        def _():
            acc_ref[...] = jnp.zeros_like(acc_ref)

        gbuf[:CONV_HALO, :] = jnp.where(i > 0, halo_ref[...], 0.0)
        gbuf[CONV_HALO:, :] = glu_ref[...]
        conv = dww_ref[0:1, :] * gbuf[pl.ds(CONV_HALO - CONV_K + 1, ts), :]
        for k in range(1, CONV_K):
            conv = conv + dww_ref[k:k + 1, :] * gbuf[pl.ds(CONV_HALO - CONV_K + 1 + k, ts), :]
        cf = conv + dwb_ref[...]
        mu = jnp.mean(cf, axis=-1, keepdims=True)
        cen = cf - mu
        rs = lax.rsqrt(jnp.mean(cen * cen, axis=-1, keepdims=True) + EPS)
        xhat = cen * rs
        cn = xhat * lng_ref[...] + lnb_ref[...]
        gate = gate_ref[...]
        sg = _sigmoid(gate)
        sc = _sigmoid(cn)
        silu_gate = gate * sg
        silu_cn = cn * sc
        ymix = (silu_cn * silu_gate).astype(BF16)
        ymix_ref[...] = ymix
        y = _dot(ymix, w_ref[...])
        yn, r = _rms_fwd(y, g_ref[...])
        err = (x1_ref[...] + yn) - t_ref[...]
        acc_ref[ACC_LOSS] += _rows8(err * err)
        dx2 = err * (1.0 / D_MODEL)
        dx2_ref[...] = dx2
        dy, dpost = _rms_bwd(y, r, g_ref[...], dx2)
        acc_ref[ACC_POST] += _rows8(dpost)
        dy = dy.astype(BF16)
        dy_ref[...] = dy
        dymix = _dot_nt(dy, w_ref[...])
        dgate_ref[...] = (dymix * silu_cn * (sg * (1.0 + gate * (1.0 - sg)))).astype(BF16)
        dcn = dymix * silu_gate * (sc * (1.0 + cn * (1.0 - sc)))
        acc_ref[ACC_LN_G] += _rows8(dcn * xhat)
        acc_ref[ACC_LN_B] += _rows8(dcn)
        dxhat = dcn * lng_ref[...]
        dcf = rs * (dxhat - jnp.mean(dxhat, axis=-1, keepdims=True)
                    - xhat * jnp.mean(dxhat * xhat, axis=-1, keepdims=True))
        acc_ref[ACC_DW_B] += _rows8(dcf)
        dcf_ref[...] = dcf

    return pl.pallas_call(
        body, name="fwd_out1", grid=(seq // ts,),
        in_specs=[_tile(ts, D_MODEL),
                  pl.BlockSpec((CONV_HALO, D_MODEL), lambda i: (jnp.maximum(i * hb - 1, 0), 0)),
                  _tile(ts, D_MODEL, 2), _full((CONV_HALO, D_MODEL)), _full((1, D_MODEL)), _full((1, D_MODEL)),
                  _full((1, D_MODEL)), _full((D_MODEL, D_MODEL)), _full((1, D_MODEL)), _tile(ts, D_MODEL),
                  _tile(ts, D_MODEL)],
        out_specs=[_tile(ts, D_MODEL), _tile(ts, D_MODEL), _tile(ts, D_MODEL), _tile(ts, D_MODEL), _tile(ts, D_MODEL),
                   _full((5, SUBLANES, D_MODEL))],
        out_shape=[jax.ShapeDtypeStruct((seq, D_MODEL), BF16), jax.ShapeDtypeStruct((seq, D_MODEL), BF16),
                   jax.ShapeDtypeStruct((seq, D_MODEL), F32), jax.ShapeDtypeStruct((seq, D_MODEL), F32),
                   jax.ShapeDtypeStruct((seq, D_MODEL), BF16), jax.ShapeDtypeStruct((5, SUBLANES, D_MODEL), F32)],
        scratch_shapes=[pltpu.VMEM((ts + CONV_HALO, D_MODEL), F32)],
        compiler_params=_params(),
    )(glu, glu, proj, dw_w, dw_b, ln_g, ln_b, w_out, g_post, x1, target)


def _bwd_in1(dcf, glu, proj, dgate, dw_w, w_in, x1, g_pre, dx2, ts):
    seq = x1.shape[0]
    hb = ts // CONV_HALO
    last = seq // CONV_HALO - 1
    nt = seq // ts

    def body(dcf_ref, dnext_ref, glu_ref, gprev_ref, ab_ref, dgate_ref, dww_ref, w_ref, x_ref, g_ref, dx2_ref,
             dproj_ref, dx1_ref, ddw_ref, dpre_ref, dbuf, gbuf):
        i = pl.program_id(0)

        @pl.when(i == 0)
        def _():
            ddw_ref[...] = jnp.zeros_like(ddw_ref)
            dpre_ref[...] = jnp.zeros_like(dpre_ref)

        dcf = dcf_ref[...]
        dbuf[:ts, :] = dcf
        dbuf[ts:, :] = jnp.where(i < nt - 1, dnext_ref[...], 0.0)
        gbuf[:CONV_HALO, :] = jnp.where(i > 0, gprev_ref[...], 0.0)
        gbuf[CONV_HALO:, :] = glu_ref[...]
        dglu = dww_ref[0:1, :] * dbuf[pl.ds(CONV_K - 1, ts), :]
        ddw_ref[0:1, :] += jnp.sum(dcf * gbuf[pl.ds(CONV_HALO - CONV_K + 1, ts), :], axis=0, keepdims=True)
        for k in range(1, CONV_K):
            dglu = dglu + dww_ref[k:k + 1, :] * dbuf[pl.ds(CONV_K - 1 - k, ts), :]
            ddw_ref[k:k + 1, :] += jnp.sum(dcf * gbuf[pl.ds(CONV_HALO - CONV_K + 1 + k, ts), :], axis=0, keepdims=True)
        a = ab_ref[:, :D_MODEL]
        sb = _sigmoid(ab_ref[:, D_MODEL:])
        dproj_ref[:, :D_MODEL] = (dglu * sb).astype(BF16)
        dproj_ref[:, D_MODEL:2 * D_MODEL] = (dglu * a * sb * (1.0 - sb)).astype(BF16)
        dproj_ref[:, 2 * D_MODEL:] = dgate_ref[...]
        dh = _dot_nt(dproj_ref[...], w_ref[...])
        x = x_ref[...]
        r = lax.rsqrt(jnp.mean(x * x, axis=-1, keepdims=True) + EPS)
        dx, dpre = _rms_bwd(x, r, g_ref[...], dh)
        dx1_ref[...] = dx2_ref[...] + dx
        dpre_ref[...] += _rows8(dpre)

    return pl.pallas_call(
        body, name="bwd_in1", grid=(nt,),
        in_specs=[_tile(ts, D_MODEL),
                  pl.BlockSpec((CONV_HALO, D_MODEL), lambda i: (jnp.minimum((i + 1) * hb, last), 0)),
                  _tile(ts, D_MODEL),
                  pl.BlockSpec((CONV_HALO, D_MODEL), lambda i: (jnp.maximum(i * hb - 1, 0), 0)),
                  _tile(ts, 2 * D_MODEL), _tile(ts, D_MODEL), _full((CONV_HALO, D_MODEL)),
                  _full((D_MODEL, ODD_IN)), _tile(ts, D_MODEL), _full((1, D_MODEL)), _tile(ts, D_MODEL)],
        out_specs=[_tile(ts, ODD_IN), _tile(ts, D_MODEL), _full((CONV_HALO, D_MODEL)), _full((SUBLANES, D_MODEL))],
        out_shape=[jax.ShapeDtypeStruct((seq, ODD_IN), BF16), jax.ShapeDtypeStruct((seq, D_MODEL), F32),
                   jax.ShapeDtypeStruct((CONV_HALO, D_MODEL), F32), jax.ShapeDtypeStruct((SUBLANES, D_MODEL), F32)],
        scratch_shapes=[pltpu.VMEM((ts + CONV_HALO, D_MODEL), F32), pltpu.VMEM((ts + CONV_HALO, D_MODEL), F32)],
        compiler_params=_params(),
    )(dcf, dcf, glu, glu, proj, dgate, dw_w, w_in, x1, g_pre, dx2)


def _bwd_out0(dx1, y0, g_post, w_out, gug, o, pooled, pool_w, pool_scale, ts):
    seq = dx1.shape[0]

    def body(dx1_ref, y_ref, g_ref, w_ref, gug_ref, o_ref, pooled_ref, pw_ref, ps_ref,
             dy_ref, do_ref, dgg_ref, dpooled_ref, dpost_ref, dscale_ref, dpw_ref):
        i = pl.program_id(0)

        @pl.when(i == 0)
        def _():
            dpost_ref[...] = jnp.zeros_like(dpost_ref)
            dscale_ref[...] = jnp.zeros_like(dscale_ref)
            dpw_ref[...] = jnp.zeros_like(dpw_ref)

        y = y_ref[...]
        r = lax.rsqrt(jnp.mean(y * y, axis=-1, keepdims=True) + EPS)
        dy, dpost = _rms_bwd(y, r, g_ref[...], dx1_ref[...])
        dpost_ref[...] += _rows8(dpost)
        dy = dy.astype(BF16)
        dy_ref[...] = dy
        dmix = _dot_nt(dy, w_ref[...])
        dya = dmix[:, :ATTN_WIDTH]
        dyb = dmix[:, ATTN_WIDTH:]
        ga = gug_ref[:, :ATTN_WIDTH]
        gb = gug_ref[:, ATTN_WIDTH + POOL_WIDTH:]
        sga = _sigmoid(ga)
        sgb = _sigmoid(gb)
        do_ref[...] = (dya * (ga * sga)).astype(BF16)
        dgg_ref[:, :ATTN_WIDTH] = (dya * o_ref[...] * (sga * (1.0 + ga * (1.0 - sga)))).astype(BF16)
        dypool = dyb * (gb * sgb)
        dsilu_gb = sgb * (1.0 + gb * (1.0 - sgb))
        for g in range(len(POOL_WINDOWS)):
            lanes = slice(POOL_GC * g, POOL_GC * (g + 1))
            pooled = pooled_ref[:, lanes]
            wg = pw_ref[g].astype(BF16)
            pw = _dot(pooled, wg)
            scale = ps_ref[:, lanes]
            dgg_ref[:, ATTN_WIDTH + POOL_GC * g:ATTN_WIDTH + POOL_GC * (g + 1)] = (
                dyb[:, lanes] * (pw * scale) * dsilu_gb[:, lanes]).astype(BF16)
            dscale_ref[:, lanes] += _rows8(dypool[:, lanes] * pw)
            dpw = (dypool[:, lanes] * scale).astype(BF16)
            dpooled_ref[:, lanes] = _dot_nt(dpw, wg)
            dpw_ref[g] += _dot_tn(pooled, dpw)

    return pl.pallas_call(
        body, name="bwd_out0", grid=(seq // ts,),
        in_specs=[_tile(ts, D_MODEL), _tile(ts, D_MODEL), _full((1, D_MODEL)), _full((D_MODEL, D_MODEL)),
                  _tile(ts, 3 * POOL_WIDTH), _tile(ts, ATTN_WIDTH), _tile(ts, POOL_WIDTH),
                  _full((4, POOL_GC, POOL_GC)), _full((1, POOL_WIDTH))],
        out_specs=[_tile(ts, D_MODEL), _tile(ts, ATTN_WIDTH), _tile(ts, ATTN_WIDTH + POOL_WIDTH), _tile(ts, POOL_WIDTH),
                   _full((SUBLANES, D_MODEL)), _full((SUBLANES, POOL_WIDTH)), _full((4, POOL_GC, POOL_GC))],
        out_shape=[jax.ShapeDtypeStruct((seq, D_MODEL), BF16), jax.ShapeDtypeStruct((seq, ATTN_WIDTH), BF16),
                   jax.ShapeDtypeStruct((seq, ATTN_WIDTH + POOL_WIDTH), BF16), jax.ShapeDtypeStruct((seq, POOL_WIDTH), F32),
                   jax.ShapeDtypeStruct((SUBLANES, D_MODEL), F32), jax.ShapeDtypeStruct((SUBLANES, POOL_WIDTH), F32),
                   jax.ShapeDtypeStruct((4, POOL_GC, POOL_GC), F32)],
        compiler_params=_params(),
    )(dx1, y0, g_post, w_out, gug, o, pooled, pool_w, pool_scale)


def _attn_bwd(q, kv, do, sinks):
    seq = q.shape[0]
    nb = seq // BLOCK

    def qblock(j):
        return jnp.minimum(j, nb - 1)

    def body(sink_ref, q_ref, kvc_ref, kvp_ref, do_ref, dq_ref, dkv_ref, dsink_ref, carry, dkv_acc):
        j = pl.program_id(0)

        @pl.when(j == 0)
        def _():
            dsink_ref[...] = jnp.zeros_like(dsink_ref)
            carry[...] = jnp.zeros_like(carry)

        @pl.when(j < nb)
        def _():
            valid, distf = _attn_mask(j == 0)
            qb = q_ref[...]
            dob = do_ref[...]
            kk = jnp.concatenate([kvp_ref[...], kvc_ref[...]], axis=0)
            lane = lax.broadcasted_iota(jnp.int32, (BLOCK, LANES), 1)
            dsink = jnp.zeros((BLOCK, LANES), F32)
            for kvh in range(Q_HEADS // GROUP):
                kh = kk[:, HEAD_DIM * kvh:HEAD_DIM * (kvh + 1)]
                vh = kk[:, KV_WIDTH + HEAD_DIM * kvh:KV_WIDTH + HEAD_DIM * (kvh + 1)]
                dk = jnp.zeros((2 * BLOCK, HEAD_DIM), F32)
                dv = jnp.zeros((2 * BLOCK, HEAD_DIM), F32)
                for h in range(GROUP * kvh, GROUP * (kvh + 1)):
                    qh = qb[:, HEAD_DIM * h:HEAD_DIM * (h + 1)]
                    doh = dob[:, HEAD_DIM * h:HEAD_DIM * (h + 1)]
                    p, psink = _attn_probs(qh, kh, sink_ref[h], SLOPES[h], valid, distf)
                    dp = _dot_nt(doh, vh)
                    delta = jnp.sum(p * dp, axis=-1, keepdims=True)
                    ds = (p * (dp - delta) * (HEAD_DIM ** -0.5)).astype(BF16)
                    dsink = dsink + jnp.where(lane == h, -psink * delta, 0.0)
                    dq_ref[:, HEAD_DIM * h:HEAD_DIM * (h + 1)] = _dot(ds, kh).astype(BF16)
                    dk = dk + _dot_tn(ds, qh)
                    dv = dv + _dot_tn(p.astype(BF16), doh)
                dkv_acc[:, HEAD_DIM * kvh:HEAD_DIM * (kvh + 1)] = dk
                dkv_acc[:, KV_WIDTH + HEAD_DIM * kvh:KV_WIDTH + HEAD_DIM * (kvh + 1)] = dv
            dsink_ref[...] += dsink

            @pl.when(j > 0)
            def _():
                dkv_ref[...] = (carry[...] + dkv_acc[:BLOCK, :]).astype(BF16)

            carry[...] = dkv_acc[BLOCK:, :]

        @pl.when(j == nb)
        def _():
            dkv_ref[...] = carry[...].astype(BF16)

    return pl.pallas_call(
        body, name="attn_bwd", grid=(nb + 1,),
        in_specs=[pl.BlockSpec(memory_space=pltpu.SMEM),
                  pl.BlockSpec((BLOCK, ATTN_WIDTH), lambda j: (qblock(j), 0)),
                  pl.BlockSpec((BLOCK, 2 * KV_WIDTH), lambda j: (qblock(j), 0)),
                  pl.BlockSpec((BLOCK, 2 * KV_WIDTH), lambda j: (jnp.maximum(qblock(j) - 1, 0), 0)),
                  pl.BlockSpec((BLOCK, ATTN_WIDTH), lambda j: (qblock(j), 0))],
        out_specs=[pl.BlockSpec((BLOCK, ATTN_WIDTH), lambda j: (qblock(j), 0)),
                   pl.BlockSpec((BLOCK, 2 * KV_WIDTH), lambda j: (jnp.maximum(j - 1, 0), 0)),
                   _full((BLOCK, LANES))],
        out_shape=[jax.ShapeDtypeStruct((seq, ATTN_WIDTH), BF16), jax.ShapeDtypeStruct((seq, 2 * KV_WIDTH), BF16),
                   jax.ShapeDtypeStruct((BLOCK, LANES), F32)],
        scratch_shapes=[pltpu.VMEM((BLOCK, 2 * KV_WIDTH), F32), pltpu.VMEM((2 * BLOCK, 2 * KV_WIDTH), F32)],
        compiler_params=_params(),
    )(sinks, q, kv, kv, do)


def _bwd_in0(dpooled, dq, dkv, dgg, w_in, x, g_pre, dx1, ts):
    seq = x.shape[0]
    hb = ts // POOL_HALO
    last = seq // POOL_HALO - 1
    nt = seq // ts

    def body(dp_ref, dnext_ref, dq_ref, dkv_ref, dgg_ref, w_ref, x_ref, g_ref, dx1_ref,
             dproj_ref, gx_ref, dpre_ref, dbuf):
        i = pl.program_id(0)

        @pl.when(i == 0)
        def _():
            dpre_ref[...] = jnp.zeros_like(dpre_ref)

        dpool = dp_ref[...]
        dnext = jnp.where(i < nt - 1, dnext_ref[...], 0.0)
        u0 = ATTN_WIDTH + 2 * KV_WIDTH + ATTN_WIDTH
        for g, window in enumerate(POOL_WINDOWS):
            lanes = slice(POOL_GC * g, POOL_GC * (g + 1))
            dbuf[:ts, lanes] = dpool[:, lanes] / _pool_counts(i * ts, ts, window)
            dbuf[ts:, lanes] = dnext[:, lanes] / _pool_counts((i + 1) * ts, POOL_HALO, window)
        for g, window in enumerate(POOL_WINDOWS):
            lanes = slice(POOL_GC * g, POOL_GC * (g + 1))
            acc = dbuf[pl.ds(0, ts), lanes]
            for k in range(1, window):
                acc = acc + dbuf[pl.ds(k, ts), lanes]
            dproj_ref[:, u0 + POOL_GC * g:u0 + POOL_GC * (g + 1)] = (acc - dpool[:, lanes]).astype(BF16)
        dproj_ref[:, :ATTN_WIDTH] = dq_ref[...]
        dproj_ref[:, ATTN_WIDTH:ATTN_WIDTH + 2 * KV_WIDTH] = dkv_ref[...]
        dproj_ref[:, ATTN_WIDTH + 2 * KV_WIDTH:u0] = dgg_ref[:, :ATTN_WIDTH]
        dproj_ref[:, u0 + POOL_WIDTH:] = dgg_ref[:, ATTN_WIDTH:]
        dh = _dot_nt(dproj_ref[...], w_ref[...])
        x = x_ref[...]
        r = lax.rsqrt(jnp.mean(x * x, axis=-1, keepdims=True) + EPS)
        dx, dpre = _rms_bwd(x, r, g_ref[...], dh)
        gx_ref[...] = dx1_ref[...] + dx
        dpre_ref[...] += _rows8(dpre)

    return pl.pallas_call(
        body, name="bwd_in0", grid=(nt,),
        in_specs=[_tile(ts, POOL_WIDTH),
                  pl.BlockSpec((POOL_HALO, POOL_WIDTH), lambda i: (jnp.minimum((i + 1) * hb, last), 0)),
                  _tile(ts, ATTN_WIDTH), _tile(ts, 2 * KV_WIDTH), _tile(ts, ATTN_WIDTH + POOL_WIDTH),
                  _full((D_MODEL, EVEN_IN)), _tile(ts, D_MODEL), _full((1, D_MODEL)), _tile(ts, D_MODEL)],
        out_specs=[_tile(ts, EVEN_IN), _tile(ts, D_MODEL), _full((SUBLANES, D_MODEL))],
        out_shape=[jax.ShapeDtypeStruct((seq, EVEN_IN), BF16), jax.ShapeDtypeStruct((seq, D_MODEL), F32),
                   jax.ShapeDtypeStruct((SUBLANES, D_MODEL), F32)],
        scratch_shapes=[pltpu.VMEM((ts + POOL_HALO, POOL_WIDTH), F32)],
        compiler_params=_params(),
    )(dpooled, dpooled, dq, dkv, dgg, w_in, x, g_pre, dx1)


def _matmul_tn(a, b, name, ts, tn):
    seq, m = a.shape
    n = b.shape[1]
    steps = seq // ts

    def body(a_ref, b_ref, o_ref):
        @pl.when(pl.program_id(1) == 0)
        def _():
            o_ref[...] = jnp.zeros_like(o_ref)

        o_ref[...] += _dot_tn(a_ref[...], b_ref[...])

    return pl.pallas_call(
        body, name=name, grid=(n // tn, steps),
        in_specs=[pl.BlockSpec((ts, m), lambda j, s: (s, 0)), pl.BlockSpec((ts, tn), lambda j, s: (s, j))],
        out_specs=pl.BlockSpec((m, tn), lambda j, s: (0, j)),
        out_shape=jax.ShapeDtypeStruct((m, n), F32),
        compiler_params=pltpu.CompilerParams(dimension_semantics=("arbitrary", "arbitrary"), vmem_limit_bytes=VMEM_LIMIT),
    )(a, b)


def _adamw_math(w, g, m, v):
    m = ADAM_B1 * m + (1.0 - ADAM_B1) * g
    v = ADAM_B2 * v + (1.0 - ADAM_B2) * (g * g)
    m_hat = m / (1.0 - ADAM_B1 ** ADAM_STEP)
    v_hat = v / (1.0 - ADAM_B2 ** ADAM_STEP)
    delta = -ADAM_LR * (m_hat / (jnp.sqrt(v_hat) + ADAM_EPS) + ADAM_WD * w)
    return delta, m, v


def _adamw(ws, gs, ms, vs, name):
    n = len(ws)

    def body(*refs):
        ins, outs = refs[:4 * n], refs[4 * n:]
        for k in range(n):
            delta, m, v = _adamw_math(ins[k][...], ins[n + k][...], ins[2 * n + k][...], ins[3 * n + k][...])
            outs[k][...] = delta
            outs[n + k][...] = m
            outs[2 * n + k][...] = v

    shapes = [jax.ShapeDtypeStruct(w.shape, F32) for w in ws]
    out = pl.pallas_call(body, name=name, out_shape=shapes * 3,
                         compiler_params=pltpu.CompilerParams(vmem_limit_bytes=VMEM_LIMIT))(*ws, *gs, *ms, *vs)
    return out[:n], out[n:2 * n], out[2 * n:]


def _pack_rows(mats):
    return jnp.concatenate([m.reshape(-1, LANES) for m in mats], axis=0)


def kernel(x, pre_norm, post_norm, a_w_in, a_sinks, b_pool_w, b_pool_scale, ab_w_out, c_w_in, c_dw_w, c_dw_b, c_ln_g, c_ln_b, c_w_out, loss_target, m_pre_norm, m_post_norm, m_a_w_in, m_a_sinks, m_b_pool_w, m_b_pool_scale, m_ab_w_out, m_c_w_in, m_c_dw_w, m_c_dw_b, m_c_ln_g, m_c_ln_b, m_c_w_out, v_pre_norm, v_post_norm, v_a_w_in, v_a_sinks, v_b_pool_w, v_b_pool_scale, v_ab_w_out, v_c_w_in, v_c_dw_w, v_c_dw_b, v_c_ln_g, v_c_ln_b, v_c_w_out):
    seq = x.shape[1]
    ts_big = min(512, seq)
    ts_conv = min(256, seq)
    x2d = x[0]
    target = loss_target[0]
    in0_cols = a_w_in.shape[2]
    in1_cols = c_w_in.shape[2]
    out_rows = ab_w_out.shape[1]
    ch = c_dw_b.shape[1]

    shard = _pack_rows([a_w_in[0], ab_w_out[0], c_w_in[0], c_w_out[0]]).astype(BF16)
    n0 = D_MODEL * in0_cols // LANES
    n1 = out_rows * D_MODEL // LANES
    n2 = D_MODEL * in1_cols // LANES
    wg = _all_gather(shard, "gather_weights")
    w_in0 = wg[:, :n0].reshape(N_DEV, D_MODEL, in0_cols).transpose(1, 0, 2).reshape(D_MODEL, EVEN_IN)
    w_out0 = wg[:, n0:n0 + n1].reshape(D_MODEL, D_MODEL)
    w_in1 = wg[:, n0 + n1:n0 + n1 + n2].reshape(N_DEV, D_MODEL, in1_cols).transpose(1, 0, 2).reshape(D_MODEL, ODD_IN)
    w_out1 = wg[:, n0 + n1 + n2:].reshape(D_MODEL, D_MODEL)

    vec_rows = 40
    vecs = jnp.concatenate([c_dw_w[0, :, 0, :], c_dw_b, c_ln_g, c_ln_b, jnp.zeros((vec_rows - CONV_K - 3, ch), F32)], axis=0)
    vg = _all_gather(vecs, "gather_vectors").transpose(1, 0, 2).reshape(vec_rows, D_MODEL)
    dw_w = vg[:CONV_HALO]
    dw_b, ln_g, ln_b = vg[CONV_K:CONV_K + 1], vg[CONV_K + 1:CONV_K + 2], vg[CONV_K + 2:CONV_K + 3]

    sinks = a_sinks[0]
    h0, q, kv, gug = _fwd_in0(x2d, pre_norm[0:1], w_in0, ts_big)
    o = _attn_fwd(q, kv, sinks)
    mix0, pooled, y0, x1 = _fwd_out0(gug, o, b_pool_w[0], b_pool_scale, w_out0, post_norm[0:1], x2d, ts_big)
    h1, proj1, glu = _fwd_in1(x1, pre_norm[1:2], w_in1, ts_big)
    ymix1, dy1, dx2, dcf, dgate, acc1 = _fwd_out1(glu, proj1, dw_w, dw_b, ln_g, ln_b, w_out1, post_norm[1:2], x1, target, ts_conv)

    dproj1, dx1, ddw_w, dpre1 = _bwd_in1(dcf, glu, proj1, dgate, dw_w, w_in1, x1, pre_norm[1:2], dx2, ts_conv)
    g_w_in1 = _matmul_tn(h1, dproj1, "grad_w_in1", min(1024, seq), 1024)
    g_w_out1 = _matmul_tn(ymix1, dy1, "grad_w_out1", min(1024, seq), 1024)
    dy0, do, dgg, dpooled, dpost0, dscale, dpool_w = _bwd_out0(dx1, y0, post_norm[0:1], w_out0, gug, o, pooled, b_pool_w[0], b_pool_scale, ts_big)
    dq, dkv, dsink = _attn_bwd(q, kv, do, sinks)
    dproj0, grad_x, dpre0 = _bwd_in0(dpooled, dq, dkv, dgg, w_in0, x2d, pre_norm[0:1], dx1, ts_big)
    g_w_in0 = _matmul_tn(h0, dproj0, "grad_w_in0", min(1024, seq), 768)
    g_w_out0 = _matmul_tn(mix0, dy0, "grad_w_out0", min(1024, seq), 1024)

    loss = 0.5 / D_MODEL * lax.psum(jnp.sum(acc1[ACC_LOSS]), AXES)

    parts = jnp.concatenate([
        g_w_in0.reshape(D_MODEL, N_DEV, in0_cols).transpose(1, 0, 2).reshape(N_DEV, n0, LANES),
        g_w_out0.reshape(N_DEV, n1, LANES),
        g_w_in1.reshape(D_MODEL, N_DEV, in1_cols).transpose(1, 0, 2).reshape(N_DEV, n2, LANES),
        g_w_out1.reshape(N_DEV, n1, LANES)], axis=1)
    shard_rows = parts.shape[1]
    by_core = parts.reshape(4, 2, shard_rows, LANES).transpose(1, 0, 2, 3).reshape(2, 4 * shard_rows, LANES)
    by_chip = _exchange_sum(by_core, "c", "reduce_cores").reshape(4, shard_rows, LANES)
    gsh = _exchange_sum(by_chip, "xy", "reduce_chips")
    g_a_w_in = gsh[:n0].reshape(D_MODEL, in0_cols)
    g_ab_w_out = gsh[n0:n0 + n1].reshape(out_rows, D_MODEL)
    g_c_w_in = gsh[n0 + n1:n0 + n1 + n2].reshape(D_MODEL, in1_cols)
    g_c_w_out = gsh[n0 + n1 + n2:].reshape(out_rows, D_MODEL)

    vec_g = jnp.concatenate([ddw_w[:CONV_K], jnp.sum(acc1[ACC_DW_B], axis=0, keepdims=True),
                             jnp.sum(acc1[ACC_LN_G], axis=0, keepdims=True), jnp.sum(acc1[ACC_LN_B], axis=0, keepdims=True),
                             jnp.zeros((vec_rows - CONV_K - 3, D_MODEL), F32)], axis=0)
    vec_g = _exchange_sum(vec_g.reshape(vec_rows, N_DEV, ch).transpose(1, 0, 2), "all", "reduce_vectors")
    g_dw_w, g_dw_b, g_ln_g, g_ln_b = vec_g[:CONV_K], vec_g[CONV_K:CONV_K + 1], vec_g[CONV_K + 1:CONV_K + 2], vec_g[CONV_K + 2:CONV_K + 3]

    rep = jnp.concatenate([
        jnp.sum(dpre0, axis=0, keepdims=True).reshape(SUBLANES, LANES), jnp.sum(dpre1, axis=0, keepdims=True).reshape(SUBLANES, LANES),
        jnp.sum(dpost0, axis=0, keepdims=True).reshape(SUBLANES, LANES), jnp.sum(acc1[ACC_POST], axis=0, keepdims=True).reshape(SUBLANES, LANES),
        jnp.sum(dscale, axis=0, keepdims=True).reshape(4, LANES), jnp.zeros((3, LANES), F32),
        jnp.sum(dsink, axis=0, keepdims=True), dpool_w.reshape(4 * POOL_GC, LANES)], axis=0)
    rep = _all_reduce(rep, "reduce_replicated")
    g_pre = rep[:16].reshape(2, D_MODEL)
    g_post = rep[16:32].reshape(2, D_MODEL)
    g_scale = rep[32:36].reshape(1, POOL_WIDTH)
    g_sinks = rep[39:40, :Q_HEADS]
    g_pool_w = rep[40:].reshape(1, 4, POOL_GC, POOL_GC)

    grads = [g_pre, g_post, g_a_w_in, g_sinks, g_pool_w.reshape(4 * POOL_GC, POOL_GC), g_scale, g_ab_w_out, g_c_w_in,
             g_dw_w, g_dw_b, g_ln_g, g_ln_b, g_c_w_out]
    weights = [pre_norm, post_norm, a_w_in, a_sinks, b_pool_w, b_pool_scale, ab_w_out, c_w_in, c_dw_w, c_dw_b, c_ln_g, c_ln_b, c_w_out]
    m_in = [m_pre_norm, m_post_norm, m_a_w_in, m_a_sinks, m_b_pool_w, m_b_pool_scale, m_ab_w_out, m_c_w_in, m_c_dw_w, m_c_dw_b, m_c_ln_g, m_c_ln_b, m_c_w_out]
    v_in = [v_pre_norm, v_post_norm, v_a_w_in, v_a_sinks, v_b_pool_w, v_b_pool_scale, v_ab_w_out, v_c_w_in, v_c_dw_w, v_c_dw_b, v_c_ln_g, v_c_ln_b, v_c_w_out]
    flat = lambda arrs: [a.reshape(g.shape) for a, g in zip(arrs, grads)]
    big = (2, 6, 7, 12)
    small = tuple(k for k in range(len(grads)) if k not in big)
    pick = lambda arrs, idx: [arrs[k] for k in idx]
    deltas, new_m, new_v = [None] * 13, [None] * 13, [None] * 13
    for idx, name in ((big, "adamw_matrices"), (small, "adamw_vectors")):
        d, m, v = _adamw(pick(flat(weights), idx), pick(grads, idx), pick(flat(m_in), idx), pick(flat(v_in), idx), name)
        for k, dk, mk, vk in zip(idx, d, m, v):
            deltas[k], new_m[k], new_v[k] = dk, mk, vk
    shaped = lambda arrs: [a.reshape(w.shape) for a, w in zip(arrs, weights)]
    return (loss, grad_x[None], *shaped(grads), *shaped(deltas), *shaped(new_m), *shaped(new_v))
```

```python
import functools

import jax
import jax.numpy as jnp
from jax import lax
from jax.experimental import pallas as pl
from jax.experimental.pallas import tpu as pltpu

F32 = jnp.float32
BF16 = jnp.bfloat16
MESH = pl.DeviceIdType.MESH
AXES = ("x", "y", "c")
N_DEV = 8

D_MODEL = 1024
HEAD_DIM = 64
Q_HEADS = 8
GROUP = 4
ATTN_WIDTH = 512
KV_WIDTH = 128
BLOCK = 128
POOL_WIDTH = 512
POOL_WINDOWS = (2, 4, 8, 16)
POOL_GC = 128
POOL_HALO = 16
EVEN_IN = 2304
CONV_K = 31
CONV_HALO = 32
ODD_IN = 3072
EPS = 1e-6
NEG = -1e30
SLOPES = tuple(2.0 ** (-8.0 * (h + 1) / Q_HEADS) for h in range(Q_HEADS))

ADAM_LR = 0.001
ADAM_B1 = 0.9
ADAM_B2 = 0.999
ADAM_EPS = 1e-08
ADAM_WD = 0.01
ADAM_STEP = 10

SUBLANES = 8
LANES = 128
VMEM_LIMIT = 56 * 1024 * 1024

NT = (((1,), (1,)), ((), ()))
TN = (((0,), (0,)), ((), ()))


def _params(**kw):
    return pltpu.CompilerParams(dimension_semantics=("arbitrary",), vmem_limit_bytes=VMEM_LIMIT, **kw)


def _dot(a, b):
    return jnp.dot(a, b, preferred_element_type=F32)


def _dot_nt(a, b):
    return lax.dot_general(a, b, NT, preferred_element_type=F32)


def _dot_tn(a, b):
    return lax.dot_general(a, b, TN, preferred_element_type=F32)


def _sigmoid(v):
    return 1.0 / (1.0 + jnp.exp(-v))


def _rows8(v):
    r, c = v.shape
    return jnp.sum(v.reshape(r // SUBLANES, SUBLANES, c), axis=0)


def _rms_fwd(v, g):
    r = lax.rsqrt(jnp.mean(v * v, axis=-1, keepdims=True) + EPS)
    return v * r * g, r


def _rms_bwd(v, r, g, dout):
    gd = dout * g
    dv = r * gd - v * (r * r * r) * jnp.mean(v * gd, axis=-1, keepdims=True)
    return dv, dout * (v * r)


def _full(shape):
    return pl.BlockSpec(shape, lambda i: (0,) * len(shape))


def _tile(ts, cols, col_block=0):
    return pl.BlockSpec((ts, cols), lambda i: (i, col_block))


def _position():
    return lax.axis_index("x"), lax.axis_index("y"), lax.axis_index("c")


def _all_gather(blocks, name):
    n = len(blocks)

    def body(*refs):
        x_refs, out_refs = refs[:n], refs[n:2 * n]
        send_sems, recv_sems, local_sems = refs[2 * n:]
        x, y, c = _position()
        me, sibling = (x, y, c), (x, y, 1 - c)
        chips = [(1 - x, y), (x, 1 - y), (1 - x, 1 - y)]

        def copy(k, j, owner, to, src=None):
            slab = out_refs[k].at[4 * owner[0] + 2 * owner[1] + owner[2]]
            return pltpu.make_async_remote_copy(
                src_ref=slab if src is None else src, dst_ref=slab,
                send_sem=send_sems.at[7 * k + j], recv_sem=recv_sems.at[7 * k + j], device_id=to, device_id_type=MESH)

        mine, first, passed = [], [], []
        for k in range(n):
            mine.append(pltpu.make_async_copy(x_refs[k], out_refs[k].at[4 * x + 2 * y + c], local_sems.at[k]))
            mine[-1].start()
            first.append(copy(k, 0, me, sibling, src=x_refs[k]))
            first += [copy(k, 1 + j, me, (*chip, c), src=x_refs[k]) for j, chip in enumerate(chips)]
        for cp in first:
            cp.start()
        for k in range(n):
            for j, chip in enumerate(chips):
                copy(k, 1 + j, (*chip, c), me).wait_recv()
                passed.append(copy(k, 4 + j, (*chip, c), sibling))
                passed[-1].start()
        for k in range(n):
            copy(k, 0, sibling, me).wait_recv()
            for j, chip in enumerate(chips):
                copy(k, 4 + j, (*chip, 1 - c), me).wait_recv()
        for cp in first + passed:
            cp.wait_send()
        for cp in mine:
            cp.wait()

    return pl.pallas_call(
        body, name=name,
        out_shape=[jax.ShapeDtypeStruct((N_DEV, *b.shape), b.dtype) for b in blocks],
        in_specs=[pl.BlockSpec(memory_space=pltpu.VMEM)] * n,
        out_specs=[pl.BlockSpec(memory_space=pltpu.VMEM)] * n,
        scratch_shapes=[pltpu.SemaphoreType.DMA((7 * n,)), pltpu.SemaphoreType.DMA((7 * n,)), pltpu.SemaphoreType.DMA((n,))],
        compiler_params=pltpu.CompilerParams(vmem_limit_bytes=VMEM_LIMIT),
    )(*blocks)


def _all_reduce(block, name):
    gathered, = _all_gather([block], name + "_gather")
    rows, cols = block.shape

    def body(g_ref, out_ref):
        acc = g_ref[0]
        for d in range(1, N_DEV):
            acc = acc + g_ref[d]
        out_ref[...] = acc

    return pl.pallas_call(body, name=name + "_sum", out_shape=jax.ShapeDtypeStruct((rows, cols), F32))(gathered)


def _exchange_sum(parts, name):
    _, rows, cols = parts.shape

    def body(parts_ref, out_ref, recv_ref, send_sems, recv_sems, local_sem):
        x, y, c = _position()
        me = 4 * x + 2 * y + c
        mine = pltpu.make_async_copy(parts_ref.at[me], out_ref, local_sem)
        mine.start()
        copies = []
        for k in range(1, N_DEV):
            to = (x ^ (k >> 2), y ^ ((k >> 1) & 1), c ^ (k & 1))
            cp = pltpu.make_async_remote_copy(
                src_ref=parts_ref.at[me ^ k], dst_ref=recv_ref.at[k - 1],
                send_sem=send_sems.at[k - 1], recv_sem=recv_sems.at[k - 1], device_id=to, device_id_type=MESH)
            cp.start()
            copies.append(cp)
        mine.wait()
        for k, cp in enumerate(copies):
            cp.wait_recv()
            out_ref[...] = out_ref[...] + recv_ref[k]
        for cp in copies:
            cp.wait_send()

    return pl.pallas_call(
        body, name=name,
        out_shape=jax.ShapeDtypeStruct((rows, cols), F32),
        in_specs=[pl.BlockSpec(memory_space=pl.ANY)],
        out_specs=pl.BlockSpec(memory_space=pltpu.VMEM),
        scratch_shapes=[pltpu.VMEM((N_DEV - 1, rows, cols), F32), pltpu.SemaphoreType.DMA((N_DEV - 1,)),
                        pltpu.SemaphoreType.DMA((N_DEV - 1,)), pltpu.SemaphoreType.DMA],
        compiler_params=pltpu.CompilerParams(vmem_limit_bytes=VMEM_LIMIT),
    )(parts)


N_CHIPS = 4


def _reduce_cores(parts, name):
    n = len(parts)

    def body(*refs):
        part_refs, out_refs = refs[:n], refs[n:2 * n]
        recv_refs, own_refs = refs[2 * n:3 * n], refs[3 * n:4 * n]
        send_sems, recv_sems, local_sems = refs[4 * n:]
        x, y, c = _position()
        copies, loads = [], []
        for k in range(n):
            for j in range(N_CHIPS):
                cp = pltpu.make_async_remote_copy(
                    src_ref=part_refs[k].at[2 * j + 1 - c], dst_ref=recv_refs[k].at[j],
                    send_sem=send_sems.at[N_CHIPS * k + j], recv_sem=recv_sems.at[N_CHIPS * k + j],
                    device_id=(x, y, 1 - c), device_id_type=MESH)
                cp.start()
                ld = pltpu.make_async_copy(part_refs[k].at[2 * j + c], own_refs[k].at[j], local_sems.at[N_CHIPS * k + j])
                ld.start()
                copies.append(cp)
                loads.append(ld)
        for k in range(n):
            for j in range(N_CHIPS):
                loads[N_CHIPS * k + j].wait()
                copies[N_CHIPS * k + j].wait_recv()
                out_refs[k][j] = (own_refs[k][j].astype(F32) + recv_refs[k][j].astype(F32)).astype(BF16)
        for cp in copies:
            cp.wait_send()

    shapes = [(N_CHIPS, *p.shape[1:]) for p in parts]
    return pl.pallas_call(
        body, name=name,
        out_shape=[jax.ShapeDtypeStruct(s, BF16) for s in shapes],
        in_specs=[pl.BlockSpec(memory_space=pl.ANY)] * n,
        out_specs=[pl.BlockSpec(memory_space=pltpu.VMEM)] * n,
        scratch_shapes=[pltpu.VMEM(s, BF16) for s in shapes] * 2 + [
            pltpu.SemaphoreType.DMA((N_CHIPS * n,)), pltpu.SemaphoreType.DMA((N_CHIPS * n,)),
            pltpu.SemaphoreType.DMA((N_CHIPS * n,))],
        compiler_params=pltpu.CompilerParams(vmem_limit_bytes=VMEM_LIMIT),
    )(*parts)


def _reduce_chips(parts, name):
    n = len(parts)
    peers = N_CHIPS - 1

    def body(*refs):
        part_refs, out_refs = refs[:n], refs[n:2 * n]
        recv_refs, own_refs = refs[2 * n:3 * n], refs[3 * n:4 * n]
        send_sems, recv_sems, local_sems = refs[4 * n:]
        x, y, c = _position()
        me = 2 * x + y
        copies, loads = [], []
        for k in range(n):
            ld = pltpu.make_async_copy(part_refs[k].at[me], own_refs[k], local_sems.at[k])
            ld.start()
            loads.append(ld)
            for d in range(1, N_CHIPS):
                cp = pltpu.make_async_remote_copy(
                    src_ref=part_refs[k].at[me ^ d], dst_ref=recv_refs[k].at[d - 1],
                    send_sem=send_sems.at[peers * k + d - 1], recv_sem=recv_sems.at[peers * k + d - 1],
                    device_id=(x ^ (d >> 1), y ^ (d & 1), c), device_id_type=MESH)
                cp.start()
                copies.append(cp)
        for k in range(n):
            loads[k].wait()
            acc = own_refs[k][...].astype(F32)
            for d in range(peers):
                copies[peers * k + d].wait_recv()
                acc = acc + recv_refs[k][d].astype(F32)
            out_refs[k][...] = acc
        for cp in copies:
            cp.wait_send()

    return pl.pallas_call(
        body, name=name,
        out_shape=[jax.ShapeDtypeStruct(p.shape[1:], F32) for p in parts],
        in_specs=[pl.BlockSpec(memory_space=pl.ANY)] * n,
        out_specs=[pl.BlockSpec(memory_space=pltpu.VMEM)] * n,
        scratch_shapes=[pltpu.VMEM((peers, *p.shape[1:]), BF16) for p in parts] + [pltpu.VMEM(p.shape[1:], BF16) for p in parts] + [
            pltpu.SemaphoreType.DMA((peers * n,)), pltpu.SemaphoreType.DMA((peers * n,)), pltpu.SemaphoreType.DMA((n,))],
        compiler_params=pltpu.CompilerParams(vmem_limit_bytes=VMEM_LIMIT),
    )(*parts)


def _fwd_in0(x, g, w, ts):
    seq = x.shape[0]

    def body(x_ref, g_ref, w_ref, h_ref, q_ref, kv_ref, gug_ref):
        h, _ = _rms_fwd(x_ref[...], g_ref[...])
        h = h.astype(BF16)
        h_ref[...] = h
        proj = _dot_nt(h, w_ref[...])
        q_ref[...] = proj[:, :ATTN_WIDTH].astype(BF16)
        kv_ref[...] = proj[:, ATTN_WIDTH:ATTN_WIDTH + 2 * KV_WIDTH].astype(BF16)
        gug_ref[...] = proj[:, ATTN_WIDTH + 2 * KV_WIDTH:]

    return pl.pallas_call(
        body, name="fwd_in0", grid=(seq // ts,),
        in_specs=[_tile(ts, D_MODEL), _full((1, D_MODEL)), _full((EVEN_IN, D_MODEL))],
        out_specs=[_tile(ts, D_MODEL), _tile(ts, ATTN_WIDTH), _tile(ts, 2 * KV_WIDTH), _tile(ts, 3 * POOL_WIDTH)],
        out_shape=[jax.ShapeDtypeStruct((seq, D_MODEL), BF16), jax.ShapeDtypeStruct((seq, ATTN_WIDTH), BF16),
                   jax.ShapeDtypeStruct((seq, 2 * KV_WIDTH), BF16), jax.ShapeDtypeStruct((seq, 3 * POOL_WIDTH), F32)],
        compiler_params=_params(),
    )(x, g, w)


def _attn_mask(first_block):
    row = lax.broadcasted_iota(jnp.int32, (BLOCK, 2 * BLOCK), 0)
    col = lax.broadcasted_iota(jnp.int32, (BLOCK, 2 * BLOCK), 1)
    dist = row + BLOCK - col
    valid = (dist >= 0) & (dist < BLOCK) & ((col >= BLOCK) | jnp.logical_not(first_block))
    return valid, dist.astype(F32)


def _attn_probs(qh, kh, sink, slope, valid, distf):
    s = _dot_nt(qh, kh) * (HEAD_DIM ** -0.5)
    s = jnp.where(valid, s - slope * distf, NEG)
    mx = jnp.maximum(jnp.max(s, axis=-1, keepdims=True), sink)
    e = jnp.exp(s - mx)
    es = jnp.exp(sink - mx)
    den = jnp.sum(e, axis=-1, keepdims=True) + es
    return e / den, es / den


def _attn_fwd(q, kv, sinks):
    seq = q.shape[0]
    nb = seq // BLOCK

    def body(sink_ref, q_ref, kvc_ref, kvp_ref, o_ref):
        n = pl.program_id(0)
        valid, distf = _attn_mask(n == 0)
        qb = q_ref[...]
        kk = jnp.concatenate([kvp_ref[...], kvc_ref[...]], axis=0)
        for h in range(Q_HEADS):
            kvh = h // GROUP
            qh = qb[:, HEAD_DIM * h:HEAD_DIM * (h + 1)]
            kh = kk[:, HEAD_DIM * kvh:HEAD_DIM * (kvh + 1)]
            vh = kk[:, KV_WIDTH + HEAD_DIM * kvh:KV_WIDTH + HEAD_DIM * (kvh + 1)]
            p, _ = _attn_probs(qh, kh, sink_ref[h], SLOPES[h], valid, distf)
            o_ref[:, HEAD_DIM * h:HEAD_DIM * (h + 1)] = _dot(p.astype(BF16), vh)

    return pl.pallas_call(
        body, name="attn_fwd", grid=(nb,),
        in_specs=[pl.BlockSpec(memory_space=pltpu.SMEM),
                  pl.BlockSpec((BLOCK, ATTN_WIDTH), lambda n: (n, 0)),
                  pl.BlockSpec((BLOCK, 2 * KV_WIDTH), lambda n: (n, 0)),
                  pl.BlockSpec((BLOCK, 2 * KV_WIDTH), lambda n: (jnp.maximum(n - 1, 0), 0))],
        out_specs=pl.BlockSpec((BLOCK, ATTN_WIDTH), lambda n: (n, 0)),
        out_shape=jax.ShapeDtypeStruct((seq, ATTN_WIDTH), F32),
        compiler_params=_params(),
    )(sinks, q, kv, kv)


def _pool_counts(first_row, rows, window):
    t = first_row + lax.broadcasted_iota(jnp.int32, (rows, 1), 0)
    return jnp.minimum(t + 1, window).astype(F32)


def _fwd_out0(gug, o, pool_w, pool_scale, w_out, g_post, x, ts):
    seq = x.shape[0]
    hb = ts // POOL_HALO

    def body(gug_ref, halo_ref, o_ref, pw_ref, ps_ref, w_ref, g_ref, x_ref, mix_ref, pooled_ref, y_ref, x1_ref, ubuf):
        i = pl.program_id(0)
        ga = gug_ref[:, :ATTN_WIDTH]
        u = gug_ref[:, ATTN_WIDTH:ATTN_WIDTH + POOL_WIDTH]
        gb = gug_ref[:, ATTN_WIDTH + POOL_WIDTH:]
        mix_ref[:, :ATTN_WIDTH] = (o_ref[...] * (ga * _sigmoid(ga))).astype(BF16)
        ubuf[:POOL_HALO, :] = jnp.where(i > 0, halo_ref[...], 0.0)
        ubuf[POOL_HALO:, :] = u
        silu_gb = gb * _sigmoid(gb)
        for g, window in enumerate(POOL_WINDOWS):
            lanes = slice(POOL_GC * g, POOL_GC * (g + 1))
            acc = ubuf[pl.ds(POOL_HALO, ts), lanes]
            for k in range(1, window):
                acc = acc + ubuf[pl.ds(POOL_HALO - k, ts), lanes]
            pooled = (acc / _pool_counts(i * ts, ts, window) - u[:, lanes]).astype(BF16)
            pooled_ref[:, lanes] = pooled
            ypool = _dot(pooled, pw_ref[g].astype(BF16)) * ps_ref[:, lanes]
            mix_ref[:, ATTN_WIDTH + POOL_GC * g:ATTN_WIDTH + POOL_GC * (g + 1)] = (ypool * silu_gb[:, lanes]).astype(BF16)
        y = _dot(mix_ref[...], w_ref[...])
        y_ref[...] = y
        yn, _ = _rms_fwd(y, g_ref[...])
        x1_ref[...] = x_ref[...] + yn

    return pl.pallas_call(
        body, name="fwd_out0", grid=(seq // ts,),
        in_specs=[_tile(ts, 3 * POOL_WIDTH),
                  pl.BlockSpec((POOL_HALO, POOL_WIDTH), lambda i: (jnp.maximum(i * hb - 1, 0), 1)),
                  _tile(ts, ATTN_WIDTH), _full((4, POOL_GC, POOL_GC)), _full((1, POOL_WIDTH)),
                  _full((D_MODEL, D_MODEL)), _full((1, D_MODEL)), _tile(ts, D_MODEL)],
        out_specs=[_tile(ts, D_MODEL), _tile(ts, POOL_WIDTH), _tile(ts, D_MODEL), _tile(ts, D_MODEL)],
        out_shape=[jax.ShapeDtypeStruct((seq, D_MODEL), BF16), jax.ShapeDtypeStruct((seq, POOL_WIDTH), BF16),
                   jax.ShapeDtypeStruct((seq, D_MODEL), F32), jax.ShapeDtypeStruct((seq, D_MODEL), F32)],
        scratch_shapes=[pltpu.VMEM((ts + POOL_HALO, POOL_WIDTH), F32)],
        compiler_params=_params(),
    )(gug, gug, o, pool_w, pool_scale, w_out, g_post, x)


def _fwd_in1(x1, g, w, ts):
    seq = x1.shape[0]

    def body(x_ref, g_ref, w_ref, h_ref, proj_ref, glu_ref):
        h, _ = _rms_fwd(x_ref[...], g_ref[...])
        h = h.astype(BF16)
        h_ref[...] = h
        proj = _dot_nt(h, w_ref[...])
        proj_ref[...] = proj
        glu_ref[...] = proj[:, :D_MODEL] * _sigmoid(proj[:, D_MODEL:2 * D_MODEL])

    return pl.pallas_call(
        body, name="fwd_in1", grid=(seq // ts,),
        in_specs=[_tile(ts, D_MODEL), _full((1, D_MODEL)), _full((ODD_IN, D_MODEL))],
        out_specs=[_tile(ts, D_MODEL), _tile(ts, ODD_IN), _tile(ts, D_MODEL)],
        out_shape=[jax.ShapeDtypeStruct((seq, D_MODEL), BF16), jax.ShapeDtypeStruct((seq, ODD_IN), F32),
                   jax.ShapeDtypeStruct((seq, D_MODEL), F32)],
        compiler_params=_params(),
    )(x1, g, w)


ACC_LOSS, ACC_POST, ACC_LN_G, ACC_LN_B, ACC_DW_B = range(5)
CONV_FIRST = CONV_HALO - CONV_K + 1


def _fwd_tap(offset):
    return offset - CONV_FIRST if CONV_FIRST <= offset <= CONV_HALO else None


def _bwd_tap(offset):
    return CONV_K - 1 - offset if offset < CONV_K else None


def _conv_taps(w_ref, buf_ref, ts, lanes, tap_of_offset):
    out = None
    for b in range(SUBLANES):
        rows = ts if b == 0 else ts + SUBLANES
        part = None
        for a in range(CONV_HALO // SUBLANES + 1):
            k = tap_of_offset(SUBLANES * a + b)
            if k is None:
                continue
            term = w_ref[k:k + 1, lanes] * buf_ref[pl.ds(SUBLANES * a, rows), lanes]
            part = term if part is None else part + term
        if part is None:
            continue
        if b:
            part = part[b:b + ts, :]
        out = part if out is None else out + part
    return out


def _fwd_out1(glu, proj, dw_w, dw_b, ln_g, ln_b, w_out, g_post, x1, target, ts):
    seq = x1.shape[0]
    hb = ts // CONV_HALO

    def body(glu_ref, halo_ref, gate_ref, dww_ref, dwb_ref, lng_ref, lnb_ref, w_ref, g_ref, x1_ref, t_ref,
             ymix_ref, dy_ref, dx2_ref, dcf_ref, dgate_ref, acc_ref, gbuf):
        i = pl.program_id(0)

        @pl.when(i == 0)
        def _():
            acc_ref[...] = jnp.zeros_like(acc_ref)

        gbuf[:CONV_HALO, :] = jnp.where(i > 0, halo_ref[...], 0.0)
        gbuf[CONV_HALO:, :] = glu_ref[...]
        for lb in range(D_MODEL // LANES):
            lanes = slice(LANES * lb, LANES * (lb + 1))
            dcf_ref[:, lanes] = _conv_taps(dww_ref, gbuf, ts, lanes, _fwd_tap)
        cf = dcf_ref[...] + dwb_ref[...]
        mu = jnp.mean(cf, axis=-1, keepdims=True)
        cen = cf - mu
        rs = lax.rsqrt(jnp.mean(cen * cen, axis=-1, keepdims=True) + EPS)
        xhat = cen * rs
        cn = xhat * lng_ref[...] + lnb_ref[...]
        gate = gate_ref[...]
        sg = _sigmoid(gate)
        sc = _sigmoid(cn)
        silu_gate = gate * sg
        silu_cn = cn * sc
        ymix = (silu_cn * silu_gate).astype(BF16)
        ymix_ref[...] = ymix
        y = _dot(ymix, w_ref[...])
        yn, r = _rms_fwd(y, g_ref[...])
        err = (x1_ref[...] + yn) - t_ref[...]
        acc_ref[ACC_LOSS] += _rows8(err * err)
        dx2 = err * (1.0 / D_MODEL)
        dx2_ref[...] = dx2
        dy, dpost = _rms_bwd(y, r, g_ref[...], dx2)
        acc_ref[ACC_POST] += _rows8(dpost)
        dy = dy.astype(BF16)
        dy_ref[...] = dy
        dymix = _dot_nt(dy, w_ref[...])
        dgate_ref[...] = (dymix * silu_cn * (sg * (1.0 + gate * (1.0 - sg)))).astype(BF16)
        dcn = dymix * silu_gate * (sc * (1.0 + cn * (1.0 - sc)))
        acc_ref[ACC_LN_G] += _rows8(dcn * xhat)
        acc_ref[ACC_LN_B] += _rows8(dcn)
        dxhat = dcn * lng_ref[...]
        dcf = rs * (dxhat - jnp.mean(dxhat, axis=-1, keepdims=True)
                    - xhat * jnp.mean(dxhat * xhat, axis=-1, keepdims=True))
        acc_ref[ACC_DW_B] += _rows8(dcf)
        dcf_ref[...] = dcf

    return pl.pallas_call(
        body, name="fwd_out1", grid=(seq // ts,),
        in_specs=[_tile(ts, D_MODEL),
                  pl.BlockSpec((CONV_HALO, D_MODEL), lambda i: (jnp.maximum(i * hb - 1, 0), 0)),
                  _tile(ts, D_MODEL, 2), _full((CONV_HALO, D_MODEL)), _full((1, D_MODEL)), _full((1, D_MODEL)),
                  _full((1, D_MODEL)), _full((D_MODEL, D_MODEL)), _full((1, D_MODEL)), _tile(ts, D_MODEL),
                  _tile(ts, D_MODEL)],
        out_specs=[_tile(ts, D_MODEL), _tile(ts, D_MODEL), _tile(ts, D_MODEL), _tile(ts, D_MODEL), _tile(ts, D_MODEL),
                   _full((5, SUBLANES, D_MODEL))],
        out_shape=[jax.ShapeDtypeStruct((seq, D_MODEL), BF16), jax.ShapeDtypeStruct((seq, D_MODEL), BF16),
                   jax.ShapeDtypeStruct((seq, D_MODEL), F32), jax.ShapeDtypeStruct((seq, D_MODEL), F32),
                   jax.ShapeDtypeStruct((seq, D_MODEL), BF16), jax.ShapeDtypeStruct((5, SUBLANES, D_MODEL), F32)],
        scratch_shapes=[pltpu.VMEM((ts + CONV_HALO, D_MODEL), F32)],
        compiler_params=_params(),
    )(glu, glu, proj, dw_w, dw_b, ln_g, ln_b, w_out, g_post, x1, target)


def _bwd_in1(dcf, glu, proj, dgate, dw_w, w_in, x1, g_pre, dx2, ts):
    seq = x1.shape[0]
    hb = ts // CONV_HALO
    last = seq // CONV_HALO - 1
    nt = seq // ts

    def body(dcf_ref, dnext_ref, glu_ref, gprev_ref, ab_ref, dgate_ref, dww_ref, w_ref, x_ref, g_ref, dx2_ref,
             dproj_ref, dx1_ref, ddw_ref, dpre_ref, dbuf, gbuf, zbuf):
        i = pl.program_id(0)

        @pl.when(i == 0)
        def _():
            ddw_ref[...] = jnp.zeros_like(ddw_ref)
            dpre_ref[...] = jnp.zeros_like(dpre_ref)

        dcf = dcf_ref[...]
        dbuf[:ts, :] = dcf
        dbuf[ts:, :] = jnp.where(i < nt - 1, dnext_ref[...], 0.0)
        gbuf[:CONV_HALO, :] = jnp.where(i > 0, gprev_ref[...], 0.0)
        gbuf[CONV_HALO:, :] = glu_ref[...]
        zbuf[:SUBLANES, :] = jnp.zeros((SUBLANES, D_MODEL), F32)
        zbuf[pl.ds(SUBLANES, ts), :] = dcf
        zbuf[pl.ds(SUBLANES + ts, SUBLANES), :] = jnp.zeros((SUBLANES, D_MODEL), F32)
        for lb in range(D_MODEL // LANES):
            lanes = slice(LANES * lb, LANES * (lb + 1))
            gate_lanes = slice(D_MODEL + LANES * lb, D_MODEL + LANES * (lb + 1))
            dglu = _conv_taps(dww_ref, dbuf, ts, lanes, _bwd_tap)
            a = ab_ref[:, lanes]
            sb = _sigmoid(ab_ref[:, gate_lanes])
            dproj_ref[:, lanes] = (dglu * sb).astype(BF16)
            dproj_ref[:, gate_lanes] = (dglu * a * sb * (1.0 - sb)).astype(BF16)
            for b in range(SUBLANES):
                rows = ts if b == 0 else ts + SUBLANES
                shifted = zbuf[pl.ds(SUBLANES - b, rows), lanes]
                for a8 in range(CONV_HALO // SUBLANES + 1):
                    k = _fwd_tap(SUBLANES * a8 + b)
                    if k is not None:
                        ddw_ref[k, :, lanes] += _rows8(shifted * gbuf[pl.ds(SUBLANES * a8, rows), lanes])
        dproj_ref[:, 2 * D_MODEL:] = dgate_ref[...]
        dh = _dot(dproj_ref[...], w_ref[...])
        x = x_ref[...]
        r = lax.rsqrt(jnp.mean(x * x, axis=-1, keepdims=True) + EPS)
        dx, dpre = _rms_bwd(x, r, g_ref[...], dh)
        dx1_ref[...] = dx2_ref[...] + dx
        dpre_ref[...] += _rows8(dpre)

    return pl.pallas_call(
        body, name="bwd_in1", grid=(nt,),
        in_specs=[_tile(ts, D_MODEL),
                  pl.BlockSpec((CONV_HALO, D_MODEL), lambda i: (jnp.minimum((i + 1) * hb, last), 0)),
                  _tile(ts, D_MODEL),
                  pl.BlockSpec((CONV_HALO, D_MODEL), lambda i: (jnp.maximum(i * hb - 1, 0), 0)),
                  _tile(ts, 2 * D_MODEL), _tile(ts, D_MODEL), _full((CONV_HALO, D_MODEL)),
                  _full((ODD_IN, D_MODEL)), _tile(ts, D_MODEL), _full((1, D_MODEL)), _tile(ts, D_MODEL)],
        out_specs=[_tile(ts, ODD_IN), _tile(ts, D_MODEL), _full((CONV_HALO, SUBLANES, D_MODEL)), _full((SUBLANES, D_MODEL))],
        out_shape=[jax.ShapeDtypeStruct((seq, ODD_IN), BF16), jax.ShapeDtypeStruct((seq, D_MODEL), F32),
                   jax.ShapeDtypeStruct((CONV_HALO, SUBLANES, D_MODEL), F32), jax.ShapeDtypeStruct((SUBLANES, D_MODEL), F32)],
        scratch_shapes=[pltpu.VMEM((ts + CONV_HALO, D_MODEL), F32), pltpu.VMEM((ts + CONV_HALO, D_MODEL), F32),
                        pltpu.VMEM((ts + 2 * SUBLANES, D_MODEL), F32)],
        compiler_params=_params(),
    )(dcf, dcf, glu, glu, proj, dgate, dw_w, w_in, x1, g_pre, dx2)


def _bwd_out0(dx1, y0, g_post, w_out, gug, o, pooled, pool_w, pool_scale, ts):
    seq = dx1.shape[0]

    def body(dx1_ref, y_ref, g_ref, w_ref, gug_ref, o_ref, pooled_ref, pw_ref, ps_ref,
             dy_ref, do_ref, dgg_ref, dpooled_ref, dpost_ref, dscale_ref, dpw_ref):
        i = pl.program_id(0)

        @pl.when(i == 0)
        def _():
            dpost_ref[...] = jnp.zeros_like(dpost_ref)
            dscale_ref[...] = jnp.zeros_like(dscale_ref)
            dpw_ref[...] = jnp.zeros_like(dpw_ref)

        y = y_ref[...]
        r = lax.rsqrt(jnp.mean(y * y, axis=-1, keepdims=True) + EPS)
        dy, dpost = _rms_bwd(y, r, g_ref[...], dx1_ref[...])
        dpost_ref[...] += _rows8(dpost)
        dy = dy.astype(BF16)
        dy_ref[...] = dy
        dmix = _dot_nt(dy, w_ref[...])
        dya = dmix[:, :ATTN_WIDTH]
        dyb = dmix[:, ATTN_WIDTH:]
        ga = gug_ref[:, :ATTN_WIDTH]
        gb = gug_ref[:, ATTN_WIDTH + POOL_WIDTH:]
        sga = _sigmoid(ga)
        sgb = _sigmoid(gb)
        do_ref[...] = (dya * (ga * sga)).astype(BF16)
        dgg_ref[:, :ATTN_WIDTH] = (dya * o_ref[...] * (sga * (1.0 + ga * (1.0 - sga)))).astype(BF16)
        dypool = dyb * (gb * sgb)
        dsilu_gb = sgb * (1.0 + gb * (1.0 - sgb))
        for g in range(len(POOL_WINDOWS)):
            lanes = slice(POOL_GC * g, POOL_GC * (g + 1))
            pooled = pooled_ref[:, lanes]
            wg = pw_ref[g].astype(BF16)
            pw = _dot(pooled, wg)
            scale = ps_ref[:, lanes]
            dgg_ref[:, ATTN_WIDTH + POOL_GC * g:ATTN_WIDTH + POOL_GC * (g + 1)] = (
                dyb[:, lanes] * (pw * scale) * dsilu_gb[:, lanes]).astype(BF16)
            dscale_ref[:, lanes] += _rows8(dypool[:, lanes] * pw)
            dpw = (dypool[:, lanes] * scale).astype(BF16)
            dpooled_ref[:, lanes] = _dot_nt(dpw, wg)
            dpw_ref[g] += _dot_tn(pooled, dpw)

    return pl.pallas_call(
        body, name="bwd_out0", grid=(seq // ts,),
        in_specs=[_tile(ts, D_MODEL), _tile(ts, D_MODEL), _full((1, D_MODEL)), _full((D_MODEL, D_MODEL)),
                  _tile(ts, 3 * POOL_WIDTH), _tile(ts, ATTN_WIDTH), _tile(ts, POOL_WIDTH),
                  _full((4, POOL_GC, POOL_GC)), _full((1, POOL_WIDTH))],
        out_specs=[_tile(ts, D_MODEL), _tile(ts, ATTN_WIDTH), _tile(ts, ATTN_WIDTH + POOL_WIDTH), _tile(ts, POOL_WIDTH),
                   _full((SUBLANES, D_MODEL)), _full((SUBLANES, POOL_WIDTH)), _full((4, POOL_GC, POOL_GC))],
        out_shape=[jax.ShapeDtypeStruct((seq, D_MODEL), BF16), jax.ShapeDtypeStruct((seq, ATTN_WIDTH), BF16),
                   jax.ShapeDtypeStruct((seq, ATTN_WIDTH + POOL_WIDTH), BF16), jax.ShapeDtypeStruct((seq, POOL_WIDTH), F32),
                   jax.ShapeDtypeStruct((SUBLANES, D_MODEL), F32), jax.ShapeDtypeStruct((SUBLANES, POOL_WIDTH), F32),
                   jax.ShapeDtypeStruct((4, POOL_GC, POOL_GC), F32)],
        compiler_params=_params(),
    )(dx1, y0, g_post, w_out, gug, o, pooled, pool_w, pool_scale)


def _attn_bwd(q, kv, do, sinks):
    seq = q.shape[0]
    nb = seq // BLOCK

    def qblock(j):
        return jnp.minimum(j, nb - 1)

    def body(sink_ref, q_ref, kvc_ref, kvp_ref, do_ref, dq_ref, dkv_ref, dsink_ref, carry, dkv_acc):
        j = pl.program_id(0)

        @pl.when(j == 0)
        def _():
            dsink_ref[...] = jnp.zeros_like(dsink_ref)
            carry[...] = jnp.zeros_like(carry)

        @pl.when(j < nb)
        def _():
            valid, distf = _attn_mask(j == 0)
            qb = q_ref[...]
            dob = do_ref[...]
            kk = jnp.concatenate([kvp_ref[...], kvc_ref[...]], axis=0)
            lane = lax.broadcasted_iota(jnp.int32, (BLOCK, LANES), 1)
            dsink = jnp.zeros((BLOCK, LANES), F32)
            for kvh in range(Q_HEADS // GROUP):
                kh = kk[:, HEAD_DIM * kvh:HEAD_DIM * (kvh + 1)]
                vh = kk[:, KV_WIDTH + HEAD_DIM * kvh:KV_WIDTH + HEAD_DIM * (kvh + 1)]
                dk = jnp.zeros((2 * BLOCK, HEAD_DIM), F32)
                dv = jnp.zeros((2 * BLOCK, HEAD_DIM), F32)
                for h in range(GROUP * kvh, GROUP * (kvh + 1)):
                    qh = qb[:, HEAD_DIM * h:HEAD_DIM * (h + 1)]
                    doh = dob[:, HEAD_DIM * h:HEAD_DIM * (h + 1)]
                    p, psink = _attn_probs(qh, kh, sink_ref[h], SLOPES[h], valid, distf)
                    dp = _dot_nt(doh, vh)
                    delta = jnp.sum(p * dp, axis=-1, keepdims=True)
                    ds = (p * (dp - delta) * (HEAD_DIM ** -0.5)).astype(BF16)
                    dsink = dsink + jnp.where(lane == h, -psink * delta, 0.0)
                    dq_ref[:, HEAD_DIM * h:HEAD_DIM * (h + 1)] = _dot(ds, kh).astype(BF16)
                    dk = dk + _dot_tn(ds, qh)
                    dv = dv + _dot_tn(p.astype(BF16), doh)
                dkv_acc[:, HEAD_DIM * kvh:HEAD_DIM * (kvh + 1)] = dk
                dkv_acc[:, KV_WIDTH + HEAD_DIM * kvh:KV_WIDTH + HEAD_DIM * (kvh + 1)] = dv
            dsink_ref[...] += dsink

            @pl.when(j > 0)
            def _():
                dkv_ref[...] = (carry[...] + dkv_acc[:BLOCK, :]).astype(BF16)

            carry[...] = dkv_acc[BLOCK:, :]

        @pl.when(j == nb)
        def _():
            dkv_ref[...] = carry[...].astype(BF16)

    return pl.pallas_call(
        body, name="attn_bwd", grid=(nb + 1,),
        in_specs=[pl.BlockSpec(memory_space=pltpu.SMEM),
                  pl.BlockSpec((BLOCK, ATTN_WIDTH), lambda j: (qblock(j), 0)),
                  pl.BlockSpec((BLOCK, 2 * KV_WIDTH), lambda j: (qblock(j), 0)),
                  pl.BlockSpec((BLOCK, 2 * KV_WIDTH), lambda j: (jnp.maximum(qblock(j) - 1, 0), 0)),
                  pl.BlockSpec((BLOCK, ATTN_WIDTH), lambda j: (qblock(j), 0))],
        out_specs=[pl.BlockSpec((BLOCK, ATTN_WIDTH), lambda j: (qblock(j), 0)),
                   pl.BlockSpec((BLOCK, 2 * KV_WIDTH), lambda j: (jnp.maximum(j - 1, 0), 0)),
                   _full((BLOCK, LANES))],
        out_shape=[jax.ShapeDtypeStruct((seq, ATTN_WIDTH), BF16), jax.ShapeDtypeStruct((seq, 2 * KV_WIDTH), BF16),
                   jax.ShapeDtypeStruct((BLOCK, LANES), F32)],
        scratch_shapes=[pltpu.VMEM((BLOCK, 2 * KV_WIDTH), F32), pltpu.VMEM((2 * BLOCK, 2 * KV_WIDTH), F32)],
        compiler_params=_params(),
    )(sinks, q, kv, kv, do)


def _bwd_in0(dpooled, dq, dkv, dgg, w_in, x, g_pre, dx1, ts):
    seq = x.shape[0]
    hb = ts // POOL_HALO
    last = seq // POOL_HALO - 1
    nt = seq // ts

    def body(dp_ref, dnext_ref, dq_ref, dkv_ref, dgg_ref, w_ref, x_ref, g_ref, dx1_ref,
             dproj_ref, gx_ref, dpre_ref, dbuf):
        i = pl.program_id(0)

        @pl.when(i == 0)
        def _():
            dpre_ref[...] = jnp.zeros_like(dpre_ref)

        dpool = dp_ref[...]
        dnext = jnp.where(i < nt - 1, dnext_ref[...], 0.0)
        u0 = ATTN_WIDTH + 2 * KV_WIDTH + ATTN_WIDTH
        for g, window in enumerate(POOL_WINDOWS):
            lanes = slice(POOL_GC * g, POOL_GC * (g + 1))
            dbuf[:ts, lanes] = dpool[:, lanes] / _pool_counts(i * ts, ts, window)
            dbuf[ts:, lanes] = dnext[:, lanes] / _pool_counts((i + 1) * ts, POOL_HALO, window)
        for g, window in enumerate(POOL_WINDOWS):
            lanes = slice(POOL_GC * g, POOL_GC * (g + 1))
            acc = dbuf[pl.ds(0, ts), lanes]
            for k in range(1, window):
                acc = acc + dbuf[pl.ds(k, ts), lanes]
            dproj_ref[:, u0 + POOL_GC * g:u0 + POOL_GC * (g + 1)] = (acc - dpool[:, lanes]).astype(BF16)
        dproj_ref[:, :ATTN_WIDTH] = dq_ref[...]
        dproj_ref[:, ATTN_WIDTH:ATTN_WIDTH + 2 * KV_WIDTH] = dkv_ref[...]
        dproj_ref[:, ATTN_WIDTH + 2 * KV_WIDTH:u0] = dgg_ref[:, :ATTN_WIDTH]
        dproj_ref[:, u0 + POOL_WIDTH:] = dgg_ref[:, ATTN_WIDTH:]
        dh = _dot(dproj_ref[...], w_ref[...])
        x = x_ref[...]
        r = lax.rsqrt(jnp.mean(x * x, axis=-1, keepdims=True) + EPS)
        dx, dpre = _rms_bwd(x, r, g_ref[...], dh)
        gx_ref[...] = dx1_ref[...] + dx
        dpre_ref[...] += _rows8(dpre)

    return pl.pallas_call(
        body, name="bwd_in0", grid=(nt,),
        in_specs=[_tile(ts, POOL_WIDTH),
                  pl.BlockSpec((POOL_HALO, POOL_WIDTH), lambda i: (jnp.minimum((i + 1) * hb, last), 0)),
                  _tile(ts, ATTN_WIDTH), _tile(ts, 2 * KV_WIDTH), _tile(ts, ATTN_WIDTH + POOL_WIDTH),
                  _full((EVEN_IN, D_MODEL)), _tile(ts, D_MODEL), _full((1, D_MODEL)), _tile(ts, D_MODEL)],
        out_specs=[_tile(ts, EVEN_IN), _tile(ts, D_MODEL), _full((SUBLANES, D_MODEL))],
        out_shape=[jax.ShapeDtypeStruct((seq, EVEN_IN), BF16), jax.ShapeDtypeStruct((seq, D_MODEL), F32),
                   jax.ShapeDtypeStruct((SUBLANES, D_MODEL), F32)],
        scratch_shapes=[pltpu.VMEM((ts + POOL_HALO, POOL_WIDTH), F32)],
        compiler_params=_params(),
    )(dpooled, dpooled, dq, dkv, dgg, w_in, x, g_pre, dx1)


def _matmul_tn(a, b, name, ts, tm):
    seq, m = a.shape
    n = b.shape[1]
    steps = seq // ts

    def body(a_ref, b_ref, o_ref, acc):
        s = pl.program_id(1)

        @pl.when(s == 0)
        def _():
            acc[...] = jnp.zeros_like(acc)

        acc[...] += _dot_tn(a_ref[...], b_ref[...])

        @pl.when(s == steps - 1)
        def _():
            o_ref[...] = acc[...].astype(BF16)

    return pl.pallas_call(
        body, name=name, grid=(m // tm, steps),
        in_specs=[pl.BlockSpec((ts, tm), lambda j, s: (s, j)), pl.BlockSpec((ts, n), lambda j, s: (s, 0))],
        out_specs=pl.BlockSpec((tm, n), lambda j, s: (j, 0)),
        out_shape=jax.ShapeDtypeStruct((m, n), BF16),
        scratch_shapes=[pltpu.VMEM((tm, n), F32)],
        compiler_params=pltpu.CompilerParams(dimension_semantics=("arbitrary", "arbitrary"), vmem_limit_bytes=VMEM_LIMIT),
    )(a, b)


def _adamw_math(w, g, m, v):
    m = ADAM_B1 * m + (1.0 - ADAM_B1) * g
    v = ADAM_B2 * v + (1.0 - ADAM_B2) * (g * g)
    m_hat = m / (1.0 - ADAM_B1 ** ADAM_STEP)
    v_hat = v / (1.0 - ADAM_B2 ** ADAM_STEP)
    delta = -ADAM_LR * (m_hat / (jnp.sqrt(v_hat) + ADAM_EPS) + ADAM_WD * w)
    return delta, m, v


def _adamw(ws, gs, ms, vs, name):
    n = len(ws)

    def body(*refs):
        ins, outs = refs[:4 * n], refs[4 * n:]
        for k in range(n):
            delta, m, v = _adamw_math(ins[k][...], ins[n + k][...], ins[2 * n + k][...], ins[3 * n + k][...])
            outs[k][...] = delta
            outs[n + k][...] = m
            outs[2 * n + k][...] = v

    shapes = [jax.ShapeDtypeStruct(w.shape, F32) for w in ws]
    out = pl.pallas_call(body, name=name, out_shape=shapes * 3,
                         compiler_params=pltpu.CompilerParams(vmem_limit_bytes=VMEM_LIMIT))(*ws, *gs, *ms, *vs)
    return out[:n], out[n:2 * n], out[2 * n:]


TS_MATMUL, TS_CONV, TS_GRAD = 512, 256, 1024


def kernel(x, pre_norm, post_norm, a_w_in, a_sinks, b_pool_w, b_pool_scale, ab_w_out, c_w_in, c_dw_w, c_dw_b, c_ln_g, c_ln_b, c_w_out, loss_target, m_pre_norm, m_post_norm, m_a_w_in, m_a_sinks, m_b_pool_w, m_b_pool_scale, m_ab_w_out, m_c_w_in, m_c_dw_w, m_c_dw_b, m_c_ln_g, m_c_ln_b, m_c_w_out, v_pre_norm, v_post_norm, v_a_w_in, v_a_sinks, v_b_pool_w, v_b_pool_scale, v_ab_w_out, v_c_w_in, v_c_dw_w, v_c_dw_b, v_c_ln_g, v_c_ln_b, v_c_w_out):
    seq = x.shape[1]
    ts_big, ts_conv, ts_grad = min(TS_MATMUL, seq), min(TS_CONV, seq), min(TS_GRAD, seq)
    x2d = x[0]
    target = loss_target[0]
    ch = c_dw_b.shape[1]

    shards = [a_w_in[0].T.astype(BF16), ab_w_out[0].astype(BF16), c_w_in[0].T.astype(BF16), c_w_out[0].astype(BF16)]
    w_in0t, w_out0, w_in1t, w_out1 = [g.reshape(-1, D_MODEL) for g in _all_gather(shards, "gather_weights")]

    vec_rows = 40
    vecs = jnp.concatenate([c_dw_w[0, :, 0, :], c_dw_b, c_ln_g, c_ln_b, jnp.zeros((vec_rows - CONV_K - 3, ch), F32)], axis=0)
    vg, = _all_gather([vecs], "gather_vectors")
    vg = vg.transpose(1, 0, 2).reshape(vec_rows, D_MODEL)
    dw_w = vg[:CONV_HALO]
    dw_b, ln_g, ln_b = vg[CONV_K:CONV_K + 1], vg[CONV_K + 1:CONV_K + 2], vg[CONV_K + 2:CONV_K + 3]

    sinks = a_sinks[0]
    h0, q, kv, gug = _fwd_in0(x2d, pre_norm[0:1], w_in0t, ts_big)
    o = _attn_fwd(q, kv, sinks)
    mix0, pooled, y0, x1 = _fwd_out0(gug, o, b_pool_w[0], b_pool_scale, w_out0, post_norm[0:1], x2d, ts_big)
    h1, proj1, glu = _fwd_in1(x1, pre_norm[1:2], w_in1t, ts_big)
    ymix1, dy1, dx2, dcf, dgate, acc1 = _fwd_out1(glu, proj1, dw_w, dw_b, ln_g, ln_b, w_out1, post_norm[1:2], x1, target, ts_conv)

    dproj1, dx1, ddw_w, dpre1 = _bwd_in1(dcf, glu, proj1, dgate, dw_w, w_in1t, x1, pre_norm[1:2], dx2, ts_conv)
    g_in1t = _matmul_tn(dproj1, h1, "grad_w_in1", ts_grad, 1024)
    g_out1 = _matmul_tn(ymix1, dy1, "grad_w_out1", ts_grad, 1024)
    dy0, do, dgg, dpooled, dpost0, dscale, dpool_w = _bwd_out0(dx1, y0, post_norm[0:1], w_out0, gug, o, pooled, b_pool_w[0], b_pool_scale, ts_big)
    dq, dkv, dsink = _attn_bwd(q, kv, do, sinks)
    dproj0, grad_x, dpre0 = _bwd_in0(dpooled, dq, dkv, dgg, w_in0t, x2d, pre_norm[0:1], dx1, ts_big)
    g_in0t = _matmul_tn(dproj0, h0, "grad_w_in0", ts_grad, 768)
    g_out0 = _matmul_tn(mix0, dy0, "grad_w_out0", ts_grad, 1024)

    parts = [g.reshape(N_DEV, -1, D_MODEL) for g in (g_in0t, g_out0, g_in1t, g_out1)]
    g_in0t, g_ab_w_out, g_in1t, g_c_w_out = _reduce_chips(_reduce_cores(parts, "reduce_cores"), "reduce_chips")
    g_a_w_in, g_c_w_in = g_in0t.T, g_in1t.T

    row = lambda a: jnp.sum(a, axis=0, keepdims=True)
    vec_g = jnp.concatenate([jnp.sum(ddw_w[:CONV_K], axis=1), row(acc1[ACC_DW_B]), row(acc1[ACC_LN_G]), row(acc1[ACC_LN_B]),
                             jnp.zeros((vec_rows - CONV_K - 3, D_MODEL), F32)], axis=0)
    vec_g = _exchange_sum(vec_g.reshape(vec_rows, N_DEV, ch).transpose(1, 0, 2), "reduce_vectors")
    g_dw_w, g_dw_b, g_ln_g, g_ln_b = vec_g[:CONV_K], vec_g[CONV_K:CONV_K + 1], vec_g[CONV_K + 1:CONV_K + 2], vec_g[CONV_K + 2:CONV_K + 3]

    lanes8 = lambda a: row(a).reshape(-1, LANES)
    loss_row = jnp.pad(jnp.sum(acc1[ACC_LOSS]).reshape(1, 1), ((0, 0), (0, LANES - 1)))
    rep = jnp.concatenate([lanes8(dpre0), lanes8(dpre1), lanes8(dpost0), lanes8(acc1[ACC_POST]), lanes8(dscale),
                           loss_row, jnp.zeros((2, LANES), F32), row(dsink), dpool_w.reshape(4 * POOL_GC, LANES)], axis=0)
    rep = _all_reduce(rep, "reduce_replicated")
    g_pre = rep[:16].reshape(2, D_MODEL)
    g_post = rep[16:32].reshape(2, D_MODEL)
    g_scale = rep[32:36].reshape(1, POOL_WIDTH)
    loss = (0.5 / D_MODEL) * rep[36, 0]
    g_sinks = rep[39:40, :Q_HEADS]
    g_pool_w = rep[40:]

    grads = [g_pre, g_post, g_a_w_in, g_sinks, g_pool_w, g_scale, g_ab_w_out, g_c_w_in, g_dw_w, g_dw_b, g_ln_g, g_ln_b, g_c_w_out]
    weights = [pre_norm, post_norm, a_w_in, a_sinks, b_pool_w, b_pool_scale, ab_w_out, c_w_in, c_dw_w, c_dw_b, c_ln_g, c_ln_b, c_w_out]
    m_in = [m_pre_norm, m_post_norm, m_a_w_in, m_a_sinks, m_b_pool_w, m_b_pool_scale, m_ab_w_out, m_c_w_in, m_c_dw_w, m_c_dw_b, m_c_ln_g, m_c_ln_b, m_c_w_out]
    v_in = [v_pre_norm, v_post_norm, v_a_w_in, v_a_sinks, v_b_pool_w, v_b_pool_scale, v_ab_w_out, v_c_w_in, v_c_dw_w, v_c_dw_b, v_c_ln_g, v_c_ln_b, v_c_w_out]
    flat = lambda arrs: [a.reshape(g.shape) for a, g in zip(arrs, grads)]
    big = (2, 6, 7, 12)
    small = tuple(k for k in range(len(grads)) if k not in big)
    pick = lambda arrs, idx: [arrs[k] for k in idx]
    deltas, new_m, new_v = [None] * 13, [None] * 13, [None] * 13
    for idx, name in ((big, "adamw_matrices"), (small, "adamw_vectors")):
        d, m, v = _adamw(pick(flat(weights), idx), pick(grads, idx), pick(flat(m_in), idx), pick(flat(v_in), idx), name)
        for k, dk, mk, vk in zip(idx, d, m, v):
            deltas[k], new_m[k], new_v[k] = dk, mk, vk
    shaped = lambda arrs: [a.reshape(w.shape) for a, w in zip(arrs, weights)]
    return (loss, grad_x[None], *shaped(grads), *shaped(deltas), *shaped(new_m), *shaped(new_v))
```

```python
import functools

import jax
import jax.numpy as jnp
from jax import lax
from jax.experimental import pallas as pl
from jax.experimental.pallas import tpu as pltpu

F32 = jnp.float32
BF16 = jnp.bfloat16
MESH = pl.DeviceIdType.MESH
AXES = ("x", "y", "c")
N_DEV = 8

D_MODEL = 1024
HEAD_DIM = 64
Q_HEADS = 8
GROUP = 4
ATTN_WIDTH = 512
KV_WIDTH = 128
BLOCK = 128
POOL_WIDTH = 512
POOL_WINDOWS = (2, 4, 8, 16)
POOL_GC = 128
POOL_HALO = 16
EVEN_IN = 2304
CONV_K = 31
CONV_HALO = 32
ODD_IN = 3072
EPS = 1e-6
NEG = -1e30
SLOPES = tuple(2.0 ** (-8.0 * (h + 1) / Q_HEADS) for h in range(Q_HEADS))

ADAM_LR = 0.001
ADAM_B1 = 0.9
ADAM_B2 = 0.999
ADAM_EPS = 1e-08
ADAM_WD = 0.01
ADAM_STEP = 10

SUBLANES = 8
LANES = 128
VMEM_LIMIT = 56 * 1024 * 1024

NT = (((1,), (1,)), ((), ()))
TN = (((0,), (0,)), ((), ()))


def _params(**kw):
    return pltpu.CompilerParams(dimension_semantics=("arbitrary",), vmem_limit_bytes=VMEM_LIMIT, **kw)


def _dot(a, b):
    return jnp.dot(a, b, preferred_element_type=F32)


def _dot_nt(a, b):
    return lax.dot_general(a, b, NT, preferred_element_type=F32)


def _dot_tn(a, b):
    return lax.dot_general(a, b, TN, preferred_element_type=F32)


def _sigmoid(v):
    return 1.0 / (1.0 + jnp.exp(-v))


def _rows8(v):
    r, c = v.shape
    return jnp.sum(v.reshape(r // SUBLANES, SUBLANES, c), axis=0)


def _rms_fwd(v, g):
    r = lax.rsqrt(jnp.mean(v * v, axis=-1, keepdims=True) + EPS)
    return v * r * g, r


def _rms_bwd(v, r, g, dout):
    gd = dout * g
    dv = r * gd - v * (r * r * r) * jnp.mean(v * gd, axis=-1, keepdims=True)
    return dv, dout * (v * r)


def _full(shape):
    return pl.BlockSpec(shape, lambda i: (0,) * len(shape))


def _tile(ts, cols, col_block=0):
    return pl.BlockSpec((ts, cols), lambda i: (i, col_block))


def _position():
    return lax.axis_index("x"), lax.axis_index("y"), lax.axis_index("c")


def _all_gather(blocks, name):
    n = len(blocks)

    def body(*refs):
        x_refs, out_refs = refs[:n], refs[n:2 * n]
        send_sems, recv_sems, local_sems = refs[2 * n:]
        x, y, c = _position()
        me, sibling = (x, y, c), (x, y, 1 - c)
        chips = [(1 - x, y), (x, 1 - y), (1 - x, 1 - y)]

        def copy(k, j, owner, to, src=None):
            slab = out_refs[k].at[4 * owner[0] + 2 * owner[1] + owner[2]]
            return pltpu.make_async_remote_copy(
                src_ref=slab if src is None else src, dst_ref=slab,
                send_sem=send_sems.at[7 * k + j], recv_sem=recv_sems.at[7 * k + j], device_id=to, device_id_type=MESH)

        mine, first, passed = [], [], []
        for k in range(n):
            mine.append(pltpu.make_async_copy(x_refs[k], out_refs[k].at[4 * x + 2 * y + c], local_sems.at[k]))
            mine[-1].start()
            first.append(copy(k, 0, me, sibling, src=x_refs[k]))
            first += [copy(k, 1 + j, me, (*chip, c), src=x_refs[k]) for j, chip in enumerate(chips)]
        for cp in first:
            cp.start()
        for k in range(n):
            for j, chip in enumerate(chips):
                copy(k, 1 + j, (*chip, c), me).wait_recv()
                passed.append(copy(k, 4 + j, (*chip, c), sibling))
                passed[-1].start()
        for k in range(n):
            copy(k, 0, sibling, me).wait_recv()
            for j, chip in enumerate(chips):
                copy(k, 4 + j, (*chip, 1 - c), me).wait_recv()
        for cp in first + passed:
            cp.wait_send()
        for cp in mine:
            cp.wait()

    return pl.pallas_call(
        body, name=name,
        out_shape=[jax.ShapeDtypeStruct((N_DEV, *b.shape), b.dtype) for b in blocks],
        in_specs=[pl.BlockSpec(memory_space=pltpu.VMEM)] * n,
        out_specs=[pl.BlockSpec(memory_space=pltpu.VMEM)] * n,
        scratch_shapes=[pltpu.SemaphoreType.DMA((7 * n,)), pltpu.SemaphoreType.DMA((7 * n,)), pltpu.SemaphoreType.DMA((n,))],
        compiler_params=pltpu.CompilerParams(vmem_limit_bytes=VMEM_LIMIT),
    )(*blocks)


def _all_reduce(block, name):
    gathered, = _all_gather([block], name + "_gather")
    rows, cols = block.shape

    def body(g_ref, out_ref):
        acc = g_ref[0]
        for d in range(1, N_DEV):
            acc = acc + g_ref[d]
        out_ref[...] = acc

    return pl.pallas_call(body, name=name + "_sum", out_shape=jax.ShapeDtypeStruct((rows, cols), F32))(gathered)


def _exchange_sum(parts, name):
    _, rows, cols = parts.shape

    def body(parts_ref, out_ref, recv_ref, send_sems, recv_sems, local_sem):
        x, y, c = _position()
        me = 4 * x + 2 * y + c
        mine = pltpu.make_async_copy(parts_ref.at[me], out_ref, local_sem)
        mine.start()
        copies = []
        for k in range(1, N_DEV):
            to = (x ^ (k >> 2), y ^ ((k >> 1) & 1), c ^ (k & 1))
            cp = pltpu.make_async_remote_copy(
                src_ref=parts_ref.at[me ^ k], dst_ref=recv_ref.at[k - 1],
                send_sem=send_sems.at[k - 1], recv_sem=recv_sems.at[k - 1], device_id=to, device_id_type=MESH)
            cp.start()
            copies.append(cp)
        mine.wait()
        for k, cp in enumerate(copies):
            cp.wait_recv()
            out_ref[...] = out_ref[...] + recv_ref[k]
        for cp in copies:
            cp.wait_send()

    return pl.pallas_call(
        body, name=name,
        out_shape=jax.ShapeDtypeStruct((rows, cols), F32),
        in_specs=[pl.BlockSpec(memory_space=pl.ANY)],
        out_specs=pl.BlockSpec(memory_space=pltpu.VMEM),
        scratch_shapes=[pltpu.VMEM((N_DEV - 1, rows, cols), F32), pltpu.SemaphoreType.DMA((N_DEV - 1,)),
                        pltpu.SemaphoreType.DMA((N_DEV - 1,)), pltpu.SemaphoreType.DMA],
        compiler_params=pltpu.CompilerParams(vmem_limit_bytes=VMEM_LIMIT),
    )(parts)


N_CHIPS = 4


def _reduce_cores(parts, name):
    n = len(parts)

    def body(*refs):
        part_refs, out_refs = refs[:n], refs[n:2 * n]
        recv_refs, own_refs = refs[2 * n:3 * n], refs[3 * n:4 * n]
        send_sems, recv_sems, local_sems = refs[4 * n:]
        x, y, c = _position()
        copies, loads = [], []
        for k in range(n):
            for j in range(N_CHIPS):
                cp = pltpu.make_async_remote_copy(
                    src_ref=part_refs[k].at[2 * j + 1 - c], dst_ref=recv_refs[k].at[j],
                    send_sem=send_sems.at[N_CHIPS * k + j], recv_sem=recv_sems.at[N_CHIPS * k + j],
                    device_id=(x, y, 1 - c), device_id_type=MESH)
                cp.start()
                ld = pltpu.make_async_copy(part_refs[k].at[2 * j + c], own_refs[k].at[j], local_sems.at[N_CHIPS * k + j])
                ld.start()
                copies.append(cp)
                loads.append(ld)
        for k in range(n):
            for j in range(N_CHIPS):
                loads[N_CHIPS * k + j].wait()
                copies[N_CHIPS * k + j].wait_recv()
                out_refs[k][j] = (own_refs[k][j].astype(F32) + recv_refs[k][j].astype(F32)).astype(BF16)
        for cp in copies:
            cp.wait_send()

    shapes = [(N_CHIPS, *p.shape[1:]) for p in parts]
    return pl.pallas_call(
        body, name=name,
        out_shape=[jax.ShapeDtypeStruct(s, BF16) for s in shapes],
        in_specs=[pl.BlockSpec(memory_space=pl.ANY)] * n,
        out_specs=[pl.BlockSpec(memory_space=pltpu.VMEM)] * n,
        scratch_shapes=[pltpu.VMEM(s, BF16) for s in shapes] * 2 + [
            pltpu.SemaphoreType.DMA((N_CHIPS * n,)), pltpu.SemaphoreType.DMA((N_CHIPS * n,)),
            pltpu.SemaphoreType.DMA((N_CHIPS * n,))],
        compiler_params=pltpu.CompilerParams(vmem_limit_bytes=VMEM_LIMIT),
    )(*parts)


def _reduce_chips(parts, name):
    n = len(parts)
    peers = N_CHIPS - 1

    def body(*refs):
        part_refs, out_refs = refs[:n], refs[n:2 * n]
        recv_refs, own_refs = refs[2 * n:3 * n], refs[3 * n:4 * n]
        send_sems, recv_sems, local_sems = refs[4 * n:]
        x, y, c = _position()
        me = 2 * x + y
        copies, loads = [], []
        for k in range(n):
            ld = pltpu.make_async_copy(part_refs[k].at[me], own_refs[k], local_sems.at[k])
            ld.start()
            loads.append(ld)
            for d in range(1, N_CHIPS):
                cp = pltpu.make_async_remote_copy(
                    src_ref=part_refs[k].at[me ^ d], dst_ref=recv_refs[k].at[d - 1],
                    send_sem=send_sems.at[peers * k + d - 1], recv_sem=recv_sems.at[peers * k + d - 1],
                    device_id=(x ^ (d >> 1), y ^ (d & 1), c), device_id_type=MESH)
                cp.start()
                copies.append(cp)
        for k in range(n):
            loads[k].wait()
            acc = own_refs[k][...].astype(F32)
            for d in range(peers):
                copies[peers * k + d].wait_recv()
                acc = acc + recv_refs[k][d].astype(F32)
            out_refs[k][...] = acc
        for cp in copies:
            cp.wait_send()

    return pl.pallas_call(
        body, name=name,
        out_shape=[jax.ShapeDtypeStruct(p.shape[1:], F32) for p in parts],
        in_specs=[pl.BlockSpec(memory_space=pl.ANY)] * n,
        out_specs=[pl.BlockSpec(memory_space=pltpu.VMEM)] * n,
        scratch_shapes=[pltpu.VMEM((peers, *p.shape[1:]), BF16) for p in parts] + [pltpu.VMEM(p.shape[1:], BF16) for p in parts] + [
            pltpu.SemaphoreType.DMA((peers * n,)), pltpu.SemaphoreType.DMA((peers * n,)), pltpu.SemaphoreType.DMA((n,))],
        compiler_params=pltpu.CompilerParams(vmem_limit_bytes=VMEM_LIMIT),
    )(*parts)


def _fwd_in0(x, g, w, ts):
    seq = x.shape[0]

    def body(x_ref, g_ref, w_ref, h_ref, q_ref, kv_ref, gug_ref):
        h, _ = _rms_fwd(x_ref[...], g_ref[...])
        h = h.astype(BF16)
        h_ref[...] = h
        proj = _dot_nt(h, w_ref[...])
        q_ref[...] = proj[:, :ATTN_WIDTH].astype(BF16)
        kv_ref[...] = proj[:, ATTN_WIDTH:ATTN_WIDTH + 2 * KV_WIDTH].astype(BF16)
        gug_ref[...] = proj[:, ATTN_WIDTH + 2 * KV_WIDTH:]

    return pl.pallas_call(
        body, name="fwd_in0", grid=(seq // ts,),
        in_specs=[_tile(ts, D_MODEL), _full((1, D_MODEL)), _full((EVEN_IN, D_MODEL))],
        out_specs=[_tile(ts, D_MODEL), _tile(ts, ATTN_WIDTH), _tile(ts, 2 * KV_WIDTH), _tile(ts, 3 * POOL_WIDTH)],
        out_shape=[jax.ShapeDtypeStruct((seq, D_MODEL), BF16), jax.ShapeDtypeStruct((seq, ATTN_WIDTH), BF16),
                   jax.ShapeDtypeStruct((seq, 2 * KV_WIDTH), BF16), jax.ShapeDtypeStruct((seq, 3 * POOL_WIDTH), F32)],
        compiler_params=_params(),
    )(x, g, w)


GROUP_ROWS = GROUP * BLOCK


def _attn_mask(rows, first_block):
    row = lax.broadcasted_iota(jnp.int32, (rows, 2 * BLOCK), 0) & (BLOCK - 1)
    col = lax.broadcasted_iota(jnp.int32, (rows, 2 * BLOCK), 1)
    dist = row + BLOCK - col
    valid = (dist >= 0) & (dist < BLOCK) & ((col >= BLOCK) | jnp.logical_not(first_block))
    return valid, dist.astype(F32)


def _group_rows(block, kvh):
    return jnp.concatenate([block[:, HEAD_DIM * h:HEAD_DIM * (h + 1)] for h in range(GROUP * kvh, GROUP * (kvh + 1))], axis=0)


def _group_columns(sink_ref, kvh):
    head = lax.broadcasted_iota(jnp.int32, (GROUP_ROWS, 1), 0) // BLOCK
    sink = jnp.zeros((GROUP_ROWS, 1), F32)
    slope = jnp.zeros((GROUP_ROWS, 1), F32)
    for i in range(GROUP):
        sink = jnp.where(head == i, sink_ref[GROUP * kvh + i], sink)
        slope = jnp.where(head == i, SLOPES[GROUP * kvh + i], slope)
    return sink, slope


def _attn_probs(qh, kh, sink, slope, valid, distf):
    s = _dot_nt(qh, kh) * (HEAD_DIM ** -0.5)
    s = jnp.where(valid, s - slope * distf, NEG)
    mx = jnp.maximum(jnp.max(s, axis=-1, keepdims=True), sink)
    e = jnp.exp(s - mx)
    es = jnp.exp(sink - mx)
    den = jnp.sum(e, axis=-1, keepdims=True) + es
    return e / den, es / den


def _attn_fwd(q, kv, sinks):
    seq = q.shape[0]
    nb = seq // BLOCK

    def body(sink_ref, q_ref, kvc_ref, kvp_ref, o_ref):
        n = pl.program_id(0)
        valid, distf = _attn_mask(BLOCK, n == 0)
        qb = q_ref[...]
        kk = jnp.concatenate([kvp_ref[...], kvc_ref[...]], axis=0)
        for h in range(Q_HEADS):
            kvh = h // GROUP
            qh = qb[:, HEAD_DIM * h:HEAD_DIM * (h + 1)]
            kh = kk[:, HEAD_DIM * kvh:HEAD_DIM * (kvh + 1)]
            vh = kk[:, KV_WIDTH + HEAD_DIM * kvh:KV_WIDTH + HEAD_DIM * (kvh + 1)]
            p, _ = _attn_probs(qh, kh, sink_ref[h], SLOPES[h], valid, distf)
            o_ref[:, HEAD_DIM * h:HEAD_DIM * (h + 1)] = _dot(p.astype(BF16), vh)

    return pl.pallas_call(
        body, name="attn_fwd", grid=(nb,),
        in_specs=[pl.BlockSpec(memory_space=pltpu.SMEM),
                  pl.BlockSpec((BLOCK, ATTN_WIDTH), lambda n: (n, 0)),
                  pl.BlockSpec((BLOCK, 2 * KV_WIDTH), lambda n: (n, 0)),
                  pl.BlockSpec((BLOCK, 2 * KV_WIDTH), lambda n: (jnp.maximum(n - 1, 0), 0))],
        out_specs=pl.BlockSpec((BLOCK, ATTN_WIDTH), lambda n: (n, 0)),
        out_shape=jax.ShapeDtypeStruct((seq, ATTN_WIDTH), F32),
        compiler_params=_params(),
    )(sinks, q, kv, kv)


def _pool_counts(first_row, rows, window):
    t = first_row + lax.broadcasted_iota(jnp.int32, (rows, 1), 0)
    return jnp.minimum(t + 1, window).astype(F32)


def _fwd_out0(gug, o, pool_w, pool_scale, w_out, g_post, x, ts):
    seq = x.shape[0]
    hb = ts // POOL_HALO

    def body(gug_ref, halo_ref, o_ref, pw_ref, ps_ref, w_ref, g_ref, x_ref, mix_ref, pooled_ref, y_ref, x1_ref, ubuf):
        i = pl.program_id(0)
        ga = gug_ref[:, :ATTN_WIDTH]
        u = gug_ref[:, ATTN_WIDTH:ATTN_WIDTH + POOL_WIDTH]
        gb = gug_ref[:, ATTN_WIDTH + POOL_WIDTH:]
        mix_ref[:, :ATTN_WIDTH] = (o_ref[...] * (ga * _sigmoid(ga))).astype(BF16)
        ubuf[:POOL_HALO, :] = jnp.where(i > 0, halo_ref[...], 0.0)
        ubuf[POOL_HALO:, :] = u
        silu_gb = gb * _sigmoid(gb)
        for g, window in enumerate(POOL_WINDOWS):
            lanes = slice(POOL_GC * g, POOL_GC * (g + 1))
            acc = ubuf[pl.ds(POOL_HALO, ts), lanes]
            for k in range(1, window):
                acc = acc + ubuf[pl.ds(POOL_HALO - k, ts), lanes]
            pooled = (acc / _pool_counts(i * ts, ts, window) - u[:, lanes]).astype(BF16)
            pooled_ref[:, lanes] = pooled
            ypool = _dot(pooled, pw_ref[g].astype(BF16)) * ps_ref[:, lanes]
            mix_ref[:, ATTN_WIDTH + POOL_GC * g:ATTN_WIDTH + POOL_GC * (g + 1)] = (ypool * silu_gb[:, lanes]).astype(BF16)
        y = _dot(mix_ref[...], w_ref[...])
        y_ref[...] = y
        yn, _ = _rms_fwd(y, g_ref[...])
        x1_ref[...] = x_ref[...] + yn

    return pl.pallas_call(
        body, name="fwd_out0", grid=(seq // ts,),
        in_specs=[_tile(ts, 3 * POOL_WIDTH),
                  pl.BlockSpec((POOL_HALO, POOL_WIDTH), lambda i: (jnp.maximum(i * hb - 1, 0), 1)),
                  _tile(ts, ATTN_WIDTH), _full((4, POOL_GC, POOL_GC)), _full((1, POOL_WIDTH)),
                  _full((D_MODEL, D_MODEL)), _full((1, D_MODEL)), _tile(ts, D_MODEL)],
        out_specs=[_tile(ts, D_MODEL), _tile(ts, POOL_WIDTH), _tile(ts, D_MODEL), _tile(ts, D_MODEL)],
        out_shape=[jax.ShapeDtypeStruct((seq, D_MODEL), BF16), jax.ShapeDtypeStruct((seq, POOL_WIDTH), BF16),
                   jax.ShapeDtypeStruct((seq, D_MODEL), F32), jax.ShapeDtypeStruct((seq, D_MODEL), F32)],
        scratch_shapes=[pltpu.VMEM((ts + POOL_HALO, POOL_WIDTH), F32)],
        compiler_params=_params(),
    )(gug, gug, o, pool_w, pool_scale, w_out, g_post, x)


def _fwd_in1(x1, g, w, ts):
    seq = x1.shape[0]

    def body(x_ref, g_ref, w_ref, h_ref, proj_ref, glu_ref):
        h, _ = _rms_fwd(x_ref[...], g_ref[...])
        h = h.astype(BF16)
        h_ref[...] = h
        proj = _dot_nt(h, w_ref[...])
        proj_ref[...] = proj
        glu_ref[...] = proj[:, :D_MODEL] * _sigmoid(proj[:, D_MODEL:2 * D_MODEL])

    return pl.pallas_call(
        body, name="fwd_in1", grid=(seq // ts,),
        in_specs=[_tile(ts, D_MODEL), _full((1, D_MODEL)), _full((ODD_IN, D_MODEL))],
        out_specs=[_tile(ts, D_MODEL), _tile(ts, ODD_IN), _tile(ts, D_MODEL)],
        out_shape=[jax.ShapeDtypeStruct((seq, D_MODEL), BF16), jax.ShapeDtypeStruct((seq, ODD_IN), F32),
                   jax.ShapeDtypeStruct((seq, D_MODEL), F32)],
        compiler_params=_params(),
    )(x1, g, w)


ACC_LOSS, ACC_POST, ACC_LN_G, ACC_LN_B, ACC_DW_B = range(5)
CONV_FIRST = CONV_HALO - CONV_K + 1


def _fwd_tap(offset):
    return offset - CONV_FIRST if CONV_FIRST <= offset <= CONV_HALO else None


def _bwd_tap(offset):
    return CONV_K - 1 - offset if offset < CONV_K else None


def _conv_taps(w_ref, buf_ref, ts, lanes, tap_of_offset):
    out = None
    for b in range(SUBLANES):
        rows = ts if b == 0 else ts + SUBLANES
        part = None
        for a in range(CONV_HALO // SUBLANES + 1):
            k = tap_of_offset(SUBLANES * a + b)
            if k is None:
                continue
            term = w_ref[k:k + 1, lanes] * buf_ref[pl.ds(SUBLANES * a, rows), lanes]
            part = term if part is None else part + term
        if part is None:
            continue
        if b:
            part = part[b:b + ts, :]
        out = part if out is None else out + part
    return out


def _fwd_out1(glu, proj, dw_w, dw_b, ln_g, ln_b, w_out, g_post, x1, target, ts):
    seq = x1.shape[0]
    hb = ts // CONV_HALO

    def body(glu_ref, halo_ref, gate_ref, dww_ref, dwb_ref, lng_ref, lnb_ref, w_ref, g_ref, x1_ref, t_ref,
             ymix_ref, dy_ref, dx2_ref, dcf_ref, dgate_ref, acc_ref, gbuf):
        i = pl.program_id(0)

        @pl.when(i == 0)
        def _():
            acc_ref[...] = jnp.zeros_like(acc_ref)

        gbuf[:CONV_HALO, :] = jnp.where(i > 0, halo_ref[...], 0.0)
        gbuf[CONV_HALO:, :] = glu_ref[...]
        for lb in range(D_MODEL // LANES):
            lanes = slice(LANES * lb, LANES * (lb + 1))
            dcf_ref[:, lanes] = _conv_taps(dww_ref, gbuf, ts, lanes, _fwd_tap)
        cf = dcf_ref[...] + dwb_ref[...]
        mu = jnp.mean(cf, axis=-1, keepdims=True)
        cen = cf - mu
        rs = lax.rsqrt(jnp.mean(cen * cen, axis=-1, keepdims=True) + EPS)
        xhat = cen * rs
        cn = xhat * lng_ref[...] + lnb_ref[...]
        gate = gate_ref[...]
        sg = _sigmoid(gate)
        sc = _sigmoid(cn)
        silu_gate = gate * sg
        silu_cn = cn * sc
        ymix = (silu_cn * silu_gate).astype(BF16)
        ymix_ref[...] = ymix
        y = _dot(ymix, w_ref[...])
        yn, r = _rms_fwd(y, g_ref[...])
        err = (x1_ref[...] + yn) - t_ref[...]
        acc_ref[ACC_LOSS] += _rows8(err * err)
        dx2 = err * (1.0 / D_MODEL)
        dx2_ref[...] = dx2
        dy, dpost = _rms_bwd(y, r, g_ref[...], dx2)
        acc_ref[ACC_POST] += _rows8(dpost)
        dy = dy.astype(BF16)
        dy_ref[...] = dy
        dymix = _dot_nt(dy, w_ref[...])
        dgate_ref[...] = (dymix * silu_cn * (sg * (1.0 + gate * (1.0 - sg)))).astype(BF16)
        dcn = dymix * silu_gate * (sc * (1.0 + cn * (1.0 - sc)))
        acc_ref[ACC_LN_G] += _rows8(dcn * xhat)
        acc_ref[ACC_LN_B] += _rows8(dcn)
        dxhat = dcn * lng_ref[...]
        dcf = rs * (dxhat - jnp.mean(dxhat, axis=-1, keepdims=True)
                    - xhat * jnp.mean(dxhat * xhat, axis=-1, keepdims=True))
        acc_ref[ACC_DW_B] += _rows8(dcf)
        dcf_ref[...] = dcf

    return pl.pallas_call(
        body, name="fwd_out1", grid=(seq // ts,),
        in_specs=[_tile(ts, D_MODEL),
                  pl.BlockSpec((CONV_HALO, D_MODEL), lambda i: (jnp.maximum(i * hb - 1, 0), 0)),
                  _tile(ts, D_MODEL, 2), _full((CONV_HALO, D_MODEL)), _full((1, D_MODEL)), _full((1, D_MODEL)),
                  _full((1, D_MODEL)), _full((D_MODEL, D_MODEL)), _full((1, D_MODEL)), _tile(ts, D_MODEL),
                  _tile(ts, D_MODEL)],
        out_specs=[_tile(ts, D_MODEL), _tile(ts, D_MODEL), _tile(ts, D_MODEL), _tile(ts, D_MODEL), _tile(ts, D_MODEL),
                   _full((5, SUBLANES, D_MODEL))],
        out_shape=[jax.ShapeDtypeStruct((seq, D_MODEL), BF16), jax.ShapeDtypeStruct((seq, D_MODEL), BF16),
                   jax.ShapeDtypeStruct((seq, D_MODEL), F32), jax.ShapeDtypeStruct((seq, D_MODEL), F32),
                   jax.ShapeDtypeStruct((seq, D_MODEL), BF16), jax.ShapeDtypeStruct((5, SUBLANES, D_MODEL), F32)],
        scratch_shapes=[pltpu.VMEM((ts + CONV_HALO, D_MODEL), F32)],
        compiler_params=_params(),
    )(glu, glu, proj, dw_w, dw_b, ln_g, ln_b, w_out, g_post, x1, target)


def _bwd_in1(dcf, glu, proj, dgate, dw_w, w_in, x1, g_pre, dx2, ts):
    seq = x1.shape[0]
    hb = ts // CONV_HALO
    last = seq // CONV_HALO - 1
    nt = seq // ts

    def body(dcf_ref, dnext_ref, glu_ref, gprev_ref, ab_ref, dgate_ref, dww_ref, w_ref, x_ref, g_ref, dx2_ref,
             dproj_ref, dx1_ref, ddw_ref, dpre_ref, dbuf, gbuf, zbuf, sbuf):
        i = pl.program_id(0)

        @pl.when(i == 0)
        def _():
            ddw_ref[...] = jnp.zeros_like(ddw_ref)
            dpre_ref[...] = jnp.zeros_like(dpre_ref)

        dcf = dcf_ref[...]
        dbuf[:ts, :] = dcf
        dbuf[ts:, :] = jnp.where(i < nt - 1, dnext_ref[...], 0.0)
        gbuf[:CONV_HALO, :] = jnp.where(i > 0, gprev_ref[...], 0.0)
        gbuf[CONV_HALO:, :] = glu_ref[...]
        zbuf[:SUBLANES, :] = jnp.zeros((SUBLANES, D_MODEL), F32)
        zbuf[pl.ds(SUBLANES, ts), :] = dcf
        zbuf[pl.ds(SUBLANES + ts, SUBLANES), :] = jnp.zeros((SUBLANES, D_MODEL), F32)
        for lb in range(D_MODEL // LANES):
            lanes = slice(LANES * lb, LANES * (lb + 1))
            gate_lanes = slice(D_MODEL + LANES * lb, D_MODEL + LANES * (lb + 1))
            dglu = _conv_taps(dww_ref, dbuf, ts, lanes, _bwd_tap)
            a = ab_ref[:, lanes]
            sb = _sigmoid(ab_ref[:, gate_lanes])
            dproj_ref[:, lanes] = (dglu * sb).astype(BF16)
            dproj_ref[:, gate_lanes] = (dglu * a * sb * (1.0 - sb)).astype(BF16)
            for b in range(SUBLANES):
                rows = ts if b == 0 else ts + SUBLANES
                sbuf[b, pl.ds(0, rows), :] = zbuf[pl.ds(SUBLANES - b, rows), lanes]
                for a8 in range(CONV_HALO // SUBLANES + 1):
                    k = _fwd_tap(SUBLANES * a8 + b)
                    if k is not None:
                        ddw_ref[k, :, lanes] += _rows8(sbuf[b, pl.ds(0, rows), :] * gbuf[pl.ds(SUBLANES * a8, rows), lanes])
        dproj_ref[:, 2 * D_MODEL:] = dgate_ref[...]
        dh = _dot(dproj_ref[...], w_ref[...])
        x = x_ref[...]
        r = lax.rsqrt(jnp.mean(x * x, axis=-1, keepdims=True) + EPS)
        dx, dpre = _rms_bwd(x, r, g_ref[...], dh)
        dx1_ref[...] = dx2_ref[...] + dx
        dpre_ref[...] += _rows8(dpre)

    return pl.pallas_call(
        body, name="bwd_in1", grid=(nt,),
        in_specs=[_tile(ts, D_MODEL),
                  pl.BlockSpec((CONV_HALO, D_MODEL), lambda i: (jnp.minimum((i + 1) * hb, last), 0)),
                  _tile(ts, D_MODEL),
                  pl.BlockSpec((CONV_HALO, D_MODEL), lambda i: (jnp.maximum(i * hb - 1, 0), 0)),
                  _tile(ts, 2 * D_MODEL), _tile(ts, D_MODEL), _full((CONV_HALO, D_MODEL)),
                  _full((ODD_IN, D_MODEL)), _tile(ts, D_MODEL), _full((1, D_MODEL)), _tile(ts, D_MODEL)],
        out_specs=[_tile(ts, ODD_IN), _tile(ts, D_MODEL), _full((CONV_HALO, SUBLANES, D_MODEL)), _full((SUBLANES, D_MODEL))],
        out_shape=[jax.ShapeDtypeStruct((seq, ODD_IN), BF16), jax.ShapeDtypeStruct((seq, D_MODEL), F32),
                   jax.ShapeDtypeStruct((CONV_HALO, SUBLANES, D_MODEL), F32), jax.ShapeDtypeStruct((SUBLANES, D_MODEL), F32)],
        scratch_shapes=[pltpu.VMEM((ts + CONV_HALO, D_MODEL), F32), pltpu.VMEM((ts + CONV_HALO, D_MODEL), F32),
                        pltpu.VMEM((ts + 2 * SUBLANES, D_MODEL), F32), pltpu.VMEM((SUBLANES, ts + SUBLANES, LANES), F32)],
        compiler_params=_params(),
    )(dcf, dcf, glu, glu, proj, dgate, dw_w, w_in, x1, g_pre, dx2)


def _bwd_out0(dx1, y0, g_post, w_out, gug, o, pooled, pool_w, pool_scale, ts):
    seq = dx1.shape[0]

    def body(dx1_ref, y_ref, g_ref, w_ref, gug_ref, o_ref, pooled_ref, pw_ref, ps_ref,
             dy_ref, do_ref, dgg_ref, dpooled_ref, dpost_ref, dscale_ref, dpw_ref):
        i = pl.program_id(0)

        @pl.when(i == 0)
        def _():
            dpost_ref[...] = jnp.zeros_like(dpost_ref)
            dscale_ref[...] = jnp.zeros_like(dscale_ref)
            dpw_ref[...] = jnp.zeros_like(dpw_ref)

        y = y_ref[...]
        r = lax.rsqrt(jnp.mean(y * y, axis=-1, keepdims=True) + EPS)
        dy, dpost = _rms_bwd(y, r, g_ref[...], dx1_ref[...])
        dpost_ref[...] += _rows8(dpost)
        dy = dy.astype(BF16)
        dy_ref[...] = dy
        dmix = _dot_nt(dy, w_ref[...])
        dya = dmix[:, :ATTN_WIDTH]
        dyb = dmix[:, ATTN_WIDTH:]
        ga = gug_ref[:, :ATTN_WIDTH]
        gb = gug_ref[:, ATTN_WIDTH + POOL_WIDTH:]
        sga = _sigmoid(ga)
        sgb = _sigmoid(gb)
        do_ref[...] = (dya * (ga * sga)).astype(BF16)
        dgg_ref[:, :ATTN_WIDTH] = (dya * o_ref[...] * (sga * (1.0 + ga * (1.0 - sga)))).astype(BF16)
        dypool = dyb * (gb * sgb)
        dsilu_gb = sgb * (1.0 + gb * (1.0 - sgb))
        for g in range(len(POOL_WINDOWS)):
            lanes = slice(POOL_GC * g, POOL_GC * (g + 1))
            pooled = pooled_ref[:, lanes]
            wg = pw_ref[g].astype(BF16)
            pw = _dot(pooled, wg)
            scale = ps_ref[:, lanes]
            dgg_ref[:, ATTN_WIDTH + POOL_GC * g:ATTN_WIDTH + POOL_GC * (g + 1)] = (
                dyb[:, lanes] * (pw * scale) * dsilu_gb[:, lanes]).astype(BF16)
            dscale_ref[:, lanes] += _rows8(dypool[:, lanes] * pw)
            dpw = (dypool[:, lanes] * scale).astype(BF16)
            dpooled_ref[:, lanes] = _dot_nt(dpw, wg)
            dpw_ref[g] += _dot_tn(pooled, dpw)

    return pl.pallas_call(
        body, name="bwd_out0", grid=(seq // ts,),
        in_specs=[_tile(ts, D_MODEL), _tile(ts, D_MODEL), _full((1, D_MODEL)), _full((D_MODEL, D_MODEL)),
                  _tile(ts, 3 * POOL_WIDTH), _tile(ts, ATTN_WIDTH), _tile(ts, POOL_WIDTH),
                  _full((4, POOL_GC, POOL_GC)), _full((1, POOL_WIDTH))],
        out_specs=[_tile(ts, D_MODEL), _tile(ts, ATTN_WIDTH), _tile(ts, ATTN_WIDTH + POOL_WIDTH), _tile(ts, POOL_WIDTH),
                   _full((SUBLANES, D_MODEL)), _full((SUBLANES, POOL_WIDTH)), _full((4, POOL_GC, POOL_GC))],
        out_shape=[jax.ShapeDtypeStruct((seq, D_MODEL), BF16), jax.ShapeDtypeStruct((seq, ATTN_WIDTH), BF16),
                   jax.ShapeDtypeStruct((seq, ATTN_WIDTH + POOL_WIDTH), BF16), jax.ShapeDtypeStruct((seq, POOL_WIDTH), F32),
                   jax.ShapeDtypeStruct((SUBLANES, D_MODEL), F32), jax.ShapeDtypeStruct((SUBLANES, POOL_WIDTH), F32),
                   jax.ShapeDtypeStruct((4, POOL_GC, POOL_GC), F32)],
        compiler_params=_params(),
    )(dx1, y0, g_post, w_out, gug, o, pooled, pool_w, pool_scale)


def _attn_bwd(q, kv, do, sinks):
    seq = q.shape[0]
    nb = seq // BLOCK

    def qblock(j):
        return jnp.minimum(j, nb - 1)

    def body(sink_ref, q_ref, kvc_ref, kvp_ref, do_ref, dq_ref, dkv_ref, dsink_ref, carry, dkv_acc):
        j = pl.program_id(0)

        @pl.when(j == 0)
        def _():
            dsink_ref[...] = jnp.zeros_like(dsink_ref)
            carry[...] = jnp.zeros_like(carry)

        @pl.when(j < nb)
        def _():
            valid, distf = _attn_mask(GROUP_ROWS, j == 0)
            qb = q_ref[...]
            dob = do_ref[...]
            kk = jnp.concatenate([kvp_ref[...], kvc_ref[...]], axis=0)
            lane = lax.broadcasted_iota(jnp.int32, (BLOCK, LANES), 1)
            dsink = jnp.zeros((BLOCK, LANES), F32)
            for kvh in range(Q_HEADS // GROUP):
                kh = kk[:, HEAD_DIM * kvh:HEAD_DIM * (kvh + 1)]
                vh = kk[:, KV_WIDTH + HEAD_DIM * kvh:KV_WIDTH + HEAD_DIM * (kvh + 1)]
                qg = _group_rows(qb, kvh)
                dog = _group_rows(dob, kvh)
                sink, slope = _group_columns(sink_ref, kvh)
                p, psink = _attn_probs(qg, kh, sink, slope, valid, distf)
                dp = _dot_nt(dog, vh)
                delta = jnp.sum(p * dp, axis=-1, keepdims=True)
                ds = (p * (dp - delta) * (HEAD_DIM ** -0.5)).astype(BF16)
                dsink_rows = -psink * delta
                dqg = _dot(ds, kh).astype(BF16)
                for i in range(GROUP):
                    h = GROUP * kvh + i
                    dsink = dsink + jnp.where(lane == h, dsink_rows[BLOCK * i:BLOCK * (i + 1), :], 0.0)
                    dq_ref[:, HEAD_DIM * h:HEAD_DIM * (h + 1)] = dqg[BLOCK * i:BLOCK * (i + 1), :]
                dkv_acc[:, HEAD_DIM * kvh:HEAD_DIM * (kvh + 1)] = _dot_tn(ds, qg)
                dkv_acc[:, KV_WIDTH + HEAD_DIM * kvh:KV_WIDTH + HEAD_DIM * (kvh + 1)] = _dot_tn(p.astype(BF16), dog)
            dsink_ref[...] += dsink

            @pl.when(j > 0)
            def _():
                dkv_ref[...] = (carry[...] + dkv_acc[:BLOCK, :]).astype(BF16)

            carry[...] = dkv_acc[BLOCK:, :]

        @pl.when(j == nb)
        def _():
            dkv_ref[...] = carry[...].astype(BF16)

    return pl.pallas_call(
        body, name="attn_bwd", grid=(nb + 1,),
        in_specs=[pl.BlockSpec(memory_space=pltpu.SMEM),
                  pl.BlockSpec((BLOCK, ATTN_WIDTH), lambda j: (qblock(j), 0)),
                  pl.BlockSpec((BLOCK, 2 * KV_WIDTH), lambda j: (qblock(j), 0)),
                  pl.BlockSpec((BLOCK, 2 * KV_WIDTH), lambda j: (jnp.maximum(qblock(j) - 1, 0), 0)),
                  pl.BlockSpec((BLOCK, ATTN_WIDTH), lambda j: (qblock(j), 0))],
        out_specs=[pl.BlockSpec((BLOCK, ATTN_WIDTH), lambda j: (qblock(j), 0)),
                   pl.BlockSpec((BLOCK, 2 * KV_WIDTH), lambda j: (jnp.maximum(j - 1, 0), 0)),
                   _full((BLOCK, LANES))],
        out_shape=[jax.ShapeDtypeStruct((seq, ATTN_WIDTH), BF16), jax.ShapeDtypeStruct((seq, 2 * KV_WIDTH), BF16),
                   jax.ShapeDtypeStruct((BLOCK, LANES), F32)],
        scratch_shapes=[pltpu.VMEM((BLOCK, 2 * KV_WIDTH), F32), pltpu.VMEM((2 * BLOCK, 2 * KV_WIDTH), F32)],
        compiler_params=_params(),
    )(sinks, q, kv, kv, do)


def _bwd_in0(dpooled, dq, dkv, dgg, w_in, x, g_pre, dx1, ts):
    seq = x.shape[0]
    hb = ts // POOL_HALO
    last = seq // POOL_HALO - 1
    nt = seq // ts

    def body(dp_ref, dnext_ref, dq_ref, dkv_ref, dgg_ref, w_ref, x_ref, g_ref, dx1_ref,
             dproj_ref, gx_ref, dpre_ref, dbuf):
        i = pl.program_id(0)

        @pl.when(i == 0)
        def _():
            dpre_ref[...] = jnp.zeros_like(dpre_ref)

        dpool = dp_ref[...]
        dnext = jnp.where(i < nt - 1, dnext_ref[...], 0.0)
        u0 = ATTN_WIDTH + 2 * KV_WIDTH + ATTN_WIDTH
        for g, window in enumerate(POOL_WINDOWS):
            lanes = slice(POOL_GC * g, POOL_GC * (g + 1))
            dbuf[:ts, lanes] = dpool[:, lanes] / _pool_counts(i * ts, ts, window)
            dbuf[ts:, lanes] = dnext[:, lanes] / _pool_counts((i + 1) * ts, POOL_HALO, window)
        for g, window in enumerate(POOL_WINDOWS):
            lanes = slice(POOL_GC * g, POOL_GC * (g + 1))
            acc = dbuf[pl.ds(0, ts), lanes]
            for k in range(1, window):
                acc = acc + dbuf[pl.ds(k, ts), lanes]
            dproj_ref[:, u0 + POOL_GC * g:u0 + POOL_GC * (g + 1)] = (acc - dpool[:, lanes]).astype(BF16)
        dproj_ref[:, :ATTN_WIDTH] = dq_ref[...]
        dproj_ref[:, ATTN_WIDTH:ATTN_WIDTH + 2 * KV_WIDTH] = dkv_ref[...]
        dproj_ref[:, ATTN_WIDTH + 2 * KV_WIDTH:u0] = dgg_ref[:, :ATTN_WIDTH]
        dproj_ref[:, u0 + POOL_WIDTH:] = dgg_ref[:, ATTN_WIDTH:]
        dh = _dot(dproj_ref[...], w_ref[...])
        x = x_ref[...]
        r = lax.rsqrt(jnp.mean(x * x, axis=-1, keepdims=True) + EPS)
        dx, dpre = _rms_bwd(x, r, g_ref[...], dh)
        gx_ref[...] = dx1_ref[...] + dx
        dpre_ref[...] += _rows8(dpre)

    return pl.pallas_call(
        body, name="bwd_in0", grid=(nt,),
        in_specs=[_tile(ts, POOL_WIDTH),
                  pl.BlockSpec((POOL_HALO, POOL_WIDTH), lambda i: (jnp.minimum((i + 1) * hb, last), 0)),
                  _tile(ts, ATTN_WIDTH), _tile(ts, 2 * KV_WIDTH), _tile(ts, ATTN_WIDTH + POOL_WIDTH),
                  _full((EVEN_IN, D_MODEL)), _tile(ts, D_MODEL), _full((1, D_MODEL)), _tile(ts, D_MODEL)],
        out_specs=[_tile(ts, EVEN_IN), _tile(ts, D_MODEL), _full((SUBLANES, D_MODEL))],
        out_shape=[jax.ShapeDtypeStruct((seq, EVEN_IN), BF16), jax.ShapeDtypeStruct((seq, D_MODEL), F32),
                   jax.ShapeDtypeStruct((SUBLANES, D_MODEL), F32)],
        scratch_shapes=[pltpu.VMEM((ts + POOL_HALO, POOL_WIDTH), F32)],
        compiler_params=_params(),
    )(dpooled, dpooled, dq, dkv, dgg, w_in, x, g_pre, dx1)


def _matmul_tn(a, b, name, ts, tm):
    seq, m = a.shape
    n = b.shape[1]
    steps = seq // ts

    def body(a_ref, b_ref, o_ref, acc):
        s = pl.program_id(1)

        @pl.when(s == 0)
        def _():
            acc[...] = jnp.zeros_like(acc)

        acc[...] += _dot_tn(a_ref[...], b_ref[...])

        @pl.when(s == steps - 1)
        def _():
            o_ref[...] = acc[...].astype(BF16)

    return pl.pallas_call(
        body, name=name, grid=(m // tm, steps),
        in_specs=[pl.BlockSpec((ts, tm), lambda j, s: (s, j)), pl.BlockSpec((ts, n), lambda j, s: (s, 0))],
        out_specs=pl.BlockSpec((tm, n), lambda j, s: (j, 0)),
        out_shape=jax.ShapeDtypeStruct((m, n), BF16),
        scratch_shapes=[pltpu.VMEM((tm, n), F32)],
        compiler_params=pltpu.CompilerParams(dimension_semantics=("arbitrary", "arbitrary"), vmem_limit_bytes=VMEM_LIMIT),
    )(a, b)


def _adamw_math(w, g, m, v):
    m = ADAM_B1 * m + (1.0 - ADAM_B1) * g
    v = ADAM_B2 * v + (1.0 - ADAM_B2) * (g * g)
    m_hat = m / (1.0 - ADAM_B1 ** ADAM_STEP)
    v_hat = v / (1.0 - ADAM_B2 ** ADAM_STEP)
    delta = -ADAM_LR * (m_hat / (jnp.sqrt(v_hat) + ADAM_EPS) + ADAM_WD * w)
    return delta, m, v


def _adamw(ws, gs, ms, vs, name):
    n = len(ws)

    def body(*refs):
        ins, outs = refs[:4 * n], refs[4 * n:]
        for k in range(n):
            delta, m, v = _adamw_math(ins[k][...], ins[n + k][...], ins[2 * n + k][...], ins[3 * n + k][...])
            outs[k][...] = delta
            outs[n + k][...] = m
            outs[2 * n + k][...] = v

    shapes = [jax.ShapeDtypeStruct(w.shape, F32) for w in ws]
    out = pl.pallas_call(body, name=name, out_shape=shapes * 3,
                         compiler_params=pltpu.CompilerParams(vmem_limit_bytes=VMEM_LIMIT))(*ws, *gs, *ms, *vs)
    return out[:n], out[n:2 * n], out[2 * n:]


TS_MATMUL, TS_CONV, TS_GRAD = 512, 256, 1024


def kernel(x, pre_norm, post_norm, a_w_in, a_sinks, b_pool_w, b_pool_scale, ab_w_out, c_w_in, c_dw_w, c_dw_b, c_ln_g, c_ln_b, c_w_out, loss_target, m_pre_norm, m_post_norm, m_a_w_in, m_a_sinks, m_b_pool_w, m_b_pool_scale, m_ab_w_out, m_c_w_in, m_c_dw_w, m_c_dw_b, m_c_ln_g, m_c_ln_b, m_c_w_out, v_pre_norm, v_post_norm, v_a_w_in, v_a_sinks, v_b_pool_w, v_b_pool_scale, v_ab_w_out, v_c_w_in, v_c_dw_w, v_c_dw_b, v_c_ln_g, v_c_ln_b, v_c_w_out):
    seq = x.shape[1]
    ts_big, ts_conv, ts_grad = min(TS_MATMUL, seq), min(TS_CONV, seq), min(TS_GRAD, seq)
    x2d = x[0]
    target = loss_target[0]
    ch = c_dw_b.shape[1]

    shards = [a_w_in[0].T.astype(BF16), ab_w_out[0].astype(BF16), c_w_in[0].T.astype(BF16), c_w_out[0].astype(BF16)]
    w_in0t, w_out0, w_in1t, w_out1 = [g.reshape(-1, D_MODEL) for g in _all_gather(shards, "gather_weights")]

    vec_rows = 40
    vecs = jnp.concatenate([c_dw_w[0, :, 0, :], c_dw_b, c_ln_g, c_ln_b, jnp.zeros((vec_rows - CONV_K - 3, ch), F32)], axis=0)
    vg, = _all_gather([vecs], "gather_vectors")
    vg = vg.transpose(1, 0, 2).reshape(vec_rows, D_MODEL)
    dw_w = vg[:CONV_HALO]
    dw_b, ln_g, ln_b = vg[CONV_K:CONV_K + 1], vg[CONV_K + 1:CONV_K + 2], vg[CONV_K + 2:CONV_K + 3]

    sinks = a_sinks[0]
    h0, q, kv, gug = _fwd_in0(x2d, pre_norm[0:1], w_in0t, ts_big)
    o = _attn_fwd(q, kv, sinks)
    mix0, pooled, y0, x1 = _fwd_out0(gug, o, b_pool_w[0], b_pool_scale, w_out0, post_norm[0:1], x2d, ts_big)
    h1, proj1, glu = _fwd_in1(x1, pre_norm[1:2], w_in1t, ts_big)
    ymix1, dy1, dx2, dcf, dgate, acc1 = _fwd_out1(glu, proj1, dw_w, dw_b, ln_g, ln_b, w_out1, post_norm[1:2], x1, target, ts_conv)

    dproj1, dx1, ddw_w, dpre1 = _bwd_in1(dcf, glu, proj1, dgate, dw_w, w_in1t, x1, pre_norm[1:2], dx2, ts_conv)
    g_in1t = _matmul_tn(dproj1, h1, "grad_w_in1", ts_grad, 1024)
    g_out1 = _matmul_tn(ymix1, dy1, "grad_w_out1", ts_grad, 1024)
    dy0, do, dgg, dpooled, dpost0, dscale, dpool_w = _bwd_out0(dx1, y0, post_norm[0:1], w_out0, gug, o, pooled, b_pool_w[0], b_pool_scale, ts_big)
    dq, dkv, dsink = _attn_bwd(q, kv, do, sinks)
    dproj0, grad_x, dpre0 = _bwd_in0(dpooled, dq, dkv, dgg, w_in0t, x2d, pre_norm[0:1], dx1, ts_big)
    g_in0t = _matmul_tn(dproj0, h0, "grad_w_in0", ts_grad, 768)
    g_out0 = _matmul_tn(mix0, dy0, "grad_w_out0", ts_grad, 1024)

    parts = [g.reshape(N_DEV, -1, D_MODEL) for g in (g_in0t, g_out0, g_in1t, g_out1)]
    g_in0t, g_ab_w_out, g_in1t, g_c_w_out = _reduce_chips(_reduce_cores(parts, "reduce_cores"), "reduce_chips")
    g_a_w_in, g_c_w_in = g_in0t.T, g_in1t.T

    row = lambda a: jnp.sum(a, axis=0, keepdims=True)
    vec_g = jnp.concatenate([jnp.sum(ddw_w[:CONV_K], axis=1), row(acc1[ACC_DW_B]), row(acc1[ACC_LN_G]), row(acc1[ACC_LN_B]),
                             jnp.zeros((vec_rows - CONV_K - 3, D_MODEL), F32)], axis=0)
    vec_g = _exchange_sum(vec_g.reshape(vec_rows, N_DEV, ch).transpose(1, 0, 2), "reduce_vectors")
    g_dw_w, g_dw_b, g_ln_g, g_ln_b = vec_g[:CONV_K], vec_g[CONV_K:CONV_K + 1], vec_g[CONV_K + 1:CONV_K + 2], vec_g[CONV_K + 2:CONV_K + 3]

    lanes8 = lambda a: row(a).reshape(-1, LANES)
    loss_row = jnp.pad(jnp.sum(acc1[ACC_LOSS]).reshape(1, 1), ((0, 0), (0, LANES - 1)))
    rep = jnp.concatenate([lanes8(dpre0), lanes8(dpre1), lanes8(dpost0), lanes8(acc1[ACC_POST]), lanes8(dscale),
                           loss_row, jnp.zeros((2, LANES), F32), row(dsink), dpool_w.reshape(4 * POOL_GC, LANES)], axis=0)
    rep = _all_reduce(rep, "reduce_replicated")
    g_pre = rep[:16].reshape(2, D_MODEL)
    g_post = rep[16:32].reshape(2, D_MODEL)
    g_scale = rep[32:36].reshape(1, POOL_WIDTH)
    loss = (0.5 / D_MODEL) * rep[36, 0]
    g_sinks = rep[39:40, :Q_HEADS]
    g_pool_w = rep[40:]

    grads = [g_pre, g_post, g_a_w_in, g_sinks, g_pool_w, g_scale, g_ab_w_out, g_c_w_in, g_dw_w, g_dw_b, g_ln_g, g_ln_b, g_c_w_out]
    weights = [pre_norm, post_norm, a_w_in, a_sinks, b_pool_w, b_pool_scale, ab_w_out, c_w_in, c_dw_w, c_dw_b, c_ln_g, c_ln_b, c_w_out]
    m_in = [m_pre_norm, m_post_norm, m_a_w_in, m_a_sinks, m_b_pool_w, m_b_pool_scale, m_ab_w_out, m_c_w_in, m_c_dw_w, m_c_dw_b, m_c_ln_g, m_c_ln_b, m_c_w_out]
    v_in = [v_pre_norm, v_post_norm, v_a_w_in, v_a_sinks, v_b_pool_w, v_b_pool_scale, v_ab_w_out, v_c_w_in, v_c_dw_w, v_c_dw_b, v_c_ln_g, v_c_ln_b, v_c_w_out]
    flat = lambda arrs: [a.reshape(g.shape) for a, g in zip(arrs, grads)]
    big = (2, 6, 7, 12)
    small = tuple(k for k in range(len(grads)) if k not in big)
    pick = lambda arrs, idx: [arrs[k] for k in idx]
    deltas, new_m, new_v = [None] * 13, [None] * 13, [None] * 13
    for idx, name in ((big, "adamw_matrices"), (small, "adamw_vectors")):
        d, m, v = _adamw(pick(flat(weights), idx), pick(grads, idx), pick(flat(m_in), idx), pick(flat(v_in), idx), name)
        for k, dk, mk, vk in zip(idx, d, m, v):
            deltas[k], new_m[k], new_v[k] = dk, mk, vk
    shaped = lambda arrs: [a.reshape(w.shape) for a, w in zip(arrs, weights)]
    return (loss, grad_x[None], *shaped(grads), *shaped(deltas), *shaped(new_m), *shaped(new_v))
```

```python
import functools

import jax
import jax.numpy as jnp
from jax import lax
from jax.experimental import pallas as pl
from jax.experimental.pallas import tpu as pltpu

F32 = jnp.float32
BF16 = jnp.bfloat16
MESH = pl.DeviceIdType.MESH
AXES = ("x", "y", "c")
N_DEV = 8

D_MODEL = 1024
HEAD_DIM = 64
Q_HEADS = 8
GROUP = 4
ATTN_WIDTH = 512
KV_WIDTH = 128
BLOCK = 128
POOL_WIDTH = 512
POOL_WINDOWS = (2, 4, 8, 16)
POOL_GC = 128
POOL_HALO = 16
EVEN_IN = 2304
CONV_K = 31
CONV_HALO = 32
ODD_IN = 3072
EPS = 1e-6
NEG = -1e30
SLOPES = tuple(2.0 ** (-8.0 * (h + 1) / Q_HEADS) for h in range(Q_HEADS))

ADAM_LR = 0.001
ADAM_B1 = 0.9
ADAM_B2 = 0.999
ADAM_EPS = 1e-08
ADAM_WD = 0.01
ADAM_STEP = 10

SUBLANES = 8
LANES = 128
VMEM_LIMIT = 56 * 1024 * 1024

NT = (((1,), (1,)), ((), ()))
TN = (((0,), (0,)), ((), ()))


def _params(**kw):
    return pltpu.CompilerParams(dimension_semantics=("arbitrary",), vmem_limit_bytes=VMEM_LIMIT, **kw)


def _dot(a, b):
    return jnp.dot(a, b, preferred_element_type=F32)


def _dot_nt(a, b):
    return lax.dot_general(a, b, NT, preferred_element_type=F32)


def _dot_tn(a, b):
    return lax.dot_general(a, b, TN, preferred_element_type=F32)


def _sigmoid(v):
    return 1.0 / (1.0 + jnp.exp(-v))


def _rows8(v):
    r, c = v.shape
    return jnp.sum(v.reshape(r // SUBLANES, SUBLANES, c), axis=0)


def _rms_fwd(v, g):
    r = lax.rsqrt(jnp.mean(v * v, axis=-1, keepdims=True) + EPS)
    return v * r * g, r


def _rms_bwd(v, r, g, dout):
    gd = dout * g
    dv = r * gd - v * (r * r * r) * jnp.mean(v * gd, axis=-1, keepdims=True)
    return dv, dout * (v * r)


def _full(shape):
    return pl.BlockSpec(shape, lambda i: (0,) * len(shape))


def _tile(ts, cols, col_block=0):
    return pl.BlockSpec((ts, cols), lambda i: (i, col_block))


def _position():
    return lax.axis_index("x"), lax.axis_index("y"), lax.axis_index("c")


class _Gather:
    def __init__(self, x_refs, out_refs, send_sems, recv_sems, local_sems):
        self.x_refs, self.out_refs = x_refs, out_refs
        self.send_sems, self.recv_sems, self.local_sems = send_sems, recv_sems, local_sems
        self.n = len(x_refs)
        x, y, c = _position()
        self.c = c
        self.me, self.sibling = (x, y, c), (x, y, 1 - c)
        self.chips = [(1 - x, y), (x, 1 - y), (1 - x, 1 - y)]

    def copy(self, k, j, owner, to, src=None):
        slab = self.out_refs[k].at[4 * owner[0] + 2 * owner[1] + owner[2]]
        return pltpu.make_async_remote_copy(
            src_ref=slab if src is None else src, dst_ref=slab, send_sem=self.send_sems.at[7 * k + j],
            recv_sem=self.recv_sems.at[7 * k + j], device_id=to, device_id_type=MESH)

    def mine(self, k):
        return pltpu.make_async_copy(self.x_refs[k], self.out_refs[k].at[4 * self.me[0] + 2 * self.me[1] + self.c],
                                     self.local_sems.at[k])

    def first(self, k):
        out = [self.copy(k, 0, self.me, self.sibling, src=self.x_refs[k])]
        return out + [self.copy(k, 1 + j, self.me, (*chip, self.c), src=self.x_refs[k]) for j, chip in enumerate(self.chips)]

    def begin(self):
        for k in range(self.n):
            self.mine(k).start()
            for cp in self.first(k):
                cp.start()

    def finish(self):
        passed = []
        for k in range(self.n):
            for j, chip in enumerate(self.chips):
                self.copy(k, 1 + j, (*chip, self.c), self.me).wait_recv()
                passed.append(self.copy(k, 4 + j, (*chip, self.c), self.sibling))
                passed[-1].start()
        for k in range(self.n):
            self.copy(k, 0, self.sibling, self.me).wait_recv()
            for j, chip in enumerate(self.chips):
                self.copy(k, 4 + j, (*chip, 1 - self.c), self.me).wait_recv()
        for k in range(self.n):
            for cp in self.first(k):
                cp.wait_send()
        for cp in passed:
            cp.wait_send()
        for k in range(self.n):
            self.mine(k).wait()

    @staticmethod
    def semaphores(n):
        return [pltpu.SemaphoreType.DMA((7 * n,)), pltpu.SemaphoreType.DMA((7 * n,)), pltpu.SemaphoreType.DMA((n,))]


def _all_gather(blocks, name):
    n = len(blocks)

    def body(*refs):
        gather = _Gather(refs[:n], refs[n:2 * n], *refs[2 * n:])
        gather.begin()
        gather.finish()

    return pl.pallas_call(
        body, name=name,
        out_shape=[jax.ShapeDtypeStruct((N_DEV, *b.shape), b.dtype) for b in blocks],
        in_specs=[pl.BlockSpec(memory_space=pltpu.VMEM)] * n,
        out_specs=[pl.BlockSpec(memory_space=pltpu.VMEM)] * n,
        scratch_shapes=_Gather.semaphores(n),
        compiler_params=pltpu.CompilerParams(vmem_limit_bytes=VMEM_LIMIT),
    )(*blocks)


class _Scatter:
    def __init__(self, part_refs, recv_refs, send_sems, recv_sems):
        self.part_refs, self.recv_refs, self.send_sems, self.recv_sems = part_refs, recv_refs, send_sems, recv_sems
        self.n = len(part_refs)

    def copies(self):
        x, y, c = _position()
        me = 4 * x + 2 * y + c
        out = []
        for k in range(self.n):
            for j in range(N_DEV - 1):
                d = j + 1
                out.append(pltpu.make_async_remote_copy(
                    src_ref=self.part_refs[k].at[me ^ d], dst_ref=self.recv_refs[k].at[j],
                    send_sem=self.send_sems.at[7 * k + j], recv_sem=self.recv_sems.at[7 * k + j],
                    device_id=(x ^ (d >> 2), y ^ ((d >> 1) & 1), c ^ (d & 1)), device_id_type=MESH))
        return out

    def begin(self):
        for cp in self.copies():
            cp.start()

    def finish(self):
        for cp in self.copies():
            cp.wait_recv()
        for cp in self.copies():
            cp.wait_send()

    @staticmethod
    def semaphores(n):
        return [pltpu.SemaphoreType.DMA((7 * n,)), pltpu.SemaphoreType.DMA((7 * n,))]


def _sum_slabs(own, recv, name):
    rows, cols = own.shape

    def body(own_ref, recv_ref, out_ref):
        acc = own_ref[...].astype(F32)
        for j in range(N_DEV - 1):
            acc = acc + recv_ref[j].astype(F32)
        out_ref[...] = acc

    return pl.pallas_call(body, name=name, out_shape=jax.ShapeDtypeStruct((rows, cols), F32))(own, recv)


def _all_reduce(block, name):
    gathered, = _all_gather([block], name + "_gather")
    rows, cols = block.shape

    def body(g_ref, out_ref):
        acc = g_ref[0]
        for d in range(1, N_DEV):
            acc = acc + g_ref[d]
        out_ref[...] = acc

    return pl.pallas_call(body, name=name + "_sum", out_shape=jax.ShapeDtypeStruct((rows, cols), F32))(gathered)


def _exchange_sum(parts, name):
    _, rows, cols = parts.shape

    def body(parts_ref, out_ref, recv_ref, send_sems, recv_sems, local_sem):
        x, y, c = _position()
        me = 4 * x + 2 * y + c
        mine = pltpu.make_async_copy(parts_ref.at[me], out_ref, local_sem)
        mine.start()
        copies = []
        for k in range(1, N_DEV):
            to = (x ^ (k >> 2), y ^ ((k >> 1) & 1), c ^ (k & 1))
            cp = pltpu.make_async_remote_copy(
                src_ref=parts_ref.at[me ^ k], dst_ref=recv_ref.at[k - 1],
                send_sem=send_sems.at[k - 1], recv_sem=recv_sems.at[k - 1], device_id=to, device_id_type=MESH)
            cp.start()
            copies.append(cp)
        mine.wait()
        for k, cp in enumerate(copies):
            cp.wait_recv()
            out_ref[...] = out_ref[...] + recv_ref[k]
        for cp in copies:
            cp.wait_send()

    return pl.pallas_call(
        body, name=name,
        out_shape=jax.ShapeDtypeStruct((rows, cols), F32),
        in_specs=[pl.BlockSpec(memory_space=pl.ANY)],
        out_specs=pl.BlockSpec(memory_space=pltpu.VMEM),
        scratch_shapes=[pltpu.VMEM((N_DEV - 1, rows, cols), F32), pltpu.SemaphoreType.DMA((N_DEV - 1,)),
                        pltpu.SemaphoreType.DMA((N_DEV - 1,)), pltpu.SemaphoreType.DMA],
        compiler_params=pltpu.CompilerParams(vmem_limit_bytes=VMEM_LIMIT),
    )(parts)


N_CHIPS = 4


def _reduce_cores(parts, name):
    n = len(parts)

    def body(*refs):
        part_refs, out_refs = refs[:n], refs[n:2 * n]
        recv_refs, own_refs = refs[2 * n:3 * n], refs[3 * n:4 * n]
        send_sems, recv_sems, local_sems = refs[4 * n:]
        x, y, c = _position()
        copies, loads = [], []
        for k in range(n):
            for j in range(N_CHIPS):
                cp = pltpu.make_async_remote_copy(
                    src_ref=part_refs[k].at[2 * j + 1 - c], dst_ref=recv_refs[k].at[j],
                    send_sem=send_sems.at[N_CHIPS * k + j], recv_sem=recv_sems.at[N_CHIPS * k + j],
                    device_id=(x, y, 1 - c), device_id_type=MESH)
                cp.start()
                ld = pltpu.make_async_copy(part_refs[k].at[2 * j + c], own_refs[k].at[j], local_sems.at[N_CHIPS * k + j])
                ld.start()
                copies.append(cp)
                loads.append(ld)
        for k in range(n):
            for j in range(N_CHIPS):
                loads[N_CHIPS * k + j].wait()
                copies[N_CHIPS * k + j].wait_recv()
                out_refs[k][j] = (own_refs[k][j].astype(F32) + recv_refs[k][j].astype(F32)).astype(BF16)
        for cp in copies:
            cp.wait_send()

    shapes = [(N_CHIPS, *p.shape[1:]) for p in parts]
    return pl.pallas_call(
        body, name=name,
        out_shape=[jax.ShapeDtypeStruct(s, BF16) for s in shapes],
        in_specs=[pl.BlockSpec(memory_space=pl.ANY)] * n,
        out_specs=[pl.BlockSpec(memory_space=pltpu.VMEM)] * n,
        scratch_shapes=[pltpu.VMEM(s, BF16) for s in shapes] * 2 + [
            pltpu.SemaphoreType.DMA((N_CHIPS * n,)), pltpu.SemaphoreType.DMA((N_CHIPS * n,)),
            pltpu.SemaphoreType.DMA((N_CHIPS * n,))],
        compiler_params=pltpu.CompilerParams(vmem_limit_bytes=VMEM_LIMIT),
    )(*parts)


def _reduce_chips(parts, name):
    n = len(parts)
    peers = N_CHIPS - 1

    def body(*refs):
        part_refs, out_refs = refs[:n], refs[n:2 * n]
        recv_refs, own_refs = refs[2 * n:3 * n], refs[3 * n:4 * n]
        send_sems, recv_sems, local_sems = refs[4 * n:]
        x, y, c = _position()
        me = 2 * x + y
        copies, loads = [], []
        for k in range(n):
            ld = pltpu.make_async_copy(part_refs[k].at[me], own_refs[k], local_sems.at[k])
            ld.start()
            loads.append(ld)
            for d in range(1, N_CHIPS):
                cp = pltpu.make_async_remote_copy(
                    src_ref=part_refs[k].at[me ^ d], dst_ref=recv_refs[k].at[d - 1],
                    send_sem=send_sems.at[peers * k + d - 1], recv_sem=recv_sems.at[peers * k + d - 1],
                    device_id=(x ^ (d >> 1), y ^ (d & 1), c), device_id_type=MESH)
                cp.start()
                copies.append(cp)
        for k in range(n):
            loads[k].wait()
            acc = own_refs[k][...].astype(F32)
            for d in range(peers):
                copies[peers * k + d].wait_recv()
                acc = acc + recv_refs[k][d].astype(F32)
            out_refs[k][...] = acc
        for cp in copies:
            cp.wait_send()

    return pl.pallas_call(
        body, name=name,
        out_shape=[jax.ShapeDtypeStruct(p.shape[1:], F32) for p in parts],
        in_specs=[pl.BlockSpec(memory_space=pl.ANY)] * n,
        out_specs=[pl.BlockSpec(memory_space=pltpu.VMEM)] * n,
        scratch_shapes=[pltpu.VMEM((peers, *p.shape[1:]), BF16) for p in parts] + [pltpu.VMEM(p.shape[1:], BF16) for p in parts] + [
            pltpu.SemaphoreType.DMA((peers * n,)), pltpu.SemaphoreType.DMA((peers * n,)), pltpu.SemaphoreType.DMA((n,))],
        compiler_params=pltpu.CompilerParams(vmem_limit_bytes=VMEM_LIMIT),
    )(*parts)


def _fwd_in0(x, g, w, shards, ts):
    seq = x.shape[0]
    n = len(shards)
    steps = seq // ts

    def body(*refs):
        x_ref, g_ref, w_ref = refs[:3]
        h_ref, q_ref, kv_ref, gug_ref = refs[3 + n:7 + n]
        gather = lambda: _Gather(refs[3:3 + n], refs[7 + n:7 + 2 * n], *refs[7 + 2 * n:])
        i = pl.program_id(0)

        @pl.when(i == 0)
        def _():
            gather().begin()

        h, _ = _rms_fwd(x_ref[...], g_ref[...])
        h = h.astype(BF16)
        h_ref[...] = h
        proj = _dot_nt(h, w_ref[...])
        q_ref[...] = proj[:, :ATTN_WIDTH].astype(BF16)
        kv_ref[...] = proj[:, ATTN_WIDTH:ATTN_WIDTH + 2 * KV_WIDTH].astype(BF16)
        gug_ref[...] = proj[:, ATTN_WIDTH + 2 * KV_WIDTH:]

        @pl.when(i == steps - 1)
        def _():
            gather().finish()

    hbm = pl.BlockSpec(memory_space=pl.ANY)
    out = pl.pallas_call(
        body, name="fwd_in0", grid=(steps,),
        in_specs=[_tile(ts, D_MODEL), _full((1, D_MODEL)), _full((EVEN_IN, D_MODEL))] + [hbm] * n,
        out_specs=[_tile(ts, D_MODEL), _tile(ts, ATTN_WIDTH), _tile(ts, 2 * KV_WIDTH), _tile(ts, 3 * POOL_WIDTH)] + [hbm] * n,
        out_shape=[jax.ShapeDtypeStruct((seq, D_MODEL), BF16), jax.ShapeDtypeStruct((seq, ATTN_WIDTH), BF16),
                   jax.ShapeDtypeStruct((seq, 2 * KV_WIDTH), BF16), jax.ShapeDtypeStruct((seq, 3 * POOL_WIDTH), F32)]
        + [jax.ShapeDtypeStruct((N_DEV, *b.shape), b.dtype) for b in shards],
        scratch_shapes=_Gather.semaphores(n),
        compiler_params=_params(),
    )(x, g, w, *shards)
    return out[:4], out[4:]


GROUP_ROWS = GROUP * BLOCK


def _attn_mask(rows, first_block):
    row = lax.broadcasted_iota(jnp.int32, (rows, 2 * BLOCK), 0) & (BLOCK - 1)
    col = lax.broadcasted_iota(jnp.int32, (rows, 2 * BLOCK), 1)
    dist = row + BLOCK - col
    valid = (dist >= 0) & (dist < BLOCK) & ((col >= BLOCK) | jnp.logical_not(first_block))
    return valid, dist.astype(F32)


def _group_rows(block, kvh):
    return jnp.concatenate([block[:, HEAD_DIM * h:HEAD_DIM * (h + 1)] for h in range(GROUP * kvh, GROUP * (kvh + 1))], axis=0)


def _group_columns(sink_ref, kvh):
    head = lax.broadcasted_iota(jnp.int32, (GROUP_ROWS, 1), 0) // BLOCK
    sink = jnp.zeros((GROUP_ROWS, 1), F32)
    slope = jnp.zeros((GROUP_ROWS, 1), F32)
    for i in range(GROUP):
        sink = jnp.where(head == i, sink_ref[GROUP * kvh + i], sink)
        slope = jnp.where(head == i, SLOPES[GROUP * kvh + i], slope)
    return sink, slope


def _attn_probs(qh, kh, sink, slope, valid, distf):
    s = _dot_nt(qh, kh) * (HEAD_DIM ** -0.5)
    s = jnp.where(valid, s - slope * distf, NEG)
    mx = jnp.maximum(jnp.max(s, axis=-1, keepdims=True), sink)
    e = jnp.exp(s - mx)
    es = jnp.exp(sink - mx)
    den = jnp.sum(e, axis=-1, keepdims=True) + es
    return e / den, es / den


def _attn_fwd(q, kv, sinks):
    seq = q.shape[0]
    nb = seq // BLOCK

    def body(sink_ref, q_ref, kvc_ref, kvp_ref, o_ref):
        n = pl.program_id(0)
        valid, distf = _attn_mask(BLOCK, n == 0)
        qb = q_ref[...]
        kk = jnp.concatenate([kvp_ref[...], kvc_ref[...]], axis=0)
        for h in range(Q_HEADS):
            kvh = h // GROUP
            qh = qb[:, HEAD_DIM * h:HEAD_DIM * (h + 1)]
            kh = kk[:, HEAD_DIM * kvh:HEAD_DIM * (kvh + 1)]
            vh = kk[:, KV_WIDTH + HEAD_DIM * kvh:KV_WIDTH + HEAD_DIM * (kvh + 1)]
            p, _ = _attn_probs(qh, kh, sink_ref[h], SLOPES[h], valid, distf)
            o_ref[:, HEAD_DIM * h:HEAD_DIM * (h + 1)] = _dot(p.astype(BF16), vh)

    return pl.pallas_call(
        body, name="attn_fwd", grid=(nb,),
        in_specs=[pl.BlockSpec(memory_space=pltpu.SMEM),
                  pl.BlockSpec((BLOCK, ATTN_WIDTH), lambda n: (n, 0)),
                  pl.BlockSpec((BLOCK, 2 * KV_WIDTH), lambda n: (n, 0)),
                  pl.BlockSpec((BLOCK, 2 * KV_WIDTH), lambda n: (jnp.maximum(n - 1, 0), 0))],
        out_specs=pl.BlockSpec((BLOCK, ATTN_WIDTH), lambda n: (n, 0)),
        out_shape=jax.ShapeDtypeStruct((seq, ATTN_WIDTH), F32),
        compiler_params=_params(),
    )(sinks, q, kv, kv)


def _pool_counts(first_row, rows, window):
    t = first_row + lax.broadcasted_iota(jnp.int32, (rows, 1), 0)
    return jnp.minimum(t + 1, window).astype(F32)


def _fwd_out0(gug, o, pool_w, pool_scale, w_out, g_post, x, ts):
    seq = x.shape[0]
    hb = ts // POOL_HALO

    def body(gug_ref, halo_ref, o_ref, pw_ref, ps_ref, w_ref, g_ref, x_ref, mix_ref, pooled_ref, y_ref, x1_ref, ubuf):
        i = pl.program_id(0)
        ga = gug_ref[:, :ATTN_WIDTH]
        u = gug_ref[:, ATTN_WIDTH:ATTN_WIDTH + POOL_WIDTH]
        gb = gug_ref[:, ATTN_WIDTH + POOL_WIDTH:]
        mix_ref[:, :ATTN_WIDTH] = (o_ref[...] * (ga * _sigmoid(ga))).astype(BF16)
        ubuf[:POOL_HALO, :] = jnp.where(i > 0, halo_ref[...], 0.0)
        ubuf[POOL_HALO:, :] = u
        silu_gb = gb * _sigmoid(gb)
        for g, window in enumerate(POOL_WINDOWS):
            lanes = slice(POOL_GC * g, POOL_GC * (g + 1))
            acc = ubuf[pl.ds(POOL_HALO, ts), lanes]
            for k in range(1, window):
                acc = acc + ubuf[pl.ds(POOL_HALO - k, ts), lanes]
            pooled = (acc / _pool_counts(i * ts, ts, window) - u[:, lanes]).astype(BF16)
            pooled_ref[:, lanes] = pooled
            ypool = _dot(pooled, pw_ref[g].astype(BF16)) * ps_ref[:, lanes]
            mix_ref[:, ATTN_WIDTH + POOL_GC * g:ATTN_WIDTH + POOL_GC * (g + 1)] = (ypool * silu_gb[:, lanes]).astype(BF16)
        y = _dot(mix_ref[...], w_ref[...])
        y_ref[...] = y
        yn, _ = _rms_fwd(y, g_ref[...])
        x1_ref[...] = x_ref[...] + yn

    return pl.pallas_call(
        body, name="fwd_out0", grid=(seq // ts,),
        in_specs=[_tile(ts, 3 * POOL_WIDTH),
                  pl.BlockSpec((POOL_HALO, POOL_WIDTH), lambda i: (jnp.maximum(i * hb - 1, 0), 1)),
                  _tile(ts, ATTN_WIDTH), _full((4, POOL_GC, POOL_GC)), _full((1, POOL_WIDTH)),
                  _full((D_MODEL, D_MODEL)), _full((1, D_MODEL)), _tile(ts, D_MODEL)],
        out_specs=[_tile(ts, D_MODEL), _tile(ts, POOL_WIDTH), _tile(ts, D_MODEL), _tile(ts, D_MODEL)],
        out_shape=[jax.ShapeDtypeStruct((seq, D_MODEL), BF16), jax.ShapeDtypeStruct((seq, POOL_WIDTH), BF16),
                   jax.ShapeDtypeStruct((seq, D_MODEL), F32), jax.ShapeDtypeStruct((seq, D_MODEL), F32)],
        scratch_shapes=[pltpu.VMEM((ts + POOL_HALO, POOL_WIDTH), F32)],
        compiler_params=_params(),
    )(gug, gug, o, pool_w, pool_scale, w_out, g_post, x)


def _fwd_in1(x1, g, w, ts):
    seq = x1.shape[0]

    def body(x_ref, g_ref, w_ref, h_ref, proj_ref, glu_ref):
        h, _ = _rms_fwd(x_ref[...], g_ref[...])
        h = h.astype(BF16)
        h_ref[...] = h
        proj = _dot_nt(h, w_ref[...])
        proj_ref[...] = proj
        glu_ref[...] = proj[:, :D_MODEL] * _sigmoid(proj[:, D_MODEL:2 * D_MODEL])

    return pl.pallas_call(
        body, name="fwd_in1", grid=(seq // ts,),
        in_specs=[_tile(ts, D_MODEL), _full((1, D_MODEL)), _full((ODD_IN, D_MODEL))],
        out_specs=[_tile(ts, D_MODEL), _tile(ts, ODD_IN), _tile(ts, D_MODEL)],
        out_shape=[jax.ShapeDtypeStruct((seq, D_MODEL), BF16), jax.ShapeDtypeStruct((seq, ODD_IN), F32),
                   jax.ShapeDtypeStruct((seq, D_MODEL), F32)],
        compiler_params=_params(),
    )(x1, g, w)


ACC_LOSS, ACC_POST, ACC_LN_G, ACC_LN_B, ACC_DW_B = range(5)
CONV_FIRST = CONV_HALO - CONV_K + 1


def _fwd_tap(offset):
    return offset - CONV_FIRST if CONV_FIRST <= offset <= CONV_HALO else None


def _bwd_tap(offset):
    return CONV_K - 1 - offset if offset < CONV_K else None


def _conv_taps(w_ref, buf_ref, ts, lanes, tap_of_offset):
    out = None
    for b in range(SUBLANES):
        rows = ts if b == 0 else ts + SUBLANES
        part = None
        for a in range(CONV_HALO // SUBLANES + 1):
            k = tap_of_offset(SUBLANES * a + b)
            if k is None:
                continue
            term = w_ref[k:k + 1, lanes] * buf_ref[pl.ds(SUBLANES * a, rows), lanes]
            part = term if part is None else part + term
        if part is None:
            continue
        if b:
            part = part[b:b + ts, :]
        out = part if out is None else out + part
    return out


def _fwd_out1(glu, proj, dw_w, dw_b, ln_g, ln_b, w_out, g_post, x1, target, ts):
    seq = x1.shape[0]
    hb = ts // CONV_HALO

    def body(glu_ref, halo_ref, gate_ref, dww_ref, dwb_ref, lng_ref, lnb_ref, w_ref, g_ref, x1_ref, t_ref,
             ymix_ref, dy_ref, dx2_ref, dcf_ref, dgate_ref, acc_ref, gbuf):
        i = pl.program_id(0)

        @pl.when(i == 0)
        def _():
            acc_ref[...] = jnp.zeros_like(acc_ref)

        gbuf[:CONV_HALO, :] = jnp.where(i > 0, halo_ref[...], 0.0)
        gbuf[CONV_HALO:, :] = glu_ref[...]
        for lb in range(D_MODEL // LANES):
            lanes = slice(LANES * lb, LANES * (lb + 1))
            dcf_ref[:, lanes] = _conv_taps(dww_ref, gbuf, ts, lanes, _fwd_tap)
        cf = dcf_ref[...] + dwb_ref[...]
        mu = jnp.mean(cf, axis=-1, keepdims=True)
        cen = cf - mu
        rs = lax.rsqrt(jnp.mean(cen * cen, axis=-1, keepdims=True) + EPS)
        xhat = cen * rs
        cn = xhat * lng_ref[...] + lnb_ref[...]
        gate = gate_ref[...]
        sg = _sigmoid(gate)
        sc = _sigmoid(cn)
        silu_gate = gate * sg
        silu_cn = cn * sc
        ymix = (silu_cn * silu_gate).astype(BF16)
        ymix_ref[...] = ymix
        y = _dot(ymix, w_ref[...])
        yn, r = _rms_fwd(y, g_ref[...])
        err = (x1_ref[...] + yn) - t_ref[...]
        acc_ref[ACC_LOSS] += _rows8(err * err)
        dx2 = err * (1.0 / D_MODEL)
        dx2_ref[...] = dx2
        dy, dpost = _rms_bwd(y, r, g_ref[...], dx2)
        acc_ref[ACC_POST] += _rows8(dpost)
        dy = dy.astype(BF16)
        dy_ref[...] = dy
        dymix = _dot_nt(dy, w_ref[...])
        dgate_ref[...] = (dymix * silu_cn * (sg * (1.0 + gate * (1.0 - sg)))).astype(BF16)
        dcn = dymix * silu_gate * (sc * (1.0 + cn * (1.0 - sc)))
        acc_ref[ACC_LN_G] += _rows8(dcn * xhat)
        acc_ref[ACC_LN_B] += _rows8(dcn)
        dxhat = dcn * lng_ref[...]
        dcf = rs * (dxhat - jnp.mean(dxhat, axis=-1, keepdims=True)
                    - xhat * jnp.mean(dxhat * xhat, axis=-1, keepdims=True))
        acc_ref[ACC_DW_B] += _rows8(dcf)
        dcf_ref[...] = dcf

    return pl.pallas_call(
        body, name="fwd_out1", grid=(seq // ts,),
        in_specs=[_tile(ts, D_MODEL),
                  pl.BlockSpec((CONV_HALO, D_MODEL), lambda i: (jnp.maximum(i * hb - 1, 0), 0)),
                  _tile(ts, D_MODEL, 2), _full((CONV_HALO, D_MODEL)), _full((1, D_MODEL)), _full((1, D_MODEL)),
                  _full((1, D_MODEL)), _full((D_MODEL, D_MODEL)), _full((1, D_MODEL)), _tile(ts, D_MODEL),
                  _tile(ts, D_MODEL)],
        out_specs=[_tile(ts, D_MODEL), _tile(ts, D_MODEL), _tile(ts, D_MODEL), _tile(ts, D_MODEL), _tile(ts, D_MODEL),
                   _full((5, SUBLANES, D_MODEL))],
        out_shape=[jax.ShapeDtypeStruct((seq, D_MODEL), BF16), jax.ShapeDtypeStruct((seq, D_MODEL), BF16),
                   jax.ShapeDtypeStruct((seq, D_MODEL), F32), jax.ShapeDtypeStruct((seq, D_MODEL), F32),
                   jax.ShapeDtypeStruct((seq, D_MODEL), BF16), jax.ShapeDtypeStruct((5, SUBLANES, D_MODEL), F32)],
        scratch_shapes=[pltpu.VMEM((ts + CONV_HALO, D_MODEL), F32)],
        compiler_params=_params(),
    )(glu, glu, proj, dw_w, dw_b, ln_g, ln_b, w_out, g_post, x1, target)


def _bwd_in1(dcf, glu, proj, dgate, dw_w, w_in, x1, g_pre, dx2, ts):
    seq = x1.shape[0]
    hb = ts // CONV_HALO
    last = seq // CONV_HALO - 1
    nt = seq // ts

    def body(dcf_ref, dnext_ref, glu_ref, gprev_ref, ab_ref, dgate_ref, dww_ref, w_ref, x_ref, g_ref, dx2_ref,
             dproj_ref, dx1_ref, ddw_ref, dpre_ref, dbuf, gbuf, zbuf, sbuf):
        i = pl.program_id(0)

        @pl.when(i == 0)
        def _():
            ddw_ref[...] = jnp.zeros_like(ddw_ref)
            dpre_ref[...] = jnp.zeros_like(dpre_ref)

        dcf = dcf_ref[...]
        dbuf[:ts, :] = dcf
        dbuf[ts:, :] = jnp.where(i < nt - 1, dnext_ref[...], 0.0)
        gbuf[:CONV_HALO, :] = jnp.where(i > 0, gprev_ref[...], 0.0)
        gbuf[CONV_HALO:, :] = glu_ref[...]
        zbuf[:SUBLANES, :] = jnp.zeros((SUBLANES, D_MODEL), F32)
        zbuf[pl.ds(SUBLANES, ts), :] = dcf
        zbuf[pl.ds(SUBLANES + ts, SUBLANES), :] = jnp.zeros((SUBLANES, D_MODEL), F32)
        for lb in range(D_MODEL // LANES):
            lanes = slice(LANES * lb, LANES * (lb + 1))
            gate_lanes = slice(D_MODEL + LANES * lb, D_MODEL + LANES * (lb + 1))
            dglu = _conv_taps(dww_ref, dbuf, ts, lanes, _bwd_tap)
            a = ab_ref[:, lanes]
            sb = _sigmoid(ab_ref[:, gate_lanes])
            dproj_ref[:, lanes] = (dglu * sb).astype(BF16)
            dproj_ref[:, gate_lanes] = (dglu * a * sb * (1.0 - sb)).astype(BF16)
            for b in range(SUBLANES):
                rows = ts if b == 0 else ts + SUBLANES
                sbuf[b, pl.ds(0, rows), :] = zbuf[pl.ds(SUBLANES - b, rows), lanes]
                for a8 in range(CONV_HALO // SUBLANES + 1):
                    k = _fwd_tap(SUBLANES * a8 + b)
                    if k is not None:
                        ddw_ref[k, :, lanes] += _rows8(sbuf[b, pl.ds(0, rows), :] * gbuf[pl.ds(SUBLANES * a8, rows), lanes])
        dproj_ref[:, 2 * D_MODEL:] = dgate_ref[...]
        dh = _dot(dproj_ref[...], w_ref[...])
        x = x_ref[...]
        r = lax.rsqrt(jnp.mean(x * x, axis=-1, keepdims=True) + EPS)
        dx, dpre = _rms_bwd(x, r, g_ref[...], dh)
        dx1_ref[...] = dx2_ref[...] + dx
        dpre_ref[...] += _rows8(dpre)

    return pl.pallas_call(
        body, name="bwd_in1", grid=(nt,),
        in_specs=[_tile(ts, D_MODEL),
                  pl.BlockSpec((CONV_HALO, D_MODEL), lambda i: (jnp.minimum((i + 1) * hb, last), 0)),
                  _tile(ts, D_MODEL),
                  pl.BlockSpec((CONV_HALO, D_MODEL), lambda i: (jnp.maximum(i * hb - 1, 0), 0)),
                  _tile(ts, 2 * D_MODEL), _tile(ts, D_MODEL), _full((CONV_HALO, D_MODEL)),
                  _full((ODD_IN, D_MODEL)), _tile(ts, D_MODEL), _full((1, D_MODEL)), _tile(ts, D_MODEL)],
        out_specs=[_tile(ts, ODD_IN), _tile(ts, D_MODEL), _full((CONV_HALO, SUBLANES, D_MODEL)), _full((SUBLANES, D_MODEL))],
        out_shape=[jax.ShapeDtypeStruct((seq, ODD_IN), BF16), jax.ShapeDtypeStruct((seq, D_MODEL), F32),
                   jax.ShapeDtypeStruct((CONV_HALO, SUBLANES, D_MODEL), F32), jax.ShapeDtypeStruct((SUBLANES, D_MODEL), F32)],
        scratch_shapes=[pltpu.VMEM((ts + CONV_HALO, D_MODEL), F32), pltpu.VMEM((ts + CONV_HALO, D_MODEL), F32),
                        pltpu.VMEM((ts + 2 * SUBLANES, D_MODEL), F32), pltpu.VMEM((SUBLANES, ts + SUBLANES, LANES), F32)],
        compiler_params=_params(),
    )(dcf, dcf, glu, glu, proj, dgate, dw_w, w_in, x1, g_pre, dx2)


def _bwd_out0(dx1, y0, g_post, w_out, gug, o, pooled, pool_w, pool_scale, ts):
    seq = dx1.shape[0]

    def body(dx1_ref, y_ref, g_ref, w_ref, gug_ref, o_ref, pooled_ref, pw_ref, ps_ref,
             dy_ref, do_ref, dgg_ref, dpooled_ref, dpost_ref, dscale_ref, dpw_ref):
        i = pl.program_id(0)

        @pl.when(i == 0)
        def _():
            dpost_ref[...] = jnp.zeros_like(dpost_ref)
            dscale_ref[...] = jnp.zeros_like(dscale_ref)
            dpw_ref[...] = jnp.zeros_like(dpw_ref)

        y = y_ref[...]
        r = lax.rsqrt(jnp.mean(y * y, axis=-1, keepdims=True) + EPS)
        dy, dpost = _rms_bwd(y, r, g_ref[...], dx1_ref[...])
        dpost_ref[...] += _rows8(dpost)
        dy = dy.astype(BF16)
        dy_ref[...] = dy
        dmix = _dot_nt(dy, w_ref[...])
        dya = dmix[:, :ATTN_WIDTH]
        dyb = dmix[:, ATTN_WIDTH:]
        ga = gug_ref[:, :ATTN_WIDTH]
        gb = gug_ref[:, ATTN_WIDTH + POOL_WIDTH:]
        sga = _sigmoid(ga)
        sgb = _sigmoid(gb)
        do_ref[...] = (dya * (ga * sga)).astype(BF16)
        dgg_ref[:, :ATTN_WIDTH] = (dya * o_ref[...] * (sga * (1.0 + ga * (1.0 - sga)))).astype(BF16)
        dypool = dyb * (gb * sgb)
        dsilu_gb = sgb * (1.0 + gb * (1.0 - sgb))
        for g in range(len(POOL_WINDOWS)):
            lanes = slice(POOL_GC * g, POOL_GC * (g + 1))
            pooled = pooled_ref[:, lanes]
            wg = pw_ref[g].astype(BF16)
            pw = _dot(pooled, wg)
            scale = ps_ref[:, lanes]
            dgg_ref[:, ATTN_WIDTH + POOL_GC * g:ATTN_WIDTH + POOL_GC * (g + 1)] = (
                dyb[:, lanes] * (pw * scale) * dsilu_gb[:, lanes]).astype(BF16)
            dscale_ref[:, lanes] += _rows8(dypool[:, lanes] * pw)
            dpw = (dypool[:, lanes] * scale).astype(BF16)
            dpooled_ref[:, lanes] = _dot_nt(dpw, wg)
            dpw_ref[g] += _dot_tn(pooled, dpw)

    return pl.pallas_call(
        body, name="bwd_out0", grid=(seq // ts,),
        in_specs=[_tile(ts, D_MODEL), _tile(ts, D_MODEL), _full((1, D_MODEL)), _full((D_MODEL, D_MODEL)),
                  _tile(ts, 3 * POOL_WIDTH), _tile(ts, ATTN_WIDTH), _tile(ts, POOL_WIDTH),
                  _full((4, POOL_GC, POOL_GC)), _full((1, POOL_WIDTH))],
        out_specs=[_tile(ts, D_MODEL), _tile(ts, ATTN_WIDTH), _tile(ts, ATTN_WIDTH + POOL_WIDTH), _tile(ts, POOL_WIDTH),
                   _full((SUBLANES, D_MODEL)), _full((SUBLANES, POOL_WIDTH)), _full((4, POOL_GC, POOL_GC))],
        out_shape=[jax.ShapeDtypeStruct((seq, D_MODEL), BF16), jax.ShapeDtypeStruct((seq, ATTN_WIDTH), BF16),
                   jax.ShapeDtypeStruct((seq, ATTN_WIDTH + POOL_WIDTH), BF16), jax.ShapeDtypeStruct((seq, POOL_WIDTH), F32),
                   jax.ShapeDtypeStruct((SUBLANES, D_MODEL), F32), jax.ShapeDtypeStruct((SUBLANES, POOL_WIDTH), F32),
                   jax.ShapeDtypeStruct((4, POOL_GC, POOL_GC), F32)],
        compiler_params=_params(),
    )(dx1, y0, g_post, w_out, gug, o, pooled, pool_w, pool_scale)


def _attn_bwd(q, kv, do, sinks, parts):
    seq = q.shape[0]
    nb = seq // BLOCK

    def qblock(j):
        return jnp.minimum(j, nb - 1)

    n = len(parts)

    def body(*refs):
        sink_ref, q_ref, kvc_ref, kvp_ref, do_ref = refs[:5]
        dq_ref, dkv_ref, dsink_ref = refs[5 + n:8 + n]
        carry, dkv_acc = refs[8 + 2 * n:10 + 2 * n]
        scatter = _Scatter(refs[5:5 + n], refs[8 + n:8 + 2 * n], *refs[10 + 2 * n:])
        j = pl.program_id(0)

        @pl.when(j == 0)
        def _():
            scatter.begin()
            dsink_ref[...] = jnp.zeros_like(dsink_ref)
            carry[...] = jnp.zeros_like(carry)

        @pl.when(j < nb)
        def _():
            valid, distf = _attn_mask(GROUP_ROWS, j == 0)
            qb = q_ref[...]
            dob = do_ref[...]
            kk = jnp.concatenate([kvp_ref[...], kvc_ref[...]], axis=0)
            lane = lax.broadcasted_iota(jnp.int32, (BLOCK, LANES), 1)
            dsink = jnp.zeros((BLOCK, LANES), F32)
            for kvh in range(Q_HEADS // GROUP):
                kh = kk[:, HEAD_DIM * kvh:HEAD_DIM * (kvh + 1)]
                vh = kk[:, KV_WIDTH + HEAD_DIM * kvh:KV_WIDTH + HEAD_DIM * (kvh + 1)]
                qg = _group_rows(qb, kvh)
                dog = _group_rows(dob, kvh)
                sink, slope = _group_columns(sink_ref, kvh)
                p, psink = _attn_probs(qg, kh, sink, slope, valid, distf)
                dp = _dot_nt(dog, vh)
                delta = jnp.sum(p * dp, axis=-1, keepdims=True)
                ds = (p * (dp - delta) * (HEAD_DIM ** -0.5)).astype(BF16)
                dsink_rows = -psink * delta
                dqg = _dot(ds, kh).astype(BF16)
                for i in range(GROUP):
                    h = GROUP * kvh + i
                    dsink = dsink + jnp.where(lane == h, dsink_rows[BLOCK * i:BLOCK * (i + 1), :], 0.0)
                    dq_ref[:, HEAD_DIM * h:HEAD_DIM * (h + 1)] = dqg[BLOCK * i:BLOCK * (i + 1), :]
                dkv_acc[:, HEAD_DIM * kvh:HEAD_DIM * (kvh + 1)] = _dot_tn(ds, qg)
                dkv_acc[:, KV_WIDTH + HEAD_DIM * kvh:KV_WIDTH + HEAD_DIM * (kvh + 1)] = _dot_tn(p.astype(BF16), dog)
            dsink_ref[...] += dsink

            @pl.when(j > 0)
            def _():
                dkv_ref[...] = (carry[...] + dkv_acc[:BLOCK, :]).astype(BF16)

            carry[...] = dkv_acc[BLOCK:, :]

        @pl.when(j == nb)
        def _():
            dkv_ref[...] = carry[...].astype(BF16)
            scatter.finish()

    hbm = pl.BlockSpec(memory_space=pl.ANY)
    out = pl.pallas_call(
        body, name="attn_bwd", grid=(nb + 1,),
        in_specs=[pl.BlockSpec(memory_space=pltpu.SMEM),
                  pl.BlockSpec((BLOCK, ATTN_WIDTH), lambda j: (qblock(j), 0)),
                  pl.BlockSpec((BLOCK, 2 * KV_WIDTH), lambda j: (qblock(j), 0)),
                  pl.BlockSpec((BLOCK, 2 * KV_WIDTH), lambda j: (jnp.maximum(qblock(j) - 1, 0), 0)),
                  pl.BlockSpec((BLOCK, ATTN_WIDTH), lambda j: (qblock(j), 0))] + [hbm] * n,
        out_specs=[pl.BlockSpec((BLOCK, ATTN_WIDTH), lambda j: (qblock(j), 0)),
                   pl.BlockSpec((BLOCK, 2 * KV_WIDTH), lambda j: (jnp.maximum(j - 1, 0), 0)),
                   _full((BLOCK, LANES))] + [hbm] * n,
        out_shape=[jax.ShapeDtypeStruct((seq, ATTN_WIDTH), BF16), jax.ShapeDtypeStruct((seq, 2 * KV_WIDTH), BF16),
                   jax.ShapeDtypeStruct((BLOCK, LANES), F32)]
        + [jax.ShapeDtypeStruct((N_DEV - 1, *p.shape[1:]), p.dtype) for p in parts],
        scratch_shapes=[pltpu.VMEM((BLOCK, 2 * KV_WIDTH), F32), pltpu.VMEM((2 * BLOCK, 2 * KV_WIDTH), F32)]
        + _Scatter.semaphores(n),
        compiler_params=_params(),
    )(sinks, q, kv, kv, do, *parts)
    return out[:3], out[3:]


def _bwd_in0(dpooled, dq, dkv, dgg, w_in, x, g_pre, dx1, ts):
    seq = x.shape[0]
    hb = ts // POOL_HALO
    last = seq // POOL_HALO - 1
    nt = seq // ts

    def body(dp_ref, dnext_ref, dq_ref, dkv_ref, dgg_ref, w_ref, x_ref, g_ref, dx1_ref,
             dproj_ref, gx_ref, dpre_ref, dbuf):
        i = pl.program_id(0)

        @pl.when(i == 0)
        def _():
            dpre_ref[...] = jnp.zeros_like(dpre_ref)

        dpool = dp_ref[...]
        dnext = jnp.where(i < nt - 1, dnext_ref[...], 0.0)
        u0 = ATTN_WIDTH + 2 * KV_WIDTH + ATTN_WIDTH
        for g, window in enumerate(POOL_WINDOWS):
            lanes = slice(POOL_GC * g, POOL_GC * (g + 1))
            dbuf[:ts, lanes] = dpool[:, lanes] / _pool_counts(i * ts, ts, window)
            dbuf[ts:, lanes] = dnext[:, lanes] / _pool_counts((i + 1) * ts, POOL_HALO, window)
        for g, window in enumerate(POOL_WINDOWS):
            lanes = slice(POOL_GC * g, POOL_GC * (g + 1))
            acc = dbuf[pl.ds(0, ts), lanes]
            for k in range(1, window):
                acc = acc + dbuf[pl.ds(k, ts), lanes]
            dproj_ref[:, u0 + POOL_GC * g:u0 + POOL_GC * (g + 1)] = (acc - dpool[:, lanes]).astype(BF16)
        dproj_ref[:, :ATTN_WIDTH] = dq_ref[...]
        dproj_ref[:, ATTN_WIDTH:ATTN_WIDTH + 2 * KV_WIDTH] = dkv_ref[...]
        dproj_ref[:, ATTN_WIDTH + 2 * KV_WIDTH:u0] = dgg_ref[:, :ATTN_WIDTH]
        dproj_ref[:, u0 + POOL_WIDTH:] = dgg_ref[:, ATTN_WIDTH:]
        dh = _dot(dproj_ref[...], w_ref[...])
        x = x_ref[...]
        r = lax.rsqrt(jnp.mean(x * x, axis=-1, keepdims=True) + EPS)
        dx, dpre = _rms_bwd(x, r, g_ref[...], dh)
        gx_ref[...] = dx1_ref[...] + dx
        dpre_ref[...] += _rows8(dpre)

    return pl.pallas_call(
        body, name="bwd_in0", grid=(nt,),
        in_specs=[_tile(ts, POOL_WIDTH),
                  pl.BlockSpec((POOL_HALO, POOL_WIDTH), lambda i: (jnp.minimum((i + 1) * hb, last), 0)),
                  _tile(ts, ATTN_WIDTH), _tile(ts, 2 * KV_WIDTH), _tile(ts, ATTN_WIDTH + POOL_WIDTH),
                  _full((EVEN_IN, D_MODEL)), _tile(ts, D_MODEL), _full((1, D_MODEL)), _tile(ts, D_MODEL)],
        out_specs=[_tile(ts, EVEN_IN), _tile(ts, D_MODEL), _full((SUBLANES, D_MODEL))],
        out_shape=[jax.ShapeDtypeStruct((seq, EVEN_IN), BF16), jax.ShapeDtypeStruct((seq, D_MODEL), F32),
                   jax.ShapeDtypeStruct((SUBLANES, D_MODEL), F32)],
        scratch_shapes=[pltpu.VMEM((ts + POOL_HALO, POOL_WIDTH), F32)],
        compiler_params=_params(),
    )(dpooled, dpooled, dq, dkv, dgg, w_in, x, g_pre, dx1)


def _matmul_tn(a, b, name, ts, tm):
    seq, m = a.shape
    n = b.shape[1]
    steps = seq // ts

    def body(a_ref, b_ref, o_ref, acc):
        s = pl.program_id(1)

        @pl.when(s == 0)
        def _():
            acc[...] = jnp.zeros_like(acc)

        acc[...] += _dot_tn(a_ref[...], b_ref[...])

        @pl.when(s == steps - 1)
        def _():
            o_ref[...] = acc[...].astype(BF16)

    return pl.pallas_call(
        body, name=name, grid=(m // tm, steps),
        in_specs=[pl.BlockSpec((ts, tm), lambda j, s: (s, j)), pl.BlockSpec((ts, n), lambda j, s: (s, 0))],
        out_specs=pl.BlockSpec((tm, n), lambda j, s: (j, 0)),
        out_shape=jax.ShapeDtypeStruct((m, n), BF16),
        scratch_shapes=[pltpu.VMEM((tm, n), F32)],
        compiler_params=pltpu.CompilerParams(dimension_semantics=("arbitrary", "arbitrary"), vmem_limit_bytes=VMEM_LIMIT),
    )(a, b)


def _adamw_math(w, g, m, v):
    m = ADAM_B1 * m + (1.0 - ADAM_B1) * g
    v = ADAM_B2 * v + (1.0 - ADAM_B2) * (g * g)
    m_hat = m / (1.0 - ADAM_B1 ** ADAM_STEP)
    v_hat = v / (1.0 - ADAM_B2 ** ADAM_STEP)
    delta = -ADAM_LR * (m_hat / (jnp.sqrt(v_hat) + ADAM_EPS) + ADAM_WD * w)
    return delta, m, v


def _adamw(ws, gs, ms, vs, name):
    n = len(ws)

    def body(*refs):
        ins, outs = refs[:4 * n], refs[4 * n:]
        for k in range(n):
            delta, m, v = _adamw_math(ins[k][...], ins[n + k][...], ins[2 * n + k][...], ins[3 * n + k][...])
            outs[k][...] = delta
            outs[n + k][...] = m
            outs[2 * n + k][...] = v

    shapes = [jax.ShapeDtypeStruct(w.shape, F32) for w in ws]
    out = pl.pallas_call(body, name=name, out_shape=shapes * 3,
                         compiler_params=pltpu.CompilerParams(vmem_limit_bytes=VMEM_LIMIT))(*ws, *gs, *ms, *vs)
    return out[:n], out[n:2 * n], out[2 * n:]


TS_MATMUL, TS_CONV, TS_GRAD = 512, 256, 1024


def kernel(x, pre_norm, post_norm, a_w_in, a_sinks, b_pool_w, b_pool_scale, ab_w_out, c_w_in, c_dw_w, c_dw_b, c_ln_g, c_ln_b, c_w_out, loss_target, m_pre_norm, m_post_norm, m_a_w_in, m_a_sinks, m_b_pool_w, m_b_pool_scale, m_ab_w_out, m_c_w_in, m_c_dw_w, m_c_dw_b, m_c_ln_g, m_c_ln_b, m_c_w_out, v_pre_norm, v_post_norm, v_a_w_in, v_a_sinks, v_b_pool_w, v_b_pool_scale, v_ab_w_out, v_c_w_in, v_c_dw_w, v_c_dw_b, v_c_ln_g, v_c_ln_b, v_c_w_out):
    seq = x.shape[1]
    ts_big, ts_conv, ts_grad = min(TS_MATMUL, seq), min(TS_CONV, seq), min(TS_GRAD, seq)
    x2d = x[0]
    target = loss_target[0]
    ch = c_dw_b.shape[1]

    whole = lambda g: g.reshape(-1, D_MODEL)
    w_in0t = whole(_all_gather([a_w_in[0].T.astype(BF16)], "gather_w_in0")[0])
    later = [ab_w_out[0].astype(BF16), c_w_in[0].T.astype(BF16), c_w_out[0].astype(BF16)]

    vec_rows = 40
    vecs = jnp.concatenate([c_dw_w[0, :, 0, :], c_dw_b, c_ln_g, c_ln_b, jnp.zeros((vec_rows - CONV_K - 3, ch), F32)], axis=0)
    vg, = _all_gather([vecs], "gather_vectors")
    vg = vg.transpose(1, 0, 2).reshape(vec_rows, D_MODEL)
    dw_w = vg[:CONV_HALO]
    dw_b, ln_g, ln_b = vg[CONV_K:CONV_K + 1], vg[CONV_K + 1:CONV_K + 2], vg[CONV_K + 2:CONV_K + 3]

    sinks = a_sinks[0]
    (h0, q, kv, gug), gathered = _fwd_in0(x2d, pre_norm[0:1], w_in0t, later, ts_big)
    w_out0, w_in1t, w_out1 = [whole(g) for g in gathered]
    o = _attn_fwd(q, kv, sinks)
    mix0, pooled, y0, x1 = _fwd_out0(gug, o, b_pool_w[0], b_pool_scale, w_out0, post_norm[0:1], x2d, ts_big)
    h1, proj1, glu = _fwd_in1(x1, pre_norm[1:2], w_in1t, ts_big)
    ymix1, dy1, dx2, dcf, dgate, acc1 = _fwd_out1(glu, proj1, dw_w, dw_b, ln_g, ln_b, w_out1, post_norm[1:2], x1, target, ts_conv)

    dproj1, dx1, ddw_w, dpre1 = _bwd_in1(dcf, glu, proj1, dgate, dw_w, w_in1t, x1, pre_norm[1:2], dx2, ts_conv)
    g_in1t = _matmul_tn(dproj1, h1, "grad_w_in1", ts_grad, 1024)
    g_out1 = _matmul_tn(ymix1, dy1, "grad_w_out1", ts_grad, 1024)
    dy0, do, dgg, dpooled, dpost0, dscale, dpool_w = _bwd_out0(dx1, y0, post_norm[0:1], w_out0, gug, o, pooled, b_pool_w[0], b_pool_scale, ts_big)
    slabs = lambda g: g.reshape(N_DEV, -1, D_MODEL)
    me = 4 * lax.axis_index("x") + 2 * lax.axis_index("y") + lax.axis_index("c")
    parts1 = [slabs(g_in1t), slabs(g_out1)]
    (dq, dkv, dsink), recv1 = _attn_bwd(q, kv, do, sinks, parts1)
    g_in1t, g_c_w_out = [_sum_slabs(lax.dynamic_index_in_dim(p, me, keepdims=False), r, name)
                         for p, r, name in zip(parts1, recv1, ("sum_w_in1", "sum_w_out1"))]
    dproj0, grad_x, dpre0 = _bwd_in0(dpooled, dq, dkv, dgg, w_in0t, x2d, pre_norm[0:1], dx1, ts_big)
    g_in0t = _matmul_tn(dproj0, h0, "grad_w_in0", ts_grad, 768)
    g_out0 = _matmul_tn(mix0, dy0, "grad_w_out0", ts_grad, 1024)

    g_in0t, g_ab_w_out = _reduce_chips(_reduce_cores([slabs(g_in0t), slabs(g_out0)], "reduce_cores"), "reduce_chips")
    g_a_w_in, g_c_w_in = g_in0t.T, g_in1t.T

    row = lambda a: jnp.sum(a, axis=0, keepdims=True)
    vec_g = jnp.concatenate([jnp.sum(ddw_w[:CONV_K], axis=1), row(acc1[ACC_DW_B]), row(acc1[ACC_LN_G]), row(acc1[ACC_LN_B]),
                             jnp.zeros((vec_rows - CONV_K - 3, D_MODEL), F32)], axis=0)
    vec_g = _exchange_sum(vec_g.reshape(vec_rows, N_DEV, ch).transpose(1, 0, 2), "reduce_vectors")
    g_dw_w, g_dw_b, g_ln_g, g_ln_b = vec_g[:CONV_K], vec_g[CONV_K:CONV_K + 1], vec_g[CONV_K + 1:CONV_K + 2], vec_g[CONV_K + 2:CONV_K + 3]

    lanes8 = lambda a: row(a).reshape(-1, LANES)
    loss_row = jnp.pad(jnp.sum(acc1[ACC_LOSS]).reshape(1, 1), ((0, 0), (0, LANES - 1)))
    rep = jnp.concatenate([lanes8(dpre0), lanes8(dpre1), lanes8(dpost0), lanes8(acc1[ACC_POST]), lanes8(dscale),
                           loss_row, jnp.zeros((2, LANES), F32), row(dsink), dpool_w.reshape(4 * POOL_GC, LANES)], axis=0)
    rep = _all_reduce(rep, "reduce_replicated")
    g_pre = rep[:16].reshape(2, D_MODEL)
    g_post = rep[16:32].reshape(2, D_MODEL)
    g_scale = rep[32:36].reshape(1, POOL_WIDTH)
    loss = (0.5 / D_MODEL) * rep[36, 0]
    g_sinks = rep[39:40, :Q_HEADS]
    g_pool_w = rep[40:]

    grads = [g_pre, g_post, g_a_w_in, g_sinks, g_pool_w, g_scale, g_ab_w_out, g_c_w_in, g_dw_w, g_dw_b, g_ln_g, g_ln_b, g_c_w_out]
    weights = [pre_norm, post_norm, a_w_in, a_sinks, b_pool_w, b_pool_scale, ab_w_out, c_w_in, c_dw_w, c_dw_b, c_ln_g, c_ln_b, c_w_out]
    m_in = [m_pre_norm, m_post_norm, m_a_w_in, m_a_sinks, m_b_pool_w, m_b_pool_scale, m_ab_w_out, m_c_w_in, m_c_dw_w, m_c_dw_b, m_c_ln_g, m_c_ln_b, m_c_w_out]
    v_in = [v_pre_norm, v_post_norm, v_a_w_in, v_a_sinks, v_b_pool_w, v_b_pool_scale, v_ab_w_out, v_c_w_in, v_c_dw_w, v_c_dw_b, v_c_ln_g, v_c_ln_b, v_c_w_out]
    flat = lambda arrs: [a.reshape(g.shape) for a, g in zip(arrs, grads)]
    big = (2, 6, 7, 12)
    small = tuple(k for k in range(len(grads)) if k not in big)
    pick = lambda arrs, idx: [arrs[k] for k in idx]
    deltas, new_m, new_v = [None] * 13, [None] * 13, [None] * 13
    for idx, name in ((big, "adamw_matrices"), (small, "adamw_vectors")):
        d, m, v = _adamw(pick(flat(weights), idx), pick(grads, idx), pick(flat(m_in), idx), pick(flat(v_in), idx), name)
        for k, dk, mk, vk in zip(idx, d, m, v):
            deltas[k], new_m[k], new_v[k] = dk, mk, vk
    shaped = lambda arrs: [a.reshape(w.shape) for a, w in zip(arrs, weights)]
    return (loss, grad_x[None], *shaped(grads), *shaped(deltas), *shaped(new_m), *shaped(new_v))
```

```python
import functools

import jax
import jax.numpy as jnp
from jax import lax
from jax.experimental import pallas as pl
from jax.experimental.pallas import tpu as pltpu

F32 = jnp.float32
BF16 = jnp.bfloat16
MESH = pl.DeviceIdType.MESH
AXES = ("x", "y", "c")
N_DEV = 8

D_MODEL = 1024
HEAD_DIM = 64
Q_HEADS = 8
GROUP = 4
ATTN_WIDTH = 512
KV_WIDTH = 128
BLOCK = 128
POOL_WIDTH = 512
POOL_WINDOWS = (2, 4, 8, 16)
POOL_GC = 128
POOL_HALO = 16
EVEN_IN = 2304
CONV_K = 31
CONV_HALO = 32
ODD_IN = 3072
EPS = 1e-6
NEG = -1e30
SLOPES = tuple(2.0 ** (-8.0 * (h + 1) / Q_HEADS) for h in range(Q_HEADS))

ADAM_LR = 0.001
ADAM_B1 = 0.9
ADAM_B2 = 0.999
ADAM_EPS = 1e-08
ADAM_WD = 0.01
ADAM_STEP = 10

SUBLANES = 8
LANES = 128
VMEM_LIMIT = 56 * 1024 * 1024

NT = (((1,), (1,)), ((), ()))
TN = (((0,), (0,)), ((), ()))


def _params(**kw):
    return pltpu.CompilerParams(dimension_semantics=("arbitrary",), vmem_limit_bytes=VMEM_LIMIT, **kw)


def _dot(a, b):
    return jnp.dot(a, b, preferred_element_type=F32)


def _dot_nt(a, b):
    return lax.dot_general(a, b, NT, preferred_element_type=F32)


def _dot_tn(a, b):
    return lax.dot_general(a, b, TN, preferred_element_type=F32)


def _sigmoid(v):
    return 1.0 / (1.0 + jnp.exp(-v))


def _rows8(v):
    r, c = v.shape
    return jnp.sum(v.reshape(r // SUBLANES, SUBLANES, c), axis=0)


def _rms_fwd(v, g):
    r = lax.rsqrt(jnp.mean(v * v, axis=-1, keepdims=True) + EPS)
    return v * r * g, r


def _rms_bwd(v, r, g, dout):
    gd = dout * g
    dv = r * gd - v * (r * r * r) * jnp.mean(v * gd, axis=-1, keepdims=True)
    return dv, dout * (v * r)


def _full(shape):
    return pl.BlockSpec(shape, lambda i: (0,) * len(shape))


def _tile(ts, cols, col_block=0):
    return pl.BlockSpec((ts, cols), lambda i: (i, col_block))


def _position():
    return lax.axis_index("x"), lax.axis_index("y"), lax.axis_index("c")


class _Gather:
    def __init__(self, x_refs, out_refs, send_sems, recv_sems, local_sems):
        self.x_refs, self.out_refs = x_refs, out_refs
        self.send_sems, self.recv_sems, self.local_sems = send_sems, recv_sems, local_sems
        self.n = len(x_refs)
        x, y, c = _position()
        self.c = c
        self.me, self.sibling = (x, y, c), (x, y, 1 - c)
        self.chips = [(1 - x, y), (x, 1 - y), (1 - x, 1 - y)]

    def copy(self, k, j, owner, to, src=None):
        slab = self.out_refs[k].at[4 * owner[0] + 2 * owner[1] + owner[2]]
        return pltpu.make_async_remote_copy(
            src_ref=slab if src is None else src, dst_ref=slab, send_sem=self.send_sems.at[7 * k + j],
            recv_sem=self.recv_sems.at[7 * k + j], device_id=to, device_id_type=MESH)

    def mine(self, k):
        return pltpu.make_async_copy(self.x_refs[k], self.out_refs[k].at[4 * self.me[0] + 2 * self.me[1] + self.c],
                                     self.local_sems.at[k])

    def first(self, k):
        out = [self.copy(k, 0, self.me, self.sibling, src=self.x_refs[k])]
        return out + [self.copy(k, 1 + j, self.me, (*chip, self.c), src=self.x_refs[k]) for j, chip in enumerate(self.chips)]

    def begin(self):
        for k in range(self.n):
            self.mine(k).start()
            for cp in self.first(k):
                cp.start()

    def finish(self):
        passed = []
        for k in range(self.n):
            for j, chip in enumerate(self.chips):
                self.copy(k, 1 + j, (*chip, self.c), self.me).wait_recv()
                passed.append(self.copy(k, 4 + j, (*chip, self.c), self.sibling))
                passed[-1].start()
        for k in range(self.n):
            self.copy(k, 0, self.sibling, self.me).wait_recv()
            for j, chip in enumerate(self.chips):
                self.copy(k, 4 + j, (*chip, 1 - self.c), self.me).wait_recv()
        for k in range(self.n):
            for cp in self.first(k):
                cp.wait_send()
        for cp in passed:
            cp.wait_send()
        for k in range(self.n):
            self.mine(k).wait()

    @staticmethod
    def semaphores(n):
        return [pltpu.SemaphoreType.DMA((7 * n,)), pltpu.SemaphoreType.DMA((7 * n,)), pltpu.SemaphoreType.DMA((n,))]


def _all_gather(blocks, name):
    n = len(blocks)

    def body(*refs):
        gather = _Gather(refs[:n], refs[n:2 * n], *refs[2 * n:])
        gather.begin()
        gather.finish()

    return pl.pallas_call(
        body, name=name,
        out_shape=[jax.ShapeDtypeStruct((N_DEV, *b.shape), b.dtype) for b in blocks],
        in_specs=[pl.BlockSpec(memory_space=pltpu.VMEM)] * n,
        out_specs=[pl.BlockSpec(memory_space=pltpu.VMEM)] * n,
        scratch_shapes=_Gather.semaphores(n),
        compiler_params=pltpu.CompilerParams(vmem_limit_bytes=VMEM_LIMIT),
    )(*blocks)


class _Scatter:
    def __init__(self, part_refs, recv_refs, send_sems, recv_sems):
        self.part_refs, self.recv_refs, self.send_sems, self.recv_sems = part_refs, recv_refs, send_sems, recv_sems
        self.n = len(part_refs)

    def copies(self):
        x, y, c = _position()
        me = 4 * x + 2 * y + c
        out = []
        for k in range(self.n):
            for j in range(N_DEV - 1):
                d = j + 1
                out.append(pltpu.make_async_remote_copy(
                    src_ref=self.part_refs[k].at[me ^ d], dst_ref=self.recv_refs[k].at[j],
                    send_sem=self.send_sems.at[7 * k + j], recv_sem=self.recv_sems.at[7 * k + j],
                    device_id=(x ^ (d >> 2), y ^ ((d >> 1) & 1), c ^ (d & 1)), device_id_type=MESH))
        return out

    def begin(self):
        for cp in self.copies():
            cp.start()

    def finish(self):
        for cp in self.copies():
            cp.wait_recv()
        for cp in self.copies():
            cp.wait_send()

    @staticmethod
    def semaphores(n):
        return [pltpu.SemaphoreType.DMA((7 * n,)), pltpu.SemaphoreType.DMA((7 * n,))]


def _sum_slabs(own, recv, name):
    rows, cols = own.shape

    def body(own_ref, recv_ref, out_ref):
        acc = own_ref[...].astype(F32)
        for j in range(N_DEV - 1):
            acc = acc + recv_ref[j].astype(F32)
        out_ref[...] = acc

    return pl.pallas_call(body, name=name, out_shape=jax.ShapeDtypeStruct((rows, cols), F32))(own, recv)


N_CHIPS = 4


def _final_reduce(parts, vec_parts, rep, name):
    _, rows, cols = parts.shape
    vrows, prows = vec_parts.shape[1], rep.shape[0]
    peers = N_CHIPS - 1

    def body(parts_ref, vec_ref, rep_ref, out_ref, vec_out, rep_out, recv_a, own_a, mid, recv_b, vec_recv, rep_all,
             a_send, a_recv, a_local, b_send, b_recv, v_send, v_recv, r_send, r_recv):
        x, y, c = _position()
        chip = 2 * x + y
        me = 2 * chip + c
        everyone = [(d, (x ^ (d >> 2), y ^ ((d >> 1) & 1), c ^ (d & 1))) for d in range(1, N_DEV)]

        stage1, loads = [], []
        for j in range(N_CHIPS):
            stage1.append(pltpu.make_async_remote_copy(
                src_ref=parts_ref.at[2 * j + 1 - c], dst_ref=recv_a.at[j], send_sem=a_send.at[j], recv_sem=a_recv.at[j],
                device_id=(x, y, 1 - c), device_id_type=MESH))
            loads.append(pltpu.make_async_copy(parts_ref.at[2 * j + c], own_a.at[j], a_local.at[j]))
            stage1[-1].start()
            loads[-1].start()
        small = []
        for d, to in everyone:
            small.append(pltpu.make_async_remote_copy(
                src_ref=vec_ref.at[me ^ d], dst_ref=vec_recv.at[d - 1], send_sem=v_send.at[d - 1], recv_sem=v_recv.at[d - 1],
                device_id=to, device_id_type=MESH))
            small.append(pltpu.make_async_remote_copy(
                src_ref=rep_ref, dst_ref=rep_all.at[me], send_sem=r_send.at[d - 1], recv_sem=r_recv.at[d - 1],
                device_id=to, device_id_type=MESH))
        for cp in small:
            cp.start()
        rep_all[me] = rep_ref[...]

        for j in range(N_CHIPS):
            loads[j].wait()
            stage1[j].wait_recv()
            mid[j] = (own_a[j].astype(F32) + recv_a[j].astype(F32)).astype(BF16)
        stage2 = []
        for d in range(1, N_CHIPS):
            stage2.append(pltpu.make_async_remote_copy(
                src_ref=mid.at[chip ^ d], dst_ref=recv_b.at[d - 1], send_sem=b_send.at[d - 1], recv_sem=b_recv.at[d - 1],
                device_id=(x ^ (d >> 1), y ^ (d & 1), c), device_id_type=MESH))
            stage2[-1].start()

        for cp in small:
            cp.wait_recv()
        vec_sum = vec_ref[me]
        for j in range(N_DEV - 1):
            vec_sum = vec_sum + vec_recv[j]
        vec_out[...] = vec_sum
        rep_sum = rep_all[0]
        for d in range(1, N_DEV):
            rep_sum = rep_sum + rep_all[d]
        rep_out[...] = rep_sum

        acc = mid[chip].astype(F32)
        for d in range(peers):
            stage2[d].wait_recv()
            acc = acc + recv_b[d].astype(F32)
        out_ref[...] = acc
        for cp in stage1 + stage2 + small:
            cp.wait_send()

    vmem = pl.BlockSpec(memory_space=pltpu.VMEM)
    dma = pltpu.SemaphoreType.DMA
    return pl.pallas_call(
        body, name=name,
        out_shape=[jax.ShapeDtypeStruct((rows, cols), F32), jax.ShapeDtypeStruct((vrows, LANES), F32),
                   jax.ShapeDtypeStruct((prows, LANES), F32)],
        in_specs=[pl.BlockSpec(memory_space=pl.ANY), vmem, vmem],
        out_specs=[vmem, vmem, vmem],
        scratch_shapes=[pltpu.VMEM((N_CHIPS, rows, cols), BF16), pltpu.VMEM((N_CHIPS, rows, cols), BF16),
                        pltpu.VMEM((N_CHIPS, rows, cols), BF16), pltpu.VMEM((peers, rows, cols), BF16),
                        pltpu.VMEM((N_DEV - 1, vrows, LANES), F32), pltpu.VMEM((N_DEV, prows, LANES), F32),
                        dma((N_CHIPS,)), dma((N_CHIPS,)), dma((N_CHIPS,)), dma((peers,)), dma((peers,)),
                        dma((N_DEV - 1,)), dma((N_DEV - 1,)), dma((N_DEV - 1,)), dma((N_DEV - 1,))],
        compiler_params=pltpu.CompilerParams(vmem_limit_bytes=VMEM_LIMIT),
    )(parts, vec_parts, rep)


def _fwd_in0(x, g, w, shards, ts):
    seq = x.shape[0]
    n = len(shards)
    steps = seq // ts

    def body(*refs):
        x_ref, g_ref, w_ref = refs[:3]
        h_ref, q_ref, kv_ref, gug_ref = refs[3 + n:7 + n]
        gather = lambda: _Gather(refs[3:3 + n], refs[7 + n:7 + 2 * n], *refs[7 + 2 * n:])
        i = pl.program_id(0)

        @pl.when(i == 0)
        def _():
            gather().begin()

        h, _ = _rms_fwd(x_ref[...], g_ref[...])
        h = h.astype(BF16)
        h_ref[...] = h
        proj = _dot_nt(h, w_ref[...])
        q_ref[...] = proj[:, :ATTN_WIDTH].astype(BF16)
        kv_ref[...] = proj[:, ATTN_WIDTH:ATTN_WIDTH + 2 * KV_WIDTH].astype(BF16)
        gug_ref[...] = proj[:, ATTN_WIDTH + 2 * KV_WIDTH:]

        @pl.when(i == steps - 1)
        def _():
            gather().finish()

    hbm = pl.BlockSpec(memory_space=pl.ANY)
    out = pl.pallas_call(
        body, name="fwd_in0", grid=(steps,),
        in_specs=[_tile(ts, D_MODEL), _full((1, D_MODEL)), _full((EVEN_IN, D_MODEL))] + [hbm] * n,
        out_specs=[_tile(ts, D_MODEL), _tile(ts, ATTN_WIDTH), _tile(ts, 2 * KV_WIDTH), _tile(ts, 3 * POOL_WIDTH)] + [hbm] * n,
        out_shape=[jax.ShapeDtypeStruct((seq, D_MODEL), BF16), jax.ShapeDtypeStruct((seq, ATTN_WIDTH), BF16),
                   jax.ShapeDtypeStruct((seq, 2 * KV_WIDTH), BF16), jax.ShapeDtypeStruct((seq, 3 * POOL_WIDTH), F32)]
        + [jax.ShapeDtypeStruct((N_DEV, *b.shape), b.dtype) for b in shards],
        scratch_shapes=_Gather.semaphores(n),
        compiler_params=_params(),
    )(x, g, w, *shards)
    return out[:4], out[4:]


GROUP_ROWS = GROUP * BLOCK


def _attn_mask(rows, first_block):
    row = lax.broadcasted_iota(jnp.int32, (rows, 2 * BLOCK), 0) & (BLOCK - 1)
    col = lax.broadcasted_iota(jnp.int32, (rows, 2 * BLOCK), 1)
    dist = row + BLOCK - col
    valid = (dist >= 0) & (dist < BLOCK) & ((col >= BLOCK) | jnp.logical_not(first_block))
    return valid, dist.astype(F32)


def _group_rows(block, kvh):
    return jnp.concatenate([block[:, HEAD_DIM * h:HEAD_DIM * (h + 1)] for h in range(GROUP * kvh, GROUP * (kvh + 1))], axis=0)


def _group_columns(sink_ref, kvh):
    head = lax.broadcasted_iota(jnp.int32, (GROUP_ROWS, 1), 0) // BLOCK
    sink = jnp.zeros((GROUP_ROWS, 1), F32)
    slope = jnp.zeros((GROUP_ROWS, 1), F32)
    for i in range(GROUP):
        sink = jnp.where(head == i, sink_ref[GROUP * kvh + i], sink)
        slope = jnp.where(head == i, SLOPES[GROUP * kvh + i], slope)
    return sink, slope


def _attn_probs(qh, kh, sink, slope, valid, distf):
    s = _dot_nt(qh, kh) * (HEAD_DIM ** -0.5)
    s = jnp.where(valid, s - slope * distf, NEG)
    mx = jnp.maximum(jnp.max(s, axis=-1, keepdims=True), sink)
    e = jnp.exp(s - mx)
    es = jnp.exp(sink - mx)
    den = jnp.sum(e, axis=-1, keepdims=True) + es
    return e / den, es / den


def _attn_fwd(q, kv, sinks, shards):
    seq = q.shape[0]
    nb = seq // BLOCK
    ns = len(shards)

    def body(*refs):
        sink_ref, q_ref, kvc_ref, kvp_ref = refs[:4]
        o_ref = refs[4 + ns]
        gather = lambda: _Gather(refs[4:4 + ns], refs[5 + ns:5 + 2 * ns], *refs[5 + 2 * ns:])
        n = pl.program_id(0)

        @pl.when(n == 0)
        def _():
            gather().begin()

        valid, distf = _attn_mask(BLOCK, n == 0)
        qb = q_ref[...]
        kk = jnp.concatenate([kvp_ref[...], kvc_ref[...]], axis=0)
        for h in range(Q_HEADS):
            kvh = h // GROUP
            qh = qb[:, HEAD_DIM * h:HEAD_DIM * (h + 1)]
            kh = kk[:, HEAD_DIM * kvh:HEAD_DIM * (kvh + 1)]
            vh = kk[:, KV_WIDTH + HEAD_DIM * kvh:KV_WIDTH + HEAD_DIM * (kvh + 1)]
            p, _ = _attn_probs(qh, kh, sink_ref[h], SLOPES[h], valid, distf)
            o_ref[:, HEAD_DIM * h:HEAD_DIM * (h + 1)] = _dot(p.astype(BF16), vh)

        @pl.when(n == nb - 1)
        def _():
            gather().finish()

    hbm = pl.BlockSpec(memory_space=pl.ANY)
    out = pl.pallas_call(
        body, name="attn_fwd", grid=(nb,),
        in_specs=[pl.BlockSpec(memory_space=pltpu.SMEM),
                  pl.BlockSpec((BLOCK, ATTN_WIDTH), lambda n: (n, 0)),
                  pl.BlockSpec((BLOCK, 2 * KV_WIDTH), lambda n: (n, 0)),
                  pl.BlockSpec((BLOCK, 2 * KV_WIDTH), lambda n: (jnp.maximum(n - 1, 0), 0))] + [hbm] * ns,
        out_specs=[pl.BlockSpec((BLOCK, ATTN_WIDTH), lambda n: (n, 0))] + [hbm] * ns,
        out_shape=[jax.ShapeDtypeStruct((seq, ATTN_WIDTH), F32)]
        + [jax.ShapeDtypeStruct((N_DEV, *b.shape), b.dtype) for b in shards],
        scratch_shapes=_Gather.semaphores(ns),
        compiler_params=_params(),
    )(sinks, q, kv, kv, *shards)
    return out[0], out[1:]


def _pool_counts(first_row, rows, window):
    t = first_row + lax.broadcasted_iota(jnp.int32, (rows, 1), 0)
    return jnp.minimum(t + 1, window).astype(F32)


def _fwd_out0(gug, o, pool_w, pool_scale, w_out, g_post, x, ts):
    seq = x.shape[0]
    hb = ts // POOL_HALO

    def body(gug_ref, halo_ref, o_ref, pw_ref, ps_ref, w_ref, g_ref, x_ref, mix_ref, pooled_ref, y_ref, x1_ref, ubuf):
        i = pl.program_id(0)
        ga = gug_ref[:, :ATTN_WIDTH]
        u = gug_ref[:, ATTN_WIDTH:ATTN_WIDTH + POOL_WIDTH]
        gb = gug_ref[:, ATTN_WIDTH + POOL_WIDTH:]
        mix_ref[:, :ATTN_WIDTH] = (o_ref[...] * (ga * _sigmoid(ga))).astype(BF16)
        ubuf[:POOL_HALO, :] = jnp.where(i > 0, halo_ref[...], 0.0)
        ubuf[POOL_HALO:, :] = u
        silu_gb = gb * _sigmoid(gb)
        for g, window in enumerate(POOL_WINDOWS):
            lanes = slice(POOL_GC * g, POOL_GC * (g + 1))
            acc = ubuf[pl.ds(POOL_HALO, ts), lanes]
            for k in range(1, window):
                acc = acc + ubuf[pl.ds(POOL_HALO - k, ts), lanes]
            pooled = (acc / _pool_counts(i * ts, ts, window) - u[:, lanes]).astype(BF16)
            pooled_ref[:, lanes] = pooled
            ypool = _dot(pooled, pw_ref[g].astype(BF16)) * ps_ref[:, lanes]
            mix_ref[:, ATTN_WIDTH + POOL_GC * g:ATTN_WIDTH + POOL_GC * (g + 1)] = (ypool * silu_gb[:, lanes]).astype(BF16)
        y = _dot(mix_ref[...], w_ref[...])
        y_ref[...] = y
        yn, _ = _rms_fwd(y, g_ref[...])
        x1_ref[...] = x_ref[...] + yn

    return pl.pallas_call(
        body, name="fwd_out0", grid=(seq // ts,),
        in_specs=[_tile(ts, 3 * POOL_WIDTH),
                  pl.BlockSpec((POOL_HALO, POOL_WIDTH), lambda i: (jnp.maximum(i * hb - 1, 0), 1)),
                  _tile(ts, ATTN_WIDTH), _full((4, POOL_GC, POOL_GC)), _full((1, POOL_WIDTH)),
                  _full((D_MODEL, D_MODEL)), _full((1, D_MODEL)), _tile(ts, D_MODEL)],
        out_specs=[_tile(ts, D_MODEL), _tile(ts, POOL_WIDTH), _tile(ts, D_MODEL), _tile(ts, D_MODEL)],
        out_shape=[jax.ShapeDtypeStruct((seq, D_MODEL), BF16), jax.ShapeDtypeStruct((seq, POOL_WIDTH), BF16),
                   jax.ShapeDtypeStruct((seq, D_MODEL), F32), jax.ShapeDtypeStruct((seq, D_MODEL), F32)],
        scratch_shapes=[pltpu.VMEM((ts + POOL_HALO, POOL_WIDTH), F32)],
        compiler_params=_params(),
    )(gug, gug, o, pool_w, pool_scale, w_out, g_post, x)


def _fwd_in1(x1, g, w, ts):
    seq = x1.shape[0]

    def body(x_ref, g_ref, w_ref, h_ref, proj_ref, glu_ref):
        h, _ = _rms_fwd(x_ref[...], g_ref[...])
        h = h.astype(BF16)
        h_ref[...] = h
        proj = _dot_nt(h, w_ref[...])
        proj_ref[...] = proj
        glu_ref[...] = proj[:, :D_MODEL] * _sigmoid(proj[:, D_MODEL:2 * D_MODEL])

    return pl.pallas_call(
        body, name="fwd_in1", grid=(seq // ts,),
        in_specs=[_tile(ts, D_MODEL), _full((1, D_MODEL)), _full((ODD_IN, D_MODEL))],
        out_specs=[_tile(ts, D_MODEL), _tile(ts, ODD_IN), _tile(ts, D_MODEL)],
        out_shape=[jax.ShapeDtypeStruct((seq, D_MODEL), BF16), jax.ShapeDtypeStruct((seq, ODD_IN), F32),
                   jax.ShapeDtypeStruct((seq, D_MODEL), F32)],
        compiler_params=_params(),
    )(x1, g, w)


ACC_LOSS, ACC_POST, ACC_LN_G, ACC_LN_B, ACC_DW_B = range(5)
CONV_FIRST = CONV_HALO - CONV_K + 1


def _fwd_tap(offset):
    return offset - CONV_FIRST if CONV_FIRST <= offset <= CONV_HALO else None


def _bwd_tap(offset):
    return CONV_K - 1 - offset if offset < CONV_K else None


def _conv_taps(w_ref, buf_ref, ts, lanes, tap_of_offset):
    out = None
    for b in range(SUBLANES):
        rows = ts if b == 0 else ts + SUBLANES
        part = None
        for a in range(CONV_HALO // SUBLANES + 1):
            k = tap_of_offset(SUBLANES * a + b)
            if k is None:
                continue
            term = w_ref[k:k + 1, lanes] * buf_ref[pl.ds(SUBLANES * a, rows), lanes]
            part = term if part is None else part + term
        if part is None:
            continue
        if b:
            part = part[b:b + ts, :]
        out = part if out is None else out + part
    return out


def _fwd_out1(glu, proj, dw_w, dw_b, ln_g, ln_b, w_out, g_post, x1, target, ts):
    seq = x1.shape[0]
    hb = ts // CONV_HALO

    def body(glu_ref, halo_ref, gate_ref, dww_ref, dwb_ref, lng_ref, lnb_ref, w_ref, g_ref, x1_ref, t_ref,
             ymix_ref, dy_ref, dx2_ref, dcf_ref, dgate_ref, acc_ref, gbuf):
        i = pl.program_id(0)

        @pl.when(i == 0)
        def _():
            acc_ref[...] = jnp.zeros_like(acc_ref)

        gbuf[:CONV_HALO, :] = jnp.where(i > 0, halo_ref[...], 0.0)
        gbuf[CONV_HALO:, :] = glu_ref[...]
        for lb in range(D_MODEL // LANES):
            lanes = slice(LANES * lb, LANES * (lb + 1))
            dcf_ref[:, lanes] = _conv_taps(dww_ref, gbuf, ts, lanes, _fwd_tap)
        cf = dcf_ref[...] + dwb_ref[...]
        mu = jnp.mean(cf, axis=-1, keepdims=True)
        cen = cf - mu
        rs = lax.rsqrt(jnp.mean(cen * cen, axis=-1, keepdims=True) + EPS)
        xhat = cen * rs
        cn = xhat * lng_ref[...] + lnb_ref[...]
        gate = gate_ref[...]
        sg = _sigmoid(gate)
        sc = _sigmoid(cn)
        silu_gate = gate * sg
        silu_cn = cn * sc
        ymix = (silu_cn * silu_gate).astype(BF16)
        ymix_ref[...] = ymix
        y = _dot(ymix, w_ref[...])
        yn, r = _rms_fwd(y, g_ref[...])
        err = (x1_ref[...] + yn) - t_ref[...]
        acc_ref[ACC_LOSS] += _rows8(err * err)
        dx2 = err * (1.0 / D_MODEL)
        dx2_ref[...] = dx2
        dy, dpost = _rms_bwd(y, r, g_ref[...], dx2)
        acc_ref[ACC_POST] += _rows8(dpost)
        dy = dy.astype(BF16)
        dy_ref[...] = dy
        dymix = _dot_nt(dy, w_ref[...])
        dgate_ref[...] = (dymix * silu_cn * (sg * (1.0 + gate * (1.0 - sg)))).astype(BF16)
        dcn = dymix * silu_gate * (sc * (1.0 + cn * (1.0 - sc)))
        acc_ref[ACC_LN_G] += _rows8(dcn * xhat)
        acc_ref[ACC_LN_B] += _rows8(dcn)
        dxhat = dcn * lng_ref[...]
        dcf = rs * (dxhat - jnp.mean(dxhat, axis=-1, keepdims=True)
                    - xhat * jnp.mean(dxhat * xhat, axis=-1, keepdims=True))
        acc_ref[ACC_DW_B] += _rows8(dcf)
        dcf_ref[...] = dcf

    return pl.pallas_call(
        body, name="fwd_out1", grid=(seq // ts,),
        in_specs=[_tile(ts, D_MODEL),
                  pl.BlockSpec((CONV_HALO, D_MODEL), lambda i: (jnp.maximum(i * hb - 1, 0), 0)),
                  _tile(ts, D_MODEL, 2), _full((CONV_HALO, D_MODEL)), _full((1, D_MODEL)), _full((1, D_MODEL)),
                  _full((1, D_MODEL)), _full((D_MODEL, D_MODEL)), _full((1, D_MODEL)), _tile(ts, D_MODEL),
                  _tile(ts, D_MODEL)],
        out_specs=[_tile(ts, D_MODEL), _tile(ts, D_MODEL), _tile(ts, D_MODEL), _tile(ts, D_MODEL), _tile(ts, D_MODEL),
                   _full((5, SUBLANES, D_MODEL))],
        out_shape=[jax.ShapeDtypeStruct((seq, D_MODEL), BF16), jax.ShapeDtypeStruct((seq, D_MODEL), BF16),
                   jax.ShapeDtypeStruct((seq, D_MODEL), F32), jax.ShapeDtypeStruct((seq, D_MODEL), F32),
                   jax.ShapeDtypeStruct((seq, D_MODEL), BF16), jax.ShapeDtypeStruct((5, SUBLANES, D_MODEL), F32)],
        scratch_shapes=[pltpu.VMEM((ts + CONV_HALO, D_MODEL), F32)],
        compiler_params=_params(),
    )(glu, glu, proj, dw_w, dw_b, ln_g, ln_b, w_out, g_post, x1, target)


def _bwd_in1(dcf, glu, proj, dgate, dw_w, w_in, x1, g_pre, dx2, ts):
    seq = x1.shape[0]
    hb = ts // CONV_HALO
    last = seq // CONV_HALO - 1
    nt = seq // ts

    def body(dcf_ref, dnext_ref, glu_ref, gprev_ref, ab_ref, dgate_ref, dww_ref, w_ref, x_ref, g_ref, dx2_ref,
             dproj_ref, dx1_ref, ddw_ref, dpre_ref, dbuf, gbuf, zbuf, sbuf):
        i = pl.program_id(0)

        @pl.when(i == 0)
        def _():
            ddw_ref[...] = jnp.zeros_like(ddw_ref)
            dpre_ref[...] = jnp.zeros_like(dpre_ref)

        dcf = dcf_ref[...]
        dbuf[:ts, :] = dcf
        dbuf[ts:, :] = jnp.where(i < nt - 1, dnext_ref[...], 0.0)
        gbuf[:CONV_HALO, :] = jnp.where(i > 0, gprev_ref[...], 0.0)
        gbuf[CONV_HALO:, :] = glu_ref[...]
        zbuf[:SUBLANES, :] = jnp.zeros((SUBLANES, D_MODEL), F32)
        zbuf[pl.ds(SUBLANES, ts), :] = dcf
        zbuf[pl.ds(SUBLANES + ts, SUBLANES), :] = jnp.zeros((SUBLANES, D_MODEL), F32)
        for lb in range(D_MODEL // LANES):
            lanes = slice(LANES * lb, LANES * (lb + 1))
            gate_lanes = slice(D_MODEL + LANES * lb, D_MODEL + LANES * (lb + 1))
            dglu = _conv_taps(dww_ref, dbuf, ts, lanes, _bwd_tap)
            a = ab_ref[:, lanes]
            sb = _sigmoid(ab_ref[:, gate_lanes])
            dproj_ref[:, lanes] = (dglu * sb).astype(BF16)
            dproj_ref[:, gate_lanes] = (dglu * a * sb * (1.0 - sb)).astype(BF16)
            for b in range(SUBLANES):
                rows = ts if b == 0 else ts + SUBLANES
                sbuf[b, pl.ds(0, rows), :] = zbuf[pl.ds(SUBLANES - b, rows), lanes]
                for a8 in range(CONV_HALO // SUBLANES + 1):
                    k = _fwd_tap(SUBLANES * a8 + b)
                    if k is not None:
                        ddw_ref[k, :, lanes] += _rows8(sbuf[b, pl.ds(0, rows), :] * gbuf[pl.ds(SUBLANES * a8, rows), lanes])
        dproj_ref[:, 2 * D_MODEL:] = dgate_ref[...]
        dh = _dot(dproj_ref[...], w_ref[...])
        x = x_ref[...]
        r = lax.rsqrt(jnp.mean(x * x, axis=-1, keepdims=True) + EPS)
        dx, dpre = _rms_bwd(x, r, g_ref[...], dh)
        dx1_ref[...] = dx2_ref[...] + dx
        dpre_ref[...] += _rows8(dpre)

    return pl.pallas_call(
        body, name="bwd_in1", grid=(nt,),
        in_specs=[_tile(ts, D_MODEL),
                  pl.BlockSpec((CONV_HALO, D_MODEL), lambda i: (jnp.minimum((i + 1) * hb, last), 0)),
                  _tile(ts, D_MODEL),
                  pl.BlockSpec((CONV_HALO, D_MODEL), lambda i: (jnp.maximum(i * hb - 1, 0), 0)),
                  _tile(ts, 2 * D_MODEL), _tile(ts, D_MODEL), _full((CONV_HALO, D_MODEL)),
                  _full((ODD_IN, D_MODEL)), _tile(ts, D_MODEL), _full((1, D_MODEL)), _tile(ts, D_MODEL)],
        out_specs=[_tile(ts, ODD_IN), _tile(ts, D_MODEL), _full((CONV_HALO, SUBLANES, D_MODEL)), _full((SUBLANES, D_MODEL))],
        out_shape=[jax.ShapeDtypeStruct((seq, ODD_IN), BF16), jax.ShapeDtypeStruct((seq, D_MODEL), F32),
                   jax.ShapeDtypeStruct((CONV_HALO, SUBLANES, D_MODEL), F32), jax.ShapeDtypeStruct((SUBLANES, D_MODEL), F32)],
        scratch_shapes=[pltpu.VMEM((ts + CONV_HALO, D_MODEL), F32), pltpu.VMEM((ts + CONV_HALO, D_MODEL), F32),
                        pltpu.VMEM((ts + 2 * SUBLANES, D_MODEL), F32), pltpu.VMEM((SUBLANES, ts + SUBLANES, LANES), F32)],
        compiler_params=_params(),
    )(dcf, dcf, glu, glu, proj, dgate, dw_w, w_in, x1, g_pre, dx2)


def _bwd_out0(dx1, y0, g_post, w_out, gug, o, pooled, pool_w, pool_scale, ts):
    seq = dx1.shape[0]

    def body(dx1_ref, y_ref, g_ref, w_ref, gug_ref, o_ref, pooled_ref, pw_ref, ps_ref,
             dy_ref, do_ref, dgg_ref, dpooled_ref, dpost_ref, dscale_ref, dpw_ref):
        i = pl.program_id(0)

        @pl.when(i == 0)
        def _():
            dpost_ref[...] = jnp.zeros_like(dpost_ref)
            dscale_ref[...] = jnp.zeros_like(dscale_ref)
            dpw_ref[...] = jnp.zeros_like(dpw_ref)

        y = y_ref[...]
        r = lax.rsqrt(jnp.mean(y * y, axis=-1, keepdims=True) + EPS)
        dy, dpost = _rms_bwd(y, r, g_ref[...], dx1_ref[...])
        dpost_ref[...] += _rows8(dpost)
        dy = dy.astype(BF16)
        dy_ref[...] = dy
        dmix = _dot_nt(dy, w_ref[...])
        dya = dmix[:, :ATTN_WIDTH]
        dyb = dmix[:, ATTN_WIDTH:]
        ga = gug_ref[:, :ATTN_WIDTH]
        gb = gug_ref[:, ATTN_WIDTH + POOL_WIDTH:]
        sga = _sigmoid(ga)
        sgb = _sigmoid(gb)
        do_ref[...] = (dya * (ga * sga)).astype(BF16)
        dgg_ref[:, :ATTN_WIDTH] = (dya * o_ref[...] * (sga * (1.0 + ga * (1.0 - sga)))).astype(BF16)
        dypool = dyb * (gb * sgb)
        dsilu_gb = sgb * (1.0 + gb * (1.0 - sgb))
        for g in range(len(POOL_WINDOWS)):
            lanes = slice(POOL_GC * g, POOL_GC * (g + 1))
            pooled = pooled_ref[:, lanes]
            wg = pw_ref[g].astype(BF16)
            pw = _dot(pooled, wg)
            scale = ps_ref[:, lanes]
            dgg_ref[:, ATTN_WIDTH + POOL_GC * g:ATTN_WIDTH + POOL_GC * (g + 1)] = (
                dyb[:, lanes] * (pw * scale) * dsilu_gb[:, lanes]).astype(BF16)
            dscale_ref[:, lanes] += _rows8(dypool[:, lanes] * pw)
            dpw = (dypool[:, lanes] * scale).astype(BF16)
            dpooled_ref[:, lanes] = _dot_nt(dpw, wg)
            dpw_ref[g] += _dot_tn(pooled, dpw)

    return pl.pallas_call(
        body, name="bwd_out0", grid=(seq // ts,),
        in_specs=[_tile(ts, D_MODEL), _tile(ts, D_MODEL), _full((1, D_MODEL)), _full((D_MODEL, D_MODEL)),
                  _tile(ts, 3 * POOL_WIDTH), _tile(ts, ATTN_WIDTH), _tile(ts, POOL_WIDTH),
                  _full((4, POOL_GC, POOL_GC)), _full((1, POOL_WIDTH))],
        out_specs=[_tile(ts, D_MODEL), _tile(ts, ATTN_WIDTH), _tile(ts, ATTN_WIDTH + POOL_WIDTH), _tile(ts, POOL_WIDTH),
                   _full((SUBLANES, D_MODEL)), _full((SUBLANES, POOL_WIDTH)), _full((4, POOL_GC, POOL_GC))],
        out_shape=[jax.ShapeDtypeStruct((seq, D_MODEL), BF16), jax.ShapeDtypeStruct((seq, ATTN_WIDTH), BF16),
                   jax.ShapeDtypeStruct((seq, ATTN_WIDTH + POOL_WIDTH), BF16), jax.ShapeDtypeStruct((seq, POOL_WIDTH), F32),
                   jax.ShapeDtypeStruct((SUBLANES, D_MODEL), F32), jax.ShapeDtypeStruct((SUBLANES, POOL_WIDTH), F32),
                   jax.ShapeDtypeStruct((4, POOL_GC, POOL_GC), F32)],
        compiler_params=_params(),
    )(dx1, y0, g_post, w_out, gug, o, pooled, pool_w, pool_scale)


def _attn_bwd(q, kv, do, sinks, parts):
    seq = q.shape[0]
    nb = seq // BLOCK

    def qblock(j):
        return jnp.minimum(j, nb - 1)

    n = len(parts)

    def body(*refs):
        sink_ref, q_ref, kvc_ref, kvp_ref, do_ref = refs[:5]
        dq_ref, dkv_ref, dsink_ref = refs[5 + n:8 + n]
        carry, dkv_acc = refs[8 + 2 * n:10 + 2 * n]
        scatter = _Scatter(refs[5:5 + n], refs[8 + n:8 + 2 * n], *refs[10 + 2 * n:])
        j = pl.program_id(0)

        @pl.when(j == 0)
        def _():
            scatter.begin()
            dsink_ref[...] = jnp.zeros_like(dsink_ref)
            carry[...] = jnp.zeros_like(carry)

        @pl.when(j < nb)
        def _():
            valid, distf = _attn_mask(GROUP_ROWS, j == 0)
            qb = q_ref[...]
            dob = do_ref[...]
            kk = jnp.concatenate([kvp_ref[...], kvc_ref[...]], axis=0)
            lane = lax.broadcasted_iota(jnp.int32, (BLOCK, LANES), 1)
            dsink = jnp.zeros((BLOCK, LANES), F32)
            for kvh in range(Q_HEADS // GROUP):
                kh = kk[:, HEAD_DIM * kvh:HEAD_DIM * (kvh + 1)]
                vh = kk[:, KV_WIDTH + HEAD_DIM * kvh:KV_WIDTH + HEAD_DIM * (kvh + 1)]
                qg = _group_rows(qb, kvh)
                dog = _group_rows(dob, kvh)
                sink, slope = _group_columns(sink_ref, kvh)
                p, psink = _attn_probs(qg, kh, sink, slope, valid, distf)
                dp = _dot_nt(dog, vh)
                delta = jnp.sum(p * dp, axis=-1, keepdims=True)
                ds = (p * (dp - delta) * (HEAD_DIM ** -0.5)).astype(BF16)
                dsink_rows = -psink * delta
                dqg = _dot(ds, kh).astype(BF16)
                for i in range(GROUP):
                    h = GROUP * kvh + i
                    dsink = dsink + jnp.where(lane == h, dsink_rows[BLOCK * i:BLOCK * (i + 1), :], 0.0)
                    dq_ref[:, HEAD_DIM * h:HEAD_DIM * (h + 1)] = dqg[BLOCK * i:BLOCK * (i + 1), :]
                dkv_acc[:, HEAD_DIM * kvh:HEAD_DIM * (kvh + 1)] = _dot_tn(ds, qg)
                dkv_acc[:, KV_WIDTH + HEAD_DIM * kvh:KV_WIDTH + HEAD_DIM * (kvh + 1)] = _dot_tn(p.astype(BF16), dog)
            dsink_ref[...] += dsink

            @pl.when(j > 0)
            def _():
                dkv_ref[...] = (carry[...] + dkv_acc[:BLOCK, :]).astype(BF16)

            carry[...] = dkv_acc[BLOCK:, :]

        @pl.when(j == nb)
        def _():
            dkv_ref[...] = carry[...].astype(BF16)
            scatter.finish()

    hbm = pl.BlockSpec(memory_space=pl.ANY)
    out = pl.pallas_call(
        body, name="attn_bwd", grid=(nb + 1,),
        in_specs=[pl.BlockSpec(memory_space=pltpu.SMEM),
                  pl.BlockSpec((BLOCK, ATTN_WIDTH), lambda j: (qblock(j), 0)),
                  pl.BlockSpec((BLOCK, 2 * KV_WIDTH), lambda j: (qblock(j), 0)),
                  pl.BlockSpec((BLOCK, 2 * KV_WIDTH), lambda j: (jnp.maximum(qblock(j) - 1, 0), 0)),
                  pl.BlockSpec((BLOCK, ATTN_WIDTH), lambda j: (qblock(j), 0))] + [hbm] * n,
        out_specs=[pl.BlockSpec((BLOCK, ATTN_WIDTH), lambda j: (qblock(j), 0)),
                   pl.BlockSpec((BLOCK, 2 * KV_WIDTH), lambda j: (jnp.maximum(j - 1, 0), 0)),
                   _full((BLOCK, LANES))] + [hbm] * n,
        out_shape=[jax.ShapeDtypeStruct((seq, ATTN_WIDTH), BF16), jax.ShapeDtypeStruct((seq, 2 * KV_WIDTH), BF16),
                   jax.ShapeDtypeStruct((BLOCK, LANES), F32)]
        + [jax.ShapeDtypeStruct((N_DEV - 1, *p.shape[1:]), p.dtype) for p in parts],
        scratch_shapes=[pltpu.VMEM((BLOCK, 2 * KV_WIDTH), F32), pltpu.VMEM((2 * BLOCK, 2 * KV_WIDTH), F32)]
        + _Scatter.semaphores(n),
        compiler_params=_params(),
    )(sinks, q, kv, kv, do, *parts)
    return out[:3], out[3:]


def _bwd_in0(dpooled, dq, dkv, dgg, w_in, x, g_pre, dx1, ts):
    seq = x.shape[0]
    hb = ts // POOL_HALO
    last = seq // POOL_HALO - 1
    nt = seq // ts

    def body(dp_ref, dnext_ref, dq_ref, dkv_ref, dgg_ref, w_ref, x_ref, g_ref, dx1_ref,
             dproj_ref, gx_ref, dpre_ref, dbuf):
        i = pl.program_id(0)

        @pl.when(i == 0)
        def _():
            dpre_ref[...] = jnp.zeros_like(dpre_ref)

        dpool = dp_ref[...]
        dnext = jnp.where(i < nt - 1, dnext_ref[...], 0.0)
        u0 = ATTN_WIDTH + 2 * KV_WIDTH + ATTN_WIDTH
        for g, window in enumerate(POOL_WINDOWS):
            lanes = slice(POOL_GC * g, POOL_GC * (g + 1))
            dbuf[:ts, lanes] = dpool[:, lanes] / _pool_counts(i * ts, ts, window)
            dbuf[ts:, lanes] = dnext[:, lanes] / _pool_counts((i + 1) * ts, POOL_HALO, window)
        for g, window in enumerate(POOL_WINDOWS):
            lanes = slice(POOL_GC * g, POOL_GC * (g + 1))
            acc = dbuf[pl.ds(0, ts), lanes]
            for k in range(1, window):
                acc = acc + dbuf[pl.ds(k, ts), lanes]
            dproj_ref[:, u0 + POOL_GC * g:u0 + POOL_GC * (g + 1)] = (acc - dpool[:, lanes]).astype(BF16)
        dproj_ref[:, :ATTN_WIDTH] = dq_ref[...]
        dproj_ref[:, ATTN_WIDTH:ATTN_WIDTH + 2 * KV_WIDTH] = dkv_ref[...]
        dproj_ref[:, ATTN_WIDTH + 2 * KV_WIDTH:u0] = dgg_ref[:, :ATTN_WIDTH]
        dproj_ref[:, u0 + POOL_WIDTH:] = dgg_ref[:, ATTN_WIDTH:]
        dh = _dot(dproj_ref[...], w_ref[...])
        x = x_ref[...]
        r = lax.rsqrt(jnp.mean(x * x, axis=-1, keepdims=True) + EPS)
        dx, dpre = _rms_bwd(x, r, g_ref[...], dh)
        gx_ref[...] = dx1_ref[...] + dx
        dpre_ref[...] += _rows8(dpre)

    return pl.pallas_call(
        body, name="bwd_in0", grid=(nt,),
        in_specs=[_tile(ts, POOL_WIDTH),
                  pl.BlockSpec((POOL_HALO, POOL_WIDTH), lambda i: (jnp.minimum((i + 1) * hb, last), 0)),
                  _tile(ts, ATTN_WIDTH), _tile(ts, 2 * KV_WIDTH), _tile(ts, ATTN_WIDTH + POOL_WIDTH),
                  _full((EVEN_IN, D_MODEL)), _tile(ts, D_MODEL), _full((1, D_MODEL)), _tile(ts, D_MODEL)],
        out_specs=[_tile(ts, EVEN_IN), _tile(ts, D_MODEL), _full((SUBLANES, D_MODEL))],
        out_shape=[jax.ShapeDtypeStruct((seq, EVEN_IN), BF16), jax.ShapeDtypeStruct((seq, D_MODEL), F32),
                   jax.ShapeDtypeStruct((SUBLANES, D_MODEL), F32)],
        scratch_shapes=[pltpu.VMEM((ts + POOL_HALO, POOL_WIDTH), F32)],
        compiler_params=_params(),
    )(dpooled, dpooled, dq, dkv, dgg, w_in, x, g_pre, dx1)


def _matmul_tn(a, b, name, ts, tm):
    seq, m = a.shape
    n = b.shape[1]
    steps = seq // ts

    def body(a_ref, b_ref, o_ref, acc):
        s = pl.program_id(1)

        @pl.when(s == 0)
        def _():
            acc[...] = jnp.zeros_like(acc)

        acc[...] += _dot_tn(a_ref[...], b_ref[...])

        @pl.when(s == steps - 1)
        def _():
            o_ref[...] = acc[...].astype(BF16)

    return pl.pallas_call(
        body, name=name, grid=(m // tm, steps),
        in_specs=[pl.BlockSpec((ts, tm), lambda j, s: (s, j)), pl.BlockSpec((ts, n), lambda j, s: (s, 0))],
        out_specs=pl.BlockSpec((tm, n), lambda j, s: (j, 0)),
        out_shape=jax.ShapeDtypeStruct((m, n), BF16),
        scratch_shapes=[pltpu.VMEM((tm, n), F32)],
        compiler_params=pltpu.CompilerParams(dimension_semantics=("arbitrary", "arbitrary"), vmem_limit_bytes=VMEM_LIMIT),
    )(a, b)


def _adamw_math(w, g, m, v):
    m = ADAM_B1 * m + (1.0 - ADAM_B1) * g
    v = ADAM_B2 * v + (1.0 - ADAM_B2) * (g * g)
    m_hat = m / (1.0 - ADAM_B1 ** ADAM_STEP)
    v_hat = v / (1.0 - ADAM_B2 ** ADAM_STEP)
    delta = -ADAM_LR * (m_hat / (jnp.sqrt(v_hat) + ADAM_EPS) + ADAM_WD * w)
    return delta, m, v


def _adamw(ws, gs, ms, vs, name):
    n = len(ws)

    def body(*refs):
        ins, outs = refs[:4 * n], refs[4 * n:]
        for k in range(n):
            delta, m, v = _adamw_math(ins[k][...], ins[n + k][...], ins[2 * n + k][...], ins[3 * n + k][...])
            outs[k][...] = delta
            outs[n + k][...] = m
            outs[2 * n + k][...] = v

    shapes = [jax.ShapeDtypeStruct(w.shape, F32) for w in ws]
    out = pl.pallas_call(body, name=name, out_shape=shapes * 3,
                         compiler_params=pltpu.CompilerParams(vmem_limit_bytes=VMEM_LIMIT))(*ws, *gs, *ms, *vs)
    return out[:n], out[n:2 * n], out[2 * n:]


TS_MATMUL, TS_CONV, TS_GRAD = 512, 256, 1024


def kernel(x, pre_norm, post_norm, a_w_in, a_sinks, b_pool_w, b_pool_scale, ab_w_out, c_w_in, c_dw_w, c_dw_b, c_ln_g, c_ln_b, c_w_out, loss_target, m_pre_norm, m_post_norm, m_a_w_in, m_a_sinks, m_b_pool_w, m_b_pool_scale, m_ab_w_out, m_c_w_in, m_c_dw_w, m_c_dw_b, m_c_ln_g, m_c_ln_b, m_c_w_out, v_pre_norm, v_post_norm, v_a_w_in, v_a_sinks, v_b_pool_w, v_b_pool_scale, v_ab_w_out, v_c_w_in, v_c_dw_w, v_c_dw_b, v_c_ln_g, v_c_ln_b, v_c_w_out):
    seq = x.shape[1]
    ts_big, ts_conv, ts_grad = min(TS_MATMUL, seq), min(TS_CONV, seq), min(TS_GRAD, seq)
    x2d = x[0]
    target = loss_target[0]
    ch = c_dw_b.shape[1]

    whole = lambda g: g.reshape(-1, D_MODEL)
    vec_rows = 40
    vecs = jnp.concatenate([c_dw_w[0, :, 0, :], c_dw_b, c_ln_g, c_ln_b, jnp.zeros((vec_rows - CONV_K - 3, ch), F32)], axis=0)
    w_in0t, vg = _all_gather([a_w_in[0].T.astype(BF16), vecs], "gather_first")
    w_in0t = whole(w_in0t)
    vg = vg.transpose(1, 0, 2).reshape(vec_rows, D_MODEL)
    dw_w = vg[:CONV_HALO]
    dw_b, ln_g, ln_b = vg[CONV_K:CONV_K + 1], vg[CONV_K + 1:CONV_K + 2], vg[CONV_K + 2:CONV_K + 3]

    sinks = a_sinks[0]
    (h0, q, kv, gug), (w_out0,) = _fwd_in0(x2d, pre_norm[0:1], w_in0t, [ab_w_out[0].astype(BF16)], ts_big)
    o, (w_in1t, w_out1) = _attn_fwd(q, kv, sinks, [c_w_in[0].T.astype(BF16), c_w_out[0].astype(BF16)])
    w_out0, w_in1t, w_out1 = whole(w_out0), whole(w_in1t), whole(w_out1)
    mix0, pooled, y0, x1 = _fwd_out0(gug, o, b_pool_w[0], b_pool_scale, w_out0, post_norm[0:1], x2d, ts_big)
    h1, proj1, glu = _fwd_in1(x1, pre_norm[1:2], w_in1t, ts_big)
    ymix1, dy1, dx2, dcf, dgate, acc1 = _fwd_out1(glu, proj1, dw_w, dw_b, ln_g, ln_b, w_out1, post_norm[1:2], x1, target, ts_conv)

    dproj1, dx1, ddw_w, dpre1 = _bwd_in1(dcf, glu, proj1, dgate, dw_w, w_in1t, x1, pre_norm[1:2], dx2, ts_conv)
    g_in1t = _matmul_tn(dproj1, h1, "grad_w_in1", ts_grad, 1024)
    g_out1 = _matmul_tn(ymix1, dy1, "grad_w_out1", ts_grad, 1024)
    dy0, do, dgg, dpooled, dpost0, dscale, dpool_w = _bwd_out0(dx1, y0, post_norm[0:1], w_out0, gug, o, pooled, b_pool_w[0], b_pool_scale, ts_big)
    g_out0 = _matmul_tn(mix0, dy0, "grad_w_out0", ts_grad, 1024)
    slabs = lambda g: g.reshape(N_DEV, -1, D_MODEL)
    me = 4 * lax.axis_index("x") + 2 * lax.axis_index("y") + lax.axis_index("c")
    early = [slabs(g_in1t), slabs(g_out1), slabs(g_out0)]
    (dq, dkv, dsink), arrived = _attn_bwd(q, kv, do, sinks, early)
    g_in1t, g_c_w_out, g_ab_w_out = [_sum_slabs(lax.dynamic_index_in_dim(p, me, keepdims=False), r, name)
                                     for p, r, name in zip(early, arrived, ("sum_w_in1", "sum_w_out1", "sum_w_out0"))]
    dproj0, grad_x, dpre0 = _bwd_in0(dpooled, dq, dkv, dgg, w_in0t, x2d, pre_norm[0:1], dx1, ts_big)
    g_in0t = _matmul_tn(dproj0, h0, "grad_w_in0", ts_grad, 768)

    row = lambda a: jnp.sum(a, axis=0, keepdims=True)
    vec_g = jnp.concatenate([jnp.sum(ddw_w[:CONV_K], axis=1), row(acc1[ACC_DW_B]), row(acc1[ACC_LN_G]), row(acc1[ACC_LN_B]),
                             jnp.zeros((vec_rows - CONV_K - 3, D_MODEL), F32)], axis=0)
    lanes8 = lambda a: row(a).reshape(-1, LANES)
    loss_row = jnp.pad(jnp.sum(acc1[ACC_LOSS]).reshape(1, 1), ((0, 0), (0, LANES - 1)))
    rep = jnp.concatenate([lanes8(dpre0), lanes8(dpre1), lanes8(dpost0), lanes8(acc1[ACC_POST]), lanes8(dscale),
                           loss_row, jnp.zeros((2, LANES), F32), row(dsink), dpool_w.reshape(4 * POOL_GC, LANES)], axis=0)
    g_in0t, vec_g, rep = _final_reduce(slabs(g_in0t), vec_g.reshape(vec_rows, N_DEV, ch).transpose(1, 0, 2), rep, "final_reduce")
    g_a_w_in, g_c_w_in = g_in0t.T, g_in1t.T
    g_dw_w, g_dw_b, g_ln_g, g_ln_b = vec_g[:CONV_K], vec_g[CONV_K:CONV_K + 1], vec_g[CONV_K + 1:CONV_K + 2], vec_g[CONV_K + 2:CONV_K + 3]
    g_pre = rep[:16].reshape(2, D_MODEL)
    g_post = rep[16:32].reshape(2, D_MODEL)
    g_scale = rep[32:36].reshape(1, POOL_WIDTH)
    loss = (0.5 / D_MODEL) * rep[36, 0]
    g_sinks = rep[39:40, :Q_HEADS]
    g_pool_w = rep[40:]

    grads = [g_pre, g_post, g_a_w_in, g_sinks, g_pool_w, g_scale, g_ab_w_out, g_c_w_in, g_dw_w, g_dw_b, g_ln_g, g_ln_b, g_c_w_out]
    weights = [pre_norm, post_norm, a_w_in, a_sinks, b_pool_w, b_pool_scale, ab_w_out, c_w_in, c_dw_w, c_dw_b, c_ln_g, c_ln_b, c_w_out]
    m_in = [m_pre_norm, m_post_norm, m_a_w_in, m_a_sinks, m_b_pool_w, m_b_pool_scale, m_ab_w_out, m_c_w_in, m_c_dw_w, m_c_dw_b, m_c_ln_g, m_c_ln_b, m_c_w_out]
    v_in = [v_pre_norm, v_post_norm, v_a_w_in, v_a_sinks, v_b_pool_w, v_b_pool_scale, v_ab_w_out, v_c_w_in, v_c_dw_w, v_c_dw_b, v_c_ln_g, v_c_ln_b, v_c_w_out]
    flat = lambda arrs: [a.reshape(g.shape) for a, g in zip(arrs, grads)]
    big = (2, 6, 7, 12)
    small = tuple(k for k in range(len(grads)) if k not in big)
    pick = lambda arrs, idx: [arrs[k] for k in idx]
    deltas, new_m, new_v = [None] * 13, [None] * 13, [None] * 13
    for idx, name in ((big, "adamw_matrices"), (small, "adamw_vectors")):
        d, m, v = _adamw(pick(flat(weights), idx), pick(grads, idx), pick(flat(m_in), idx), pick(flat(v_in), idx), name)
        for k, dk, mk, vk in zip(idx, d, m, v):
            deltas[k], new_m[k], new_v[k] = dk, mk, vk
    shaped = lambda arrs: [a.reshape(w.shape) for a, w in zip(arrs, weights)]
    return (loss, grad_x[None], *shaped(grads), *shaped(deltas), *shaped(new_m), *shaped(new_v))
```

```python
import functools

import jax
import jax.numpy as jnp
from jax import lax
from jax.experimental import pallas as pl
from jax.experimental.pallas import tpu as pltpu

F32 = jnp.float32
BF16 = jnp.bfloat16
MESH = pl.DeviceIdType.MESH
AXES = ("x", "y", "c")
N_DEV = 8

D_MODEL = 1024
HEAD_DIM = 64
Q_HEADS = 8
GROUP = 4
ATTN_WIDTH = 512
KV_WIDTH = 128
BLOCK = 128
POOL_WIDTH = 512
POOL_WINDOWS = (2, 4, 8, 16)
POOL_GC = 128
POOL_HALO = 16
EVEN_IN = 2304
CONV_K = 31
CONV_HALO = 32
ODD_IN = 3072
EPS = 1e-6
NEG = -1e30
SLOPES = tuple(2.0 ** (-8.0 * (h + 1) / Q_HEADS) for h in range(Q_HEADS))

ADAM_LR = 0.001
ADAM_B1 = 0.9
ADAM_B2 = 0.999
ADAM_EPS = 1e-08
ADAM_WD = 0.01
ADAM_STEP = 10

SUBLANES = 8
LANES = 128
VMEM_LIMIT = 56 * 1024 * 1024

NT = (((1,), (1,)), ((), ()))
TN = (((0,), (0,)), ((), ()))


def _params(**kw):
    return pltpu.CompilerParams(dimension_semantics=("arbitrary",), vmem_limit_bytes=VMEM_LIMIT, **kw)


def _dot(a, b):
    return jnp.dot(a, b, preferred_element_type=F32)


def _dot_nt(a, b):
    return lax.dot_general(a, b, NT, preferred_element_type=F32)


def _dot_tn(a, b):
    return lax.dot_general(a, b, TN, preferred_element_type=F32)


def _sigmoid(v):
    return 1.0 / (1.0 + jnp.exp(-v))


def _rows8(v):
    r, c = v.shape
    return jnp.sum(v.reshape(r // SUBLANES, SUBLANES, c), axis=0)


def _rms_fwd(v, g):
    r = lax.rsqrt(jnp.mean(v * v, axis=-1, keepdims=True) + EPS)
    return v * r * g, r


def _rms_bwd(v, r, g, dout):
    gd = dout * g
    dv = r * gd - v * (r * r * r) * jnp.mean(v * gd, axis=-1, keepdims=True)
    return dv, dout * (v * r)


def _full(shape):
    return pl.BlockSpec(shape, lambda i: (0,) * len(shape))


def _resident(shape):
    return pl.BlockSpec(shape, lambda i: (0,) * len(shape), pipeline_mode=pl.Buffered(1))


def _accumulate_tn(acc, a, b, first_step):
    @pl.when(first_step)
    def _():
        acc[...] = jnp.zeros_like(acc)

    acc[...] += _dot_tn(a, b)


GRAD_CHUNK = 256


def _emit_grad(acc, stage, out_hbm, sem):
    for c in range(acc.shape[0] // GRAD_CHUNK):
        rows = pl.ds(GRAD_CHUNK * c, GRAD_CHUNK)
        stage[...] = acc[rows, :].astype(BF16)
        cp = pltpu.make_async_copy(stage, out_hbm.at[rows, :], sem)
        cp.start()
        cp.wait()


def _tile(ts, cols, col_block=0):
    return pl.BlockSpec((ts, cols), lambda i: (i, col_block))


def _position():
    return lax.axis_index("x"), lax.axis_index("y"), lax.axis_index("c")


class _Gather:
    def __init__(self, x_refs, out_refs, send_sems, recv_sems, local_sems):
        self.x_refs, self.out_refs = x_refs, out_refs
        self.send_sems, self.recv_sems, self.local_sems = send_sems, recv_sems, local_sems
        self.n = len(x_refs)
        x, y, c = _position()
        self.c = c
        self.me, self.sibling = (x, y, c), (x, y, 1 - c)
        self.chips = [(1 - x, y), (x, 1 - y), (1 - x, 1 - y)]

    def copy(self, k, j, owner, to, src=None):
        slab = self.out_refs[k].at[4 * owner[0] + 2 * owner[1] + owner[2]]
        return pltpu.make_async_remote_copy(
            src_ref=slab if src is None else src, dst_ref=slab, send_sem=self.send_sems.at[7 * k + j],
            recv_sem=self.recv_sems.at[7 * k + j], device_id=to, device_id_type=MESH)

    def mine(self, k):
        return pltpu.make_async_copy(self.x_refs[k], self.out_refs[k].at[4 * self.me[0] + 2 * self.me[1] + self.c],
                                     self.local_sems.at[k])

    def first(self, k):
        out = [self.copy(k, 0, self.me, self.sibling, src=self.x_refs[k])]
        return out + [self.copy(k, 1 + j, self.me, (*chip, self.c), src=self.x_refs[k]) for j, chip in enumerate(self.chips)]

    def begin(self):
        for k in range(self.n):
            self.mine(k).start()
            for cp in self.first(k):
                cp.start()

    def finish(self):
        passed = []
        for k in range(self.n):
            for j, chip in enumerate(self.chips):
                self.copy(k, 1 + j, (*chip, self.c), self.me).wait_recv()
                passed.append(self.copy(k, 4 + j, (*chip, self.c), self.sibling))
                passed[-1].start()
        for k in range(self.n):
            self.copy(k, 0, self.sibling, self.me).wait_recv()
            for j, chip in enumerate(self.chips):
                self.copy(k, 4 + j, (*chip, 1 - self.c), self.me).wait_recv()
        for k in range(self.n):
            for cp in self.first(k):
                cp.wait_send()
        for cp in passed:
            cp.wait_send()
        for k in range(self.n):
            self.mine(k).wait()

    @staticmethod
    def semaphores(n):
        return [pltpu.SemaphoreType.DMA((7 * n,)), pltpu.SemaphoreType.DMA((7 * n,)), pltpu.SemaphoreType.DMA((n,))]


def _all_gather(blocks, name):
    n = len(blocks)

    def body(*refs):
        gather = _Gather(refs[:n], refs[n:2 * n], *refs[2 * n:])
        gather.begin()
        gather.finish()

    return pl.pallas_call(
        body, name=name,
        out_shape=[jax.ShapeDtypeStruct((N_DEV, *b.shape), b.dtype) for b in blocks],
        in_specs=[pl.BlockSpec(memory_space=pltpu.VMEM)] * n,
        out_specs=[pl.BlockSpec(memory_space=pltpu.VMEM)] * n,
        scratch_shapes=_Gather.semaphores(n),
        compiler_params=pltpu.CompilerParams(vmem_limit_bytes=VMEM_LIMIT),
    )(*blocks)


class _Scatter:
    def __init__(self, part_refs, recv_refs, send_sems, recv_sems):
        self.part_refs, self.recv_refs, self.send_sems, self.recv_sems = part_refs, recv_refs, send_sems, recv_sems
        self.n = len(part_refs)

    def copies(self):
        x, y, c = _position()
        me = 4 * x + 2 * y + c
        out = []
        for k in range(self.n):
            for j in range(N_DEV - 1):
                d = j + 1
                out.append(pltpu.make_async_remote_copy(
                    src_ref=self.part_refs[k].at[me ^ d], dst_ref=self.recv_refs[k].at[j],
                    send_sem=self.send_sems.at[7 * k + j], recv_sem=self.recv_sems.at[7 * k + j],
                    device_id=(x ^ (d >> 2), y ^ ((d >> 1) & 1), c ^ (d & 1)), device_id_type=MESH))
        return out

    def begin(self):
        for cp in self.copies():
            cp.start()

    def finish(self):
        for cp in self.copies():
            cp.wait_recv()
        for cp in self.copies():
            cp.wait_send()

    @staticmethod
    def semaphores(n):
        return [pltpu.SemaphoreType.DMA((7 * n,)), pltpu.SemaphoreType.DMA((7 * n,))]


def _sum_slabs(own, recv, name):
    rows, cols = own.shape

    def body(own_ref, recv_ref, out_ref):
        acc = own_ref[...].astype(F32)
        for j in range(N_DEV - 1):
            acc = acc + recv_ref[j].astype(F32)
        out_ref[...] = acc

    return pl.pallas_call(body, name=name, out_shape=jax.ShapeDtypeStruct((rows, cols), F32))(own, recv)


N_CHIPS = 4


def _final_reduce(parts, vec_parts, rep, early_all, name):
    _, rows, cols = parts.shape
    vrows, prows = vec_parts.shape[1], rep.shape[0]
    peers = N_CHIPS - 1

    def body(parts_ref, vec_ref, rep_ref, early_ref, out_ref, vec_out, rep_out, early_out, recv_a, own_a, mid, recv_b, vec_recv, rep_all,
             a_send, a_recv, a_local, b_send, b_recv, v_send, v_recv, r_send, r_recv):
        x, y, c = _position()
        chip = 2 * x + y
        me = 2 * chip + c
        everyone = [(d, (x ^ (d >> 2), y ^ ((d >> 1) & 1), c ^ (d & 1))) for d in range(1, N_DEV)]

        stage1, loads = [], []
        for j in range(N_CHIPS):
            stage1.append(pltpu.make_async_remote_copy(
                src_ref=parts_ref.at[2 * j + 1 - c], dst_ref=recv_a.at[j], send_sem=a_send.at[j], recv_sem=a_recv.at[j],
                device_id=(x, y, 1 - c), device_id_type=MESH))
            loads.append(pltpu.make_async_copy(parts_ref.at[2 * j + c], own_a.at[j], a_local.at[j]))
            stage1[-1].start()
            loads[-1].start()
        small = []
        for d, to in everyone:
            small.append(pltpu.make_async_remote_copy(
                src_ref=vec_ref.at[me ^ d], dst_ref=vec_recv.at[d - 1], send_sem=v_send.at[d - 1], recv_sem=v_recv.at[d - 1],
                device_id=to, device_id_type=MESH))
            small.append(pltpu.make_async_remote_copy(
                src_ref=rep_ref, dst_ref=rep_all.at[me], send_sem=r_send.at[d - 1], recv_sem=r_recv.at[d - 1],
                device_id=to, device_id_type=MESH))
        for cp in small:
            cp.start()
        rep_all[me] = rep_ref[...]

        for j in range(N_CHIPS):
            loads[j].wait()
            stage1[j].wait_recv()
            mid[j] = (own_a[j].astype(F32) + recv_a[j].astype(F32)).astype(BF16)
        stage2 = []
        for d in range(1, N_CHIPS):
            stage2.append(pltpu.make_async_remote_copy(
                src_ref=mid.at[chip ^ d], dst_ref=recv_b.at[d - 1], send_sem=b_send.at[d - 1], recv_sem=b_recv.at[d - 1],
                device_id=(x ^ (d >> 1), y ^ (d & 1), c), device_id_type=MESH))
            stage2[-1].start()

        for cp in small:
            cp.wait_recv()
        vec_sum = vec_ref[me]
        for j in range(N_DEV - 1):
            vec_sum = vec_sum + vec_recv[j]
        vec_out[...] = vec_sum
        rep_sum = rep_all[0]
        for d in range(1, N_DEV):
            rep_sum = rep_sum + rep_all[d]
        rep_out[...] = rep_sum
        early_sum = early_ref[0]
        for d in range(1, N_DEV):
            early_sum = early_sum + early_ref[d]
        early_out[...] = early_sum

        acc = mid[chip].astype(F32)
        for d in range(peers):
            stage2[d].wait_recv()
            acc = acc + recv_b[d].astype(F32)
        out_ref[...] = acc
        for cp in stage1 + stage2 + small:
            cp.wait_send()

    vmem = pl.BlockSpec(memory_space=pltpu.VMEM)
    dma = pltpu.SemaphoreType.DMA
    return pl.pallas_call(
        body, name=name,
        out_shape=[jax.ShapeDtypeStruct((rows, cols), F32), jax.ShapeDtypeStruct((vrows, LANES), F32),
                   jax.ShapeDtypeStruct((prows, LANES), F32), jax.ShapeDtypeStruct(early_all.shape[1:], F32)],
        in_specs=[pl.BlockSpec(memory_space=pl.ANY), vmem, vmem, vmem],
        out_specs=[vmem, vmem, vmem, vmem],
        scratch_shapes=[pltpu.VMEM((N_CHIPS, rows, cols), BF16), pltpu.VMEM((N_CHIPS, rows, cols), BF16),
                        pltpu.VMEM((N_CHIPS, rows, cols), BF16), pltpu.VMEM((peers, rows, cols), BF16),
                        pltpu.VMEM((N_DEV - 1, vrows, LANES), F32), pltpu.VMEM((N_DEV, prows, LANES), F32),
                        dma((N_CHIPS,)), dma((N_CHIPS,)), dma((N_CHIPS,)), dma((peers,)), dma((peers,)),
                        dma((N_DEV - 1,)), dma((N_DEV - 1,)), dma((N_DEV - 1,)), dma((N_DEV - 1,))],
        compiler_params=pltpu.CompilerParams(vmem_limit_bytes=VMEM_LIMIT),
    )(parts, vec_parts, rep, early_all)


def _fwd_in0(x, g, w, shards, ts):
    seq = x.shape[0]
    n = len(shards)
    steps = seq // ts

    def body(*refs):
        x_ref, g_ref, w_ref = refs[:3]
        h_ref, q_ref, kv_ref, gug_ref = refs[3 + n:7 + n]
        gather = lambda: _Gather(refs[3:3 + n], refs[7 + n:7 + 2 * n], *refs[7 + 2 * n:])
        i = pl.program_id(0)

        @pl.when(i == 0)
        def _():
            gather().begin()

        h, _ = _rms_fwd(x_ref[...], g_ref[...])
        h = h.astype(BF16)
        h_ref[...] = h
        proj = _dot_nt(h, w_ref[...])
        q_ref[...] = proj[:, :ATTN_WIDTH].astype(BF16)
        kv_ref[...] = proj[:, ATTN_WIDTH:ATTN_WIDTH + 2 * KV_WIDTH].astype(BF16)
        gug_ref[...] = proj[:, ATTN_WIDTH + 2 * KV_WIDTH:]

        @pl.when(i == steps - 1)
        def _():
            gather().finish()

    hbm = pl.BlockSpec(memory_space=pl.ANY)
    out = pl.pallas_call(
        body, name="fwd_in0", grid=(steps,),
        in_specs=[_tile(ts, D_MODEL), _full((1, D_MODEL)), _full((EVEN_IN, D_MODEL))] + [hbm] * n,
        out_specs=[_tile(ts, D_MODEL), _tile(ts, ATTN_WIDTH), _tile(ts, 2 * KV_WIDTH), _tile(ts, 3 * POOL_WIDTH)] + [hbm] * n,
        out_shape=[jax.ShapeDtypeStruct((seq, D_MODEL), BF16), jax.ShapeDtypeStruct((seq, ATTN_WIDTH), BF16),
                   jax.ShapeDtypeStruct((seq, 2 * KV_WIDTH), BF16), jax.ShapeDtypeStruct((seq, 3 * POOL_WIDTH), F32)]
        + [jax.ShapeDtypeStruct((N_DEV, *b.shape), b.dtype) for b in shards],
        scratch_shapes=_Gather.semaphores(n),
        compiler_params=_params(),
    )(x, g, w, *shards)
    return out[:4], out[4:]


GROUP_ROWS = GROUP * BLOCK


def _attn_mask(rows, first_block):
    row = lax.broadcasted_iota(jnp.int32, (rows, 2 * BLOCK), 0) & (BLOCK - 1)
    col = lax.broadcasted_iota(jnp.int32, (rows, 2 * BLOCK), 1)
    dist = row + BLOCK - col
    valid = (dist >= 0) & (dist < BLOCK) & ((col >= BLOCK) | jnp.logical_not(first_block))
    return valid, dist.astype(F32)


def _group_rows(block, kvh):
    return jnp.concatenate([block[:, HEAD_DIM * h:HEAD_DIM * (h + 1)] for h in range(GROUP * kvh, GROUP * (kvh + 1))], axis=0)


def _group_columns(sink_ref, kvh):
    head = lax.broadcasted_iota(jnp.int32, (GROUP_ROWS, 1), 0) // BLOCK
    sink = jnp.zeros((GROUP_ROWS, 1), F32)
    slope = jnp.zeros((GROUP_ROWS, 1), F32)
    for i in range(GROUP):
        sink = jnp.where(head == i, sink_ref[GROUP * kvh + i], sink)
        slope = jnp.where(head == i, SLOPES[GROUP * kvh + i], slope)
    return sink, slope


def _attn_probs(qh, kh, sink, slope, valid, distf):
    s = _dot_nt(qh, kh) * (HEAD_DIM ** -0.5)
    s = jnp.where(valid, s - slope * distf, NEG)
    mx = jnp.maximum(jnp.max(s, axis=-1, keepdims=True), sink)
    e = jnp.exp(s - mx)
    es = jnp.exp(sink - mx)
    den = jnp.sum(e, axis=-1, keepdims=True) + es
    return e / den, es / den


def _attn_fwd(q, kv, sinks, shards):
    seq = q.shape[0]
    nb = seq // BLOCK
    ns = len(shards)

    def body(*refs):
        sink_ref, q_ref, kvc_ref, kvp_ref = refs[:4]
        o_ref = refs[4 + ns]
        gather = lambda: _Gather(refs[4:4 + ns], refs[5 + ns:5 + 2 * ns], *refs[5 + 2 * ns:])
        n = pl.program_id(0)

        @pl.when(n == 0)
        def _():
            gather().begin()

        valid, distf = _attn_mask(BLOCK, n == 0)
        qb = q_ref[...]
        kk = jnp.concatenate([kvp_ref[...], kvc_ref[...]], axis=0)
        for h in range(Q_HEADS):
            kvh = h // GROUP
            qh = qb[:, HEAD_DIM * h:HEAD_DIM * (h + 1)]
            kh = kk[:, HEAD_DIM * kvh:HEAD_DIM * (kvh + 1)]
            vh = kk[:, KV_WIDTH + HEAD_DIM * kvh:KV_WIDTH + HEAD_DIM * (kvh + 1)]
            p, _ = _attn_probs(qh, kh, sink_ref[h], SLOPES[h], valid, distf)
            o_ref[:, HEAD_DIM * h:HEAD_DIM * (h + 1)] = _dot(p.astype(BF16), vh)

        @pl.when(n == nb - 1)
        def _():
            gather().finish()

    hbm = pl.BlockSpec(memory_space=pl.ANY)
    out = pl.pallas_call(
        body, name="attn_fwd", grid=(nb,),
        in_specs=[pl.BlockSpec(memory_space=pltpu.SMEM),
                  pl.BlockSpec((BLOCK, ATTN_WIDTH), lambda n: (n, 0)),
                  pl.BlockSpec((BLOCK, 2 * KV_WIDTH), lambda n: (n, 0)),
                  pl.BlockSpec((BLOCK, 2 * KV_WIDTH), lambda n: (jnp.maximum(n - 1, 0), 0))] + [hbm] * ns,
        out_specs=[pl.BlockSpec((BLOCK, ATTN_WIDTH), lambda n: (n, 0))] + [hbm] * ns,
        out_shape=[jax.ShapeDtypeStruct((seq, ATTN_WIDTH), F32)]
        + [jax.ShapeDtypeStruct((N_DEV, *b.shape), b.dtype) for b in shards],
        scratch_shapes=_Gather.semaphores(ns),
        compiler_params=_params(),
    )(sinks, q, kv, kv, *shards)
    return out[0], out[1:]


def _pool_counts(first_row, rows, window):
    t = first_row + lax.broadcasted_iota(jnp.int32, (rows, 1), 0)
    return jnp.minimum(t + 1, window).astype(F32)


def _fwd_out0(gug, o, pool_w, pool_scale, w_out, g_post, x, ts):
    seq = x.shape[0]
    hb = ts // POOL_HALO

    def body(gug_ref, halo_ref, o_ref, pw_ref, ps_ref, w_ref, g_ref, x_ref, mix_ref, pooled_ref, y_ref, x1_ref, ubuf):
        i = pl.program_id(0)
        ga = gug_ref[:, :ATTN_WIDTH]
        u = gug_ref[:, ATTN_WIDTH:ATTN_WIDTH + POOL_WIDTH]
        gb = gug_ref[:, ATTN_WIDTH + POOL_WIDTH:]
        mix_ref[:, :ATTN_WIDTH] = (o_ref[...] * (ga * _sigmoid(ga))).astype(BF16)
        ubuf[:POOL_HALO, :] = jnp.where(i > 0, halo_ref[...], 0.0)
        ubuf[POOL_HALO:, :] = u
        silu_gb = gb * _sigmoid(gb)
        for g, window in enumerate(POOL_WINDOWS):
            lanes = slice(POOL_GC * g, POOL_GC * (g + 1))
            acc = ubuf[pl.ds(POOL_HALO, ts), lanes]
            for k in range(1, window):
                acc = acc + ubuf[pl.ds(POOL_HALO - k, ts), lanes]
            pooled = (acc / _pool_counts(i * ts, ts, window) - u[:, lanes]).astype(BF16)
            pooled_ref[:, lanes] = pooled
            ypool = _dot(pooled, pw_ref[g].astype(BF16)) * ps_ref[:, lanes]
            mix_ref[:, ATTN_WIDTH + POOL_GC * g:ATTN_WIDTH + POOL_GC * (g + 1)] = (ypool * silu_gb[:, lanes]).astype(BF16)
        y = _dot(mix_ref[...], w_ref[...])
        y_ref[...] = y
        yn, _ = _rms_fwd(y, g_ref[...])
        x1_ref[...] = x_ref[...] + yn

    return pl.pallas_call(
        body, name="fwd_out0", grid=(seq // ts,),
        in_specs=[_tile(ts, 3 * POOL_WIDTH),
                  pl.BlockSpec((POOL_HALO, POOL_WIDTH), lambda i: (jnp.maximum(i * hb - 1, 0), 1)),
                  _tile(ts, ATTN_WIDTH), _full((4, POOL_GC, POOL_GC)), _full((1, POOL_WIDTH)),
                  _full((D_MODEL, D_MODEL)), _full((1, D_MODEL)), _tile(ts, D_MODEL)],
        out_specs=[_tile(ts, D_MODEL), _tile(ts, POOL_WIDTH), _tile(ts, D_MODEL), _tile(ts, D_MODEL)],
        out_shape=[jax.ShapeDtypeStruct((seq, D_MODEL), BF16), jax.ShapeDtypeStruct((seq, POOL_WIDTH), BF16),
                   jax.ShapeDtypeStruct((seq, D_MODEL), F32), jax.ShapeDtypeStruct((seq, D_MODEL), F32)],
        scratch_shapes=[pltpu.VMEM((ts + POOL_HALO, POOL_WIDTH), F32)],
        compiler_params=_params(),
    )(gug, gug, o, pool_w, pool_scale, w_out, g_post, x)


def _fwd_in1(x1, g, w, ts):
    seq = x1.shape[0]

    def body(x_ref, g_ref, w_ref, h_ref, proj_ref, glu_ref):
        h, _ = _rms_fwd(x_ref[...], g_ref[...])
        h = h.astype(BF16)
        h_ref[...] = h
        proj = _dot_nt(h, w_ref[...])
        proj_ref[...] = proj
        glu_ref[...] = proj[:, :D_MODEL] * _sigmoid(proj[:, D_MODEL:2 * D_MODEL])

    return pl.pallas_call(
        body, name="fwd_in1", grid=(seq // ts,),
        in_specs=[_tile(ts, D_MODEL), _full((1, D_MODEL)), _full((ODD_IN, D_MODEL))],
        out_specs=[_tile(ts, D_MODEL), _tile(ts, ODD_IN), _tile(ts, D_MODEL)],
        out_shape=[jax.ShapeDtypeStruct((seq, D_MODEL), BF16), jax.ShapeDtypeStruct((seq, ODD_IN), F32),
                   jax.ShapeDtypeStruct((seq, D_MODEL), F32)],
        compiler_params=_params(),
    )(x1, g, w)


ACC_LOSS, ACC_POST, ACC_LN_G, ACC_LN_B, ACC_DW_B = range(5)
CONV_FIRST = CONV_HALO - CONV_K + 1


def _fwd_tap(offset):
    return offset - CONV_FIRST if CONV_FIRST <= offset <= CONV_HALO else None


def _bwd_tap(offset):
    return CONV_K - 1 - offset if offset < CONV_K else None


def _conv_taps(w_ref, buf_ref, ts, lanes, tap_of_offset):
    out = None
    for b in range(SUBLANES):
        rows = ts if b == 0 else ts + SUBLANES
        part = None
        for a in range(CONV_HALO // SUBLANES + 1):
            k = tap_of_offset(SUBLANES * a + b)
            if k is None:
                continue
            term = w_ref[k:k + 1, lanes] * buf_ref[pl.ds(SUBLANES * a, rows), lanes]
            part = term if part is None else part + term
        if part is None:
            continue
        if b:
            part = part[b:b + ts, :]
        out = part if out is None else out + part
    return out


def _fwd_out1(glu, proj, dw_w, dw_b, ln_g, ln_b, w_out, g_post, x1, target, ts):
    seq = x1.shape[0]
    hb = ts // CONV_HALO

    def body(glu_ref, halo_ref, gate_ref, dww_ref, dwb_ref, lng_ref, lnb_ref, w_ref, g_ref, x1_ref, t_ref,
             ymix_ref, dy_ref, dx2_ref, dcf_ref, dgate_ref, acc_ref, gbuf):
        i = pl.program_id(0)

        @pl.when(i == 0)
        def _():
            acc_ref[...] = jnp.zeros_like(acc_ref)

        gbuf[:CONV_HALO, :] = jnp.where(i > 0, halo_ref[...], 0.0)
        gbuf[CONV_HALO:, :] = glu_ref[...]
        for lb in range(D_MODEL // LANES):
            lanes = slice(LANES * lb, LANES * (lb + 1))
            dcf_ref[:, lanes] = _conv_taps(dww_ref, gbuf, ts, lanes, _fwd_tap)
        cf = dcf_ref[...] + dwb_ref[...]
        mu = jnp.mean(cf, axis=-1, keepdims=True)
        cen = cf - mu
        rs = lax.rsqrt(jnp.mean(cen * cen, axis=-1, keepdims=True) + EPS)
        xhat = cen * rs
        cn = xhat * lng_ref[...] + lnb_ref[...]
        gate = gate_ref[...]
        sg = _sigmoid(gate)
        sc = _sigmoid(cn)
        silu_gate = gate * sg
        silu_cn = cn * sc
        ymix = (silu_cn * silu_gate).astype(BF16)
        ymix_ref[...] = ymix
        y = _dot(ymix, w_ref[...])
        yn, r = _rms_fwd(y, g_ref[...])
        err = (x1_ref[...] + yn) - t_ref[...]
        acc_ref[ACC_LOSS] += _rows8(err * err)
        dx2 = err * (1.0 / D_MODEL)
        dx2_ref[...] = dx2
        dy, dpost = _rms_bwd(y, r, g_ref[...], dx2)
        acc_ref[ACC_POST] += _rows8(dpost)
        dy = dy.astype(BF16)
        dy_ref[...] = dy
        dymix = _dot_nt(dy, w_ref[...])
        dgate_ref[...] = (dymix * silu_cn * (sg * (1.0 + gate * (1.0 - sg)))).astype(BF16)
        dcn = dymix * silu_gate * (sc * (1.0 + cn * (1.0 - sc)))
        acc_ref[ACC_LN_G] += _rows8(dcn * xhat)
        acc_ref[ACC_LN_B] += _rows8(dcn)
        dxhat = dcn * lng_ref[...]
        dcf = rs * (dxhat - jnp.mean(dxhat, axis=-1, keepdims=True)
                    - xhat * jnp.mean(dxhat * xhat, axis=-1, keepdims=True))
        acc_ref[ACC_DW_B] += _rows8(dcf)
        dcf_ref[...] = dcf

    return pl.pallas_call(
        body, name="fwd_out1", grid=(seq // ts,),
        in_specs=[_tile(ts, D_MODEL),
                  pl.BlockSpec((CONV_HALO, D_MODEL), lambda i: (jnp.maximum(i * hb - 1, 0), 0)),
                  _tile(ts, D_MODEL, 2), _full((CONV_HALO, D_MODEL)), _full((1, D_MODEL)), _full((1, D_MODEL)),
                  _full((1, D_MODEL)), _full((D_MODEL, D_MODEL)), _full((1, D_MODEL)), _tile(ts, D_MODEL),
                  _tile(ts, D_MODEL)],
        out_specs=[_tile(ts, D_MODEL), _tile(ts, D_MODEL), _tile(ts, D_MODEL), _tile(ts, D_MODEL), _tile(ts, D_MODEL),
                   _full((5, SUBLANES, D_MODEL))],
        out_shape=[jax.ShapeDtypeStruct((seq, D_MODEL), BF16), jax.ShapeDtypeStruct((seq, D_MODEL), BF16),
                   jax.ShapeDtypeStruct((seq, D_MODEL), F32), jax.ShapeDtypeStruct((seq, D_MODEL), F32),
                   jax.ShapeDtypeStruct((seq, D_MODEL), BF16), jax.ShapeDtypeStruct((5, SUBLANES, D_MODEL), F32)],
        scratch_shapes=[pltpu.VMEM((ts + CONV_HALO, D_MODEL), F32)],
        compiler_params=_params(),
    )(glu, glu, proj, dw_w, dw_b, ln_g, ln_b, w_out, g_post, x1, target)


def _bwd_in1(dcf, glu, proj, dgate, dw_w, w_in, x1, g_pre, dx2, ts):
    seq = x1.shape[0]
    hb = ts // CONV_HALO
    last = seq // CONV_HALO - 1
    nt = seq // ts

    def body(dcf_ref, dnext_ref, glu_ref, gprev_ref, ab_ref, dgate_ref, dww_ref, w_ref, x_ref, g_ref, dx2_ref,
             dproj_ref, dx1_ref, ddw_ref, dpre_ref, dbuf, gbuf, zbuf, sbuf):
        i = pl.program_id(0)

        @pl.when(i == 0)
        def _():
            ddw_ref[...] = jnp.zeros_like(ddw_ref)
            dpre_ref[...] = jnp.zeros_like(dpre_ref)

        dcf = dcf_ref[...]
        dbuf[:ts, :] = dcf
        dbuf[ts:, :] = jnp.where(i < nt - 1, dnext_ref[...], 0.0)
        gbuf[:CONV_HALO, :] = jnp.where(i > 0, gprev_ref[...], 0.0)
        gbuf[CONV_HALO:, :] = glu_ref[...]
        zbuf[:SUBLANES, :] = jnp.zeros((SUBLANES, D_MODEL), F32)
        zbuf[pl.ds(SUBLANES, ts), :] = dcf
        zbuf[pl.ds(SUBLANES + ts, SUBLANES), :] = jnp.zeros((SUBLANES, D_MODEL), F32)
        for lb in range(D_MODEL // LANES):
            lanes = slice(LANES * lb, LANES * (lb + 1))
            gate_lanes = slice(D_MODEL + LANES * lb, D_MODEL + LANES * (lb + 1))
            dglu = _conv_taps(dww_ref, dbuf, ts, lanes, _bwd_tap)
            a = ab_ref[:, lanes]
            sb = _sigmoid(ab_ref[:, gate_lanes])
            dproj_ref[:, lanes] = (dglu * sb).astype(BF16)
            dproj_ref[:, gate_lanes] = (dglu * a * sb * (1.0 - sb)).astype(BF16)
            for b in range(SUBLANES):
                rows = ts if b == 0 else ts + SUBLANES
                sbuf[b, pl.ds(0, rows), :] = zbuf[pl.ds(SUBLANES - b, rows), lanes]
                for a8 in range(CONV_HALO // SUBLANES + 1):
                    k = _fwd_tap(SUBLANES * a8 + b)
                    if k is not None:
                        ddw_ref[k, :, lanes] += _rows8(sbuf[b, pl.ds(0, rows), :] * gbuf[pl.ds(SUBLANES * a8, rows), lanes])
        dproj_ref[:, 2 * D_MODEL:] = dgate_ref[...]
        dh = _dot(dproj_ref[...], w_ref[...])
        x = x_ref[...]
        r = lax.rsqrt(jnp.mean(x * x, axis=-1, keepdims=True) + EPS)
        dx, dpre = _rms_bwd(x, r, g_ref[...], dh)
        dx1_ref[...] = dx2_ref[...] + dx
        dpre_ref[...] += _rows8(dpre)

    return pl.pallas_call(
        body, name="bwd_in1", grid=(nt,),
        in_specs=[_tile(ts, D_MODEL),
                  pl.BlockSpec((CONV_HALO, D_MODEL), lambda i: (jnp.minimum((i + 1) * hb, last), 0)),
                  _tile(ts, D_MODEL),
                  pl.BlockSpec((CONV_HALO, D_MODEL), lambda i: (jnp.maximum(i * hb - 1, 0), 0)),
                  _tile(ts, 2 * D_MODEL), _tile(ts, D_MODEL), _full((CONV_HALO, D_MODEL)),
                  _full((ODD_IN, D_MODEL)), _tile(ts, D_MODEL), _full((1, D_MODEL)), _tile(ts, D_MODEL)],
        out_specs=[_tile(ts, ODD_IN), _tile(ts, D_MODEL), _full((CONV_HALO, SUBLANES, D_MODEL)), _full((SUBLANES, D_MODEL))],
        out_shape=[jax.ShapeDtypeStruct((seq, ODD_IN), BF16), jax.ShapeDtypeStruct((seq, D_MODEL), F32),
                   jax.ShapeDtypeStruct((CONV_HALO, SUBLANES, D_MODEL), F32), jax.ShapeDtypeStruct((SUBLANES, D_MODEL), F32)],
        scratch_shapes=[pltpu.VMEM((ts + CONV_HALO, D_MODEL), F32), pltpu.VMEM((ts + CONV_HALO, D_MODEL), F32),
                        pltpu.VMEM((ts + 2 * SUBLANES, D_MODEL), F32), pltpu.VMEM((SUBLANES, ts + SUBLANES, LANES), F32)],
        compiler_params=_params(),
    )(dcf, dcf, glu, glu, proj, dgate, dw_w, w_in, x1, g_pre, dx2)


def _bwd_out0(dx1, y0, g_post, w_out, gug, o, pooled, pool_w, pool_scale, mix, ts):
    seq = dx1.shape[0]

    def body(dx1_ref, y_ref, g_ref, w_ref, gug_ref, o_ref, pooled_ref, pw_ref, ps_ref, mix_ref,
             do_ref, dgg_ref, dpooled_ref, dpost_ref, dscale_ref, dpw_ref, gw_ref, gacc):
        i = pl.program_id(0)

        @pl.when(i == 0)
        def _():
            dpost_ref[...] = jnp.zeros_like(dpost_ref)
            dscale_ref[...] = jnp.zeros_like(dscale_ref)
            dpw_ref[...] = jnp.zeros_like(dpw_ref)

        y = y_ref[...]
        r = lax.rsqrt(jnp.mean(y * y, axis=-1, keepdims=True) + EPS)
        dy, dpost = _rms_bwd(y, r, g_ref[...], dx1_ref[...])
        dpost_ref[...] += _rows8(dpost)
        dy = dy.astype(BF16)
        _accumulate_tn(gacc, mix_ref[...], dy, i == 0)
        dmix = _dot_nt(dy, w_ref[...])
        dya = dmix[:, :ATTN_WIDTH]
        dyb = dmix[:, ATTN_WIDTH:]
        ga = gug_ref[:, :ATTN_WIDTH]
        gb = gug_ref[:, ATTN_WIDTH + POOL_WIDTH:]
        sga = _sigmoid(ga)
        sgb = _sigmoid(gb)
        do_ref[...] = (dya * (ga * sga)).astype(BF16)
        dgg_ref[:, :ATTN_WIDTH] = (dya * o_ref[...] * (sga * (1.0 + ga * (1.0 - sga)))).astype(BF16)
        dypool = dyb * (gb * sgb)
        dsilu_gb = sgb * (1.0 + gb * (1.0 - sgb))
        for g in range(len(POOL_WINDOWS)):
            lanes = slice(POOL_GC * g, POOL_GC * (g + 1))
            pooled = pooled_ref[:, lanes]
            wg = pw_ref[g].astype(BF16)
            pw = _dot(pooled, wg)
            scale = ps_ref[:, lanes]
            dgg_ref[:, ATTN_WIDTH + POOL_GC * g:ATTN_WIDTH + POOL_GC * (g + 1)] = (
                dyb[:, lanes] * (pw * scale) * dsilu_gb[:, lanes]).astype(BF16)
            dscale_ref[:, lanes] += _rows8(dypool[:, lanes] * pw)
            dpw = (dypool[:, lanes] * scale).astype(BF16)
            dpooled_ref[:, lanes] = _dot_nt(dpw, wg)
            dpw_ref[g] += _dot_tn(pooled, dpw)

        @pl.when(i == seq // ts - 1)
        def _():
            gw_ref[...] = gacc[...].astype(BF16)

    return pl.pallas_call(
        body, name="bwd_out0", grid=(seq // ts,),
        in_specs=[_tile(ts, D_MODEL), _tile(ts, D_MODEL), _full((1, D_MODEL)), _resident((D_MODEL, D_MODEL)),
                  _tile(ts, 3 * POOL_WIDTH), _tile(ts, ATTN_WIDTH), _tile(ts, POOL_WIDTH),
                  _full((4, POOL_GC, POOL_GC)), _full((1, POOL_WIDTH)), _tile(ts, D_MODEL)],
        out_specs=[_tile(ts, ATTN_WIDTH), _tile(ts, ATTN_WIDTH + POOL_WIDTH), _tile(ts, POOL_WIDTH),
                   _full((SUBLANES, D_MODEL)), _full((SUBLANES, POOL_WIDTH)), _full((4, POOL_GC, POOL_GC)),
                   _full((D_MODEL, D_MODEL))],
        out_shape=[jax.ShapeDtypeStruct((seq, ATTN_WIDTH), BF16),
                   jax.ShapeDtypeStruct((seq, ATTN_WIDTH + POOL_WIDTH), BF16), jax.ShapeDtypeStruct((seq, POOL_WIDTH), F32),
                   jax.ShapeDtypeStruct((SUBLANES, D_MODEL), F32), jax.ShapeDtypeStruct((SUBLANES, POOL_WIDTH), F32),
                   jax.ShapeDtypeStruct((4, POOL_GC, POOL_GC), F32), jax.ShapeDtypeStruct((D_MODEL, D_MODEL), BF16)],
        scratch_shapes=[pltpu.VMEM((D_MODEL, D_MODEL), F32)],
        compiler_params=_params(),
    )(dx1, y0, g_post, w_out, gug, o, pooled, pool_w, pool_scale, mix)


def _attn_bwd(q, kv, do, sinks, parts):
    seq = q.shape[0]
    nb = seq // BLOCK

    def qblock(j):
        return jnp.minimum(j, nb - 1)

    n = len(parts)

    def body(*refs):
        sink_ref, q_ref, kvc_ref, kvp_ref, do_ref = refs[:5]
        dq_ref, dkv_ref, dsink_ref = refs[5 + n:8 + n]
        carry, dkv_acc = refs[8 + 2 * n:10 + 2 * n]
        scatter = _Scatter(refs[5:5 + n], refs[8 + n:8 + 2 * n], *refs[10 + 2 * n:])
        j = pl.program_id(0)

        @pl.when(j == 0)
        def _():
            scatter.begin()
            dsink_ref[...] = jnp.zeros_like(dsink_ref)
            carry[...] = jnp.zeros_like(carry)

        @pl.when(j < nb)
        def _():
            valid, distf = _attn_mask(GROUP_ROWS, j == 0)
            qb = q_ref[...]
            dob = do_ref[...]
            kk = jnp.concatenate([kvp_ref[...], kvc_ref[...]], axis=0)
            lane = lax.broadcasted_iota(jnp.int32, (BLOCK, LANES), 1)
            dsink = jnp.zeros((BLOCK, LANES), F32)
            for kvh in range(Q_HEADS // GROUP):
                kh = kk[:, HEAD_DIM * kvh:HEAD_DIM * (kvh + 1)]
                vh = kk[:, KV_WIDTH + HEAD_DIM * kvh:KV_WIDTH + HEAD_DIM * (kvh + 1)]
                qg = _group_rows(qb, kvh)
                dog = _group_rows(dob, kvh)
                sink, slope = _group_columns(sink_ref, kvh)
                p, psink = _attn_probs(qg, kh, sink, slope, valid, distf)
                dp = _dot_nt(dog, vh)
                delta = jnp.sum(p * dp, axis=-1, keepdims=True)
                ds = (p * (dp - delta) * (HEAD_DIM ** -0.5)).astype(BF16)
                dsink_rows = -psink * delta
                dqg = _dot(ds, kh).astype(BF16)
                for i in range(GROUP):
                    h = GROUP * kvh + i
                    dsink = dsink + jnp.where(lane == h, dsink_rows[BLOCK * i:BLOCK * (i + 1), :], 0.0)
                    dq_ref[:, HEAD_DIM * h:HEAD_DIM * (h + 1)] = dqg[BLOCK * i:BLOCK * (i + 1), :]
                dkv_acc[:, HEAD_DIM * kvh:HEAD_DIM * (kvh + 1)] = _dot_tn(ds, qg)
                dkv_acc[:, KV_WIDTH + HEAD_DIM * kvh:KV_WIDTH + HEAD_DIM * (kvh + 1)] = _dot_tn(p.astype(BF16), dog)
            dsink_ref[...] += dsink

            @pl.when(j > 0)
            def _():
                dkv_ref[...] = (carry[...] + dkv_acc[:BLOCK, :]).astype(BF16)

            carry[...] = dkv_acc[BLOCK:, :]

        @pl.when(j == nb)
        def _():
            dkv_ref[...] = carry[...].astype(BF16)
            scatter.finish()

    hbm = pl.BlockSpec(memory_space=pl.ANY)
    out = pl.pallas_call(
        body, name="attn_bwd", grid=(nb + 1,),
        in_specs=[pl.BlockSpec(memory_space=pltpu.SMEM),
                  pl.BlockSpec((BLOCK, ATTN_WIDTH), lambda j: (qblock(j), 0)),
                  pl.BlockSpec((BLOCK, 2 * KV_WIDTH), lambda j: (qblock(j), 0)),
                  pl.BlockSpec((BLOCK, 2 * KV_WIDTH), lambda j: (jnp.maximum(qblock(j) - 1, 0), 0)),
                  pl.BlockSpec((BLOCK, ATTN_WIDTH), lambda j: (qblock(j), 0))] + [hbm] * n,
        out_specs=[pl.BlockSpec((BLOCK, ATTN_WIDTH), lambda j: (qblock(j), 0)),
                   pl.BlockSpec((BLOCK, 2 * KV_WIDTH), lambda j: (jnp.maximum(j - 1, 0), 0)),
                   _full((BLOCK, LANES))] + [hbm] * n,
        out_shape=[jax.ShapeDtypeStruct((seq, ATTN_WIDTH), BF16), jax.ShapeDtypeStruct((seq, 2 * KV_WIDTH), BF16),
                   jax.ShapeDtypeStruct((BLOCK, LANES), F32)]
        + [jax.ShapeDtypeStruct((N_DEV - 1, *p.shape[1:]), p.dtype) for p in parts],
        scratch_shapes=[pltpu.VMEM((BLOCK, 2 * KV_WIDTH), F32), pltpu.VMEM((2 * BLOCK, 2 * KV_WIDTH), F32)]
        + _Scatter.semaphores(n),
        compiler_params=_params(),
    )(sinks, q, kv, kv, do, *parts)
    return out[:3], out[3:]


def _bwd_in0(dpooled, dq, dkv, dgg, w_in, x, g_pre, dx1, h0, early, ts):
    seq = x.shape[0]
    hb = ts // POOL_HALO
    last = seq // POOL_HALO - 1
    nt = seq // ts

    def body(dp_ref, dnext_ref, dq_ref, dkv_ref, dgg_ref, w_ref, x_ref, g_ref, dx1_ref, h_ref, early_ref,
             gx_ref, dpre_ref, gw_ref, all_ref, dbuf, dproj_ref, gacc, stage, sem, *gather_sems):
        i = pl.program_id(0)
        gather = lambda: _Gather([early_ref], [all_ref], *gather_sems)

        @pl.when(i == 0)
        def _():
            gather().begin()
            dpre_ref[...] = jnp.zeros_like(dpre_ref)

        dpool = dp_ref[...]
        dnext = jnp.where(i < nt - 1, dnext_ref[...], 0.0)
        u0 = ATTN_WIDTH + 2 * KV_WIDTH + ATTN_WIDTH
        for g, window in enumerate(POOL_WINDOWS):
            lanes = slice(POOL_GC * g, POOL_GC * (g + 1))
            dbuf[:ts, lanes] = dpool[:, lanes] / _pool_counts(i * ts, ts, window)
            dbuf[ts:, lanes] = dnext[:, lanes] / _pool_counts((i + 1) * ts, POOL_HALO, window)
        for g, window in enumerate(POOL_WINDOWS):
            lanes = slice(POOL_GC * g, POOL_GC * (g + 1))
            acc = dbuf[pl.ds(0, ts), lanes]
            for k in range(1, window):
                acc = acc + dbuf[pl.ds(k, ts), lanes]
            dproj_ref[:, u0 + POOL_GC * g:u0 + POOL_GC * (g + 1)] = (acc - dpool[:, lanes]).astype(BF16)
        dproj_ref[:, :ATTN_WIDTH] = dq_ref[...]
        dproj_ref[:, ATTN_WIDTH:ATTN_WIDTH + 2 * KV_WIDTH] = dkv_ref[...]
        dproj_ref[:, ATTN_WIDTH + 2 * KV_WIDTH:u0] = dgg_ref[:, :ATTN_WIDTH]
        dproj_ref[:, u0 + POOL_WIDTH:] = dgg_ref[:, ATTN_WIDTH:]
        dh = _dot(dproj_ref[...], w_ref[...])
        _accumulate_tn(gacc, dproj_ref[...], h_ref[...], i == 0)
        x = x_ref[...]
        r = lax.rsqrt(jnp.mean(x * x, axis=-1, keepdims=True) + EPS)
        dx, dpre = _rms_bwd(x, r, g_ref[...], dh)
        gx_ref[...] = dx1_ref[...] + dx
        dpre_ref[...] += _rows8(dpre)

        @pl.when(i == nt - 1)
        def _():
            _emit_grad(gacc, stage, gw_ref, sem)
            gather().finish()

    return pl.pallas_call(
        body, name="bwd_in0", grid=(nt,),
        in_specs=[_tile(ts, POOL_WIDTH),
                  pl.BlockSpec((POOL_HALO, POOL_WIDTH), lambda i: (jnp.minimum((i + 1) * hb, last), 0)),
                  _tile(ts, ATTN_WIDTH), _tile(ts, 2 * KV_WIDTH), _tile(ts, ATTN_WIDTH + POOL_WIDTH),
                  _resident((EVEN_IN, D_MODEL)), _tile(ts, D_MODEL), _full((1, D_MODEL)), _tile(ts, D_MODEL), _tile(ts, D_MODEL),
                  pl.BlockSpec(memory_space=pl.ANY)],
        out_specs=[_tile(ts, D_MODEL), _full((SUBLANES, D_MODEL)), pl.BlockSpec(memory_space=pl.ANY),
                   pl.BlockSpec(memory_space=pl.ANY)],
        out_shape=[jax.ShapeDtypeStruct((seq, D_MODEL), F32), jax.ShapeDtypeStruct((SUBLANES, D_MODEL), F32),
                   jax.ShapeDtypeStruct((EVEN_IN, D_MODEL), BF16), jax.ShapeDtypeStruct((N_DEV, *early.shape), early.dtype)],
        scratch_shapes=[pltpu.VMEM((ts + POOL_HALO, POOL_WIDTH), F32), pltpu.VMEM((ts, EVEN_IN), BF16),
                        pltpu.VMEM((EVEN_IN, D_MODEL), F32), pltpu.VMEM((GRAD_CHUNK, D_MODEL), BF16), pltpu.SemaphoreType.DMA]
        + _Gather.semaphores(1),
        compiler_params=_params(),
    )(dpooled, dpooled, dq, dkv, dgg, w_in, x, g_pre, dx1, h0, early)


def _matmul_tn(a, b, name, ts, tm):
    seq, m = a.shape
    n = b.shape[1]
    steps = seq // ts

    def body(a_ref, b_ref, o_ref, acc):
        s = pl.program_id(1)

        @pl.when(s == 0)
        def _():
            acc[...] = jnp.zeros_like(acc)

        acc[...] += _dot_tn(a_ref[...], b_ref[...])

        @pl.when(s == steps - 1)
        def _():
            o_ref[...] = acc[...].astype(BF16)

    return pl.pallas_call(
        body, name=name, grid=(m // tm, steps),
        in_specs=[pl.BlockSpec((ts, tm), lambda j, s: (s, j)), pl.BlockSpec((ts, n), lambda j, s: (s, 0))],
        out_specs=pl.BlockSpec((tm, n), lambda j, s: (j, 0)),
        out_shape=jax.ShapeDtypeStruct((m, n), BF16),
        scratch_shapes=[pltpu.VMEM((tm, n), F32)],
        compiler_params=pltpu.CompilerParams(dimension_semantics=("arbitrary", "arbitrary"), vmem_limit_bytes=VMEM_LIMIT),
    )(a, b)


def _adamw_math(w, g, m, v):
    m = ADAM_B1 * m + (1.0 - ADAM_B1) * g
    v = ADAM_B2 * v + (1.0 - ADAM_B2) * (g * g)
    m_hat = m / (1.0 - ADAM_B1 ** ADAM_STEP)
    v_hat = v / (1.0 - ADAM_B2 ** ADAM_STEP)
    delta = -ADAM_LR * (m_hat / (jnp.sqrt(v_hat) + ADAM_EPS) + ADAM_WD * w)
    return delta, m, v


def _adamw(ws, gs, ms, vs, name):
    n = len(ws)

    def body(*refs):
        ins, outs = refs[:4 * n], refs[4 * n:]
        for k in range(n):
            delta, m, v = _adamw_math(ins[k][...], ins[n + k][...], ins[2 * n + k][...], ins[3 * n + k][...])
            outs[k][...] = delta
            outs[n + k][...] = m
            outs[2 * n + k][...] = v

    shapes = [jax.ShapeDtypeStruct(w.shape, F32) for w in ws]
    out = pl.pallas_call(body, name=name, out_shape=shapes * 3,
                         compiler_params=pltpu.CompilerParams(vmem_limit_bytes=VMEM_LIMIT))(*ws, *gs, *ms, *vs)
    return out[:n], out[n:2 * n], out[2 * n:]


TS_MATMUL, TS_CONV, TS_GRAD = 512, 256, 1024


def kernel(x, pre_norm, post_norm, a_w_in, a_sinks, b_pool_w, b_pool_scale, ab_w_out, c_w_in, c_dw_w, c_dw_b, c_ln_g, c_ln_b, c_w_out, loss_target, m_pre_norm, m_post_norm, m_a_w_in, m_a_sinks, m_b_pool_w, m_b_pool_scale, m_ab_w_out, m_c_w_in, m_c_dw_w, m_c_dw_b, m_c_ln_g, m_c_ln_b, m_c_w_out, v_pre_norm, v_post_norm, v_a_w_in, v_a_sinks, v_b_pool_w, v_b_pool_scale, v_ab_w_out, v_c_w_in, v_c_dw_w, v_c_dw_b, v_c_ln_g, v_c_ln_b, v_c_w_out):
    seq = x.shape[1]
    ts_big, ts_conv, ts_grad = min(TS_MATMUL, seq), min(TS_CONV, seq), min(TS_GRAD, seq)
    x2d = x[0]
    target = loss_target[0]
    ch = c_dw_b.shape[1]

    whole = lambda g: g.reshape(-1, D_MODEL)
    vec_rows = 40
    vecs = jnp.concatenate([c_dw_w[0, :, 0, :], c_dw_b, c_ln_g, c_ln_b, jnp.zeros((vec_rows - CONV_K - 3, ch), F32)], axis=0)
    w_in0t, vg = _all_gather([a_w_in[0].T.astype(BF16), vecs], "gather_first")
    w_in0t = whole(w_in0t)
    vg = vg.transpose(1, 0, 2).reshape(vec_rows, D_MODEL)
    dw_w = vg[:CONV_HALO]
    dw_b, ln_g, ln_b = vg[CONV_K:CONV_K + 1], vg[CONV_K + 1:CONV_K + 2], vg[CONV_K + 2:CONV_K + 3]

    sinks = a_sinks[0]
    (h0, q, kv, gug), (w_out0,) = _fwd_in0(x2d, pre_norm[0:1], w_in0t, [ab_w_out[0].astype(BF16)], ts_big)
    o, (w_in1t, w_out1) = _attn_fwd(q, kv, sinks, [c_w_in[0].T.astype(BF16), c_w_out[0].astype(BF16)])
    w_out0, w_in1t, w_out1 = whole(w_out0), whole(w_in1t), whole(w_out1)
    mix0, pooled, y0, x1 = _fwd_out0(gug, o, b_pool_w[0], b_pool_scale, w_out0, post_norm[0:1], x2d, ts_big)
    h1, proj1, glu = _fwd_in1(x1, pre_norm[1:2], w_in1t, ts_big)
    ymix1, dy1, dx2, dcf, dgate, acc1 = _fwd_out1(glu, proj1, dw_w, dw_b, ln_g, ln_b, w_out1, post_norm[1:2], x1, target, ts_conv)

    dproj1, dx1, ddw_w, dpre1 = _bwd_in1(dcf, glu, proj1, dgate, dw_w, w_in1t, x1, pre_norm[1:2], dx2, ts_conv)
    g_in1t = _matmul_tn(dproj1, h1, "grad_w_in1", ts_grad, 1024)
    g_out1 = _matmul_tn(ymix1, dy1, "grad_w_out1", ts_grad, 1024)
    do, dgg, dpooled, dpost0, dscale, dpool_w, g_out0 = _bwd_out0(dx1, y0, post_norm[0:1], w_out0, gug, o, pooled, b_pool_w[0], b_pool_scale, mix0, ts_big)
    slabs = lambda g: g.reshape(N_DEV, -1, D_MODEL)
    me = 4 * lax.axis_index("x") + 2 * lax.axis_index("y") + lax.axis_index("c")
    early = [slabs(g_in1t), slabs(g_out1), slabs(g_out0)]
    (dq, dkv, dsink), arrived = _attn_bwd(q, kv, do, sinks, early)
    g_in1t, g_c_w_out, g_ab_w_out = [_sum_slabs(lax.dynamic_index_in_dim(p, me, keepdims=False), r, name)
                                     for p, r, name in zip(early, arrived, ("sum_w_in1", "sum_w_out1", "sum_w_out0"))]
    row = lambda a: jnp.sum(a, axis=0, keepdims=True)
    lanes8 = lambda a: row(a).reshape(-1, LANES)
    loss_row = jnp.pad(jnp.sum(acc1[ACC_LOSS]).reshape(1, 1), ((0, 0), (0, LANES - 1)))
    early = jnp.concatenate([lanes8(dpre1), lanes8(dpost0), lanes8(acc1[ACC_POST]), lanes8(dscale),
                             loss_row, jnp.zeros((2, LANES), F32), row(dsink), dpool_w.reshape(4 * POOL_GC, LANES)], axis=0)
    grad_x, dpre0, g_in0t, early_all = _bwd_in0(dpooled, dq, dkv, dgg, w_in0t, x2d, pre_norm[0:1], dx1, h0, early, ts_big)

    vec_g = jnp.concatenate([jnp.sum(ddw_w[:CONV_K], axis=1), row(acc1[ACC_DW_B]), row(acc1[ACC_LN_G]), row(acc1[ACC_LN_B]),
                             jnp.zeros((vec_rows - CONV_K - 3, D_MODEL), F32)], axis=0)
    g_in0t, vec_g, g_pre0, rep = _final_reduce(slabs(g_in0t), vec_g.reshape(vec_rows, N_DEV, ch).transpose(1, 0, 2),
                                               lanes8(dpre0), early_all, "final_reduce")
    g_a_w_in, g_c_w_in = g_in0t.T, g_in1t.T
    g_dw_w, g_dw_b, g_ln_g, g_ln_b = vec_g[:CONV_K], vec_g[CONV_K:CONV_K + 1], vec_g[CONV_K + 1:CONV_K + 2], vec_g[CONV_K + 2:CONV_K + 3]
    g_pre = jnp.concatenate([g_pre0, rep[:8]], axis=0).reshape(2, D_MODEL)
    g_post = rep[8:24].reshape(2, D_MODEL)
    g_scale = rep[24:28].reshape(1, POOL_WIDTH)
    loss = (0.5 / D_MODEL) * rep[28, 0]
    g_sinks = rep[31:32, :Q_HEADS]
    g_pool_w = rep[32:]

    grads = [g_pre, g_post, g_a_w_in, g_sinks, g_pool_w, g_scale, g_ab_w_out, g_c_w_in, g_dw_w, g_dw_b, g_ln_g, g_ln_b, g_c_w_out]
    weights = [pre_norm, post_norm, a_w_in, a_sinks, b_pool_w, b_pool_scale, ab_w_out, c_w_in, c_dw_w, c_dw_b, c_ln_g, c_ln_b, c_w_out]
    m_in = [m_pre_norm, m_post_norm, m_a_w_in, m_a_sinks, m_b_pool_w, m_b_pool_scale, m_ab_w_out, m_c_w_in, m_c_dw_w, m_c_dw_b, m_c_ln_g, m_c_ln_b, m_c_w_out]
    v_in = [v_pre_norm, v_post_norm, v_a_w_in, v_a_sinks, v_b_pool_w, v_b_pool_scale, v_ab_w_out, v_c_w_in, v_c_dw_w, v_c_dw_b, v_c_ln_g, v_c_ln_b, v_c_w_out]
    flat = lambda arrs: [a.reshape(g.shape) for a, g in zip(arrs, grads)]
    big = (2, 6, 7, 12)
    small = tuple(k for k in range(len(grads)) if k not in big)
    pick = lambda arrs, idx: [arrs[k] for k in idx]
    deltas, new_m, new_v = [None] * 13, [None] * 13, [None] * 13
    for idx, name in ((big, "adamw_matrices"), (small, "adamw_vectors")):
        d, m, v = _adamw(pick(flat(weights), idx), pick(grads, idx), pick(flat(m_in), idx), pick(flat(v_in), idx), name)
        for k, dk, mk, vk in zip(idx, d, m, v):
            deltas[k], new_m[k], new_v[k] = dk, mk, vk
    shaped = lambda arrs: [a.reshape(w.shape) for a, w in zip(arrs, weights)]
    return (loss, grad_x[None], *shaped(grads), *shaped(deltas), *shaped(new_m), *shaped(new_v))
```

```python
import functools

import jax
import jax.numpy as jnp
from jax import lax
from jax.experimental import pallas as pl
from jax.experimental.pallas import tpu as pltpu

F32 = jnp.float32
BF16 = jnp.bfloat16
MESH = pl.DeviceIdType.MESH
AXES = ("x", "y", "c")
N_DEV = 8

D_MODEL = 1024
HEAD_DIM = 64
Q_HEADS = 8
GROUP = 4
ATTN_WIDTH = 512
KV_WIDTH = 128
BLOCK = 128
POOL_WIDTH = 512
POOL_WINDOWS = (2, 4, 8, 16)
POOL_GC = 128
POOL_HALO = 16
EVEN_IN = 2304
CONV_K = 31
CONV_HALO = 32
ODD_IN = 3072
EPS = 1e-6
NEG = -1e30
SLOPES = tuple(2.0 ** (-8.0 * (h + 1) / Q_HEADS) for h in range(Q_HEADS))

ADAM_LR = 0.001
ADAM_B1 = 0.9
ADAM_B2 = 0.999
ADAM_EPS = 1e-08
ADAM_WD = 0.01
ADAM_STEP = 10

SUBLANES = 8
LANES = 128
VMEM_LIMIT = 56 * 1024 * 1024

NT = (((1,), (1,)), ((), ()))
TN = (((0,), (0,)), ((), ()))


def _params(**kw):
    return pltpu.CompilerParams(dimension_semantics=("arbitrary",), vmem_limit_bytes=VMEM_LIMIT, **kw)


def _dot(a, b):
    return jnp.dot(a, b, preferred_element_type=F32)


def _dot_nt(a, b):
    return lax.dot_general(a, b, NT, preferred_element_type=F32)


def _dot_tn(a, b):
    return lax.dot_general(a, b, TN, preferred_element_type=F32)


def _sigmoid(v):
    return 1.0 / (1.0 + jnp.exp(-v))


def _rows8(v):
    r, c = v.shape
    return jnp.sum(v.reshape(r // SUBLANES, SUBLANES, c), axis=0)


def _rms_fwd(v, g):
    r = lax.rsqrt(jnp.mean(v * v, axis=-1, keepdims=True) + EPS)
    return v * r * g, r


def _rms_bwd(v, r, g, dout):
    gd = dout * g
    dv = r * gd - v * (r * r * r) * jnp.mean(v * gd, axis=-1, keepdims=True)
    return dv, dout * (v * r)


def _full(shape):
    return pl.BlockSpec(shape, lambda i: (0,) * len(shape))


def _resident(shape):
    return pl.BlockSpec(shape, lambda i: (0,) * len(shape), pipeline_mode=pl.Buffered(1))


def _accumulate_tn(acc, a, b, first_step):
    @pl.when(first_step)
    def _():
        acc[...] = jnp.zeros_like(acc)

    acc[...] += _dot_tn(a, b)


def _tile(ts, cols, col_block=0):
    return pl.BlockSpec((ts, cols), lambda i: (i, col_block))


def _position():
    return lax.axis_index("x"), lax.axis_index("y"), lax.axis_index("c")


class _Gather:
    def __init__(self, x_refs, out_refs, send_sems, recv_sems, local_sems):
        self.x_refs, self.out_refs = x_refs, out_refs
        self.send_sems, self.recv_sems, self.local_sems = send_sems, recv_sems, local_sems
        self.n = len(x_refs)
        x, y, c = _position()
        self.c = c
        self.me, self.sibling = (x, y, c), (x, y, 1 - c)
        self.chips = [(1 - x, y), (x, 1 - y), (1 - x, 1 - y)]

    def copy(self, k, j, owner, to, src=None):
        slab = self.out_refs[k].at[4 * owner[0] + 2 * owner[1] + owner[2]]
        return pltpu.make_async_remote_copy(
            src_ref=slab if src is None else src, dst_ref=slab, send_sem=self.send_sems.at[7 * k + j],
            recv_sem=self.recv_sems.at[7 * k + j], device_id=to, device_id_type=MESH)

    def mine(self, k):
        return pltpu.make_async_copy(self.x_refs[k], self.out_refs[k].at[4 * self.me[0] + 2 * self.me[1] + self.c],
                                     self.local_sems.at[k])

    def first(self, k):
        out = [self.copy(k, 0, self.me, self.sibling, src=self.x_refs[k])]
        return out + [self.copy(k, 1 + j, self.me, (*chip, self.c), src=self.x_refs[k]) for j, chip in enumerate(self.chips)]

    def begin(self):
        for k in range(self.n):
            self.mine(k).start()
            for cp in self.first(k):
                cp.start()

    def finish(self):
        passed = []
        for k in range(self.n):
            for j, chip in enumerate(self.chips):
                self.copy(k, 1 + j, (*chip, self.c), self.me).wait_recv()
                passed.append(self.copy(k, 4 + j, (*chip, self.c), self.sibling))
                passed[-1].start()
        for k in range(self.n):
            self.copy(k, 0, self.sibling, self.me).wait_recv()
            for j, chip in enumerate(self.chips):
                self.copy(k, 4 + j, (*chip, 1 - self.c), self.me).wait_recv()
        for k in range(self.n):
            for cp in self.first(k):
                cp.wait_send()
        for cp in passed:
            cp.wait_send()
        for k in range(self.n):
            self.mine(k).wait()

    @staticmethod
    def semaphores(n):
        return [pltpu.SemaphoreType.DMA((7 * n,)), pltpu.SemaphoreType.DMA((7 * n,)), pltpu.SemaphoreType.DMA((n,))]


def _all_gather(blocks, name):
    n = len(blocks)

    def body(*refs):
        gather = _Gather(refs[:n], refs[n:2 * n], *refs[2 * n:])
        gather.begin()
        gather.finish()

    return pl.pallas_call(
        body, name=name,
        out_shape=[jax.ShapeDtypeStruct((N_DEV, *b.shape), b.dtype) for b in blocks],
        in_specs=[pl.BlockSpec(memory_space=pltpu.VMEM)] * n,
        out_specs=[pl.BlockSpec(memory_space=pltpu.VMEM)] * n,
        scratch_shapes=_Gather.semaphores(n),
        compiler_params=pltpu.CompilerParams(vmem_limit_bytes=VMEM_LIMIT),
    )(*blocks)


class _Scatter:
    def __init__(self, part_refs, recv_refs, send_sems, recv_sems):
        self.part_refs, self.recv_refs, self.send_sems, self.recv_sems = part_refs, recv_refs, send_sems, recv_sems
        self.n = len(part_refs)

    def copies(self):
        x, y, c = _position()
        me = 4 * x + 2 * y + c
        out = []
        for k in range(self.n):
            for j in range(N_DEV - 1):
                d = j + 1
                out.append(pltpu.make_async_remote_copy(
                    src_ref=self.part_refs[k].at[me ^ d], dst_ref=self.recv_refs[k].at[j],
                    send_sem=self.send_sems.at[7 * k + j], recv_sem=self.recv_sems.at[7 * k + j],
                    device_id=(x ^ (d >> 2), y ^ ((d >> 1) & 1), c ^ (d & 1)), device_id_type=MESH))
        return out

    def begin(self):
        for cp in self.copies():
            cp.start()

    def finish(self):
        for cp in self.copies():
            cp.wait_recv()
        for cp in self.copies():
            cp.wait_send()

    @staticmethod
    def semaphores(n):
        return [pltpu.SemaphoreType.DMA((7 * n,)), pltpu.SemaphoreType.DMA((7 * n,))]


def _sum_slabs(own, recv, name):
    rows, cols = own.shape

    def body(own_ref, recv_ref, out_ref):
        acc = own_ref[...].astype(F32)
        for j in range(N_DEV - 1):
            acc = acc + recv_ref[j].astype(F32)
        out_ref[...] = acc

    return pl.pallas_call(body, name=name, out_shape=jax.ShapeDtypeStruct((rows, cols), F32))(own, recv)


N_CHIPS = 4


def _final_reduce(parts, vec_parts, rep, early_all, name):
    _, rows, cols = parts.shape
    vrows, prows = vec_parts.shape[1], rep.shape[0]
    peers = N_CHIPS - 1

    def body(parts_ref, vec_ref, rep_ref, early_ref, out_ref, vec_out, rep_out, early_out, recv_a, own_a, mid, recv_b, vec_recv, rep_all,
             a_send, a_recv, a_local, b_send, b_recv, v_send, v_recv, r_send, r_recv):
        x, y, c = _position()
        chip = 2 * x + y
        me = 2 * chip + c
        everyone = [(d, (x ^ (d >> 2), y ^ ((d >> 1) & 1), c ^ (d & 1))) for d in range(1, N_DEV)]

        stage1, loads = [], []
        for j in range(N_CHIPS):
            stage1.append(pltpu.make_async_remote_copy(
                src_ref=parts_ref.at[2 * j + 1 - c], dst_ref=recv_a.at[j], send_sem=a_send.at[j], recv_sem=a_recv.at[j],
                device_id=(x, y, 1 - c), device_id_type=MESH))
            loads.append(pltpu.make_async_copy(parts_ref.at[2 * j + c], own_a.at[j], a_local.at[j]))
            stage1[-1].start()
            loads[-1].start()
        small = []
        for d, to in everyone:
            small.append(pltpu.make_async_remote_copy(
                src_ref=vec_ref.at[me ^ d], dst_ref=vec_recv.at[d - 1], send_sem=v_send.at[d - 1], recv_sem=v_recv.at[d - 1],
                device_id=to, device_id_type=MESH))
            small.append(pltpu.make_async_remote_copy(
                src_ref=rep_ref, dst_ref=rep_all.at[me], send_sem=r_send.at[d - 1], recv_sem=r_recv.at[d - 1],
                device_id=to, device_id_type=MESH))
        for cp in small:
            cp.start()
        rep_all[me] = rep_ref[...]

        for j in range(N_CHIPS):
            loads[j].wait()
            stage1[j].wait_recv()
            mid[j] = (own_a[j].astype(F32) + recv_a[j].astype(F32)).astype(BF16)
        stage2 = []
        for d in range(1, N_CHIPS):
            stage2.append(pltpu.make_async_remote_copy(
                src_ref=mid.at[chip ^ d], dst_ref=recv_b.at[d - 1], send_sem=b_send.at[d - 1], recv_sem=b_recv.at[d - 1],
                device_id=(x ^ (d >> 1), y ^ (d & 1), c), device_id_type=MESH))
            stage2[-1].start()

        for cp in small:
            cp.wait_recv()
        vec_sum = vec_ref[me]
        for j in range(N_DEV - 1):
            vec_sum = vec_sum + vec_recv[j]
        vec_out[...] = vec_sum
        rep_sum = rep_all[0]
        for d in range(1, N_DEV):
            rep_sum = rep_sum + rep_all[d]
        rep_out[...] = rep_sum
        early_sum = early_ref[0]
        for d in range(1, N_DEV):
            early_sum = early_sum + early_ref[d]
        early_out[...] = early_sum

        acc = mid[chip].astype(F32)
        for d in range(peers):
            stage2[d].wait_recv()
            acc = acc + recv_b[d].astype(F32)
        out_ref[...] = acc
        for cp in stage1 + stage2 + small:
            cp.wait_send()

    vmem = pl.BlockSpec(memory_space=pltpu.VMEM)
    dma = pltpu.SemaphoreType.DMA
    return pl.pallas_call(
        body, name=name,
        out_shape=[jax.ShapeDtypeStruct((rows, cols), F32), jax.ShapeDtypeStruct((vrows, LANES), F32),
                   jax.ShapeDtypeStruct((prows, LANES), F32), jax.ShapeDtypeStruct(early_all.shape[1:], F32)],
        in_specs=[pl.BlockSpec(memory_space=pl.ANY), vmem, vmem, vmem],
        out_specs=[vmem, vmem, vmem, vmem],
        scratch_shapes=[pltpu.VMEM((N_CHIPS, rows, cols), BF16), pltpu.VMEM((N_CHIPS, rows, cols), BF16),
                        pltpu.VMEM((N_CHIPS, rows, cols), BF16), pltpu.VMEM((peers, rows, cols), BF16),
                        pltpu.VMEM((N_DEV - 1, vrows, LANES), F32), pltpu.VMEM((N_DEV, prows, LANES), F32),
                        dma((N_CHIPS,)), dma((N_CHIPS,)), dma((N_CHIPS,)), dma((peers,)), dma((peers,)),
                        dma((N_DEV - 1,)), dma((N_DEV - 1,)), dma((N_DEV - 1,)), dma((N_DEV - 1,))],
        compiler_params=pltpu.CompilerParams(vmem_limit_bytes=VMEM_LIMIT),
    )(parts, vec_parts, rep, early_all)


def _fwd_in0(x, g, w, shards, ts):
    seq = x.shape[0]
    n = len(shards)
    steps = seq // ts

    def body(*refs):
        x_ref, g_ref, w_ref = refs[:3]
        h_ref, q_ref, kv_ref, gug_ref = refs[3 + n:7 + n]
        gather = lambda: _Gather(refs[3:3 + n], refs[7 + n:7 + 2 * n], *refs[7 + 2 * n:])
        i = pl.program_id(0)

        @pl.when(i == 0)
        def _():
            gather().begin()

        h, _ = _rms_fwd(x_ref[...], g_ref[...])
        h = h.astype(BF16)
        h_ref[...] = h
        proj = _dot_nt(h, w_ref[...])
        q_ref[...] = proj[:, :ATTN_WIDTH].astype(BF16)
        kv_ref[...] = proj[:, ATTN_WIDTH:ATTN_WIDTH + 2 * KV_WIDTH].astype(BF16)
        gug_ref[...] = proj[:, ATTN_WIDTH + 2 * KV_WIDTH:]

        @pl.when(i == steps - 1)
        def _():
            gather().finish()

    hbm = pl.BlockSpec(memory_space=pl.ANY)
    out = pl.pallas_call(
        body, name="fwd_in0", grid=(steps,),
        in_specs=[_tile(ts, D_MODEL), _full((1, D_MODEL)), _full((EVEN_IN, D_MODEL))] + [hbm] * n,
        out_specs=[_tile(ts, D_MODEL), _tile(ts, ATTN_WIDTH), _tile(ts, 2 * KV_WIDTH), _tile(ts, 3 * POOL_WIDTH)] + [hbm] * n,
        out_shape=[jax.ShapeDtypeStruct((seq, D_MODEL), BF16), jax.ShapeDtypeStruct((seq, ATTN_WIDTH), BF16),
                   jax.ShapeDtypeStruct((seq, 2 * KV_WIDTH), BF16), jax.ShapeDtypeStruct((seq, 3 * POOL_WIDTH), F32)]
        + [jax.ShapeDtypeStruct((N_DEV, *b.shape), b.dtype) for b in shards],
        scratch_shapes=_Gather.semaphores(n),
        compiler_params=_params(),
    )(x, g, w, *shards)
    return out[:4], out[4:]


GROUP_ROWS = GROUP * BLOCK


def _attn_mask(rows, first_block):
    row = lax.broadcasted_iota(jnp.int32, (rows, 2 * BLOCK), 0) & (BLOCK - 1)
    col = lax.broadcasted_iota(jnp.int32, (rows, 2 * BLOCK), 1)
    dist = row + BLOCK - col
    valid = (dist >= 0) & (dist < BLOCK) & ((col >= BLOCK) | jnp.logical_not(first_block))
    return valid, dist.astype(F32)


def _group_rows(block, kvh):
    return jnp.concatenate([block[:, HEAD_DIM * h:HEAD_DIM * (h + 1)] for h in range(GROUP * kvh, GROUP * (kvh + 1))], axis=0)


def _group_columns(sink_ref, kvh):
    head = lax.broadcasted_iota(jnp.int32, (GROUP_ROWS, 1), 0) // BLOCK
    sink = jnp.zeros((GROUP_ROWS, 1), F32)
    slope = jnp.zeros((GROUP_ROWS, 1), F32)
    for i in range(GROUP):
        sink = jnp.where(head == i, sink_ref[GROUP * kvh + i], sink)
        slope = jnp.where(head == i, SLOPES[GROUP * kvh + i], slope)
    return sink, slope


def _attn_probs(qh, kh, sink, slope, valid, distf):
    s = _dot_nt(qh, kh) * (HEAD_DIM ** -0.5)
    s = jnp.where(valid, s - slope * distf, NEG)
    mx = jnp.maximum(jnp.max(s, axis=-1, keepdims=True), sink)
    e = jnp.exp(s - mx)
    es = jnp.exp(sink - mx)
    den = jnp.sum(e, axis=-1, keepdims=True) + es
    return e / den, es / den


def _attn_fwd(q, kv, sinks, shards):
    seq = q.shape[0]
    nb = seq // BLOCK
    ns = len(shards)

    def body(*refs):
        sink_ref, q_ref, kvc_ref, kvp_ref = refs[:4]
        o_ref = refs[4 + ns]
        gather = lambda: _Gather(refs[4:4 + ns], refs[5 + ns:5 + 2 * ns], *refs[5 + 2 * ns:])
        n = pl.program_id(0)

        @pl.when(n == 0)
        def _():
            gather().begin()

        valid, distf = _attn_mask(BLOCK, n == 0)
        qb = q_ref[...]
        kk = jnp.concatenate([kvp_ref[...], kvc_ref[...]], axis=0)
        for h in range(Q_HEADS):
            kvh = h // GROUP
            qh = qb[:, HEAD_DIM * h:HEAD_DIM * (h + 1)]
            kh = kk[:, HEAD_DIM * kvh:HEAD_DIM * (kvh + 1)]
            vh = kk[:, KV_WIDTH + HEAD_DIM * kvh:KV_WIDTH + HEAD_DIM * (kvh + 1)]
            p, _ = _attn_probs(qh, kh, sink_ref[h], SLOPES[h], valid, distf)
            o_ref[:, HEAD_DIM * h:HEAD_DIM * (h + 1)] = _dot(p.astype(BF16), vh)

        @pl.when(n == nb - 1)
        def _():
            gather().finish()

    hbm = pl.BlockSpec(memory_space=pl.ANY)
    out = pl.pallas_call(
        body, name="attn_fwd", grid=(nb,),
        in_specs=[pl.BlockSpec(memory_space=pltpu.SMEM),
                  pl.BlockSpec((BLOCK, ATTN_WIDTH), lambda n: (n, 0)),
                  pl.BlockSpec((BLOCK, 2 * KV_WIDTH), lambda n: (n, 0)),
                  pl.BlockSpec((BLOCK, 2 * KV_WIDTH), lambda n: (jnp.maximum(n - 1, 0), 0))] + [hbm] * ns,
        out_specs=[pl.BlockSpec((BLOCK, ATTN_WIDTH), lambda n: (n, 0))] + [hbm] * ns,
        out_shape=[jax.ShapeDtypeStruct((seq, ATTN_WIDTH), F32)]
        + [jax.ShapeDtypeStruct((N_DEV, *b.shape), b.dtype) for b in shards],
        scratch_shapes=_Gather.semaphores(ns),
        compiler_params=_params(),
    )(sinks, q, kv, kv, *shards)
    return out[0], out[1:]


def _pool_counts(first_row, rows, window):
    t = first_row + lax.broadcasted_iota(jnp.int32, (rows, 1), 0)
    return jnp.minimum(t + 1, window).astype(F32)


def _fwd_out0(gug, o, pool_w, pool_scale, w_out, g_post, x, ts):
    seq = x.shape[0]
    hb = ts // POOL_HALO

    def body(gug_ref, halo_ref, o_ref, pw_ref, ps_ref, w_ref, g_ref, x_ref, mix_ref, pooled_ref, y_ref, x1_ref, ubuf):
        i = pl.program_id(0)
        ga = gug_ref[:, :ATTN_WIDTH]
        u = gug_ref[:, ATTN_WIDTH:ATTN_WIDTH + POOL_WIDTH]
        gb = gug_ref[:, ATTN_WIDTH + POOL_WIDTH:]
        mix_ref[:, :ATTN_WIDTH] = (o_ref[...] * (ga * _sigmoid(ga))).astype(BF16)
        ubuf[:POOL_HALO, :] = jnp.where(i > 0, halo_ref[...], 0.0)
        ubuf[POOL_HALO:, :] = u
        silu_gb = gb * _sigmoid(gb)
        for g, window in enumerate(POOL_WINDOWS):
            lanes = slice(POOL_GC * g, POOL_GC * (g + 1))
            acc = ubuf[pl.ds(POOL_HALO, ts), lanes]
            for k in range(1, window):
                acc = acc + ubuf[pl.ds(POOL_HALO - k, ts), lanes]
            pooled = (acc / _pool_counts(i * ts, ts, window) - u[:, lanes]).astype(BF16)
            pooled_ref[:, lanes] = pooled
            ypool = _dot(pooled, pw_ref[g].astype(BF16)) * ps_ref[:, lanes]
            mix_ref[:, ATTN_WIDTH + POOL_GC * g:ATTN_WIDTH + POOL_GC * (g + 1)] = (ypool * silu_gb[:, lanes]).astype(BF16)
        y = _dot(mix_ref[...], w_ref[...])
        y_ref[...] = y
        yn, _ = _rms_fwd(y, g_ref[...])
        x1_ref[...] = x_ref[...] + yn

    return pl.pallas_call(
        body, name="fwd_out0", grid=(seq // ts,),
        in_specs=[_tile(ts, 3 * POOL_WIDTH),
                  pl.BlockSpec((POOL_HALO, POOL_WIDTH), lambda i: (jnp.maximum(i * hb - 1, 0), 1)),
                  _tile(ts, ATTN_WIDTH), _full((4, POOL_GC, POOL_GC)), _full((1, POOL_WIDTH)),
                  _full((D_MODEL, D_MODEL)), _full((1, D_MODEL)), _tile(ts, D_MODEL)],
        out_specs=[_tile(ts, D_MODEL), _tile(ts, POOL_WIDTH), _tile(ts, D_MODEL), _tile(ts, D_MODEL)],
        out_shape=[jax.ShapeDtypeStruct((seq, D_MODEL), BF16), jax.ShapeDtypeStruct((seq, POOL_WIDTH), BF16),
                   jax.ShapeDtypeStruct((seq, D_MODEL), F32), jax.ShapeDtypeStruct((seq, D_MODEL), F32)],
        scratch_shapes=[pltpu.VMEM((ts + POOL_HALO, POOL_WIDTH), F32)],
        compiler_params=_params(),
    )(gug, gug, o, pool_w, pool_scale, w_out, g_post, x)


def _fwd_in1(x1, g, w, ts):
    seq = x1.shape[0]

    def body(x_ref, g_ref, w_ref, h_ref, proj_ref, glu_ref):
        h, _ = _rms_fwd(x_ref[...], g_ref[...])
        h = h.astype(BF16)
        h_ref[...] = h
        proj = _dot_nt(h, w_ref[...])
        proj_ref[...] = proj
        glu_ref[...] = proj[:, :D_MODEL] * _sigmoid(proj[:, D_MODEL:2 * D_MODEL])

    return pl.pallas_call(
        body, name="fwd_in1", grid=(seq // ts,),
        in_specs=[_tile(ts, D_MODEL), _full((1, D_MODEL)), _full((ODD_IN, D_MODEL))],
        out_specs=[_tile(ts, D_MODEL), _tile(ts, ODD_IN), _tile(ts, D_MODEL)],
        out_shape=[jax.ShapeDtypeStruct((seq, D_MODEL), BF16), jax.ShapeDtypeStruct((seq, ODD_IN), F32),
                   jax.ShapeDtypeStruct((seq, D_MODEL), F32)],
        compiler_params=_params(),
    )(x1, g, w)


ACC_LOSS, ACC_POST, ACC_LN_G, ACC_LN_B, ACC_DW_B = range(5)
CONV_FIRST = CONV_HALO - CONV_K + 1


def _fwd_tap(offset):
    return offset - CONV_FIRST if CONV_FIRST <= offset <= CONV_HALO else None


def _bwd_tap(offset):
    return CONV_K - 1 - offset if offset < CONV_K else None


def _conv_taps(w_ref, buf_ref, ts, lanes, tap_of_offset):
    out = None
    for b in range(SUBLANES):
        rows = ts if b == 0 else ts + SUBLANES
        part = None
        for a in range(CONV_HALO // SUBLANES + 1):
            k = tap_of_offset(SUBLANES * a + b)
            if k is None:
                continue
            term = w_ref[k:k + 1, lanes] * buf_ref[pl.ds(SUBLANES * a, rows), lanes]
            part = term if part is None else part + term
        if part is None:
            continue
        if b:
            part = part[b:b + ts, :]
        out = part if out is None else out + part
    return out


def _fwd_out1(glu, proj, dw_w, dw_b, ln_g, ln_b, w_out, g_post, x1, target, ts):
    seq = x1.shape[0]
    hb = ts // CONV_HALO

    def body(glu_ref, halo_ref, gate_ref, dww_ref, dwb_ref, lng_ref, lnb_ref, w_ref, g_ref, x1_ref, t_ref,
             ymix_ref, dy_ref, dx2_ref, dcf_ref, dgate_ref, acc_ref, gbuf):
        i = pl.program_id(0)

        @pl.when(i == 0)
        def _():
            acc_ref[...] = jnp.zeros_like(acc_ref)

        gbuf[:CONV_HALO, :] = jnp.where(i > 0, halo_ref[...], 0.0)
        gbuf[CONV_HALO:, :] = glu_ref[...]
        for lb in range(D_MODEL // LANES):
            lanes = slice(LANES * lb, LANES * (lb + 1))
            dcf_ref[:, lanes] = _conv_taps(dww_ref, gbuf, ts, lanes, _fwd_tap)
        cf = dcf_ref[...] + dwb_ref[...]
        mu = jnp.mean(cf, axis=-1, keepdims=True)
        cen = cf - mu
        rs = lax.rsqrt(jnp.mean(cen * cen, axis=-1, keepdims=True) + EPS)
        xhat = cen * rs
        cn = xhat * lng_ref[...] + lnb_ref[...]
        gate = gate_ref[...]
        sg = _sigmoid(gate)
        sc = _sigmoid(cn)
        silu_gate = gate * sg
        silu_cn = cn * sc
        ymix = (silu_cn * silu_gate).astype(BF16)
        ymix_ref[...] = ymix
        y = _dot(ymix, w_ref[...])
        yn, r = _rms_fwd(y, g_ref[...])
        err = (x1_ref[...] + yn) - t_ref[...]
        acc_ref[ACC_LOSS] += _rows8(err * err)
        dx2 = err * (1.0 / D_MODEL)
        dx2_ref[...] = dx2
        dy, dpost = _rms_bwd(y, r, g_ref[...], dx2)
        acc_ref[ACC_POST] += _rows8(dpost)
        dy = dy.astype(BF16)
        dy_ref[...] = dy
        dymix = _dot_nt(dy, w_ref[...])
        dgate_ref[...] = (dymix * silu_cn * (sg * (1.0 + gate * (1.0 - sg)))).astype(BF16)
        dcn = dymix * silu_gate * (sc * (1.0 + cn * (1.0 - sc)))
        acc_ref[ACC_LN_G] += _rows8(dcn * xhat)
        acc_ref[ACC_LN_B] += _rows8(dcn)
        dxhat = dcn * lng_ref[...]
        dcf = rs * (dxhat - jnp.mean(dxhat, axis=-1, keepdims=True)
                    - xhat * jnp.mean(dxhat * xhat, axis=-1, keepdims=True))
        acc_ref[ACC_DW_B] += _rows8(dcf)
        dcf_ref[...] = dcf

    return pl.pallas_call(
        body, name="fwd_out1", grid=(seq // ts,),
        in_specs=[_tile(ts, D_MODEL),
                  pl.BlockSpec((CONV_HALO, D_MODEL), lambda i: (jnp.maximum(i * hb - 1, 0), 0)),
                  _tile(ts, D_MODEL, 2), _full((CONV_HALO, D_MODEL)), _full((1, D_MODEL)), _full((1, D_MODEL)),
                  _full((1, D_MODEL)), _full((D_MODEL, D_MODEL)), _full((1, D_MODEL)), _tile(ts, D_MODEL),
                  _tile(ts, D_MODEL)],
        out_specs=[_tile(ts, D_MODEL), _tile(ts, D_MODEL), _tile(ts, D_MODEL), _tile(ts, D_MODEL), _tile(ts, D_MODEL),
                   _full((5, SUBLANES, D_MODEL))],
        out_shape=[jax.ShapeDtypeStruct((seq, D_MODEL), BF16), jax.ShapeDtypeStruct((seq, D_MODEL), BF16),
                   jax.ShapeDtypeStruct((seq, D_MODEL), F32), jax.ShapeDtypeStruct((seq, D_MODEL), F32),
                   jax.ShapeDtypeStruct((seq, D_MODEL), BF16), jax.ShapeDtypeStruct((5, SUBLANES, D_MODEL), F32)],
        scratch_shapes=[pltpu.VMEM((ts + CONV_HALO, D_MODEL), F32)],
        compiler_params=_params(),
    )(glu, glu, proj, dw_w, dw_b, ln_g, ln_b, w_out, g_post, x1, target)


def _bwd_in1(dcf, glu, proj, dgate, dw_w, w_in, x1, g_pre, dx2, ts):
    seq = x1.shape[0]
    hb = ts // CONV_HALO
    last = seq // CONV_HALO - 1
    nt = seq // ts

    def body(dcf_ref, dnext_ref, glu_ref, gprev_ref, ab_ref, dgate_ref, dww_ref, w_ref, x_ref, g_ref, dx2_ref,
             dproj_ref, dx1_ref, ddw_ref, dpre_ref, dbuf, gbuf, zbuf, sbuf):
        i = pl.program_id(0)

        @pl.when(i == 0)
        def _():
            ddw_ref[...] = jnp.zeros_like(ddw_ref)
            dpre_ref[...] = jnp.zeros_like(dpre_ref)

        dcf = dcf_ref[...]
        dbuf[:ts, :] = dcf
        dbuf[ts:, :] = jnp.where(i < nt - 1, dnext_ref[...], 0.0)
        gbuf[:CONV_HALO, :] = jnp.where(i > 0, gprev_ref[...], 0.0)
        gbuf[CONV_HALO:, :] = glu_ref[...]
        zbuf[:SUBLANES, :] = jnp.zeros((SUBLANES, D_MODEL), F32)
        zbuf[pl.ds(SUBLANES, ts), :] = dcf
        zbuf[pl.ds(SUBLANES + ts, SUBLANES), :] = jnp.zeros((SUBLANES, D_MODEL), F32)
        for lb in range(D_MODEL // LANES):
            lanes = slice(LANES * lb, LANES * (lb + 1))
            gate_lanes = slice(D_MODEL + LANES * lb, D_MODEL + LANES * (lb + 1))
            dglu = _conv_taps(dww_ref, dbuf, ts, lanes, _bwd_tap)
            a = ab_ref[:, lanes]
            sb = _sigmoid(ab_ref[:, gate_lanes])
            dproj_ref[:, lanes] = (dglu * sb).astype(BF16)
            dproj_ref[:, gate_lanes] = (dglu * a * sb * (1.0 - sb)).astype(BF16)
            for b in range(SUBLANES):
                rows = ts if b == 0 else ts + SUBLANES
                sbuf[b, pl.ds(0, rows), :] = zbuf[pl.ds(SUBLANES - b, rows), lanes]
                for a8 in range(CONV_HALO // SUBLANES + 1):
                    k = _fwd_tap(SUBLANES * a8 + b)
                    if k is not None:
                        ddw_ref[k, :, lanes] += _rows8(sbuf[b, pl.ds(0, rows), :] * gbuf[pl.ds(SUBLANES * a8, rows), lanes])
        dproj_ref[:, 2 * D_MODEL:] = dgate_ref[...]
        dh = _dot(dproj_ref[...], w_ref[...])
        x = x_ref[...]
        r = lax.rsqrt(jnp.mean(x * x, axis=-1, keepdims=True) + EPS)
        dx, dpre = _rms_bwd(x, r, g_ref[...], dh)
        dx1_ref[...] = dx2_ref[...] + dx
        dpre_ref[...] += _rows8(dpre)

    return pl.pallas_call(
        body, name="bwd_in1", grid=(nt,),
        in_specs=[_tile(ts, D_MODEL),
                  pl.BlockSpec((CONV_HALO, D_MODEL), lambda i: (jnp.minimum((i + 1) * hb, last), 0)),
                  _tile(ts, D_MODEL),
                  pl.BlockSpec((CONV_HALO, D_MODEL), lambda i: (jnp.maximum(i * hb - 1, 0), 0)),
                  _tile(ts, 2 * D_MODEL), _tile(ts, D_MODEL), _full((CONV_HALO, D_MODEL)),
                  _full((ODD_IN, D_MODEL)), _tile(ts, D_MODEL), _full((1, D_MODEL)), _tile(ts, D_MODEL)],
        out_specs=[_tile(ts, ODD_IN), _tile(ts, D_MODEL), _full((CONV_HALO, SUBLANES, D_MODEL)), _full((SUBLANES, D_MODEL))],
        out_shape=[jax.ShapeDtypeStruct((seq, ODD_IN), BF16), jax.ShapeDtypeStruct((seq, D_MODEL), F32),
                   jax.ShapeDtypeStruct((CONV_HALO, SUBLANES, D_MODEL), F32), jax.ShapeDtypeStruct((SUBLANES, D_MODEL), F32)],
        scratch_shapes=[pltpu.VMEM((ts + CONV_HALO, D_MODEL), F32), pltpu.VMEM((ts + CONV_HALO, D_MODEL), F32),
                        pltpu.VMEM((ts + 2 * SUBLANES, D_MODEL), F32), pltpu.VMEM((SUBLANES, ts + SUBLANES, LANES), F32)],
        compiler_params=_params(),
    )(dcf, dcf, glu, glu, proj, dgate, dw_w, w_in, x1, g_pre, dx2)


def _bwd_out0(dx1, y0, g_post, w_out, gug, o, pooled, pool_w, pool_scale, mix, ts):
    seq = dx1.shape[0]

    def body(dx1_ref, y_ref, g_ref, w_ref, gug_ref, o_ref, pooled_ref, pw_ref, ps_ref, mix_ref,
             do_ref, dgg_ref, dpooled_ref, dpost_ref, dscale_ref, dpw_ref, gw_ref, gacc):
        i = pl.program_id(0)

        @pl.when(i == 0)
        def _():
            dpost_ref[...] = jnp.zeros_like(dpost_ref)
            dscale_ref[...] = jnp.zeros_like(dscale_ref)
            dpw_ref[...] = jnp.zeros_like(dpw_ref)

        y = y_ref[...]
        r = lax.rsqrt(jnp.mean(y * y, axis=-1, keepdims=True) + EPS)
        dy, dpost = _rms_bwd(y, r, g_ref[...], dx1_ref[...])
        dpost_ref[...] += _rows8(dpost)
        dy = dy.astype(BF16)
        _accumulate_tn(gacc, mix_ref[...], dy, i == 0)
        dmix = _dot_nt(dy, w_ref[...])
        dya = dmix[:, :ATTN_WIDTH]
        dyb = dmix[:, ATTN_WIDTH:]
        ga = gug_ref[:, :ATTN_WIDTH]
        gb = gug_ref[:, ATTN_WIDTH + POOL_WIDTH:]
        sga = _sigmoid(ga)
        sgb = _sigmoid(gb)
        do_ref[...] = (dya * (ga * sga)).astype(BF16)
        dgg_ref[:, :ATTN_WIDTH] = (dya * o_ref[...] * (sga * (1.0 + ga * (1.0 - sga)))).astype(BF16)
        dypool = dyb * (gb * sgb)
        dsilu_gb = sgb * (1.0 + gb * (1.0 - sgb))
        for g in range(len(POOL_WINDOWS)):
            lanes = slice(POOL_GC * g, POOL_GC * (g + 1))
            pooled = pooled_ref[:, lanes]
            wg = pw_ref[g].astype(BF16)
            pw = _dot(pooled, wg)
            scale = ps_ref[:, lanes]
            dgg_ref[:, ATTN_WIDTH + POOL_GC * g:ATTN_WIDTH + POOL_GC * (g + 1)] = (
                dyb[:, lanes] * (pw * scale) * dsilu_gb[:, lanes]).astype(BF16)
            dscale_ref[:, lanes] += _rows8(dypool[:, lanes] * pw)
            dpw = (dypool[:, lanes] * scale).astype(BF16)
            dpooled_ref[:, lanes] = _dot_nt(dpw, wg)
            dpw_ref[g] += _dot_tn(pooled, dpw)

        @pl.when(i == seq // ts - 1)
        def _():
            gw_ref[...] = gacc[...].astype(BF16)

    return pl.pallas_call(
        body, name="bwd_out0", grid=(seq // ts,),
        in_specs=[_tile(ts, D_MODEL), _tile(ts, D_MODEL), _full((1, D_MODEL)), _resident((D_MODEL, D_MODEL)),
                  _tile(ts, 3 * POOL_WIDTH), _tile(ts, ATTN_WIDTH), _tile(ts, POOL_WIDTH),
                  _full((4, POOL_GC, POOL_GC)), _full((1, POOL_WIDTH)), _tile(ts, D_MODEL)],
        out_specs=[_tile(ts, ATTN_WIDTH), _tile(ts, ATTN_WIDTH + POOL_WIDTH), _tile(ts, POOL_WIDTH),
                   _full((SUBLANES, D_MODEL)), _full((SUBLANES, POOL_WIDTH)), _full((4, POOL_GC, POOL_GC)),
                   _full((D_MODEL, D_MODEL))],
        out_shape=[jax.ShapeDtypeStruct((seq, ATTN_WIDTH), BF16),
                   jax.ShapeDtypeStruct((seq, ATTN_WIDTH + POOL_WIDTH), BF16), jax.ShapeDtypeStruct((seq, POOL_WIDTH), F32),
                   jax.ShapeDtypeStruct((SUBLANES, D_MODEL), F32), jax.ShapeDtypeStruct((SUBLANES, POOL_WIDTH), F32),
                   jax.ShapeDtypeStruct((4, POOL_GC, POOL_GC), F32), jax.ShapeDtypeStruct((D_MODEL, D_MODEL), BF16)],
        scratch_shapes=[pltpu.VMEM((D_MODEL, D_MODEL), F32)],
        compiler_params=_params(),
    )(dx1, y0, g_post, w_out, gug, o, pooled, pool_w, pool_scale, mix)


def _attn_bwd(q, kv, do, sinks, parts, early):
    seq = q.shape[0]
    nb = seq // BLOCK

    def qblock(j):
        return jnp.minimum(j, nb - 1)

    n = len(parts)

    def body(*refs):
        sink_ref, q_ref, kvc_ref, kvp_ref, do_ref = refs[:5]
        early_ref = refs[5 + n]
        dq_ref, dkv_ref, dsink_ref = refs[6 + n:9 + n]
        all_ref = refs[9 + 2 * n]
        carry, dkv_acc = refs[10 + 2 * n:12 + 2 * n]
        scatter = _Scatter(refs[5:5 + n], refs[9 + n:9 + 2 * n], *refs[12 + 2 * n:14 + 2 * n])
        gather = lambda: _Gather([early_ref], [all_ref], *refs[14 + 2 * n:])
        j = pl.program_id(0)

        @pl.when(j == 0)
        def _():
            scatter.begin()
            gather().begin()
            dsink_ref[...] = jnp.zeros_like(dsink_ref)
            carry[...] = jnp.zeros_like(carry)

        @pl.when(j < nb)
        def _():
            valid, distf = _attn_mask(GROUP_ROWS, j == 0)
            qb = q_ref[...]
            dob = do_ref[...]
            kk = jnp.concatenate([kvp_ref[...], kvc_ref[...]], axis=0)
            lane = lax.broadcasted_iota(jnp.int32, (BLOCK, LANES), 1)
            dsink = jnp.zeros((BLOCK, LANES), F32)
            for kvh in range(Q_HEADS // GROUP):
                kh = kk[:, HEAD_DIM * kvh:HEAD_DIM * (kvh + 1)]
                vh = kk[:, KV_WIDTH + HEAD_DIM * kvh:KV_WIDTH + HEAD_DIM * (kvh + 1)]
                qg = _group_rows(qb, kvh)
                dog = _group_rows(dob, kvh)
                sink, slope = _group_columns(sink_ref, kvh)
                p, psink = _attn_probs(qg, kh, sink, slope, valid, distf)
                dp = _dot_nt(dog, vh)
                delta = jnp.sum(p * dp, axis=-1, keepdims=True)
                ds = (p * (dp - delta) * (HEAD_DIM ** -0.5)).astype(BF16)
                dsink_rows = -psink * delta
                dqg = _dot(ds, kh).astype(BF16)
                for i in range(GROUP):
                    h = GROUP * kvh + i
                    dsink = dsink + jnp.where(lane == h, dsink_rows[BLOCK * i:BLOCK * (i + 1), :], 0.0)
                    dq_ref[:, HEAD_DIM * h:HEAD_DIM * (h + 1)] = dqg[BLOCK * i:BLOCK * (i + 1), :]
                dkv_acc[:, HEAD_DIM * kvh:HEAD_DIM * (kvh + 1)] = _dot_tn(ds, qg)
                dkv_acc[:, KV_WIDTH + HEAD_DIM * kvh:KV_WIDTH + HEAD_DIM * (kvh + 1)] = _dot_tn(p.astype(BF16), dog)
            dsink_ref[...] += dsink

            @pl.when(j > 0)
            def _():
                dkv_ref[...] = (carry[...] + dkv_acc[:BLOCK, :]).astype(BF16)

            carry[...] = dkv_acc[BLOCK:, :]

        @pl.when(j == nb)
        def _():
            dkv_ref[...] = carry[...].astype(BF16)
            scatter.finish()
            gather().finish()

    hbm = pl.BlockSpec(memory_space=pl.ANY)
    out = pl.pallas_call(
        body, name="attn_bwd", grid=(nb + 1,),
        in_specs=[pl.BlockSpec(memory_space=pltpu.SMEM),
                  pl.BlockSpec((BLOCK, ATTN_WIDTH), lambda j: (qblock(j), 0)),
                  pl.BlockSpec((BLOCK, 2 * KV_WIDTH), lambda j: (qblock(j), 0)),
                  pl.BlockSpec((BLOCK, 2 * KV_WIDTH), lambda j: (jnp.maximum(qblock(j) - 1, 0), 0)),
                  pl.BlockSpec((BLOCK, ATTN_WIDTH), lambda j: (qblock(j), 0))] + [hbm] * (n + 1),
        out_specs=[pl.BlockSpec((BLOCK, ATTN_WIDTH), lambda j: (qblock(j), 0)),
                   pl.BlockSpec((BLOCK, 2 * KV_WIDTH), lambda j: (jnp.maximum(j - 1, 0), 0)),
                   _full((BLOCK, LANES))] + [hbm] * (n + 1),
        out_shape=[jax.ShapeDtypeStruct((seq, ATTN_WIDTH), BF16), jax.ShapeDtypeStruct((seq, 2 * KV_WIDTH), BF16),
                   jax.ShapeDtypeStruct((BLOCK, LANES), F32)]
        + [jax.ShapeDtypeStruct((N_DEV - 1, *p.shape[1:]), p.dtype) for p in parts]
        + [jax.ShapeDtypeStruct((N_DEV, *early.shape), early.dtype)],
        scratch_shapes=[pltpu.VMEM((BLOCK, 2 * KV_WIDTH), F32), pltpu.VMEM((2 * BLOCK, 2 * KV_WIDTH), F32)]
        + _Scatter.semaphores(n) + _Gather.semaphores(1),
        compiler_params=_params(),
    )(sinks, q, kv, kv, do, *parts, early)
    return out[:3], out[3:3 + n], out[3 + n]


def _bwd_in0(dpooled, dq, dkv, dgg, w_in, x, g_pre, dx1, ts):
    seq = x.shape[0]
    hb = ts // POOL_HALO
    last = seq // POOL_HALO - 1
    nt = seq // ts

    def body(dp_ref, dnext_ref, dq_ref, dkv_ref, dgg_ref, w_ref, x_ref, g_ref, dx1_ref,
             dproj_ref, gx_ref, dpre_ref, dbuf):
        i = pl.program_id(0)

        @pl.when(i == 0)
        def _():
            dpre_ref[...] = jnp.zeros_like(dpre_ref)

        dpool = dp_ref[...]
        dnext = jnp.where(i < nt - 1, dnext_ref[...], 0.0)
        u0 = ATTN_WIDTH + 2 * KV_WIDTH + ATTN_WIDTH
        for g, window in enumerate(POOL_WINDOWS):
            lanes = slice(POOL_GC * g, POOL_GC * (g + 1))
            dbuf[:ts, lanes] = dpool[:, lanes] / _pool_counts(i * ts, ts, window)
            dbuf[ts:, lanes] = dnext[:, lanes] / _pool_counts((i + 1) * ts, POOL_HALO, window)
        for g, window in enumerate(POOL_WINDOWS):
            lanes = slice(POOL_GC * g, POOL_GC * (g + 1))
            acc = dbuf[pl.ds(0, ts), lanes]
            for k in range(1, window):
                acc = acc + dbuf[pl.ds(k, ts), lanes]
            dproj_ref[:, u0 + POOL_GC * g:u0 + POOL_GC * (g + 1)] = (acc - dpool[:, lanes]).astype(BF16)
        dproj_ref[:, :ATTN_WIDTH] = dq_ref[...]
        dproj_ref[:, ATTN_WIDTH:ATTN_WIDTH + 2 * KV_WIDTH] = dkv_ref[...]
        dproj_ref[:, ATTN_WIDTH + 2 * KV_WIDTH:u0] = dgg_ref[:, :ATTN_WIDTH]
        dproj_ref[:, u0 + POOL_WIDTH:] = dgg_ref[:, ATTN_WIDTH:]
        dh = _dot(dproj_ref[...], w_ref[...])
        x = x_ref[...]
        r = lax.rsqrt(jnp.mean(x * x, axis=-1, keepdims=True) + EPS)
        dx, dpre = _rms_bwd(x, r, g_ref[...], dh)
        gx_ref[...] = dx1_ref[...] + dx
        dpre_ref[...] += _rows8(dpre)

    return pl.pallas_call(
        body, name="bwd_in0", grid=(nt,),
        in_specs=[_tile(ts, POOL_WIDTH),
                  pl.BlockSpec((POOL_HALO, POOL_WIDTH), lambda i: (jnp.minimum((i + 1) * hb, last), 0)),
                  _tile(ts, ATTN_WIDTH), _tile(ts, 2 * KV_WIDTH), _tile(ts, ATTN_WIDTH + POOL_WIDTH),
                  _full((EVEN_IN, D_MODEL)), _tile(ts, D_MODEL), _full((1, D_MODEL)), _tile(ts, D_MODEL)],
        out_specs=[_tile(ts, EVEN_IN), _tile(ts, D_MODEL), _full((SUBLANES, D_MODEL))],
        out_shape=[jax.ShapeDtypeStruct((seq, EVEN_IN), BF16), jax.ShapeDtypeStruct((seq, D_MODEL), F32),
                   jax.ShapeDtypeStruct((SUBLANES, D_MODEL), F32)],
        scratch_shapes=[pltpu.VMEM((ts + POOL_HALO, POOL_WIDTH), F32)],
        compiler_params=_params(),
    )(dpooled, dpooled, dq, dkv, dgg, w_in, x, g_pre, dx1)


def _matmul_tn(a, b, name, ts, tm):
    seq, m = a.shape
    n = b.shape[1]
    steps = seq // ts

    def body(a_ref, b_ref, o_ref, acc):
        s = pl.program_id(1)

        @pl.when(s == 0)
        def _():
            acc[...] = jnp.zeros_like(acc)

        acc[...] += _dot_tn(a_ref[...], b_ref[...])

        @pl.when(s == steps - 1)
        def _():
            o_ref[...] = acc[...].astype(BF16)

    return pl.pallas_call(
        body, name=name, grid=(m // tm, steps),
        in_specs=[pl.BlockSpec((ts, tm), lambda j, s: (s, j)), pl.BlockSpec((ts, n), lambda j, s: (s, 0))],
        out_specs=pl.BlockSpec((tm, n), lambda j, s: (j, 0)),
        out_shape=jax.ShapeDtypeStruct((m, n), BF16),
        scratch_shapes=[pltpu.VMEM((tm, n), F32)],
        compiler_params=pltpu.CompilerParams(dimension_semantics=("arbitrary", "arbitrary"), vmem_limit_bytes=VMEM_LIMIT),
    )(a, b)


def _adamw_math(w, g, m, v):
    m = ADAM_B1 * m + (1.0 - ADAM_B1) * g
    v = ADAM_B2 * v + (1.0 - ADAM_B2) * (g * g)
    m_hat = m / (1.0 - ADAM_B1 ** ADAM_STEP)
    v_hat = v / (1.0 - ADAM_B2 ** ADAM_STEP)
    delta = -ADAM_LR * (m_hat / (jnp.sqrt(v_hat) + ADAM_EPS) + ADAM_WD * w)
    return delta, m, v


def _adamw(ws, gs, ms, vs, name):
    n = len(ws)

    def body(*refs):
        ins, outs = refs[:4 * n], refs[4 * n:]
        for k in range(n):
            delta, m, v = _adamw_math(ins[k][...], ins[n + k][...], ins[2 * n + k][...], ins[3 * n + k][...])
            outs[k][...] = delta
            outs[n + k][...] = m
            outs[2 * n + k][...] = v

    shapes = [jax.ShapeDtypeStruct(w.shape, F32) for w in ws]
    out = pl.pallas_call(body, name=name, out_shape=shapes * 3,
                         compiler_params=pltpu.CompilerParams(vmem_limit_bytes=VMEM_LIMIT))(*ws, *gs, *ms, *vs)
    return out[:n], out[n:2 * n], out[2 * n:]


TS_MATMUL, TS_CONV_FWD, TS_CONV_BWD, TS_GRAD = 512, 512, 256, 1024


def kernel(x, pre_norm, post_norm, a_w_in, a_sinks, b_pool_w, b_pool_scale, ab_w_out, c_w_in, c_dw_w, c_dw_b, c_ln_g, c_ln_b, c_w_out, loss_target, m_pre_norm, m_post_norm, m_a_w_in, m_a_sinks, m_b_pool_w, m_b_pool_scale, m_ab_w_out, m_c_w_in, m_c_dw_w, m_c_dw_b, m_c_ln_g, m_c_ln_b, m_c_w_out, v_pre_norm, v_post_norm, v_a_w_in, v_a_sinks, v_b_pool_w, v_b_pool_scale, v_ab_w_out, v_c_w_in, v_c_dw_w, v_c_dw_b, v_c_ln_g, v_c_ln_b, v_c_w_out):
    seq = x.shape[1]
    ts_big, ts_grad = min(TS_MATMUL, seq), min(TS_GRAD, seq)
    x2d = x[0]
    target = loss_target[0]
    ch = c_dw_b.shape[1]

    whole = lambda g: g.reshape(-1, D_MODEL)
    vec_rows = 40
    vecs = jnp.concatenate([c_dw_w[0, :, 0, :], c_dw_b, c_ln_g, c_ln_b, jnp.zeros((vec_rows - CONV_K - 3, ch), F32)], axis=0)
    w_in0t, vg = _all_gather([a_w_in[0].T.astype(BF16), vecs], "gather_first")
    w_in0t = whole(w_in0t)
    vg = vg.transpose(1, 0, 2).reshape(vec_rows, D_MODEL)
    dw_w = vg[:CONV_HALO]
    dw_b, ln_g, ln_b = vg[CONV_K:CONV_K + 1], vg[CONV_K + 1:CONV_K + 2], vg[CONV_K + 2:CONV_K + 3]

    sinks = a_sinks[0]
    (h0, q, kv, gug), (w_out0,) = _fwd_in0(x2d, pre_norm[0:1], w_in0t, [ab_w_out[0].astype(BF16)], ts_big)
    o, (w_in1t, w_out1) = _attn_fwd(q, kv, sinks, [c_w_in[0].T.astype(BF16), c_w_out[0].astype(BF16)])
    w_out0, w_in1t, w_out1 = whole(w_out0), whole(w_in1t), whole(w_out1)
    mix0, pooled, y0, x1 = _fwd_out0(gug, o, b_pool_w[0], b_pool_scale, w_out0, post_norm[0:1], x2d, ts_big)
    h1, proj1, glu = _fwd_in1(x1, pre_norm[1:2], w_in1t, ts_big)
    ymix1, dy1, dx2, dcf, dgate, acc1 = _fwd_out1(glu, proj1, dw_w, dw_b, ln_g, ln_b, w_out1, post_norm[1:2], x1, target, min(TS_CONV_FWD, seq))

    dproj1, dx1, ddw_w, dpre1 = _bwd_in1(dcf, glu, proj1, dgate, dw_w, w_in1t, x1, pre_norm[1:2], dx2, min(TS_CONV_BWD, seq))
    g_in1t = _matmul_tn(dproj1, h1, "grad_w_in1", ts_grad, 1024)
    g_out1 = _matmul_tn(ymix1, dy1, "grad_w_out1", ts_grad, 1024)
    do, dgg, dpooled, dpost0, dscale, dpool_w, g_out0 = _bwd_out0(dx1, y0, post_norm[0:1], w_out0, gug, o, pooled, b_pool_w[0], b_pool_scale, mix0, ts_big)
    slabs = lambda g: g.reshape(N_DEV, -1, D_MODEL)
    me = 4 * lax.axis_index("x") + 2 * lax.axis_index("y") + lax.axis_index("c")
    early = [slabs(g_in1t), slabs(g_out1), slabs(g_out0)]
    row = lambda a: jnp.sum(a, axis=0, keepdims=True)
    lanes8 = lambda a: row(a).reshape(-1, LANES)
    loss_row = jnp.pad(jnp.sum(acc1[ACC_LOSS]).reshape(1, 1), ((0, 0), (0, LANES - 1)))
    done = jnp.concatenate([lanes8(dpre1), lanes8(dpost0), lanes8(acc1[ACC_POST]), lanes8(dscale),
                            loss_row, jnp.zeros((3, LANES), F32), dpool_w.reshape(4 * POOL_GC, LANES)], axis=0)
    (dq, dkv, dsink), arrived, done_all = _attn_bwd(q, kv, do, sinks, early, done)
    g_in1t, g_c_w_out, g_ab_w_out = [_sum_slabs(lax.dynamic_index_in_dim(p, me, keepdims=False), r, name)
                                     for p, r, name in zip(early, arrived, ("sum_w_in1", "sum_w_out1", "sum_w_out0"))]
    dproj0, grad_x, dpre0 = _bwd_in0(dpooled, dq, dkv, dgg, w_in0t, x2d, pre_norm[0:1], dx1, ts_big)
    g_in0t = _matmul_tn(dproj0, h0, "grad_w_in0", ts_grad, 768)

    vec_g = jnp.concatenate([jnp.sum(ddw_w[:CONV_K], axis=1), row(acc1[ACC_DW_B]), row(acc1[ACC_LN_G]), row(acc1[ACC_LN_B]),
                             jnp.zeros((vec_rows - CONV_K - 3, D_MODEL), F32)], axis=0)
    last = jnp.concatenate([lanes8(dpre0), row(dsink), jnp.zeros((SUBLANES - 1, LANES), F32)], axis=0)
    g_in0t, vec_g, last, rep = _final_reduce(slabs(g_in0t), vec_g.reshape(vec_rows, N_DEV, ch).transpose(1, 0, 2),
                                             last, done_all, "final_reduce")
    g_a_w_in, g_c_w_in = g_in0t.T, g_in1t.T
    g_dw_w, g_dw_b, g_ln_g, g_ln_b = vec_g[:CONV_K], vec_g[CONV_K:CONV_K + 1], vec_g[CONV_K + 1:CONV_K + 2], vec_g[CONV_K + 2:CONV_K + 3]
    g_pre = jnp.concatenate([last[:8], rep[:8]], axis=0).reshape(2, D_MODEL)
    g_sinks = last[8:9, :Q_HEADS]
    g_post = rep[8:24].reshape(2, D_MODEL)
    g_scale = rep[24:28].reshape(1, POOL_WIDTH)
    loss = (0.5 / D_MODEL) * rep[28, 0]
    g_pool_w = rep[32:]

    grads = [g_pre, g_post, g_a_w_in, g_sinks, g_pool_w, g_scale, g_ab_w_out, g_c_w_in, g_dw_w, g_dw_b, g_ln_g, g_ln_b, g_c_w_out]
    weights = [pre_norm, post_norm, a_w_in, a_sinks, b_pool_w, b_pool_scale, ab_w_out, c_w_in, c_dw_w, c_dw_b, c_ln_g, c_ln_b, c_w_out]
    m_in = [m_pre_norm, m_post_norm, m_a_w_in, m_a_sinks, m_b_pool_w, m_b_pool_scale, m_ab_w_out, m_c_w_in, m_c_dw_w, m_c_dw_b, m_c_ln_g, m_c_ln_b, m_c_w_out]
    v_in = [v_pre_norm, v_post_norm, v_a_w_in, v_a_sinks, v_b_pool_w, v_b_pool_scale, v_ab_w_out, v_c_w_in, v_c_dw_w, v_c_dw_b, v_c_ln_g, v_c_ln_b, v_c_w_out]
    flat = lambda arrs: [a.reshape(g.shape) for a, g in zip(arrs, grads)]
    big = (2, 6, 7, 12)
    small = tuple(k for k in range(len(grads)) if k not in big)
    pick = lambda arrs, idx: [arrs[k] for k in idx]
    deltas, new_m, new_v = [None] * 13, [None] * 13, [None] * 13
    for idx, name in ((big, "adamw_matrices"), (small, "adamw_vectors")):
        d, m, v = _adamw(pick(flat(weights), idx), pick(grads, idx), pick(flat(m_in), idx), pick(flat(v_in), idx), name)
        for k, dk, mk, vk in zip(idx, d, m, v):
            deltas[k], new_m[k], new_v[k] = dk, mk, vk
    shaped = lambda arrs: [a.reshape(w.shape) for a, w in zip(arrs, weights)]
    return (loss, grad_x[None], *shaped(grads), *shaped(deltas), *shaped(new_m), *shaped(new_v))
```

```python
import functools

import jax
import jax.numpy as jnp
from jax import lax
from jax.experimental import pallas as pl
from jax.experimental.pallas import tpu as pltpu

F32 = jnp.float32
BF16 = jnp.bfloat16
MESH = pl.DeviceIdType.MESH
AXES = ("x", "y", "c")
N_DEV = 8

D_MODEL = 1024
HEAD_DIM = 64
Q_HEADS = 8
GROUP = 4
ATTN_WIDTH = 512
KV_WIDTH = 128
BLOCK = 128
POOL_WIDTH = 512
POOL_WINDOWS = (2, 4, 8, 16)
POOL_GC = 128
POOL_HALO = 16
EVEN_IN = 2304
CONV_K = 31
CONV_HALO = 32
ODD_IN = 3072
EPS = 1e-6
NEG = -1e30
SLOPES = tuple(2.0 ** (-8.0 * (h + 1) / Q_HEADS) for h in range(Q_HEADS))

ADAM_LR = 0.001
ADAM_B1 = 0.9
ADAM_B2 = 0.999
ADAM_EPS = 1e-08
ADAM_WD = 0.01
ADAM_STEP = 10

SUBLANES = 8
LANES = 128
VMEM_LIMIT = 56 * 1024 * 1024

NT = (((1,), (1,)), ((), ()))
TN = (((0,), (0,)), ((), ()))


def _params(**kw):
    return pltpu.CompilerParams(dimension_semantics=("arbitrary",), vmem_limit_bytes=VMEM_LIMIT, **kw)


def _dot(a, b):
    return jnp.dot(a, b, preferred_element_type=F32)


def _dot_nt(a, b):
    return lax.dot_general(a, b, NT, preferred_element_type=F32)


def _dot_tn(a, b):
    return lax.dot_general(a, b, TN, preferred_element_type=F32)


def _sigmoid(v):
    return 1.0 / (1.0 + jnp.exp(-v))


def _rows8(v):
    r, c = v.shape
    return jnp.sum(v.reshape(r // SUBLANES, SUBLANES, c), axis=0)


def _rms_fwd(v, g):
    r = lax.rsqrt(jnp.mean(v * v, axis=-1, keepdims=True) + EPS)
    return v * r * g, r


def _rms_bwd(v, r, g, dout):
    gd = dout * g
    dv = r * gd - v * (r * r * r) * jnp.mean(v * gd, axis=-1, keepdims=True)
    return dv, dout * (v * r)


def _full(shape):
    return pl.BlockSpec(shape, lambda i: (0,) * len(shape))


def _resident(shape):
    return pl.BlockSpec(shape, lambda i: (0,) * len(shape), pipeline_mode=pl.Buffered(1))


def _accumulate_tn(acc, a, b, first_step):
    @pl.when(first_step)
    def _():
        acc[...] = jnp.zeros_like(acc)

    acc[...] += _dot_tn(a, b)


def _tile(ts, cols, col_block=0):
    return pl.BlockSpec((ts, cols), lambda i: (i, col_block))


def _position():
    return lax.axis_index("x"), lax.axis_index("y"), lax.axis_index("c")


class _Gather:
    def __init__(self, x_refs, out_refs, send_sems, recv_sems, local_sems):
        self.x_refs, self.out_refs = x_refs, out_refs
        self.send_sems, self.recv_sems, self.local_sems = send_sems, recv_sems, local_sems
        self.n = len(x_refs)
        x, y, c = _position()
        self.c = c
        self.me, self.sibling = (x, y, c), (x, y, 1 - c)
        self.chips = [(1 - x, y), (x, 1 - y), (1 - x, 1 - y)]

    def copy(self, k, j, owner, to, src=None):
        slab = self.out_refs[k].at[4 * owner[0] + 2 * owner[1] + owner[2]]
        return pltpu.make_async_remote_copy(
            src_ref=slab if src is None else src, dst_ref=slab, send_sem=self.send_sems.at[7 * k + j],
            recv_sem=self.recv_sems.at[7 * k + j], device_id=to, device_id_type=MESH)

    def mine(self, k):
        return pltpu.make_async_copy(self.x_refs[k], self.out_refs[k].at[4 * self.me[0] + 2 * self.me[1] + self.c],
                                     self.local_sems.at[k])

    def first(self, k):
        out = [self.copy(k, 0, self.me, self.sibling, src=self.x_refs[k])]
        return out + [self.copy(k, 1 + j, self.me, (*chip, self.c), src=self.x_refs[k]) for j, chip in enumerate(self.chips)]

    def begin(self):
        for k in range(self.n):
            self.mine(k).start()
            for cp in self.first(k):
                cp.start()

    def passed(self, k):
        return [self.copy(k, 4 + j, (*chip, self.c), self.sibling) for j, chip in enumerate(self.chips)]

    def relay(self):
        for k in range(self.n):
            for j, chip in enumerate(self.chips):
                self.copy(k, 1 + j, (*chip, self.c), self.me).wait_recv()
                self.passed(k)[j].start()

    def finish(self):
        for k in range(self.n):
            self.copy(k, 0, self.sibling, self.me).wait_recv()
            for j, chip in enumerate(self.chips):
                self.copy(k, 4 + j, (*chip, 1 - self.c), self.me).wait_recv()
        for k in range(self.n):
            for cp in self.first(k) + self.passed(k):
                cp.wait_send()
            self.mine(k).wait()

    @staticmethod
    def semaphores(n):
        return [pltpu.SemaphoreType.DMA((7 * n,)), pltpu.SemaphoreType.DMA((7 * n,)), pltpu.SemaphoreType.DMA((n,))]


def RELAY_AT(steps):
    return steps * 3 // 4


def _all_gather(blocks, name):
    n = len(blocks)

    def body(*refs):
        gather = _Gather(refs[:n], refs[n:2 * n], *refs[2 * n:])
        gather.begin()
        gather.relay()
        gather.finish()

    return pl.pallas_call(
        body, name=name,
        out_shape=[jax.ShapeDtypeStruct((N_DEV, *b.shape), b.dtype) for b in blocks],
        in_specs=[pl.BlockSpec(memory_space=pltpu.VMEM)] * n,
        out_specs=[pl.BlockSpec(memory_space=pltpu.VMEM)] * n,
        scratch_shapes=_Gather.semaphores(n),
        compiler_params=pltpu.CompilerParams(vmem_limit_bytes=VMEM_LIMIT),
    )(*blocks)


class _Scatter:
    def __init__(self, part_refs, recv_refs, send_sems, recv_sems):
        self.part_refs, self.recv_refs, self.send_sems, self.recv_sems = part_refs, recv_refs, send_sems, recv_sems
        self.n = len(part_refs)

    def copies(self):
        x, y, c = _position()
        me = 4 * x + 2 * y + c
        out = []
        for k in range(self.n):
            for j in range(N_DEV - 1):
                d = j + 1
                out.append(pltpu.make_async_remote_copy(
                    src_ref=self.part_refs[k].at[me ^ d], dst_ref=self.recv_refs[k].at[j],
                    send_sem=self.send_sems.at[7 * k + j], recv_sem=self.recv_sems.at[7 * k + j],
                    device_id=(x ^ (d >> 2), y ^ ((d >> 1) & 1), c ^ (d & 1)), device_id_type=MESH))
        return out

    def begin(self):
        for cp in self.copies():
            cp.start()

    def finish(self):
        for cp in self.copies():
            cp.wait_recv()
        for cp in self.copies():
            cp.wait_send()

    @staticmethod
    def semaphores(n):
        return [pltpu.SemaphoreType.DMA((7 * n,)), pltpu.SemaphoreType.DMA((7 * n,))]


def _sum_slabs(own, recv, name):
    rows, cols = own.shape

    def body(own_ref, recv_ref, out_ref):
        acc = own_ref[...].astype(F32)
        for j in range(N_DEV - 1):
            acc = acc + recv_ref[j].astype(F32)
        out_ref[...] = acc

    return pl.pallas_call(body, name=name, out_shape=jax.ShapeDtypeStruct((rows, cols), F32))(own, recv)


N_CHIPS = 4


def _final_reduce(parts, vec_parts, rep, early_all, name):
    _, rows, cols = parts.shape
    vrows, prows = vec_parts.shape[1], rep.shape[0]
    peers = N_CHIPS - 1

    def body(parts_ref, vec_ref, rep_ref, early_ref, out_ref, vec_out, rep_out, early_out, recv_a, own_a, mid, recv_b, vec_recv, rep_all,
             a_send, a_recv, a_local, b_send, b_recv, v_send, v_recv, r_send, r_recv):
        x, y, c = _position()
        chip = 2 * x + y
        me = 2 * chip + c
        everyone = [(d, (x ^ (d >> 2), y ^ ((d >> 1) & 1), c ^ (d & 1))) for d in range(1, N_DEV)]

        stage1, loads = [], []
        for j in range(N_CHIPS):
            stage1.append(pltpu.make_async_remote_copy(
                src_ref=parts_ref.at[2 * j + 1 - c], dst_ref=recv_a.at[j], send_sem=a_send.at[j], recv_sem=a_recv.at[j],
                device_id=(x, y, 1 - c), device_id_type=MESH))
            loads.append(pltpu.make_async_copy(parts_ref.at[2 * j + c], own_a.at[j], a_local.at[j]))
            stage1[-1].start()
            loads[-1].start()
        small = []
        for d, to in everyone:
            small.append(pltpu.make_async_remote_copy(
                src_ref=vec_ref.at[me ^ d], dst_ref=vec_recv.at[d - 1], send_sem=v_send.at[d - 1], recv_sem=v_recv.at[d - 1],
                device_id=to, device_id_type=MESH))
            small.append(pltpu.make_async_remote_copy(
                src_ref=rep_ref, dst_ref=rep_all.at[me], send_sem=r_send.at[d - 1], recv_sem=r_recv.at[d - 1],
                device_id=to, device_id_type=MESH))
        for cp in small:
            cp.start()
        rep_all[me] = rep_ref[...]

        for j in range(N_CHIPS):
            loads[j].wait()
            stage1[j].wait_recv()
            mid[j] = (own_a[j].astype(F32) + recv_a[j].astype(F32)).astype(BF16)
        stage2 = []
        for d in range(1, N_CHIPS):
            stage2.append(pltpu.make_async_remote_copy(
                src_ref=mid.at[chip ^ d], dst_ref=recv_b.at[d - 1], send_sem=b_send.at[d - 1], recv_sem=b_recv.at[d - 1],
                device_id=(x ^ (d >> 1), y ^ (d & 1), c), device_id_type=MESH))
            stage2[-1].start()

        for cp in small:
            cp.wait_recv()
        vec_sum = vec_ref[me]
        for j in range(N_DEV - 1):
            vec_sum = vec_sum + vec_recv[j]
        vec_out[...] = vec_sum
        rep_sum = rep_all[0]
        for d in range(1, N_DEV):
            rep_sum = rep_sum + rep_all[d]
        rep_out[...] = rep_sum
        early_sum = early_ref[0]
        for d in range(1, N_DEV):
            early_sum = early_sum + early_ref[d]
        early_out[...] = early_sum

        acc = mid[chip].astype(F32)
        for d in range(peers):
            stage2[d].wait_recv()
            acc = acc + recv_b[d].astype(F32)
        out_ref[...] = acc
        for cp in stage1 + stage2 + small:
            cp.wait_send()

    vmem = pl.BlockSpec(memory_space=pltpu.VMEM)
    dma = pltpu.SemaphoreType.DMA
    return pl.pallas_call(
        body, name=name,
        out_shape=[jax.ShapeDtypeStruct((rows, cols), F32), jax.ShapeDtypeStruct((vrows, LANES), F32),
                   jax.ShapeDtypeStruct((prows, LANES), F32), jax.ShapeDtypeStruct(early_all.shape[1:], F32)],
        in_specs=[pl.BlockSpec(memory_space=pl.ANY), vmem, vmem, vmem],
        out_specs=[vmem, vmem, vmem, vmem],
        scratch_shapes=[pltpu.VMEM((N_CHIPS, rows, cols), BF16), pltpu.VMEM((N_CHIPS, rows, cols), BF16),
                        pltpu.VMEM((N_CHIPS, rows, cols), BF16), pltpu.VMEM((peers, rows, cols), BF16),
                        pltpu.VMEM((N_DEV - 1, vrows, LANES), F32), pltpu.VMEM((N_DEV, prows, LANES), F32),
                        dma((N_CHIPS,)), dma((N_CHIPS,)), dma((N_CHIPS,)), dma((peers,)), dma((peers,)),
                        dma((N_DEV - 1,)), dma((N_DEV - 1,)), dma((N_DEV - 1,)), dma((N_DEV - 1,))],
        compiler_params=pltpu.CompilerParams(vmem_limit_bytes=VMEM_LIMIT),
    )(parts, vec_parts, rep, early_all)


def _fwd_in0(x, g, w, shards, ts):
    seq = x.shape[0]
    n = len(shards)
    steps = seq // ts

    def body(*refs):
        x_ref, g_ref, w_ref = refs[:3]
        h_ref, q_ref, kv_ref, gug_ref = refs[3 + n:7 + n]
        gather = lambda: _Gather(refs[3:3 + n], refs[7 + n:7 + 2 * n], *refs[7 + 2 * n:])
        i = pl.program_id(0)

        @pl.when(i == 0)
        def _():
            gather().begin()

        h, _ = _rms_fwd(x_ref[...], g_ref[...])
        h = h.astype(BF16)
        h_ref[...] = h
        proj = _dot_nt(h, w_ref[...])
        q_ref[...] = proj[:, :ATTN_WIDTH].astype(BF16)
        kv_ref[...] = proj[:, ATTN_WIDTH:ATTN_WIDTH + 2 * KV_WIDTH].astype(BF16)
        gug_ref[...] = proj[:, ATTN_WIDTH + 2 * KV_WIDTH:].astype(BF16)

        @pl.when(i == RELAY_AT(steps))
        def _():
            gather().relay()

        @pl.when(i == steps - 1)
        def _():
            gather().finish()

    hbm = pl.BlockSpec(memory_space=pl.ANY)
    out = pl.pallas_call(
        body, name="fwd_in0", grid=(steps,),
        in_specs=[_tile(ts, D_MODEL), _full((1, D_MODEL)), _full((EVEN_IN, D_MODEL))] + [hbm] * n,
        out_specs=[_tile(ts, D_MODEL), _tile(ts, ATTN_WIDTH), _tile(ts, 2 * KV_WIDTH), _tile(ts, 3 * POOL_WIDTH)] + [hbm] * n,
        out_shape=[jax.ShapeDtypeStruct((seq, D_MODEL), BF16), jax.ShapeDtypeStruct((seq, ATTN_WIDTH), BF16),
                   jax.ShapeDtypeStruct((seq, 2 * KV_WIDTH), BF16), jax.ShapeDtypeStruct((seq, 3 * POOL_WIDTH), BF16)]
        + [jax.ShapeDtypeStruct((N_DEV, *b.shape), b.dtype) for b in shards],
        scratch_shapes=_Gather.semaphores(n),
        compiler_params=_params(),
    )(x, g, w, *shards)
    return out[:4], out[4:]


GROUP_ROWS = GROUP * BLOCK


def _attn_mask(rows, first_block):
    row = lax.broadcasted_iota(jnp.int32, (rows, 2 * BLOCK), 0) & (BLOCK - 1)
    col = lax.broadcasted_iota(jnp.int32, (rows, 2 * BLOCK), 1)
    dist = row + BLOCK - col
    valid = (dist >= 0) & (dist < BLOCK) & ((col >= BLOCK) | jnp.logical_not(first_block))
    return valid, dist.astype(F32)


def _group_rows(block, kvh):
    return jnp.concatenate([block[:, HEAD_DIM * h:HEAD_DIM * (h + 1)] for h in range(GROUP * kvh, GROUP * (kvh + 1))], axis=0)


def _group_columns(sink_ref, kvh):
    head = lax.broadcasted_iota(jnp.int32, (GROUP_ROWS, 1), 0) // BLOCK
    sink = jnp.zeros((GROUP_ROWS, 1), F32)
    slope = jnp.zeros((GROUP_ROWS, 1), F32)
    for i in range(GROUP):
        sink = jnp.where(head == i, sink_ref[GROUP * kvh + i], sink)
        slope = jnp.where(head == i, SLOPES[GROUP * kvh + i], slope)
    return sink, slope


def _attn_probs(qh, kh, sink, slope, valid, distf):
    s = _dot_nt(qh, kh) * (HEAD_DIM ** -0.5)
    s = jnp.where(valid, s - slope * distf, NEG)
    mx = jnp.maximum(jnp.max(s, axis=-1, keepdims=True), sink)
    e = jnp.exp(s - mx)
    es = jnp.exp(sink - mx)
    den = jnp.sum(e, axis=-1, keepdims=True) + es
    return e / den, es / den


def _attn_fwd(q, kv, sinks, shards):
    seq = q.shape[0]
    nb = seq // BLOCK
    ns = len(shards)

    def body(*refs):
        sink_ref, q_ref, kvc_ref, kvp_ref = refs[:4]
        o_ref = refs[4 + ns]
        gather = lambda: _Gather(refs[4:4 + ns], refs[5 + ns:5 + 2 * ns], *refs[5 + 2 * ns:])
        n = pl.program_id(0)

        @pl.when(n == 0)
        def _():
            gather().begin()

        valid, distf = _attn_mask(BLOCK, n == 0)
        qb = q_ref[...]
        kk = jnp.concatenate([kvp_ref[...], kvc_ref[...]], axis=0)
        for h in range(Q_HEADS):
            kvh = h // GROUP
            qh = qb[:, HEAD_DIM * h:HEAD_DIM * (h + 1)]
            kh = kk[:, HEAD_DIM * kvh:HEAD_DIM * (kvh + 1)]
            vh = kk[:, KV_WIDTH + HEAD_DIM * kvh:KV_WIDTH + HEAD_DIM * (kvh + 1)]
            p, _ = _attn_probs(qh, kh, sink_ref[h], SLOPES[h], valid, distf)
            o_ref[:, HEAD_DIM * h:HEAD_DIM * (h + 1)] = _dot(p.astype(BF16), vh)

        @pl.when(n == RELAY_AT(nb))
        def _():
            gather().relay()

        @pl.when(n == nb - 1)
        def _():
            gather().finish()

    hbm = pl.BlockSpec(memory_space=pl.ANY)
    out = pl.pallas_call(
        body, name="attn_fwd", grid=(nb,),
        in_specs=[pl.BlockSpec(memory_space=pltpu.SMEM),
                  pl.BlockSpec((BLOCK, ATTN_WIDTH), lambda n: (n, 0)),
                  pl.BlockSpec((BLOCK, 2 * KV_WIDTH), lambda n: (n, 0)),
                  pl.BlockSpec((BLOCK, 2 * KV_WIDTH), lambda n: (jnp.maximum(n - 1, 0), 0))] + [hbm] * ns,
        out_specs=[pl.BlockSpec((BLOCK, ATTN_WIDTH), lambda n: (n, 0))] + [hbm] * ns,
        out_shape=[jax.ShapeDtypeStruct((seq, ATTN_WIDTH), F32)]
        + [jax.ShapeDtypeStruct((N_DEV, *b.shape), b.dtype) for b in shards],
        scratch_shapes=_Gather.semaphores(ns),
        compiler_params=_params(),
    )(sinks, q, kv, kv, *shards)
    return out[0], out[1:]


def _pool_counts(first_row, rows, window):
    t = first_row + lax.broadcasted_iota(jnp.int32, (rows, 1), 0)
    return jnp.minimum(t + 1, window).astype(F32)


def _fwd_out0(gug, o, pool_w, pool_scale, w_out, g_post, x, ts):
    seq = x.shape[0]
    hb = ts // POOL_HALO

    def body(gug_ref, halo_ref, o_ref, pw_ref, ps_ref, w_ref, g_ref, x_ref, mix_ref, pooled_ref, y_ref, x1_ref, ubuf):
        i = pl.program_id(0)
        ga = gug_ref[:, :ATTN_WIDTH].astype(F32)
        u = gug_ref[:, ATTN_WIDTH:ATTN_WIDTH + POOL_WIDTH].astype(F32)
        gb = gug_ref[:, ATTN_WIDTH + POOL_WIDTH:].astype(F32)
        mix_ref[:, :ATTN_WIDTH] = (o_ref[...] * (ga * _sigmoid(ga))).astype(BF16)
        ubuf[:POOL_HALO, :] = jnp.where(i > 0, halo_ref[...].astype(F32), 0.0)
        ubuf[POOL_HALO:, :] = u
        silu_gb = gb * _sigmoid(gb)
        for g, window in enumerate(POOL_WINDOWS):
            lanes = slice(POOL_GC * g, POOL_GC * (g + 1))
            acc = ubuf[pl.ds(POOL_HALO, ts), lanes]
            for k in range(1, window):
                acc = acc + ubuf[pl.ds(POOL_HALO - k, ts), lanes]
            pooled = (acc / _pool_counts(i * ts, ts, window) - u[:, lanes]).astype(BF16)
            pooled_ref[:, lanes] = pooled
            ypool = _dot(pooled, pw_ref[g].astype(BF16)) * ps_ref[:, lanes]
            mix_ref[:, ATTN_WIDTH + POOL_GC * g:ATTN_WIDTH + POOL_GC * (g + 1)] = (ypool * silu_gb[:, lanes]).astype(BF16)
        y = _dot(mix_ref[...], w_ref[...])
        y_ref[...] = y
        yn, _ = _rms_fwd(y, g_ref[...])
        x1_ref[...] = x_ref[...] + yn

    return pl.pallas_call(
        body, name="fwd_out0", grid=(seq // ts,),
        in_specs=[_tile(ts, 3 * POOL_WIDTH),
                  pl.BlockSpec((POOL_HALO, POOL_WIDTH), lambda i: (jnp.maximum(i * hb - 1, 0), 1)),
                  _tile(ts, ATTN_WIDTH), _full((4, POOL_GC, POOL_GC)), _full((1, POOL_WIDTH)),
                  _full((D_MODEL, D_MODEL)), _full((1, D_MODEL)), _tile(ts, D_MODEL)],
        out_specs=[_tile(ts, D_MODEL), _tile(ts, POOL_WIDTH), _tile(ts, D_MODEL), _tile(ts, D_MODEL)],
        out_shape=[jax.ShapeDtypeStruct((seq, D_MODEL), BF16), jax.ShapeDtypeStruct((seq, POOL_WIDTH), BF16),
                   jax.ShapeDtypeStruct((seq, D_MODEL), F32), jax.ShapeDtypeStruct((seq, D_MODEL), F32)],
        scratch_shapes=[pltpu.VMEM((ts + POOL_HALO, POOL_WIDTH), F32)],
        compiler_params=_params(),
    )(gug, gug, o, pool_w, pool_scale, w_out, g_post, x)


def _fwd_in1(x1, g, w, ts):
    seq = x1.shape[0]

    def body(x_ref, g_ref, w_ref, h_ref, proj_ref, glu_ref):
        h, _ = _rms_fwd(x_ref[...], g_ref[...])
        h = h.astype(BF16)
        h_ref[...] = h
        proj = _dot_nt(h, w_ref[...])
        proj_ref[...] = proj.astype(BF16)
        glu_ref[...] = proj[:, :D_MODEL] * _sigmoid(proj[:, D_MODEL:2 * D_MODEL])

    return pl.pallas_call(
        body, name="fwd_in1", grid=(seq // ts,),
        in_specs=[_tile(ts, D_MODEL), _full((1, D_MODEL)), _full((ODD_IN, D_MODEL))],
        out_specs=[_tile(ts, D_MODEL), _tile(ts, ODD_IN), _tile(ts, D_MODEL)],
        out_shape=[jax.ShapeDtypeStruct((seq, D_MODEL), BF16), jax.ShapeDtypeStruct((seq, ODD_IN), BF16),
                   jax.ShapeDtypeStruct((seq, D_MODEL), F32)],
        compiler_params=_params(),
    )(x1, g, w)


ACC_LOSS, ACC_POST, ACC_LN_G, ACC_LN_B, ACC_DW_B = range(5)
CONV_FIRST = CONV_HALO - CONV_K + 1


def _fwd_tap(offset):
    return offset - CONV_FIRST if CONV_FIRST <= offset <= CONV_HALO else None


def _bwd_tap(offset):
    return CONV_K - 1 - offset if offset < CONV_K else None


def _conv_taps(w_ref, buf_ref, ts, lanes, tap_of_offset):
    out = None
    for b in range(SUBLANES):
        rows = ts if b == 0 else ts + SUBLANES
        part = None
        for a in range(CONV_HALO // SUBLANES + 1):
            k = tap_of_offset(SUBLANES * a + b)
            if k is None:
                continue
            term = w_ref[k:k + 1, lanes] * buf_ref[pl.ds(SUBLANES * a, rows), lanes]
            part = term if part is None else part + term
        if part is None:
            continue
        if b:
            part = part[b:b + ts, :]
        out = part if out is None else out + part
    return out


def _fwd_out1(glu, proj, dw_w, dw_b, ln_g, ln_b, w_out, g_post, x1, target, ts):
    seq = x1.shape[0]
    hb = ts // CONV_HALO

    def body(glu_ref, halo_ref, gate_ref, dww_ref, dwb_ref, lng_ref, lnb_ref, w_ref, g_ref, x1_ref, t_ref,
             ymix_ref, dy_ref, dx2_ref, dcf_ref, dgate_ref, acc_ref, gbuf):
        i = pl.program_id(0)

        @pl.when(i == 0)
        def _():
            acc_ref[...] = jnp.zeros_like(acc_ref)

        gbuf[:CONV_HALO, :] = jnp.where(i > 0, halo_ref[...], 0.0)
        gbuf[CONV_HALO:, :] = glu_ref[...]
        for lb in range(D_MODEL // LANES):
            lanes = slice(LANES * lb, LANES * (lb + 1))
            dcf_ref[:, lanes] = _conv_taps(dww_ref, gbuf, ts, lanes, _fwd_tap)
        cf = dcf_ref[...] + dwb_ref[...]
        mu = jnp.mean(cf, axis=-1, keepdims=True)
        cen = cf - mu
        rs = lax.rsqrt(jnp.mean(cen * cen, axis=-1, keepdims=True) + EPS)
        xhat = cen * rs
        cn = xhat * lng_ref[...] + lnb_ref[...]
        gate = gate_ref[...].astype(F32)
        sg = _sigmoid(gate)
        sc = _sigmoid(cn)
        silu_gate = gate * sg
        silu_cn = cn * sc
        ymix = (silu_cn * silu_gate).astype(BF16)
        ymix_ref[...] = ymix
        y = _dot(ymix, w_ref[...])
        yn, r = _rms_fwd(y, g_ref[...])
        err = (x1_ref[...] + yn) - t_ref[...]
        acc_ref[ACC_LOSS] += _rows8(err * err)
        dx2 = err * (1.0 / D_MODEL)
        dx2_ref[...] = dx2
        dy, dpost = _rms_bwd(y, r, g_ref[...], dx2)
        acc_ref[ACC_POST] += _rows8(dpost)
        dy = dy.astype(BF16)
        dy_ref[...] = dy
        dymix = _dot_nt(dy, w_ref[...])
        dgate_ref[...] = (dymix * silu_cn * (sg * (1.0 + gate * (1.0 - sg)))).astype(BF16)
        dcn = dymix * silu_gate * (sc * (1.0 + cn * (1.0 - sc)))
        acc_ref[ACC_LN_G] += _rows8(dcn * xhat)
        acc_ref[ACC_LN_B] += _rows8(dcn)
        dxhat = dcn * lng_ref[...]
        dcf = rs * (dxhat - jnp.mean(dxhat, axis=-1, keepdims=True)
                    - xhat * jnp.mean(dxhat * xhat, axis=-1, keepdims=True))
        acc_ref[ACC_DW_B] += _rows8(dcf)
        dcf_ref[...] = dcf

    return pl.pallas_call(
        body, name="fwd_out1", grid=(seq // ts,),
        in_specs=[_tile(ts, D_MODEL),
                  pl.BlockSpec((CONV_HALO, D_MODEL), lambda i: (jnp.maximum(i * hb - 1, 0), 0)),
                  _tile(ts, D_MODEL, 2), _full((CONV_HALO, D_MODEL)), _full((1, D_MODEL)), _full((1, D_MODEL)),
                  _full((1, D_MODEL)), _full((D_MODEL, D_MODEL)), _full((1, D_MODEL)), _tile(ts, D_MODEL),
                  _tile(ts, D_MODEL)],
        out_specs=[_tile(ts, D_MODEL), _tile(ts, D_MODEL), _tile(ts, D_MODEL), _tile(ts, D_MODEL), _tile(ts, D_MODEL),
                   _full((5, SUBLANES, D_MODEL))],
        out_shape=[jax.ShapeDtypeStruct((seq, D_MODEL), BF16), jax.ShapeDtypeStruct((seq, D_MODEL), BF16),
                   jax.ShapeDtypeStruct((seq, D_MODEL), F32), jax.ShapeDtypeStruct((seq, D_MODEL), F32),
                   jax.ShapeDtypeStruct((seq, D_MODEL), BF16), jax.ShapeDtypeStruct((5, SUBLANES, D_MODEL), F32)],
        scratch_shapes=[pltpu.VMEM((ts + CONV_HALO, D_MODEL), F32)],
        compiler_params=_params(),
    )(glu, glu, proj, dw_w, dw_b, ln_g, ln_b, w_out, g_post, x1, target)


def _bwd_in1(dcf, glu, proj, dgate, dw_w, w_in, x1, g_pre, dx2, ts):
    seq = x1.shape[0]
    hb = ts // CONV_HALO
    last = seq // CONV_HALO - 1
    nt = seq // ts

    def body(dcf_ref, dnext_ref, glu_ref, gprev_ref, ab_ref, dgate_ref, dww_ref, w_ref, x_ref, g_ref, dx2_ref,
             dproj_ref, dx1_ref, ddw_ref, dpre_ref, dbuf, gbuf, zbuf, sbuf):
        i = pl.program_id(0)

        @pl.when(i == 0)
        def _():
            ddw_ref[...] = jnp.zeros_like(ddw_ref)
            dpre_ref[...] = jnp.zeros_like(dpre_ref)

        dcf = dcf_ref[...]
        dbuf[:ts, :] = dcf
        dbuf[ts:, :] = jnp.where(i < nt - 1, dnext_ref[...], 0.0)
        gbuf[:CONV_HALO, :] = jnp.where(i > 0, gprev_ref[...], 0.0)
        gbuf[CONV_HALO:, :] = glu_ref[...]
        zbuf[:SUBLANES, :] = jnp.zeros((SUBLANES, D_MODEL), F32)
        zbuf[pl.ds(SUBLANES, ts), :] = dcf
        zbuf[pl.ds(SUBLANES + ts, SUBLANES), :] = jnp.zeros((SUBLANES, D_MODEL), F32)
        for lb in range(D_MODEL // LANES):
            lanes = slice(LANES * lb, LANES * (lb + 1))
            gate_lanes = slice(D_MODEL + LANES * lb, D_MODEL + LANES * (lb + 1))
            dglu = _conv_taps(dww_ref, dbuf, ts, lanes, _bwd_tap)
            a = ab_ref[:, lanes].astype(F32)
            sb = _sigmoid(ab_ref[:, gate_lanes].astype(F32))
            dproj_ref[:, lanes] = (dglu * sb).astype(BF16)
            dproj_ref[:, gate_lanes] = (dglu * a * sb * (1.0 - sb)).astype(BF16)
            for b in range(SUBLANES):
                rows = ts if b == 0 else ts + SUBLANES
                sbuf[b, pl.ds(0, rows), :] = zbuf[pl.ds(SUBLANES - b, rows), lanes]
                for a8 in range(CONV_HALO // SUBLANES + 1):
                    k = _fwd_tap(SUBLANES * a8 + b)
                    if k is not None:
                        ddw_ref[k, :, lanes] += _rows8(sbuf[b, pl.ds(0, rows), :] * gbuf[pl.ds(SUBLANES * a8, rows), lanes])
        dproj_ref[:, 2 * D_MODEL:] = dgate_ref[...]
        dh = _dot(dproj_ref[...], w_ref[...])
        x = x_ref[...]
        r = lax.rsqrt(jnp.mean(x * x, axis=-1, keepdims=True) + EPS)
        dx, dpre = _rms_bwd(x, r, g_ref[...], dh)
        dx1_ref[...] = dx2_ref[...] + dx
        dpre_ref[...] += _rows8(dpre)

    return pl.pallas_call(
        body, name="bwd_in1", grid=(nt,),
        in_specs=[_tile(ts, D_MODEL),
                  pl.BlockSpec((CONV_HALO, D_MODEL), lambda i: (jnp.minimum((i + 1) * hb, last), 0)),
                  _tile(ts, D_MODEL),
                  pl.BlockSpec((CONV_HALO, D_MODEL), lambda i: (jnp.maximum(i * hb - 1, 0), 0)),
                  _tile(ts, 2 * D_MODEL), _tile(ts, D_MODEL), _full((CONV_HALO, D_MODEL)),
                  _full((ODD_IN, D_MODEL)), _tile(ts, D_MODEL), _full((1, D_MODEL)), _tile(ts, D_MODEL)],
        out_specs=[_tile(ts, ODD_IN), _tile(ts, D_MODEL), _full((CONV_HALO, SUBLANES, D_MODEL)), _full((SUBLANES, D_MODEL))],
        out_shape=[jax.ShapeDtypeStruct((seq, ODD_IN), BF16), jax.ShapeDtypeStruct((seq, D_MODEL), F32),
                   jax.ShapeDtypeStruct((CONV_HALO, SUBLANES, D_MODEL), F32), jax.ShapeDtypeStruct((SUBLANES, D_MODEL), F32)],
        scratch_shapes=[pltpu.VMEM((ts + CONV_HALO, D_MODEL), F32), pltpu.VMEM((ts + CONV_HALO, D_MODEL), F32),
                        pltpu.VMEM((ts + 2 * SUBLANES, D_MODEL), F32), pltpu.VMEM((SUBLANES, ts + SUBLANES, LANES), F32)],
        compiler_params=_params(),
    )(dcf, dcf, glu, glu, proj, dgate, dw_w, w_in, x1, g_pre, dx2)


def _bwd_out0(dx1, y0, g_post, w_out, gug, o, pooled, pool_w, pool_scale, mix, ts):
    seq = dx1.shape[0]

    def body(dx1_ref, y_ref, g_ref, w_ref, gug_ref, o_ref, pooled_ref, pw_ref, ps_ref, mix_ref,
             do_ref, dgg_ref, dpooled_ref, dpost_ref, dscale_ref, dpw_ref, gw_ref, gacc):
        i = pl.program_id(0)

        @pl.when(i == 0)
        def _():
            dpost_ref[...] = jnp.zeros_like(dpost_ref)
            dscale_ref[...] = jnp.zeros_like(dscale_ref)
            dpw_ref[...] = jnp.zeros_like(dpw_ref)

        y = y_ref[...]
        r = lax.rsqrt(jnp.mean(y * y, axis=-1, keepdims=True) + EPS)
        dy, dpost = _rms_bwd(y, r, g_ref[...], dx1_ref[...])
        dpost_ref[...] += _rows8(dpost)
        dy = dy.astype(BF16)
        _accumulate_tn(gacc, mix_ref[...], dy, i == 0)
        dmix = _dot_nt(dy, w_ref[...])
        dya = dmix[:, :ATTN_WIDTH]
        dyb = dmix[:, ATTN_WIDTH:]
        ga = gug_ref[:, :ATTN_WIDTH].astype(F32)
        gb = gug_ref[:, ATTN_WIDTH + POOL_WIDTH:].astype(F32)
        sga = _sigmoid(ga)
        sgb = _sigmoid(gb)
        do_ref[...] = (dya * (ga * sga)).astype(BF16)
        dgg_ref[:, :ATTN_WIDTH] = (dya * o_ref[...] * (sga * (1.0 + ga * (1.0 - sga)))).astype(BF16)
        dypool = dyb * (gb * sgb)
        dsilu_gb = sgb * (1.0 + gb * (1.0 - sgb))
        for g in range(len(POOL_WINDOWS)):
            lanes = slice(POOL_GC * g, POOL_GC * (g + 1))
            pooled = pooled_ref[:, lanes]
            wg = pw_ref[g].astype(BF16)
            pw = _dot(pooled, wg)
            scale = ps_ref[:, lanes]
            dgg_ref[:, ATTN_WIDTH + POOL_GC * g:ATTN_WIDTH + POOL_GC * (g + 1)] = (
                dyb[:, lanes] * (pw * scale) * dsilu_gb[:, lanes]).astype(BF16)
            dscale_ref[:, lanes] += _rows8(dypool[:, lanes] * pw)
            dpw = (dypool[:, lanes] * scale).astype(BF16)
            dpooled_ref[:, lanes] = _dot_nt(dpw, wg)
            dpw_ref[g] += _dot_tn(pooled, dpw)

        @pl.when(i == seq // ts - 1)
        def _():
            gw_ref[...] = gacc[...].astype(BF16)

    return pl.pallas_call(
        body, name="bwd_out0", grid=(seq // ts,),
        in_specs=[_tile(ts, D_MODEL), _tile(ts, D_MODEL), _full((1, D_MODEL)), _resident((D_MODEL, D_MODEL)),
                  _tile(ts, 3 * POOL_WIDTH), _tile(ts, ATTN_WIDTH), _tile(ts, POOL_WIDTH),
                  _full((4, POOL_GC, POOL_GC)), _full((1, POOL_WIDTH)), _tile(ts, D_MODEL)],
        out_specs=[_tile(ts, ATTN_WIDTH), _tile(ts, ATTN_WIDTH + POOL_WIDTH), _tile(ts, POOL_WIDTH),
                   _full((SUBLANES, D_MODEL)), _full((SUBLANES, POOL_WIDTH)), _full((4, POOL_GC, POOL_GC)),
                   _full((D_MODEL, D_MODEL))],
        out_shape=[jax.ShapeDtypeStruct((seq, ATTN_WIDTH), BF16),
                   jax.ShapeDtypeStruct((seq, ATTN_WIDTH + POOL_WIDTH), BF16), jax.ShapeDtypeStruct((seq, POOL_WIDTH), F32),
                   jax.ShapeDtypeStruct((SUBLANES, D_MODEL), F32), jax.ShapeDtypeStruct((SUBLANES, POOL_WIDTH), F32),
                   jax.ShapeDtypeStruct((4, POOL_GC, POOL_GC), F32), jax.ShapeDtypeStruct((D_MODEL, D_MODEL), BF16)],
        scratch_shapes=[pltpu.VMEM((D_MODEL, D_MODEL), F32)],
        compiler_params=_params(),
    )(dx1, y0, g_post, w_out, gug, o, pooled, pool_w, pool_scale, mix)


def _attn_bwd(q, kv, do, sinks, parts, early):
    seq = q.shape[0]
    nb = seq // BLOCK

    def qblock(j):
        return jnp.minimum(j, nb - 1)

    n = len(parts)

    def body(*refs):
        sink_ref, q_ref, kvc_ref, kvp_ref, do_ref = refs[:5]
        early_ref = refs[5 + n]
        dq_ref, dkv_ref, dsink_ref = refs[6 + n:9 + n]
        all_ref = refs[9 + 2 * n]
        carry, dkv_acc = refs[10 + 2 * n:12 + 2 * n]
        scatter = _Scatter(refs[5:5 + n], refs[9 + n:9 + 2 * n], *refs[12 + 2 * n:14 + 2 * n])
        gather = lambda: _Gather([early_ref], [all_ref], *refs[14 + 2 * n:])
        j = pl.program_id(0)

        @pl.when(j == 0)
        def _():
            scatter.begin()
            gather().begin()
            dsink_ref[...] = jnp.zeros_like(dsink_ref)
            carry[...] = jnp.zeros_like(carry)

        @pl.when(j < nb)
        def _():
            valid, distf = _attn_mask(GROUP_ROWS, j == 0)
            qb = q_ref[...]
            dob = do_ref[...]
            kk = jnp.concatenate([kvp_ref[...], kvc_ref[...]], axis=0)
            lane = lax.broadcasted_iota(jnp.int32, (BLOCK, LANES), 1)
            dsink = jnp.zeros((BLOCK, LANES), F32)
            for kvh in range(Q_HEADS // GROUP):
                kh = kk[:, HEAD_DIM * kvh:HEAD_DIM * (kvh + 1)]
                vh = kk[:, KV_WIDTH + HEAD_DIM * kvh:KV_WIDTH + HEAD_DIM * (kvh + 1)]
                qg = _group_rows(qb, kvh)
                dog = _group_rows(dob, kvh)
                sink, slope = _group_columns(sink_ref, kvh)
                p, psink = _attn_probs(qg, kh, sink, slope, valid, distf)
                dp = _dot_nt(dog, vh)
                delta = jnp.sum(p * dp, axis=-1, keepdims=True)
                ds = (p * (dp - delta) * (HEAD_DIM ** -0.5)).astype(BF16)
                dsink_rows = -psink * delta
                dqg = _dot(ds, kh).astype(BF16)
                for i in range(GROUP):
                    h = GROUP * kvh + i
                    dsink = dsink + jnp.where(lane == h, dsink_rows[BLOCK * i:BLOCK * (i + 1), :], 0.0)
                    dq_ref[:, HEAD_DIM * h:HEAD_DIM * (h + 1)] = dqg[BLOCK * i:BLOCK * (i + 1), :]
                dkv_acc[:, HEAD_DIM * kvh:HEAD_DIM * (kvh + 1)] = _dot_tn(ds, qg)
                dkv_acc[:, KV_WIDTH + HEAD_DIM * kvh:KV_WIDTH + HEAD_DIM * (kvh + 1)] = _dot_tn(p.astype(BF16), dog)
            dsink_ref[...] += dsink

            @pl.when(j > 0)
            def _():
                dkv_ref[...] = (carry[...] + dkv_acc[:BLOCK, :]).astype(BF16)

            carry[...] = dkv_acc[BLOCK:, :]

        @pl.when(j == RELAY_AT(nb))
        def _():
            gather().relay()

        @pl.when(j == nb)
        def _():
            dkv_ref[...] = carry[...].astype(BF16)
            scatter.finish()
            gather().finish()

    hbm = pl.BlockSpec(memory_space=pl.ANY)
    out = pl.pallas_call(
        body, name="attn_bwd", grid=(nb + 1,),
        in_specs=[pl.BlockSpec(memory_space=pltpu.SMEM),
                  pl.BlockSpec((BLOCK, ATTN_WIDTH), lambda j: (qblock(j), 0)),
                  pl.BlockSpec((BLOCK, 2 * KV_WIDTH), lambda j: (qblock(j), 0)),
                  pl.BlockSpec((BLOCK, 2 * KV_WIDTH), lambda j: (jnp.maximum(qblock(j) - 1, 0), 0)),
                  pl.BlockSpec((BLOCK, ATTN_WIDTH), lambda j: (qblock(j), 0))] + [hbm] * (n + 1),
        out_specs=[pl.BlockSpec((BLOCK, ATTN_WIDTH), lambda j: (qblock(j), 0)),
                   pl.BlockSpec((BLOCK, 2 * KV_WIDTH), lambda j: (jnp.maximum(j - 1, 0), 0)),
                   _full((BLOCK, LANES))] + [hbm] * (n + 1),
        out_shape=[jax.ShapeDtypeStruct((seq, ATTN_WIDTH), BF16), jax.ShapeDtypeStruct((seq, 2 * KV_WIDTH), BF16),
                   jax.ShapeDtypeStruct((BLOCK, LANES), F32)]
        + [jax.ShapeDtypeStruct((N_DEV - 1, *p.shape[1:]), p.dtype) for p in parts]
        + [jax.ShapeDtypeStruct((N_DEV, *early.shape), early.dtype)],
        scratch_shapes=[pltpu.VMEM((BLOCK, 2 * KV_WIDTH), F32), pltpu.VMEM((2 * BLOCK, 2 * KV_WIDTH), F32)]
        + _Scatter.semaphores(n) + _Gather.semaphores(1),
        compiler_params=_params(),
    )(sinks, q, kv, kv, do, *parts, early)
    return out[:3], out[3:3 + n], out[3 + n]


def _bwd_in0(dpooled, dq, dkv, dgg, w_in, x, g_pre, dx1, ts):
    seq = x.shape[0]
    hb = ts // POOL_HALO
    last = seq // POOL_HALO - 1
    nt = seq // ts

    def body(dp_ref, dnext_ref, dq_ref, dkv_ref, dgg_ref, w_ref, x_ref, g_ref, dx1_ref,
             dproj_ref, gx_ref, dpre_ref, dbuf):
        i = pl.program_id(0)

        @pl.when(i == 0)
        def _():
            dpre_ref[...] = jnp.zeros_like(dpre_ref)

        dpool = dp_ref[...]
        dnext = jnp.where(i < nt - 1, dnext_ref[...], 0.0)
        u0 = ATTN_WIDTH + 2 * KV_WIDTH + ATTN_WIDTH
        for g, window in enumerate(POOL_WINDOWS):
            lanes = slice(POOL_GC * g, POOL_GC * (g + 1))
            dbuf[:ts, lanes] = dpool[:, lanes] / _pool_counts(i * ts, ts, window)
            dbuf[ts:, lanes] = dnext[:, lanes] / _pool_counts((i + 1) * ts, POOL_HALO, window)
        for g, window in enumerate(POOL_WINDOWS):
            lanes = slice(POOL_GC * g, POOL_GC * (g + 1))
            acc = dbuf[pl.ds(0, ts), lanes]
            for k in range(1, window):
                acc = acc + dbuf[pl.ds(k, ts), lanes]
            dproj_ref[:, u0 + POOL_GC * g:u0 + POOL_GC * (g + 1)] = (acc - dpool[:, lanes]).astype(BF16)
        dproj_ref[:, :ATTN_WIDTH] = dq_ref[...]
        dproj_ref[:, ATTN_WIDTH:ATTN_WIDTH + 2 * KV_WIDTH] = dkv_ref[...]
        dproj_ref[:, ATTN_WIDTH + 2 * KV_WIDTH:u0] = dgg_ref[:, :ATTN_WIDTH]
        dproj_ref[:, u0 + POOL_WIDTH:] = dgg_ref[:, ATTN_WIDTH:]
        dh = _dot(dproj_ref[...], w_ref[...])
        x = x_ref[...]
        r = lax.rsqrt(jnp.mean(x * x, axis=-1, keepdims=True) + EPS)
        dx, dpre = _rms_bwd(x, r, g_ref[...], dh)
        gx_ref[...] = dx1_ref[...] + dx
        dpre_ref[...] += _rows8(dpre)

    return pl.pallas_call(
        body, name="bwd_in0", grid=(nt,),
        in_specs=[_tile(ts, POOL_WIDTH),
                  pl.BlockSpec((POOL_HALO, POOL_WIDTH), lambda i: (jnp.minimum((i + 1) * hb, last), 0)),
                  _tile(ts, ATTN_WIDTH), _tile(ts, 2 * KV_WIDTH), _tile(ts, ATTN_WIDTH + POOL_WIDTH),
                  _full((EVEN_IN, D_MODEL)), _tile(ts, D_MODEL), _full((1, D_MODEL)), _tile(ts, D_MODEL)],
        out_specs=[_tile(ts, EVEN_IN), _tile(ts, D_MODEL), _full((SUBLANES, D_MODEL))],
        out_shape=[jax.ShapeDtypeStruct((seq, EVEN_IN), BF16), jax.ShapeDtypeStruct((seq, D_MODEL), F32),
                   jax.ShapeDtypeStruct((SUBLANES, D_MODEL), F32)],
        scratch_shapes=[pltpu.VMEM((ts + POOL_HALO, POOL_WIDTH), F32)],
        compiler_params=_params(),
    )(dpooled, dpooled, dq, dkv, dgg, w_in, x, g_pre, dx1)


def _matmul_tn(a, b, name, ts, tm):
    seq, m = a.shape
    n = b.shape[1]
    steps = seq // ts

    def body(a_ref, b_ref, o_ref, acc):
        s = pl.program_id(1)

        @pl.when(s == 0)
        def _():
            acc[...] = jnp.zeros_like(acc)

        acc[...] += _dot_tn(a_ref[...], b_ref[...])

        @pl.when(s == steps - 1)
        def _():
            o_ref[...] = acc[...].astype(BF16)

    return pl.pallas_call(
        body, name=name, grid=(m // tm, steps),
        in_specs=[pl.BlockSpec((ts, tm), lambda j, s: (s, j)), pl.BlockSpec((ts, n), lambda j, s: (s, 0))],
        out_specs=pl.BlockSpec((tm, n), lambda j, s: (j, 0)),
        out_shape=jax.ShapeDtypeStruct((m, n), BF16),
        scratch_shapes=[pltpu.VMEM((tm, n), F32)],
        compiler_params=pltpu.CompilerParams(dimension_semantics=("arbitrary", "arbitrary"), vmem_limit_bytes=VMEM_LIMIT),
    )(a, b)


def _adamw_math(w, g, m, v):
    m = ADAM_B1 * m + (1.0 - ADAM_B1) * g
    v = ADAM_B2 * v + (1.0 - ADAM_B2) * (g * g)
    m_hat = m / (1.0 - ADAM_B1 ** ADAM_STEP)
    v_hat = v / (1.0 - ADAM_B2 ** ADAM_STEP)
    delta = -ADAM_LR * (m_hat / (jnp.sqrt(v_hat) + ADAM_EPS) + ADAM_WD * w)
    return delta, m, v


def _adamw(ws, gs, ms, vs, name):
    n = len(ws)

    def body(*refs):
        ins, outs = refs[:4 * n], refs[4 * n:]
        for k in range(n):
            delta, m, v = _adamw_math(ins[k][...], ins[n + k][...], ins[2 * n + k][...], ins[3 * n + k][...])
            outs[k][...] = delta
            outs[n + k][...] = m
            outs[2 * n + k][...] = v

    shapes = [jax.ShapeDtypeStruct(w.shape, F32) for w in ws]
    out = pl.pallas_call(body, name=name, out_shape=shapes * 3,
                         compiler_params=pltpu.CompilerParams(vmem_limit_bytes=VMEM_LIMIT))(*ws, *gs, *ms, *vs)
    return out[:n], out[n:2 * n], out[2 * n:]


TS_MATMUL, TS_CONV_FWD, TS_CONV_BWD, TS_GRAD = 512, 512, 256, 1024


def kernel(x, pre_norm, post_norm, a_w_in, a_sinks, b_pool_w, b_pool_scale, ab_w_out, c_w_in, c_dw_w, c_dw_b, c_ln_g, c_ln_b, c_w_out, loss_target, m_pre_norm, m_post_norm, m_a_w_in, m_a_sinks, m_b_pool_w, m_b_pool_scale, m_ab_w_out, m_c_w_in, m_c_dw_w, m_c_dw_b, m_c_ln_g, m_c_ln_b, m_c_w_out, v_pre_norm, v_post_norm, v_a_w_in, v_a_sinks, v_b_pool_w, v_b_pool_scale, v_ab_w_out, v_c_w_in, v_c_dw_w, v_c_dw_b, v_c_ln_g, v_c_ln_b, v_c_w_out):
    seq = x.shape[1]
    ts_big, ts_grad = min(TS_MATMUL, seq), min(TS_GRAD, seq)
    x2d = x[0]
    target = loss_target[0]
    ch = c_dw_b.shape[1]

    whole = lambda g: g.reshape(-1, D_MODEL)
    vec_rows = 40
    vecs = jnp.concatenate([c_dw_w[0, :, 0, :], c_dw_b, c_ln_g, c_ln_b, jnp.zeros((vec_rows - CONV_K - 3, ch), F32)], axis=0)
    w_in0t, vg = _all_gather([a_w_in[0].T.astype(BF16), vecs], "gather_first")
    w_in0t = whole(w_in0t)
    vg = vg.transpose(1, 0, 2).reshape(vec_rows, D_MODEL)
    dw_w = vg[:CONV_HALO]
    dw_b, ln_g, ln_b = vg[CONV_K:CONV_K + 1], vg[CONV_K + 1:CONV_K + 2], vg[CONV_K + 2:CONV_K + 3]

    sinks = a_sinks[0]
    (h0, q, kv, gug), (w_out0,) = _fwd_in0(x2d, pre_norm[0:1], w_in0t, [ab_w_out[0].astype(BF16)], ts_big)
    o, (w_in1t, w_out1) = _attn_fwd(q, kv, sinks, [c_w_in[0].T.astype(BF16), c_w_out[0].astype(BF16)])
    w_out0, w_in1t, w_out1 = whole(w_out0), whole(w_in1t), whole(w_out1)
    mix0, pooled, y0, x1 = _fwd_out0(gug, o, b_pool_w[0], b_pool_scale, w_out0, post_norm[0:1], x2d, ts_big)
    h1, proj1, glu = _fwd_in1(x1, pre_norm[1:2], w_in1t, ts_big)
    ymix1, dy1, dx2, dcf, dgate, acc1 = _fwd_out1(glu, proj1, dw_w, dw_b, ln_g, ln_b, w_out1, post_norm[1:2], x1, target, min(TS_CONV_FWD, seq))

    dproj1, dx1, ddw_w, dpre1 = _bwd_in1(dcf, glu, proj1, dgate, dw_w, w_in1t, x1, pre_norm[1:2], dx2, min(TS_CONV_BWD, seq))
    g_in1t = _matmul_tn(dproj1, h1, "grad_w_in1", ts_grad, 1024)
    g_out1 = _matmul_tn(ymix1, dy1, "grad_w_out1", ts_grad, 1024)
    do, dgg, dpooled, dpost0, dscale, dpool_w, g_out0 = _bwd_out0(dx1, y0, post_norm[0:1], w_out0, gug, o, pooled, b_pool_w[0], b_pool_scale, mix0, ts_big)
    slabs = lambda g: g.reshape(N_DEV, -1, D_MODEL)
    me = 4 * lax.axis_index("x") + 2 * lax.axis_index("y") + lax.axis_index("c")
    early = [slabs(g_in1t), slabs(g_out1), slabs(g_out0)]
    row = lambda a: jnp.sum(a, axis=0, keepdims=True)
    lanes8 = lambda a: row(a).reshape(-1, LANES)
    loss_row = jnp.pad(jnp.sum(acc1[ACC_LOSS]).reshape(1, 1), ((0, 0), (0, LANES - 1)))
    done = jnp.concatenate([lanes8(dpre1), lanes8(dpost0), lanes8(acc1[ACC_POST]), lanes8(dscale),
                            loss_row, jnp.zeros((3, LANES), F32), dpool_w.reshape(4 * POOL_GC, LANES)], axis=0)
    (dq, dkv, dsink), arrived, done_all = _attn_bwd(q, kv, do, sinks, early, done)
    g_in1t, g_c_w_out, g_ab_w_out = [_sum_slabs(lax.dynamic_index_in_dim(p, me, keepdims=False), r, name)
                                     for p, r, name in zip(early, arrived, ("sum_w_in1", "sum_w_out1", "sum_w_out0"))]
    dproj0, grad_x, dpre0 = _bwd_in0(dpooled, dq, dkv, dgg, w_in0t, x2d, pre_norm[0:1], dx1, ts_big)
    g_in0t = _matmul_tn(dproj0, h0, "grad_w_in0", ts_grad, 768)

    vec_g = jnp.concatenate([jnp.sum(ddw_w[:CONV_K], axis=1), row(acc1[ACC_DW_B]), row(acc1[ACC_LN_G]), row(acc1[ACC_LN_B]),
                             jnp.zeros((vec_rows - CONV_K - 3, D_MODEL), F32)], axis=0)
    last = jnp.concatenate([lanes8(dpre0), row(dsink), jnp.zeros((SUBLANES - 1, LANES), F32)], axis=0)
    g_in0t, vec_g, last, rep = _final_reduce(slabs(g_in0t), vec_g.reshape(vec_rows, N_DEV, ch).transpose(1, 0, 2),
                                             last, done_all, "final_reduce")
    g_a_w_in, g_c_w_in = g_in0t.T, g_in1t.T
    g_dw_w, g_dw_b, g_ln_g, g_ln_b = vec_g[:CONV_K], vec_g[CONV_K:CONV_K + 1], vec_g[CONV_K + 1:CONV_K + 2], vec_g[CONV_K + 2:CONV_K + 3]
    g_pre = jnp.concatenate([last[:8], rep[:8]], axis=0).reshape(2, D_MODEL)
    g_sinks = last[8:9, :Q_HEADS]
    g_post = rep[8:24].reshape(2, D_MODEL)
    g_scale = rep[24:28].reshape(1, POOL_WIDTH)
    loss = (0.5 / D_MODEL) * rep[28, 0]
    g_pool_w = rep[32:]

    grads = [g_pre, g_post, g_a_w_in, g_sinks, g_pool_w, g_scale, g_ab_w_out, g_c_w_in, g_dw_w, g_dw_b, g_ln_g, g_ln_b, g_c_w_out]
    weights = [pre_norm, post_norm, a_w_in, a_sinks, b_pool_w, b_pool_scale, ab_w_out, c_w_in, c_dw_w, c_dw_b, c_ln_g, c_ln_b, c_w_out]
    m_in = [m_pre_norm, m_post_norm, m_a_w_in, m_a_sinks, m_b_pool_w, m_b_pool_scale, m_ab_w_out, m_c_w_in, m_c_dw_w, m_c_dw_b, m_c_ln_g, m_c_ln_b, m_c_w_out]
    v_in = [v_pre_norm, v_post_norm, v_a_w_in, v_a_sinks, v_b_pool_w, v_b_pool_scale, v_ab_w_out, v_c_w_in, v_c_dw_w, v_c_dw_b, v_c_ln_g, v_c_ln_b, v_c_w_out]
    flat = lambda arrs: [a.reshape(g.shape) for a, g in zip(arrs, grads)]
    big = (2, 6, 7, 12)
    small = tuple(k for k in range(len(grads)) if k not in big)
    pick = lambda arrs, idx: [arrs[k] for k in idx]
    deltas, new_m, new_v = [None] * 13, [None] * 13, [None] * 13
    for idx, name in ((big, "adamw_matrices"), (small, "adamw_vectors")):
        d, m, v = _adamw(pick(flat(weights), idx), pick(grads, idx), pick(flat(m_in), idx), pick(flat(v_in), idx), name)
        for k, dk, mk, vk in zip(idx, d, m, v):
            deltas[k], new_m[k], new_v[k] = dk, mk, vk
    shaped = lambda arrs: [a.reshape(w.shape) for a, w in zip(arrs, weights)]
    return (loss, grad_x[None], *shaped(grads), *shaped(deltas), *shaped(new_m), *shaped(new_v))
```

```python
import functools

import jax
import jax.numpy as jnp
from jax import lax
from jax.experimental import pallas as pl
from jax.experimental.pallas import tpu as pltpu

F32 = jnp.float32
BF16 = jnp.bfloat16
MESH = pl.DeviceIdType.MESH
AXES = ("x", "y", "c")
N_DEV = 8

D_MODEL = 1024
HEAD_DIM = 64
Q_HEADS = 8
GROUP = 4
ATTN_WIDTH = 512
KV_WIDTH = 128
BLOCK = 128
POOL_WIDTH = 512
POOL_WINDOWS = (2, 4, 8, 16)
POOL_GC = 128
POOL_HALO = 16
EVEN_IN = 2304
CONV_K = 31
CONV_HALO = 32
ODD_IN = 3072
EPS = 1e-6
NEG = -1e30
SLOPES = tuple(2.0 ** (-8.0 * (h + 1) / Q_HEADS) for h in range(Q_HEADS))

ADAM_LR = 0.001
ADAM_B1 = 0.9
ADAM_B2 = 0.999
ADAM_EPS = 1e-08
ADAM_WD = 0.01
ADAM_STEP = 10

SUBLANES = 8
LANES = 128
VMEM_LIMIT = 56 * 1024 * 1024

NT = (((1,), (1,)), ((), ()))
TN = (((0,), (0,)), ((), ()))


def _params(**kw):
    return pltpu.CompilerParams(dimension_semantics=("arbitrary",), vmem_limit_bytes=VMEM_LIMIT, **kw)


def _dot(a, b):
    return jnp.dot(a, b, preferred_element_type=F32)


def _dot_nt(a, b):
    return lax.dot_general(a, b, NT, preferred_element_type=F32)


def _dot_tn(a, b):
    return lax.dot_general(a, b, TN, preferred_element_type=F32)


def _sigmoid(v):
    return 1.0 / (1.0 + jnp.exp(-v))


def _rows8(v):
    r, c = v.shape
    return jnp.sum(v.reshape(r // SUBLANES, SUBLANES, c), axis=0)


def _rms_fwd(v, g):
    r = lax.rsqrt(jnp.mean(v * v, axis=-1, keepdims=True) + EPS)
    return v * r * g, r


def _rms_bwd(v, r, g, dout):
    gd = dout * g
    dv = r * gd - v * (r * r * r) * jnp.mean(v * gd, axis=-1, keepdims=True)
    return dv, dout * (v * r)


def _full(shape):
    return pl.BlockSpec(shape, lambda i: (0,) * len(shape))


def _resident(shape):
    return pl.BlockSpec(shape, lambda i: (0,) * len(shape), pipeline_mode=pl.Buffered(1))


def _accumulate_tn(acc, a, b, first_step):
    @pl.when(first_step)
    def _():
        acc[...] = jnp.zeros_like(acc)

    acc[...] += _dot_tn(a, b)


def _tile(ts, cols, col_block=0):
    return pl.BlockSpec((ts, cols), lambda i: (i, col_block))


def _position():
    return lax.axis_index("x"), lax.axis_index("y"), lax.axis_index("c")


class _Gather:
    def __init__(self, x_refs, out_refs, send_sems, recv_sems, local_sems):
        self.x_refs, self.out_refs = x_refs, out_refs
        self.send_sems, self.recv_sems, self.local_sems = send_sems, recv_sems, local_sems
        self.n = len(x_refs)
        x, y, c = _position()
        self.c = c
        self.me, self.sibling = (x, y, c), (x, y, 1 - c)
        self.chips = [(1 - x, y), (x, 1 - y), (1 - x, 1 - y)]

    def copy(self, k, j, owner, to, src=None):
        slab = self.out_refs[k].at[4 * owner[0] + 2 * owner[1] + owner[2]]
        return pltpu.make_async_remote_copy(
            src_ref=slab if src is None else src, dst_ref=slab, send_sem=self.send_sems.at[7 * k + j],
            recv_sem=self.recv_sems.at[7 * k + j], device_id=to, device_id_type=MESH)

    def mine(self, k):
        return pltpu.make_async_copy(self.x_refs[k], self.out_refs[k].at[4 * self.me[0] + 2 * self.me[1] + self.c],
                                     self.local_sems.at[k])

    def first(self, k):
        out = [self.copy(k, 0, self.me, self.sibling, src=self.x_refs[k])]
        return out + [self.copy(k, 1 + j, self.me, (*chip, self.c), src=self.x_refs[k]) for j, chip in enumerate(self.chips)]

    def begin(self):
        for k in range(self.n):
            self.mine(k).start()
            for cp in self.first(k):
                cp.start()

    def passed(self, k):
        return [self.copy(k, 4 + j, (*chip, self.c), self.sibling) for j, chip in enumerate(self.chips)]

    def relay(self):
        for k in range(self.n):
            for j, chip in enumerate(self.chips):
                self.copy(k, 1 + j, (*chip, self.c), self.me).wait_recv()
                self.passed(k)[j].start()

    def finish(self):
        for k in range(self.n):
            self.copy(k, 0, self.sibling, self.me).wait_recv()
            for j, chip in enumerate(self.chips):
                self.copy(k, 4 + j, (*chip, 1 - self.c), self.me).wait_recv()
        for k in range(self.n):
            for cp in self.first(k) + self.passed(k):
                cp.wait_send()
            self.mine(k).wait()

    @staticmethod
    def semaphores(n):
        return [pltpu.SemaphoreType.DMA((7 * n,)), pltpu.SemaphoreType.DMA((7 * n,)), pltpu.SemaphoreType.DMA((n,))]


def RELAY_AT(steps):
    return steps * 3 // 4


def _all_gather(blocks, name):
    n = len(blocks)

    def body(*refs):
        gather = _Gather(refs[:n], refs[n:2 * n], *refs[2 * n:])
        gather.begin()
        gather.relay()
        gather.finish()

    return pl.pallas_call(
        body, name=name,
        out_shape=[jax.ShapeDtypeStruct((N_DEV, *b.shape), b.dtype) for b in blocks],
        in_specs=[pl.BlockSpec(memory_space=pltpu.VMEM)] * n,
        out_specs=[pl.BlockSpec(memory_space=pltpu.VMEM)] * n,
        scratch_shapes=_Gather.semaphores(n),
        compiler_params=pltpu.CompilerParams(vmem_limit_bytes=VMEM_LIMIT),
    )(*blocks)


class _Scatter:
    def __init__(self, part_refs, recv_refs, send_sems, recv_sems):
        self.part_refs, self.recv_refs, self.send_sems, self.recv_sems = part_refs, recv_refs, send_sems, recv_sems
        self.n = len(part_refs)

    def copies(self):
        x, y, c = _position()
        me = 4 * x + 2 * y + c
        out = []
        for k in range(self.n):
            for j in range(N_DEV - 1):
                d = j + 1
                out.append(pltpu.make_async_remote_copy(
                    src_ref=self.part_refs[k].at[me ^ d], dst_ref=self.recv_refs[k].at[j],
                    send_sem=self.send_sems.at[7 * k + j], recv_sem=self.recv_sems.at[7 * k + j],
                    device_id=(x ^ (d >> 2), y ^ ((d >> 1) & 1), c ^ (d & 1)), device_id_type=MESH))
        return out

    def begin(self):
        for cp in self.copies():
            cp.start()

    def finish(self):
        for cp in self.copies():
            cp.wait_recv()
        for cp in self.copies():
            cp.wait_send()

    @staticmethod
    def semaphores(n):
        return [pltpu.SemaphoreType.DMA((7 * n,)), pltpu.SemaphoreType.DMA((7 * n,))]


def _sum_slabs(own, recv, name):
    rows, cols = own.shape

    def body(own_ref, recv_ref, out_ref):
        acc = own_ref[...].astype(F32)
        for j in range(N_DEV - 1):
            acc = acc + recv_ref[j].astype(F32)
        out_ref[...] = acc

    return pl.pallas_call(body, name=name, out_shape=jax.ShapeDtypeStruct((rows, cols), F32))(own, recv)


N_CHIPS = 4


def _final_reduce(parts, vec_parts, rep, early_all, name):
    _, rows, cols = parts.shape
    vrows, prows = vec_parts.shape[1], rep.shape[0]
    peers = N_CHIPS - 1

    def body(parts_ref, vec_ref, rep_ref, early_ref, out_ref, vec_out, rep_out, early_out, recv_a, own_a, mid, recv_b, vec_recv, rep_all,
             a_send, a_recv, a_local, b_send, b_recv, v_send, v_recv, r_send, r_recv):
        x, y, c = _position()
        chip = 2 * x + y
        me = 2 * chip + c
        everyone = [(d, (x ^ (d >> 2), y ^ ((d >> 1) & 1), c ^ (d & 1))) for d in range(1, N_DEV)]

        stage1, loads = [], []
        for j in range(N_CHIPS):
            stage1.append(pltpu.make_async_remote_copy(
                src_ref=parts_ref.at[2 * j + 1 - c], dst_ref=recv_a.at[j], send_sem=a_send.at[j], recv_sem=a_recv.at[j],
                device_id=(x, y, 1 - c), device_id_type=MESH))
            loads.append(pltpu.make_async_copy(parts_ref.at[2 * j + c], own_a.at[j], a_local.at[j]))
            stage1[-1].start()
            loads[-1].start()
        small = []
        for d, to in everyone:
            small.append(pltpu.make_async_remote_copy(
                src_ref=vec_ref.at[me ^ d], dst_ref=vec_recv.at[d - 1], send_sem=v_send.at[d - 1], recv_sem=v_recv.at[d - 1],
                device_id=to, device_id_type=MESH))
            small.append(pltpu.make_async_remote_copy(
                src_ref=rep_ref, dst_ref=rep_all.at[me], send_sem=r_send.at[d - 1], recv_sem=r_recv.at[d - 1],
                device_id=to, device_id_type=MESH))
        for cp in small:
            cp.start()
        rep_all[me] = rep_ref[...]

        for j in range(N_CHIPS):
            loads[j].wait()
            stage1[j].wait_recv()
            mid[j] = (own_a[j].astype(F32) + recv_a[j].astype(F32)).astype(BF16)
        stage2 = []
        for d in range(1, N_CHIPS):
            stage2.append(pltpu.make_async_remote_copy(
                src_ref=mid.at[chip ^ d], dst_ref=recv_b.at[d - 1], send_sem=b_send.at[d - 1], recv_sem=b_recv.at[d - 1],
                device_id=(x ^ (d >> 1), y ^ (d & 1), c), device_id_type=MESH))
            stage2[-1].start()

        for cp in small:
            cp.wait_recv()
        vec_sum = vec_ref[me]
        for j in range(N_DEV - 1):
            vec_sum = vec_sum + vec_recv[j]
        vec_out[...] = vec_sum
        rep_sum = rep_all[0]
        for d in range(1, N_DEV):
            rep_sum = rep_sum + rep_all[d]
        rep_out[...] = rep_sum
        early_sum = early_ref[0]
        for d in range(1, N_DEV):
            early_sum = early_sum + early_ref[d]
        early_out[...] = early_sum

        acc = mid[chip].astype(F32)
        for d in range(peers):
            stage2[d].wait_recv()
            acc = acc + recv_b[d].astype(F32)
        out_ref[...] = acc
        for cp in stage1 + stage2 + small:
            cp.wait_send()

    vmem = pl.BlockSpec(memory_space=pltpu.VMEM)
    dma = pltpu.SemaphoreType.DMA
    return pl.pallas_call(
        body, name=name,
        out_shape=[jax.ShapeDtypeStruct((rows, cols), F32), jax.ShapeDtypeStruct((vrows, LANES), F32),
                   jax.ShapeDtypeStruct((prows, LANES), F32), jax.ShapeDtypeStruct(early_all.shape[1:], F32)],
        in_specs=[pl.BlockSpec(memory_space=pl.ANY), vmem, vmem, vmem],
        out_specs=[vmem, vmem, vmem, vmem],
        scratch_shapes=[pltpu.VMEM((N_CHIPS, rows, cols), BF16), pltpu.VMEM((N_CHIPS, rows, cols), BF16),
                        pltpu.VMEM((N_CHIPS, rows, cols), BF16), pltpu.VMEM((peers, rows, cols), BF16),
                        pltpu.VMEM((N_DEV - 1, vrows, LANES), F32), pltpu.VMEM((N_DEV, prows, LANES), F32),
                        dma((N_CHIPS,)), dma((N_CHIPS,)), dma((N_CHIPS,)), dma((peers,)), dma((peers,)),
                        dma((N_DEV - 1,)), dma((N_DEV - 1,)), dma((N_DEV - 1,)), dma((N_DEV - 1,))],
        compiler_params=pltpu.CompilerParams(vmem_limit_bytes=VMEM_LIMIT),
    )(parts, vec_parts, rep, early_all)


def _fwd_in0(x, g, w, shards, ts):
    seq = x.shape[0]
    n = len(shards)
    steps = seq // ts

    def body(*refs):
        x_ref, g_ref, w_ref = refs[:3]
        h_ref, q_ref, kv_ref, gug_ref = refs[3 + n:7 + n]
        gather = lambda: _Gather(refs[3:3 + n], refs[7 + n:7 + 2 * n], *refs[7 + 2 * n:])
        i = pl.program_id(0)

        @pl.when(i == 0)
        def _():
            gather().begin()

        h, _ = _rms_fwd(x_ref[...], g_ref[...])
        h = h.astype(BF16)
        h_ref[...] = h
        proj = _dot_nt(h, w_ref[...])
        q_ref[...] = proj[:, :ATTN_WIDTH].astype(BF16)
        kv_ref[...] = proj[:, ATTN_WIDTH:ATTN_WIDTH + 2 * KV_WIDTH].astype(BF16)
        gug_ref[...] = proj[:, ATTN_WIDTH + 2 * KV_WIDTH:].astype(BF16)

        @pl.when(i == RELAY_AT(steps))
        def _():
            gather().relay()

        @pl.when(i == steps - 1)
        def _():
            gather().finish()

    hbm = pl.BlockSpec(memory_space=pl.ANY)
    out = pl.pallas_call(
        body, name="fwd_in0", grid=(steps,),
        in_specs=[_tile(ts, D_MODEL), _full((1, D_MODEL)), _full((EVEN_IN, D_MODEL))] + [hbm] * n,
        out_specs=[_tile(ts, D_MODEL), _tile(ts, ATTN_WIDTH), _tile(ts, 2 * KV_WIDTH), _tile(ts, 3 * POOL_WIDTH)] + [hbm] * n,
        out_shape=[jax.ShapeDtypeStruct((seq, D_MODEL), BF16), jax.ShapeDtypeStruct((seq, ATTN_WIDTH), BF16),
                   jax.ShapeDtypeStruct((seq, 2 * KV_WIDTH), BF16), jax.ShapeDtypeStruct((seq, 3 * POOL_WIDTH), BF16)]
        + [jax.ShapeDtypeStruct((N_DEV, *b.shape), b.dtype) for b in shards],
        scratch_shapes=_Gather.semaphores(n),
        compiler_params=_params(),
    )(x, g, w, *shards)
    return out[:4], out[4:]


GROUP_ROWS = GROUP * BLOCK


def _attn_mask(rows, first_block):
    row = lax.broadcasted_iota(jnp.int32, (rows, 2 * BLOCK), 0) & (BLOCK - 1)
    col = lax.broadcasted_iota(jnp.int32, (rows, 2 * BLOCK), 1)
    dist = row + BLOCK - col
    valid = (dist >= 0) & (dist < BLOCK) & ((col >= BLOCK) | jnp.logical_not(first_block))
    return valid, dist.astype(F32)


def _group_rows(block, kvh):
    return jnp.concatenate([block[:, HEAD_DIM * h:HEAD_DIM * (h + 1)] for h in range(GROUP * kvh, GROUP * (kvh + 1))], axis=0)


def _group_columns(sink_ref, kvh):
    head = lax.broadcasted_iota(jnp.int32, (GROUP_ROWS, 1), 0) // BLOCK
    sink = jnp.zeros((GROUP_ROWS, 1), F32)
    slope = jnp.zeros((GROUP_ROWS, 1), F32)
    for i in range(GROUP):
        sink = jnp.where(head == i, sink_ref[GROUP * kvh + i], sink)
        slope = jnp.where(head == i, SLOPES[GROUP * kvh + i], slope)
    return sink, slope


def _attn_probs(qh, kh, sink, slope, valid, distf):
    s = _dot_nt(qh, kh) * (HEAD_DIM ** -0.5)
    s = jnp.where(valid, s - slope * distf, NEG)
    mx = jnp.maximum(jnp.max(s, axis=-1, keepdims=True), sink)
    e = jnp.exp(s - mx)
    es = jnp.exp(sink - mx)
    den = jnp.sum(e, axis=-1, keepdims=True) + es
    return e / den, es / den


def _attn_fwd(q, kv, sinks, shards):
    seq = q.shape[0]
    nb = seq // BLOCK
    ns = len(shards)

    def body(*refs):
        sink_ref, q_ref, kvc_ref, kvp_ref = refs[:4]
        o_ref = refs[4 + ns]
        gather = lambda: _Gather(refs[4:4 + ns], refs[5 + ns:5 + 2 * ns], *refs[5 + 2 * ns:])
        n = pl.program_id(0)

        @pl.when(n == 0)
        def _():
            gather().begin()

        valid, distf = _attn_mask(BLOCK, n == 0)
        qb = q_ref[...]
        kk = jnp.concatenate([kvp_ref[...], kvc_ref[...]], axis=0)
        for h in range(Q_HEADS):
            kvh = h // GROUP
            qh = qb[:, HEAD_DIM * h:HEAD_DIM * (h + 1)]
            kh = kk[:, HEAD_DIM * kvh:HEAD_DIM * (kvh + 1)]
            vh = kk[:, KV_WIDTH + HEAD_DIM * kvh:KV_WIDTH + HEAD_DIM * (kvh + 1)]
            p, _ = _attn_probs(qh, kh, sink_ref[h], SLOPES[h], valid, distf)
            o_ref[:, HEAD_DIM * h:HEAD_DIM * (h + 1)] = _dot(p.astype(BF16), vh)

        @pl.when(n == RELAY_AT(nb))
        def _():
            gather().relay()

        @pl.when(n == nb - 1)
        def _():
            gather().finish()

    hbm = pl.BlockSpec(memory_space=pl.ANY)
    out = pl.pallas_call(
        body, name="attn_fwd", grid=(nb,),
        in_specs=[pl.BlockSpec(memory_space=pltpu.SMEM),
                  pl.BlockSpec((BLOCK, ATTN_WIDTH), lambda n: (n, 0)),
                  pl.BlockSpec((BLOCK, 2 * KV_WIDTH), lambda n: (n, 0)),
                  pl.BlockSpec((BLOCK, 2 * KV_WIDTH), lambda n: (jnp.maximum(n - 1, 0), 0))] + [hbm] * ns,
        out_specs=[pl.BlockSpec((BLOCK, ATTN_WIDTH), lambda n: (n, 0))] + [hbm] * ns,
        out_shape=[jax.ShapeDtypeStruct((seq, ATTN_WIDTH), F32)]
        + [jax.ShapeDtypeStruct((N_DEV, *b.shape), b.dtype) for b in shards],
        scratch_shapes=_Gather.semaphores(ns),
        compiler_params=_params(),
    )(sinks, q, kv, kv, *shards)
    return out[0], out[1:]


def _pool_counts(first_row, rows, window):
    t = first_row + lax.broadcasted_iota(jnp.int32, (rows, 1), 0)
    return jnp.minimum(t + 1, window).astype(F32)


def _fwd_out0(gug, o, pool_w, pool_scale, w_out, g_post, x, ts):
    seq = x.shape[0]
    hb = ts // POOL_HALO

    def body(gug_ref, halo_ref, o_ref, pw_ref, ps_ref, w_ref, g_ref, x_ref, mix_ref, pooled_ref, y_ref, x1_ref, ubuf):
        i = pl.program_id(0)
        ga = gug_ref[:, :ATTN_WIDTH].astype(F32)
        u = gug_ref[:, ATTN_WIDTH:ATTN_WIDTH + POOL_WIDTH].astype(F32)
        gb = gug_ref[:, ATTN_WIDTH + POOL_WIDTH:].astype(F32)
        mix_ref[:, :ATTN_WIDTH] = (o_ref[...] * (ga * _sigmoid(ga))).astype(BF16)
        ubuf[:POOL_HALO, :] = jnp.where(i > 0, halo_ref[...].astype(F32), 0.0)
        ubuf[POOL_HALO:, :] = u
        silu_gb = gb * _sigmoid(gb)
        for g, window in enumerate(POOL_WINDOWS):
            lanes = slice(POOL_GC * g, POOL_GC * (g + 1))
            acc = ubuf[pl.ds(POOL_HALO, ts), lanes]
            for k in range(1, window):
                acc = acc + ubuf[pl.ds(POOL_HALO - k, ts), lanes]
            pooled = (acc / _pool_counts(i * ts, ts, window) - u[:, lanes]).astype(BF16)
            pooled_ref[:, lanes] = pooled
            ypool = _dot(pooled, pw_ref[g].astype(BF16)) * ps_ref[:, lanes]
            mix_ref[:, ATTN_WIDTH + POOL_GC * g:ATTN_WIDTH + POOL_GC * (g + 1)] = (ypool * silu_gb[:, lanes]).astype(BF16)
        y = _dot(mix_ref[...], w_ref[...])
        y_ref[...] = y
        yn, _ = _rms_fwd(y, g_ref[...])
        x1_ref[...] = x_ref[...] + yn

    return pl.pallas_call(
        body, name="fwd_out0", grid=(seq // ts,),
        in_specs=[_tile(ts, 3 * POOL_WIDTH),
                  pl.BlockSpec((POOL_HALO, POOL_WIDTH), lambda i: (jnp.maximum(i * hb - 1, 0), 1)),
                  _tile(ts, ATTN_WIDTH), _full((4, POOL_GC, POOL_GC)), _full((1, POOL_WIDTH)),
                  _full((D_MODEL, D_MODEL)), _full((1, D_MODEL)), _tile(ts, D_MODEL)],
        out_specs=[_tile(ts, D_MODEL), _tile(ts, POOL_WIDTH), _tile(ts, D_MODEL), _tile(ts, D_MODEL)],
        out_shape=[jax.ShapeDtypeStruct((seq, D_MODEL), BF16), jax.ShapeDtypeStruct((seq, POOL_WIDTH), BF16),
                   jax.ShapeDtypeStruct((seq, D_MODEL), F32), jax.ShapeDtypeStruct((seq, D_MODEL), F32)],
        scratch_shapes=[pltpu.VMEM((ts + POOL_HALO, POOL_WIDTH), F32)],
        compiler_params=_params(),
    )(gug, gug, o, pool_w, pool_scale, w_out, g_post, x)


def _fwd_in1(x1, g, w, ts):
    seq = x1.shape[0]

    def body(x_ref, g_ref, w_ref, h_ref, proj_ref, glu_ref):
        h, _ = _rms_fwd(x_ref[...], g_ref[...])
        h = h.astype(BF16)
        h_ref[...] = h
        proj = _dot_nt(h, w_ref[...])
        proj_ref[...] = proj.astype(BF16)
        glu_ref[...] = proj[:, :D_MODEL] * _sigmoid(proj[:, D_MODEL:2 * D_MODEL])

    return pl.pallas_call(
        body, name="fwd_in1", grid=(seq // ts,),
        in_specs=[_tile(ts, D_MODEL), _full((1, D_MODEL)), _full((ODD_IN, D_MODEL))],
        out_specs=[_tile(ts, D_MODEL), _tile(ts, ODD_IN), _tile(ts, D_MODEL)],
        out_shape=[jax.ShapeDtypeStruct((seq, D_MODEL), BF16), jax.ShapeDtypeStruct((seq, ODD_IN), BF16),
                   jax.ShapeDtypeStruct((seq, D_MODEL), F32)],
        compiler_params=_params(),
    )(x1, g, w)


ACC_LOSS, ACC_POST, ACC_LN_G, ACC_LN_B, ACC_DW_B = range(5)
CONV_FIRST = CONV_HALO - CONV_K + 1


def _fwd_tap(offset):
    return offset - CONV_FIRST if CONV_FIRST <= offset <= CONV_HALO else None


def _bwd_tap(offset):
    return CONV_K - 1 - offset if offset < CONV_K else None


def _conv_taps(w_ref, buf_ref, ts, lanes, tap_of_offset):
    out = None
    for b in range(SUBLANES):
        rows = ts if b == 0 else ts + SUBLANES
        part = None
        for a in range(CONV_HALO // SUBLANES + 1):
            k = tap_of_offset(SUBLANES * a + b)
            if k is None:
                continue
            term = w_ref[k:k + 1, lanes] * buf_ref[pl.ds(SUBLANES * a, rows), lanes]
            part = term if part is None else part + term
        if part is None:
            continue
        if b:
            part = part[b:b + ts, :]
        out = part if out is None else out + part
    return out


def _fwd_out1(glu, proj, dw_w, dw_b, ln_g, ln_b, w_out, g_post, x1, target, ts):
    seq = x1.shape[0]
    hb = ts // CONV_HALO

    def body(glu_ref, halo_ref, gate_ref, dww_ref, dwb_ref, lng_ref, lnb_ref, w_ref, g_ref, x1_ref, t_ref,
             ymix_ref, dy_ref, dx2_ref, dcf_ref, dgate_ref, acc_ref, gbuf):
        i = pl.program_id(0)

        @pl.when(i == 0)
        def _():
            acc_ref[...] = jnp.zeros_like(acc_ref)

        gbuf[:CONV_HALO, :] = jnp.where(i > 0, halo_ref[...], 0.0)
        gbuf[CONV_HALO:, :] = glu_ref[...]
        for lb in range(D_MODEL // LANES):
            lanes = slice(LANES * lb, LANES * (lb + 1))
            dcf_ref[:, lanes] = _conv_taps(dww_ref, gbuf, ts, lanes, _fwd_tap)
        cf = dcf_ref[...] + dwb_ref[...]
        mu = jnp.mean(cf, axis=-1, keepdims=True)
        cen = cf - mu
        rs = lax.rsqrt(jnp.mean(cen * cen, axis=-1, keepdims=True) + EPS)
        xhat = cen * rs
        cn = xhat * lng_ref[...] + lnb_ref[...]
        gate = gate_ref[...].astype(F32)
        sg = _sigmoid(gate)
        sc = _sigmoid(cn)
        silu_gate = gate * sg
        silu_cn = cn * sc
        ymix = (silu_cn * silu_gate).astype(BF16)
        ymix_ref[...] = ymix
        y = _dot(ymix, w_ref[...])
        yn, r = _rms_fwd(y, g_ref[...])
        err = (x1_ref[...] + yn) - t_ref[...]
        acc_ref[ACC_LOSS] += _rows8(err * err)
        dx2 = err * (1.0 / D_MODEL)
        dx2_ref[...] = dx2
        dy, dpost = _rms_bwd(y, r, g_ref[...], dx2)
        acc_ref[ACC_POST] += _rows8(dpost)
        dy = dy.astype(BF16)
        dy_ref[...] = dy
        dymix = _dot_nt(dy, w_ref[...])
        dgate_ref[...] = (dymix * silu_cn * (sg * (1.0 + gate * (1.0 - sg)))).astype(BF16)
        dcn = dymix * silu_gate * (sc * (1.0 + cn * (1.0 - sc)))
        acc_ref[ACC_LN_G] += _rows8(dcn * xhat)
        acc_ref[ACC_LN_B] += _rows8(dcn)
        dxhat = dcn * lng_ref[...]
        dcf = rs * (dxhat - jnp.mean(dxhat, axis=-1, keepdims=True)
                    - xhat * jnp.mean(dxhat * xhat, axis=-1, keepdims=True))
        acc_ref[ACC_DW_B] += _rows8(dcf)
        dcf_ref[...] = dcf

    return pl.pallas_call(
        body, name="fwd_out1", grid=(seq // ts,),
        in_specs=[_tile(ts, D_MODEL),
                  pl.BlockSpec((CONV_HALO, D_MODEL), lambda i: (jnp.maximum(i * hb - 1, 0), 0)),
                  _tile(ts, D_MODEL, 2), _full((CONV_HALO, D_MODEL)), _full((1, D_MODEL)), _full((1, D_MODEL)),
                  _full((1, D_MODEL)), _full((D_MODEL, D_MODEL)), _full((1, D_MODEL)), _tile(ts, D_MODEL),
                  _tile(ts, D_MODEL)],
        out_specs=[_tile(ts, D_MODEL), _tile(ts, D_MODEL), _tile(ts, D_MODEL), _tile(ts, D_MODEL), _tile(ts, D_MODEL),
                   _full((5, SUBLANES, D_MODEL))],
        out_shape=[jax.ShapeDtypeStruct((seq, D_MODEL), BF16), jax.ShapeDtypeStruct((seq, D_MODEL), BF16),
                   jax.ShapeDtypeStruct((seq, D_MODEL), F32), jax.ShapeDtypeStruct((seq, D_MODEL), F32),
                   jax.ShapeDtypeStruct((seq, D_MODEL), BF16), jax.ShapeDtypeStruct((5, SUBLANES, D_MODEL), F32)],
        scratch_shapes=[pltpu.VMEM((ts + CONV_HALO, D_MODEL), F32)],
        compiler_params=_params(),
    )(glu, glu, proj, dw_w, dw_b, ln_g, ln_b, w_out, g_post, x1, target)


def _bwd_in1(dcf, glu, proj, dgate, dw_w, w_in, x1, g_pre, dx2, ts):
    seq = x1.shape[0]
    hb = ts // CONV_HALO
    last = seq // CONV_HALO - 1
    nt = seq // ts

    def body(dcf_ref, dnext_ref, glu_ref, gprev_ref, ab_ref, dgate_ref, dww_ref, w_ref, x_ref, g_ref, dx2_ref,
             dproj_ref, dx1_ref, ddw_ref, dpre_ref, dbuf, gbuf, zbuf, sbuf):
        i = pl.program_id(0)

        @pl.when(i == 0)
        def _():
            ddw_ref[...] = jnp.zeros_like(ddw_ref)
            dpre_ref[...] = jnp.zeros_like(dpre_ref)

        dcf = dcf_ref[...]
        dbuf[:ts, :] = dcf
        dbuf[ts:, :] = jnp.where(i < nt - 1, dnext_ref[...], 0.0)
        gbuf[:CONV_HALO, :] = jnp.where(i > 0, gprev_ref[...], 0.0)
        gbuf[CONV_HALO:, :] = glu_ref[...]
        zbuf[:SUBLANES, :] = jnp.zeros((SUBLANES, D_MODEL), F32)
        zbuf[pl.ds(SUBLANES, ts), :] = dcf
        zbuf[pl.ds(SUBLANES + ts, SUBLANES), :] = jnp.zeros((SUBLANES, D_MODEL), F32)
        for lb in range(D_MODEL // LANES):
            lanes = slice(LANES * lb, LANES * (lb + 1))
            gate_lanes = slice(D_MODEL + LANES * lb, D_MODEL + LANES * (lb + 1))
            dglu = _conv_taps(dww_ref, dbuf, ts, lanes, _bwd_tap)
            a = ab_ref[:, lanes].astype(F32)
            sb = _sigmoid(ab_ref[:, gate_lanes].astype(F32))
            dproj_ref[:, lanes] = (dglu * sb).astype(BF16)
            dproj_ref[:, gate_lanes] = (dglu * a * sb * (1.0 - sb)).astype(BF16)
            for b in range(SUBLANES):
                rows = ts if b == 0 else ts + SUBLANES
                sbuf[b, pl.ds(0, rows), :] = zbuf[pl.ds(SUBLANES - b, rows), lanes]
                for a8 in range(CONV_HALO // SUBLANES + 1):
                    k = _fwd_tap(SUBLANES * a8 + b)
                    if k is not None:
                        ddw_ref[k, :, lanes] += _rows8(sbuf[b, pl.ds(0, rows), :] * gbuf[pl.ds(SUBLANES * a8, rows), lanes])
        dproj_ref[:, 2 * D_MODEL:] = dgate_ref[...]
        dh = _dot(dproj_ref[...], w_ref[...])
        x = x_ref[...]
        r = lax.rsqrt(jnp.mean(x * x, axis=-1, keepdims=True) + EPS)
        dx, dpre = _rms_bwd(x, r, g_ref[...], dh)
        dx1_ref[...] = dx2_ref[...] + dx
        dpre_ref[...] += _rows8(dpre)

    return pl.pallas_call(
        body, name="bwd_in1", grid=(nt,),
        in_specs=[_tile(ts, D_MODEL),
                  pl.BlockSpec((CONV_HALO, D_MODEL), lambda i: (jnp.minimum((i + 1) * hb, last), 0)),
                  _tile(ts, D_MODEL),
                  pl.BlockSpec((CONV_HALO, D_MODEL), lambda i: (jnp.maximum(i * hb - 1, 0), 0)),
                  _tile(ts, 2 * D_MODEL), _tile(ts, D_MODEL), _full((CONV_HALO, D_MODEL)),
                  _full((ODD_IN, D_MODEL)), _tile(ts, D_MODEL), _full((1, D_MODEL)), _tile(ts, D_MODEL)],
        out_specs=[_tile(ts, ODD_IN), _tile(ts, D_MODEL), _full((CONV_HALO, SUBLANES, D_MODEL)), _full((SUBLANES, D_MODEL))],
        out_shape=[jax.ShapeDtypeStruct((seq, ODD_IN), BF16), jax.ShapeDtypeStruct((seq, D_MODEL), F32),
                   jax.ShapeDtypeStruct((CONV_HALO, SUBLANES, D_MODEL), F32), jax.ShapeDtypeStruct((SUBLANES, D_MODEL), F32)],
        scratch_shapes=[pltpu.VMEM((ts + CONV_HALO, D_MODEL), F32), pltpu.VMEM((ts + CONV_HALO, D_MODEL), F32),
                        pltpu.VMEM((ts + 2 * SUBLANES, D_MODEL), F32), pltpu.VMEM((SUBLANES, ts + SUBLANES, LANES), F32)],
        compiler_params=_params(),
    )(dcf, dcf, glu, glu, proj, dgate, dw_w, w_in, x1, g_pre, dx2)


def _bwd_out0(dx1, y0, g_post, w_out, gug, o, pooled, pool_w, pool_scale, mix, ts):
    seq = dx1.shape[0]

    def body(dx1_ref, y_ref, g_ref, w_ref, gug_ref, o_ref, pooled_ref, pw_ref, ps_ref, mix_ref,
             do_ref, dgg_ref, dpooled_ref, dpost_ref, dscale_ref, dpw_ref, gw_ref, gacc):
        i = pl.program_id(0)

        @pl.when(i == 0)
        def _():
            dpost_ref[...] = jnp.zeros_like(dpost_ref)
            dscale_ref[...] = jnp.zeros_like(dscale_ref)
            dpw_ref[...] = jnp.zeros_like(dpw_ref)

        y = y_ref[...]
        r = lax.rsqrt(jnp.mean(y * y, axis=-1, keepdims=True) + EPS)
        dy, dpost = _rms_bwd(y, r, g_ref[...], dx1_ref[...])
        dpost_ref[...] += _rows8(dpost)
        dy = dy.astype(BF16)
        _accumulate_tn(gacc, mix_ref[...], dy, i == 0)
        dmix = _dot_nt(dy, w_ref[...])
        dya = dmix[:, :ATTN_WIDTH]
        dyb = dmix[:, ATTN_WIDTH:]
        ga = gug_ref[:, :ATTN_WIDTH].astype(F32)
        gb = gug_ref[:, ATTN_WIDTH + POOL_WIDTH:].astype(F32)
        sga = _sigmoid(ga)
        sgb = _sigmoid(gb)
        do_ref[...] = (dya * (ga * sga)).astype(BF16)
        dgg_ref[:, :ATTN_WIDTH] = (dya * o_ref[...] * (sga * (1.0 + ga * (1.0 - sga)))).astype(BF16)
        dypool = dyb * (gb * sgb)
        dsilu_gb = sgb * (1.0 + gb * (1.0 - sgb))
        for g in range(len(POOL_WINDOWS)):
            lanes = slice(POOL_GC * g, POOL_GC * (g + 1))
            pooled = pooled_ref[:, lanes]
            wg = pw_ref[g].astype(BF16)
            pw = _dot(pooled, wg)
            scale = ps_ref[:, lanes]
            dgg_ref[:, ATTN_WIDTH + POOL_GC * g:ATTN_WIDTH + POOL_GC * (g + 1)] = (
                dyb[:, lanes] * (pw * scale) * dsilu_gb[:, lanes]).astype(BF16)
            dscale_ref[:, lanes] += _rows8(dypool[:, lanes] * pw)
            dpw = (dypool[:, lanes] * scale).astype(BF16)
            dpooled_ref[:, lanes] = _dot_nt(dpw, wg)
            dpw_ref[g] += _dot_tn(pooled, dpw)

        @pl.when(i == seq // ts - 1)
        def _():
            gw_ref[...] = gacc[...].astype(BF16)

    return pl.pallas_call(
        body, name="bwd_out0", grid=(seq // ts,),
        in_specs=[_tile(ts, D_MODEL), _tile(ts, D_MODEL), _full((1, D_MODEL)), _resident((D_MODEL, D_MODEL)),
                  _tile(ts, 3 * POOL_WIDTH), _tile(ts, ATTN_WIDTH), _tile(ts, POOL_WIDTH),
                  _full((4, POOL_GC, POOL_GC)), _full((1, POOL_WIDTH)), _tile(ts, D_MODEL)],
        out_specs=[_tile(ts, ATTN_WIDTH), _tile(ts, ATTN_WIDTH + POOL_WIDTH), _tile(ts, POOL_WIDTH),
                   _full((SUBLANES, D_MODEL)), _full((SUBLANES, POOL_WIDTH)), _full((4, POOL_GC, POOL_GC)),
                   _full((D_MODEL, D_MODEL))],
        out_shape=[jax.ShapeDtypeStruct((seq, ATTN_WIDTH), BF16),
                   jax.ShapeDtypeStruct((seq, ATTN_WIDTH + POOL_WIDTH), BF16), jax.ShapeDtypeStruct((seq, POOL_WIDTH), F32),
                   jax.ShapeDtypeStruct((SUBLANES, D_MODEL), F32), jax.ShapeDtypeStruct((SUBLANES, POOL_WIDTH), F32),
                   jax.ShapeDtypeStruct((4, POOL_GC, POOL_GC), F32), jax.ShapeDtypeStruct((D_MODEL, D_MODEL), BF16)],
        scratch_shapes=[pltpu.VMEM((D_MODEL, D_MODEL), F32)],
        compiler_params=_params(),
    )(dx1, y0, g_post, w_out, gug, o, pooled, pool_w, pool_scale, mix)


def _attn_bwd(q, kv, do, sinks, parts, early):
    seq = q.shape[0]
    nb = seq // BLOCK

    def qblock(j):
        return jnp.minimum(j, nb - 1)

    n = len(parts)

    def body(*refs):
        sink_ref, q_ref, kvc_ref, kvp_ref, do_ref = refs[:5]
        early_ref = refs[5 + n]
        dq_ref, dkv_ref, dsink_ref = refs[6 + n:9 + n]
        all_ref = refs[9 + 2 * n]
        carry, dkv_acc = refs[10 + 2 * n:12 + 2 * n]
        scatter = _Scatter(refs[5:5 + n], refs[9 + n:9 + 2 * n], *refs[12 + 2 * n:14 + 2 * n])
        gather = lambda: _Gather([early_ref], [all_ref], *refs[14 + 2 * n:])
        j = pl.program_id(0)

        @pl.when(j == 0)
        def _():
            gather().begin()
            scatter.begin()
            dsink_ref[...] = jnp.zeros_like(dsink_ref)
            carry[...] = jnp.zeros_like(carry)

        @pl.when(j < nb)
        def _():
            valid, distf = _attn_mask(GROUP_ROWS, j == 0)
            qb = q_ref[...]
            dob = do_ref[...]
            kk = jnp.concatenate([kvp_ref[...], kvc_ref[...]], axis=0)
            lane = lax.broadcasted_iota(jnp.int32, (BLOCK, LANES), 1)
            dsink = jnp.zeros((BLOCK, LANES), F32)
            for kvh in range(Q_HEADS // GROUP):
                kh = kk[:, HEAD_DIM * kvh:HEAD_DIM * (kvh + 1)]
                vh = kk[:, KV_WIDTH + HEAD_DIM * kvh:KV_WIDTH + HEAD_DIM * (kvh + 1)]
                qg = _group_rows(qb, kvh)
                dog = _group_rows(dob, kvh)
                sink, slope = _group_columns(sink_ref, kvh)
                p, psink = _attn_probs(qg, kh, sink, slope, valid, distf)
                dp = _dot_nt(dog, vh)
                delta = jnp.sum(p * dp, axis=-1, keepdims=True)
                ds = (p * (dp - delta) * (HEAD_DIM ** -0.5)).astype(BF16)
                dsink_rows = -psink * delta
                dqg = _dot(ds, kh).astype(BF16)
                for i in range(GROUP):
                    h = GROUP * kvh + i
                    dsink = dsink + jnp.where(lane == h, dsink_rows[BLOCK * i:BLOCK * (i + 1), :], 0.0)
                    dq_ref[:, HEAD_DIM * h:HEAD_DIM * (h + 1)] = dqg[BLOCK * i:BLOCK * (i + 1), :]
                dkv_acc[:, HEAD_DIM * kvh:HEAD_DIM * (kvh + 1)] = _dot_tn(ds, qg)
                dkv_acc[:, KV_WIDTH + HEAD_DIM * kvh:KV_WIDTH + HEAD_DIM * (kvh + 1)] = _dot_tn(p.astype(BF16), dog)
            dsink_ref[...] += dsink

            @pl.when(j > 0)
            def _():
                dkv_ref[...] = (carry[...] + dkv_acc[:BLOCK, :]).astype(BF16)

            carry[...] = dkv_acc[BLOCK:, :]

        @pl.when(j == nb)
        def _():
            dkv_ref[...] = carry[...].astype(BF16)
            gather().relay()
            scatter.finish()
            gather().finish()

    hbm = pl.BlockSpec(memory_space=pl.ANY)
    out = pl.pallas_call(
        body, name="attn_bwd", grid=(nb + 1,),
        in_specs=[pl.BlockSpec(memory_space=pltpu.SMEM),
                  pl.BlockSpec((BLOCK, ATTN_WIDTH), lambda j: (qblock(j), 0)),
                  pl.BlockSpec((BLOCK, 2 * KV_WIDTH), lambda j: (qblock(j), 0)),
                  pl.BlockSpec((BLOCK, 2 * KV_WIDTH), lambda j: (jnp.maximum(qblock(j) - 1, 0), 0)),
                  pl.BlockSpec((BLOCK, ATTN_WIDTH), lambda j: (qblock(j), 0))] + [hbm] * (n + 1),
        out_specs=[pl.BlockSpec((BLOCK, ATTN_WIDTH), lambda j: (qblock(j), 0)),
                   pl.BlockSpec((BLOCK, 2 * KV_WIDTH), lambda j: (jnp.maximum(j - 1, 0), 0)),
                   _full((BLOCK, LANES))] + [hbm] * (n + 1),
        out_shape=[jax.ShapeDtypeStruct((seq, ATTN_WIDTH), BF16), jax.ShapeDtypeStruct((seq, 2 * KV_WIDTH), BF16),
                   jax.ShapeDtypeStruct((BLOCK, LANES), F32)]
        + [jax.ShapeDtypeStruct((N_DEV - 1, *p.shape[1:]), p.dtype) for p in parts]
        + [jax.ShapeDtypeStruct((N_DEV, *early.shape), early.dtype)],
        scratch_shapes=[pltpu.VMEM((BLOCK, 2 * KV_WIDTH), F32), pltpu.VMEM((2 * BLOCK, 2 * KV_WIDTH), F32)]
        + _Scatter.semaphores(n) + _Gather.semaphores(1),
        compiler_params=_params(),
    )(sinks, q, kv, kv, do, *parts, early)
    return out[:3], out[3:3 + n], out[3 + n]


def _bwd_in0(dpooled, dq, dkv, dgg, w_in, x, g_pre, dx1, ts):
    seq = x.shape[0]
    hb = ts // POOL_HALO
    last = seq // POOL_HALO - 1
    nt = seq // ts

    def body(dp_ref, dnext_ref, dq_ref, dkv_ref, dgg_ref, w_ref, x_ref, g_ref, dx1_ref,
             dproj_ref, gx_ref, dpre_ref, dbuf):
        i = pl.program_id(0)

        @pl.when(i == 0)
        def _():
            dpre_ref[...] = jnp.zeros_like(dpre_ref)

        dpool = dp_ref[...]
        dnext = jnp.where(i < nt - 1, dnext_ref[...], 0.0)
        u0 = ATTN_WIDTH + 2 * KV_WIDTH + ATTN_WIDTH
        for g, window in enumerate(POOL_WINDOWS):
            lanes = slice(POOL_GC * g, POOL_GC * (g + 1))
            dbuf[:ts, lanes] = dpool[:, lanes] / _pool_counts(i * ts, ts, window)
            dbuf[ts:, lanes] = dnext[:, lanes] / _pool_counts((i + 1) * ts, POOL_HALO, window)
        for g, window in enumerate(POOL_WINDOWS):
            lanes = slice(POOL_GC * g, POOL_GC * (g + 1))
            acc = dbuf[pl.ds(0, ts), lanes]
            for k in range(1, window):
                acc = acc + dbuf[pl.ds(k, ts), lanes]
            dproj_ref[:, u0 + POOL_GC * g:u0 + POOL_GC * (g + 1)] = (acc - dpool[:, lanes]).astype(BF16)
        dproj_ref[:, :ATTN_WIDTH] = dq_ref[...]
        dproj_ref[:, ATTN_WIDTH:ATTN_WIDTH + 2 * KV_WIDTH] = dkv_ref[...]
        dproj_ref[:, ATTN_WIDTH + 2 * KV_WIDTH:u0] = dgg_ref[:, :ATTN_WIDTH]
        dproj_ref[:, u0 + POOL_WIDTH:] = dgg_ref[:, ATTN_WIDTH:]
        dh = _dot(dproj_ref[...], w_ref[...])
        x = x_ref[...]
        r = lax.rsqrt(jnp.mean(x * x, axis=-1, keepdims=True) + EPS)
        dx, dpre = _rms_bwd(x, r, g_ref[...], dh)
        gx_ref[...] = dx1_ref[...] + dx
        dpre_ref[...] += _rows8(dpre)

    return pl.pallas_call(
        body, name="bwd_in0", grid=(nt,),
        in_specs=[_tile(ts, POOL_WIDTH),
                  pl.BlockSpec((POOL_HALO, POOL_WIDTH), lambda i: (jnp.minimum((i + 1) * hb, last), 0)),
                  _tile(ts, ATTN_WIDTH), _tile(ts, 2 * KV_WIDTH), _tile(ts, ATTN_WIDTH + POOL_WIDTH),
                  _full((EVEN_IN, D_MODEL)), _tile(ts, D_MODEL), _full((1, D_MODEL)), _tile(ts, D_MODEL)],
        out_specs=[_tile(ts, EVEN_IN), _tile(ts, D_MODEL), _full((SUBLANES, D_MODEL))],
        out_shape=[jax.ShapeDtypeStruct((seq, EVEN_IN), BF16), jax.ShapeDtypeStruct((seq, D_MODEL), F32),
                   jax.ShapeDtypeStruct((SUBLANES, D_MODEL), F32)],
        scratch_shapes=[pltpu.VMEM((ts + POOL_HALO, POOL_WIDTH), F32)],
        compiler_params=_params(),
    )(dpooled, dpooled, dq, dkv, dgg, w_in, x, g_pre, dx1)


def _matmul_tn(a, b, name, ts, tm):
    seq, m = a.shape
    n = b.shape[1]
    steps = seq // ts

    def body(a_ref, b_ref, o_ref, acc):
        s = pl.program_id(1)

        @pl.when(s == 0)
        def _():
            acc[...] = jnp.zeros_like(acc)

        acc[...] += _dot_tn(a_ref[...], b_ref[...])

        @pl.when(s == steps - 1)
        def _():
            o_ref[...] = acc[...].astype(BF16)

    return pl.pallas_call(
        body, name=name, grid=(m // tm, steps),
        in_specs=[pl.BlockSpec((ts, tm), lambda j, s: (s, j)), pl.BlockSpec((ts, n), lambda j, s: (s, 0))],
        out_specs=pl.BlockSpec((tm, n), lambda j, s: (j, 0)),
        out_shape=jax.ShapeDtypeStruct((m, n), BF16),
        scratch_shapes=[pltpu.VMEM((tm, n), F32)],
        compiler_params=pltpu.CompilerParams(dimension_semantics=("arbitrary", "arbitrary"), vmem_limit_bytes=VMEM_LIMIT),
    )(a, b)


def _adamw_math(w, g, m, v):
    m = ADAM_B1 * m + (1.0 - ADAM_B1) * g
    v = ADAM_B2 * v + (1.0 - ADAM_B2) * (g * g)
    m_hat = m / (1.0 - ADAM_B1 ** ADAM_STEP)
    v_hat = v / (1.0 - ADAM_B2 ** ADAM_STEP)
    delta = -ADAM_LR * (m_hat / (jnp.sqrt(v_hat) + ADAM_EPS) + ADAM_WD * w)
    return delta, m, v


def _adamw(ws, gs, ms, vs, name):
    n = len(ws)

    def body(*refs):
        ins, outs = refs[:4 * n], refs[4 * n:]
        for k in range(n):
            delta, m, v = _adamw_math(ins[k][...], ins[n + k][...], ins[2 * n + k][...], ins[3 * n + k][...])
            outs[k][...] = delta
            outs[n + k][...] = m
            outs[2 * n + k][...] = v

    shapes = [jax.ShapeDtypeStruct(w.shape, F32) for w in ws]
    out = pl.pallas_call(body, name=name, out_shape=shapes * 3,
                         compiler_params=pltpu.CompilerParams(vmem_limit_bytes=VMEM_LIMIT))(*ws, *gs, *ms, *vs)
    return out[:n], out[n:2 * n], out[2 * n:]


TS_MATMUL, TS_CONV_FWD, TS_CONV_BWD, TS_GRAD = 512, 512, 256, 1024


def kernel(x, pre_norm, post_norm, a_w_in, a_sinks, b_pool_w, b_pool_scale, ab_w_out, c_w_in, c_dw_w, c_dw_b, c_ln_g, c_ln_b, c_w_out, loss_target, m_pre_norm, m_post_norm, m_a_w_in, m_a_sinks, m_b_pool_w, m_b_pool_scale, m_ab_w_out, m_c_w_in, m_c_dw_w, m_c_dw_b, m_c_ln_g, m_c_ln_b, m_c_w_out, v_pre_norm, v_post_norm, v_a_w_in, v_a_sinks, v_b_pool_w, v_b_pool_scale, v_ab_w_out, v_c_w_in, v_c_dw_w, v_c_dw_b, v_c_ln_g, v_c_ln_b, v_c_w_out):
    seq = x.shape[1]
    ts_big, ts_grad = min(TS_MATMUL, seq), min(TS_GRAD, seq)
    x2d = x[0]
    target = loss_target[0]
    ch = c_dw_b.shape[1]

    whole = lambda g: g.reshape(-1, D_MODEL)
    vec_rows = 40
    vecs = jnp.concatenate([c_dw_w[0, :, 0, :], c_dw_b, c_ln_g, c_ln_b, jnp.zeros((vec_rows - CONV_K - 3, ch), F32)], axis=0)
    w_in0t, vg = _all_gather([a_w_in[0].T.astype(BF16), vecs], "gather_first")
    w_in0t = whole(w_in0t)
    vg = vg.transpose(1, 0, 2).reshape(vec_rows, D_MODEL)
    dw_w = vg[:CONV_HALO]
    dw_b, ln_g, ln_b = vg[CONV_K:CONV_K + 1], vg[CONV_K + 1:CONV_K + 2], vg[CONV_K + 2:CONV_K + 3]

    sinks = a_sinks[0]
    (h0, q, kv, gug), (w_out0,) = _fwd_in0(x2d, pre_norm[0:1], w_in0t, [ab_w_out[0].astype(BF16)], ts_big)
    o, (w_in1t, w_out1) = _attn_fwd(q, kv, sinks, [c_w_in[0].T.astype(BF16), c_w_out[0].astype(BF16)])
    w_out0, w_in1t, w_out1 = whole(w_out0), whole(w_in1t), whole(w_out1)
    mix0, pooled, y0, x1 = _fwd_out0(gug, o, b_pool_w[0], b_pool_scale, w_out0, post_norm[0:1], x2d, ts_big)
    h1, proj1, glu = _fwd_in1(x1, pre_norm[1:2], w_in1t, ts_big)
    ymix1, dy1, dx2, dcf, dgate, acc1 = _fwd_out1(glu, proj1, dw_w, dw_b, ln_g, ln_b, w_out1, post_norm[1:2], x1, target, min(TS_CONV_FWD, seq))

    dproj1, dx1, ddw_w, dpre1 = _bwd_in1(dcf, glu, proj1, dgate, dw_w, w_in1t, x1, pre_norm[1:2], dx2, min(TS_CONV_BWD, seq))
    g_in1t = _matmul_tn(dproj1, h1, "grad_w_in1", ts_grad, 1024)
    g_out1 = _matmul_tn(ymix1, dy1, "grad_w_out1", ts_grad, 1024)
    do, dgg, dpooled, dpost0, dscale, dpool_w, g_out0 = _bwd_out0(dx1, y0, post_norm[0:1], w_out0, gug, o, pooled, b_pool_w[0], b_pool_scale, mix0, ts_big)
    slabs = lambda g: g.reshape(N_DEV, -1, D_MODEL)
    me = 4 * lax.axis_index("x") + 2 * lax.axis_index("y") + lax.axis_index("c")
    early = [slabs(g_in1t), slabs(g_out1), slabs(g_out0)]
    row = lambda a: jnp.sum(a, axis=0, keepdims=True)
    lanes8 = lambda a: row(a).reshape(-1, LANES)
    loss_row = jnp.pad(jnp.sum(acc1[ACC_LOSS]).reshape(1, 1), ((0, 0), (0, LANES - 1)))
    done = jnp.concatenate([lanes8(dpre1), lanes8(dpost0), lanes8(acc1[ACC_POST]), lanes8(dscale),
                            loss_row, jnp.zeros((3, LANES), F32), dpool_w.reshape(4 * POOL_GC, LANES)], axis=0)
    (dq, dkv, dsink), arrived, done_all = _attn_bwd(q, kv, do, sinks, early, done)
    g_in1t, g_c_w_out, g_ab_w_out = [_sum_slabs(lax.dynamic_index_in_dim(p, me, keepdims=False), r, name)
                                     for p, r, name in zip(early, arrived, ("sum_w_in1", "sum_w_out1", "sum_w_out0"))]
    dproj0, grad_x, dpre0 = _bwd_in0(dpooled, dq, dkv, dgg, w_in0t, x2d, pre_norm[0:1], dx1, ts_big)
    g_in0t = _matmul_tn(dproj0, h0, "grad_w_in0", ts_grad, 768)

    vec_g = jnp.concatenate([jnp.sum(ddw_w[:CONV_K], axis=1), row(acc1[ACC_DW_B]), row(acc1[ACC_LN_G]), row(acc1[ACC_LN_B]),
                             jnp.zeros((vec_rows - CONV_K - 3, D_MODEL), F32)], axis=0)
    last = jnp.concatenate([lanes8(dpre0), row(dsink), jnp.zeros((SUBLANES - 1, LANES), F32)], axis=0)
    g_in0t, vec_g, last, rep = _final_reduce(slabs(g_in0t), vec_g.reshape(vec_rows, N_DEV, ch).transpose(1, 0, 2),
                                             last, done_all, "final_reduce")
    g_a_w_in, g_c_w_in = g_in0t.T, g_in1t.T
    g_dw_w, g_dw_b, g_ln_g, g_ln_b = vec_g[:CONV_K], vec_g[CONV_K:CONV_K + 1], vec_g[CONV_K + 1:CONV_K + 2], vec_g[CONV_K + 2:CONV_K + 3]
    g_pre = jnp.concatenate([last[:8], rep[:8]], axis=0).reshape(2, D_MODEL)
    g_sinks = last[8:9, :Q_HEADS]
    g_post = rep[8:24].reshape(2, D_MODEL)
    g_scale = rep[24:28].reshape(1, POOL_WIDTH)
    loss = (0.5 / D_MODEL) * rep[28, 0]
    g_pool_w = rep[32:]

    grads = [g_pre, g_post, g_a_w_in, g_sinks, g_pool_w, g_scale, g_ab_w_out, g_c_w_in, g_dw_w, g_dw_b, g_ln_g, g_ln_b, g_c_w_out]
    weights = [pre_norm, post_norm, a_w_in, a_sinks, b_pool_w, b_pool_scale, ab_w_out, c_w_in, c_dw_w, c_dw_b, c_ln_g, c_ln_b, c_w_out]
    m_in = [m_pre_norm, m_post_norm, m_a_w_in, m_a_sinks, m_b_pool_w, m_b_pool_scale, m_ab_w_out, m_c_w_in, m_c_dw_w, m_c_dw_b, m_c_ln_g, m_c_ln_b, m_c_w_out]
    v_in = [v_pre_norm, v_post_norm, v_a_w_in, v_a_sinks, v_b_pool_w, v_b_pool_scale, v_ab_w_out, v_c_w_in, v_c_dw_w, v_c_dw_b, v_c_ln_g, v_c_ln_b, v_c_w_out]
    flat = lambda arrs: [a.reshape(g.shape) for a, g in zip(arrs, grads)]
    big = (2, 6, 7, 12)
    small = tuple(k for k in range(len(grads)) if k not in big)
    pick = lambda arrs, idx: [arrs[k] for k in idx]
    deltas, new_m, new_v = [None] * 13, [None] * 13, [None] * 13
    for idx, name in ((big, "adamw_matrices"), (small, "adamw_vectors")):
        d, m, v = _adamw(pick(flat(weights), idx), pick(grads, idx), pick(flat(m_in), idx), pick(flat(v_in), idx), name)
        for k, dk, mk, vk in zip(idx, d, m, v):
            deltas[k], new_m[k], new_v[k] = dk, mk, vk
    shaped = lambda arrs: [a.reshape(w.shape) for a, w in zip(arrs, weights)]
    return (loss, grad_x[None], *shaped(grads), *shaped(deltas), *shaped(new_m), *shaped(new_v))
```

```python
import functools

import jax
import jax.numpy as jnp
from jax import lax
from jax.experimental import pallas as pl
from jax.experimental.pallas import tpu as pltpu

F32 = jnp.float32
BF16 = jnp.bfloat16
MESH = pl.DeviceIdType.MESH
AXES = ("x", "y", "c")
N_DEV = 8

D_MODEL = 1024
HEAD_DIM = 64
Q_HEADS = 8
GROUP = 4
ATTN_WIDTH = 512
KV_WIDTH = 128
BLOCK = 128
POOL_WIDTH = 512
POOL_WINDOWS = (2, 4, 8, 16)
POOL_GC = 128
POOL_HALO = 16
EVEN_IN = 2304
CONV_K = 31
CONV_HALO = 32
ODD_IN = 3072
EPS = 1e-6
NEG = -1e30
SLOPES = tuple(2.0 ** (-8.0 * (h + 1) / Q_HEADS) for h in range(Q_HEADS))

ADAM_LR = 0.001
ADAM_B1 = 0.9
ADAM_B2 = 0.999
ADAM_EPS = 1e-08
ADAM_WD = 0.01
ADAM_STEP = 10

SUBLANES = 8
LANES = 128
VMEM_LIMIT = 56 * 1024 * 1024

NT = (((1,), (1,)), ((), ()))
TN = (((0,), (0,)), ((), ()))


def _params(**kw):
    return pltpu.CompilerParams(dimension_semantics=("arbitrary",), vmem_limit_bytes=VMEM_LIMIT, **kw)


def _dot(a, b):
    return jnp.dot(a, b, preferred_element_type=F32)


def _dot_nt(a, b):
    return lax.dot_general(a, b, NT, preferred_element_type=F32)


def _dot_tn(a, b):
    return lax.dot_general(a, b, TN, preferred_element_type=F32)


def _sigmoid(v):
    return 1.0 / (1.0 + jnp.exp(-v))


def _rows8(v):
    r, c = v.shape
    return jnp.sum(v.reshape(r // SUBLANES, SUBLANES, c), axis=0)


def _rms_fwd(v, g):
    r = lax.rsqrt(jnp.mean(v * v, axis=-1, keepdims=True) + EPS)
    return v * r * g, r


def _rms_bwd(v, r, g, dout):
    gd = dout * g
    dv = r * gd - v * (r * r * r) * jnp.mean(v * gd, axis=-1, keepdims=True)
    return dv, dout * (v * r)


def _full(shape):
    return pl.BlockSpec(shape, lambda i: (0,) * len(shape))


def _resident(shape):
    return pl.BlockSpec(shape, lambda i: (0,) * len(shape), pipeline_mode=pl.Buffered(1))


def _accumulate_tn(acc, a, b, first_step):
    @pl.when(first_step)
    def _():
        acc[...] = jnp.zeros_like(acc)

    acc[...] += _dot_tn(a, b)


def _tile(ts, cols, col_block=0):
    return pl.BlockSpec((ts, cols), lambda i: (i, col_block))


def _position():
    return lax.axis_index("x"), lax.axis_index("y"), lax.axis_index("c")


class _Gather:
    def __init__(self, x_refs, out_refs, send_sems, recv_sems, local_sems):
        self.x_refs, self.out_refs = x_refs, out_refs
        self.send_sems, self.recv_sems, self.local_sems = send_sems, recv_sems, local_sems
        self.n = len(x_refs)
        x, y, c = _position()
        self.c = c
        self.me, self.sibling = (x, y, c), (x, y, 1 - c)
        self.chips = [(1 - x, y), (x, 1 - y), (1 - x, 1 - y)]

    def copy(self, k, j, owner, to, src=None):
        slab = self.out_refs[k].at[4 * owner[0] + 2 * owner[1] + owner[2]]
        return pltpu.make_async_remote_copy(
            src_ref=slab if src is None else src, dst_ref=slab, send_sem=self.send_sems.at[7 * k + j],
            recv_sem=self.recv_sems.at[7 * k + j], device_id=to, device_id_type=MESH)

    def mine(self, k):
        return pltpu.make_async_copy(self.x_refs[k], self.out_refs[k].at[4 * self.me[0] + 2 * self.me[1] + self.c],
                                     self.local_sems.at[k])

    def first(self, k):
        out = [self.copy(k, 0, self.me, self.sibling, src=self.x_refs[k])]
        return out + [self.copy(k, 1 + j, self.me, (*chip, self.c), src=self.x_refs[k]) for j, chip in enumerate(self.chips)]

    def begin(self):
        for k in range(self.n):
            self.mine(k).start()
            for cp in self.first(k):
                cp.start()

    def passed(self, k):
        return [self.copy(k, 4 + j, (*chip, self.c), self.sibling) for j, chip in enumerate(self.chips)]

    def relay(self):
        for k in range(self.n):
            for j, chip in enumerate(self.chips):
                self.copy(k, 1 + j, (*chip, self.c), self.me).wait_recv()
                self.passed(k)[j].start()

    def finish(self):
        for k in range(self.n):
            self.copy(k, 0, self.sibling, self.me).wait_recv()
            for j, chip in enumerate(self.chips):
                self.copy(k, 4 + j, (*chip, 1 - self.c), self.me).wait_recv()
        for k in range(self.n):
            for cp in self.first(k) + self.passed(k):
                cp.wait_send()
            self.mine(k).wait()

    @staticmethod
    def semaphores(n):
        return [pltpu.SemaphoreType.DMA((7 * n,)), pltpu.SemaphoreType.DMA((7 * n,)), pltpu.SemaphoreType.DMA((n,))]


def RELAY_AT(steps):
    return steps * 3 // 4


def _all_gather(blocks, name):
    n = len(blocks)

    def body(*refs):
        gather = _Gather(refs[:n], refs[n:2 * n], *refs[2 * n:])
        gather.begin()
        gather.relay()
        gather.finish()

    return pl.pallas_call(
        body, name=name,
        out_shape=[jax.ShapeDtypeStruct((N_DEV, *b.shape), b.dtype) for b in blocks],
        in_specs=[pl.BlockSpec(memory_space=pltpu.VMEM)] * n,
        out_specs=[pl.BlockSpec(memory_space=pltpu.VMEM)] * n,
        scratch_shapes=_Gather.semaphores(n),
        compiler_params=pltpu.CompilerParams(vmem_limit_bytes=VMEM_LIMIT),
    )(*blocks)


class _Scatter:
    def __init__(self, part_refs, recv_refs, send_sems, recv_sems):
        self.part_refs, self.recv_refs, self.send_sems, self.recv_sems = part_refs, recv_refs, send_sems, recv_sems
        self.n = len(part_refs)

    def copies(self):
        x, y, c = _position()
        me = 4 * x + 2 * y + c
        out = []
        for k in range(self.n):
            for j in range(N_DEV - 1):
                d = j + 1
                out.append(pltpu.make_async_remote_copy(
                    src_ref=self.part_refs[k].at[me ^ d], dst_ref=self.recv_refs[k].at[j],
                    send_sem=self.send_sems.at[7 * k + j], recv_sem=self.recv_sems.at[7 * k + j],
                    device_id=(x ^ (d >> 2), y ^ ((d >> 1) & 1), c ^ (d & 1)), device_id_type=MESH))
        return out

    def begin(self):
        for cp in self.copies():
            cp.start()

    def finish(self):
        for cp in self.copies():
            cp.wait_recv()
        for cp in self.copies():
            cp.wait_send()

    @staticmethod
    def semaphores(n):
        return [pltpu.SemaphoreType.DMA((7 * n,)), pltpu.SemaphoreType.DMA((7 * n,))]


def _sum_slabs(own, recv, name):
    rows, cols = own.shape

    def body(own_ref, recv_ref, out_ref):
        acc = own_ref[...].astype(F32)
        for j in range(N_DEV - 1):
            acc = acc + recv_ref[j].astype(F32)
        out_ref[...] = acc

    return pl.pallas_call(body, name=name, out_shape=jax.ShapeDtypeStruct((rows, cols), F32))(own, recv)


N_CHIPS = 4


def _final_reduce(parts, vec_parts, rep, early_all, name):
    _, rows, cols = parts.shape
    vrows, prows = vec_parts.shape[1], rep.shape[0]
    peers = N_CHIPS - 1

    def body(parts_ref, vec_ref, rep_ref, early_ref, out_ref, vec_out, rep_out, early_out, recv_a, own_a, mid, recv_b, vec_recv, rep_all,
             a_send, a_recv, a_local, b_send, b_recv, v_send, v_recv, r_send, r_recv):
        x, y, c = _position()
        chip = 2 * x + y
        me = 2 * chip + c
        everyone = [(d, (x ^ (d >> 2), y ^ ((d >> 1) & 1), c ^ (d & 1))) for d in range(1, N_DEV)]

        stage1, loads = [], []
        for j in range(N_CHIPS):
            stage1.append(pltpu.make_async_remote_copy(
                src_ref=parts_ref.at[2 * j + 1 - c], dst_ref=recv_a.at[j], send_sem=a_send.at[j], recv_sem=a_recv.at[j],
                device_id=(x, y, 1 - c), device_id_type=MESH))
            loads.append(pltpu.make_async_copy(parts_ref.at[2 * j + c], own_a.at[j], a_local.at[j]))
            stage1[-1].start()
            loads[-1].start()
        small = []
        for d, to in everyone:
            small.append(pltpu.make_async_remote_copy(
                src_ref=vec_ref.at[me ^ d], dst_ref=vec_recv.at[d - 1], send_sem=v_send.at[d - 1], recv_sem=v_recv.at[d - 1],
                device_id=to, device_id_type=MESH))
            small.append(pltpu.make_async_remote_copy(
                src_ref=rep_ref, dst_ref=rep_all.at[me], send_sem=r_send.at[d - 1], recv_sem=r_recv.at[d - 1],
                device_id=to, device_id_type=MESH))
        for cp in small:
            cp.start()
        rep_all[me] = rep_ref[...]

        for j in range(N_CHIPS):
            loads[j].wait()
            stage1[j].wait_recv()
            mid[j] = (own_a[j].astype(F32) + recv_a[j].astype(F32)).astype(BF16)
        stage2 = []
        for d in range(1, N_CHIPS):
            stage2.append(pltpu.make_async_remote_copy(
                src_ref=mid.at[chip ^ d], dst_ref=recv_b.at[d - 1], send_sem=b_send.at[d - 1], recv_sem=b_recv.at[d - 1],
                device_id=(x ^ (d >> 1), y ^ (d & 1), c), device_id_type=MESH))
            stage2[-1].start()

        for cp in small:
            cp.wait_recv()
        vec_sum = vec_ref[me]
        for j in range(N_DEV - 1):
            vec_sum = vec_sum + vec_recv[j]
        vec_out[...] = vec_sum
        rep_sum = rep_all[0]
        for d in range(1, N_DEV):
            rep_sum = rep_sum + rep_all[d]
        rep_out[...] = rep_sum
        early_sum = early_ref[0]
        for d in range(1, N_DEV):
            early_sum = early_sum + early_ref[d]
        early_out[...] = early_sum

        acc = mid[chip].astype(F32)
        for d in range(peers):
            stage2[d].wait_recv()
            acc = acc + recv_b[d].astype(F32)
        out_ref[...] = acc
        for cp in stage1 + stage2 + small:
            cp.wait_send()

    vmem = pl.BlockSpec(memory_space=pltpu.VMEM)
    dma = pltpu.SemaphoreType.DMA
    return pl.pallas_call(
        body, name=name,
        out_shape=[jax.ShapeDtypeStruct((rows, cols), F32), jax.ShapeDtypeStruct((vrows, LANES), F32),
                   jax.ShapeDtypeStruct((prows, LANES), F32), jax.ShapeDtypeStruct(early_all.shape[1:], F32)],
        in_specs=[pl.BlockSpec(memory_space=pl.ANY), vmem, vmem, vmem],
        out_specs=[vmem, vmem, vmem, vmem],
        scratch_shapes=[pltpu.VMEM((N_CHIPS, rows, cols), BF16), pltpu.VMEM((N_CHIPS, rows, cols), BF16),
                        pltpu.VMEM((N_CHIPS, rows, cols), BF16), pltpu.VMEM((peers, rows, cols), BF16),
                        pltpu.VMEM((N_DEV - 1, vrows, LANES), F32), pltpu.VMEM((N_DEV, prows, LANES), F32),
                        dma((N_CHIPS,)), dma((N_CHIPS,)), dma((N_CHIPS,)), dma((peers,)), dma((peers,)),
                        dma((N_DEV - 1,)), dma((N_DEV - 1,)), dma((N_DEV - 1,)), dma((N_DEV - 1,))],
        compiler_params=pltpu.CompilerParams(vmem_limit_bytes=VMEM_LIMIT),
    )(parts, vec_parts, rep, early_all)


def _fwd_in0(x, g, w, shards, ts):
    seq = x.shape[0]
    n = len(shards)
    steps = seq // ts

    def body(*refs):
        x_ref, g_ref, w_ref = refs[:3]
        h_ref, q_ref, kv_ref, gug_ref = refs[3 + n:7 + n]
        gather = lambda: _Gather(refs[3:3 + n], refs[7 + n:7 + 2 * n], *refs[7 + 2 * n:])
        i = pl.program_id(0)

        @pl.when(i == 0)
        def _():
            gather().begin()

        h, _ = _rms_fwd(x_ref[...], g_ref[...])
        h = h.astype(BF16)
        h_ref[...] = h
        proj = _dot_nt(h, w_ref[...])
        q_ref[...] = proj[:, :ATTN_WIDTH].astype(BF16)
        kv_ref[...] = proj[:, ATTN_WIDTH:ATTN_WIDTH + 2 * KV_WIDTH].astype(BF16)
        gug_ref[...] = proj[:, ATTN_WIDTH + 2 * KV_WIDTH:].astype(BF16)

        @pl.when(i == RELAY_AT(steps))
        def _():
            gather().relay()

        @pl.when(i == steps - 1)
        def _():
            gather().finish()

    hbm = pl.BlockSpec(memory_space=pl.ANY)
    out = pl.pallas_call(
        body, name="fwd_in0", grid=(steps,),
        in_specs=[_tile(ts, D_MODEL), _full((1, D_MODEL)), _full((EVEN_IN, D_MODEL))] + [hbm] * n,
        out_specs=[_tile(ts, D_MODEL), _tile(ts, ATTN_WIDTH), _tile(ts, 2 * KV_WIDTH), _tile(ts, 3 * POOL_WIDTH)] + [hbm] * n,
        out_shape=[jax.ShapeDtypeStruct((seq, D_MODEL), BF16), jax.ShapeDtypeStruct((seq, ATTN_WIDTH), BF16),
                   jax.ShapeDtypeStruct((seq, 2 * KV_WIDTH), BF16), jax.ShapeDtypeStruct((seq, 3 * POOL_WIDTH), BF16)]
        + [jax.ShapeDtypeStruct((N_DEV, *b.shape), b.dtype) for b in shards],
        scratch_shapes=_Gather.semaphores(n),
        compiler_params=_params(),
    )(x, g, w, *shards)
    return out[:4], out[4:]


GROUP_ROWS = GROUP * BLOCK


def _attn_mask(rows, first_block):
    row = lax.broadcasted_iota(jnp.int32, (rows, 2 * BLOCK), 0) & (BLOCK - 1)
    col = lax.broadcasted_iota(jnp.int32, (rows, 2 * BLOCK), 1)
    dist = row + BLOCK - col
    valid = (dist >= 0) & (dist < BLOCK) & ((col >= BLOCK) | jnp.logical_not(first_block))
    return valid, dist.astype(F32)


def _group_rows(block, kvh):
    return jnp.concatenate([block[:, HEAD_DIM * h:HEAD_DIM * (h + 1)] for h in range(GROUP * kvh, GROUP * (kvh + 1))], axis=0)


def _group_columns(sink_ref, kvh):
    head = lax.broadcasted_iota(jnp.int32, (GROUP_ROWS, 1), 0) // BLOCK
    sink = jnp.zeros((GROUP_ROWS, 1), F32)
    slope = jnp.zeros((GROUP_ROWS, 1), F32)
    for i in range(GROUP):
        sink = jnp.where(head == i, sink_ref[GROUP * kvh + i], sink)
        slope = jnp.where(head == i, SLOPES[GROUP * kvh + i], slope)
    return sink, slope


def _attn_probs(qh, kh, sink, slope, valid, distf):
    s = _dot_nt(qh, kh) * (HEAD_DIM ** -0.5)
    s = jnp.where(valid, s - slope * distf, NEG)
    mx = jnp.maximum(jnp.max(s, axis=-1, keepdims=True), sink)
    e = jnp.exp(s - mx)
    es = jnp.exp(sink - mx)
    den = jnp.sum(e, axis=-1, keepdims=True) + es
    return e / den, es / den


def _attn_fwd(q, kv, sinks, shards):
    seq = q.shape[0]
    nb = seq // BLOCK
    ns = len(shards)

    def body(*refs):
        sink_ref, q_ref, kvc_ref, kvp_ref = refs[:4]
        o_ref = refs[4 + ns]
        gather = lambda: _Gather(refs[4:4 + ns], refs[5 + ns:5 + 2 * ns], *refs[5 + 2 * ns:])
        n = pl.program_id(0)

        @pl.when(n == 0)
        def _():
            gather().begin()

        valid, distf = _attn_mask(BLOCK, n == 0)
        qb = q_ref[...]
        kk = jnp.concatenate([kvp_ref[...], kvc_ref[...]], axis=0)
        for h in range(Q_HEADS):
            kvh = h // GROUP
            qh = qb[:, HEAD_DIM * h:HEAD_DIM * (h + 1)]
            kh = kk[:, HEAD_DIM * kvh:HEAD_DIM * (kvh + 1)]
            vh = kk[:, KV_WIDTH + HEAD_DIM * kvh:KV_WIDTH + HEAD_DIM * (kvh + 1)]
            p, _ = _attn_probs(qh, kh, sink_ref[h], SLOPES[h], valid, distf)
            o_ref[:, HEAD_DIM * h:HEAD_DIM * (h + 1)] = _dot(p.astype(BF16), vh)

        @pl.when(n == RELAY_AT(nb))
        def _():
            gather().relay()

        @pl.when(n == nb - 1)
        def _():
            gather().finish()

    hbm = pl.BlockSpec(memory_space=pl.ANY)
    out = pl.pallas_call(
        body, name="attn_fwd", grid=(nb,),
        in_specs=[pl.BlockSpec(memory_space=pltpu.SMEM),
                  pl.BlockSpec((BLOCK, ATTN_WIDTH), lambda n: (n, 0)),
                  pl.BlockSpec((BLOCK, 2 * KV_WIDTH), lambda n: (n, 0)),
                  pl.BlockSpec((BLOCK, 2 * KV_WIDTH), lambda n: (jnp.maximum(n - 1, 0), 0))] + [hbm] * ns,
        out_specs=[pl.BlockSpec((BLOCK, ATTN_WIDTH), lambda n: (n, 0))] + [hbm] * ns,
        out_shape=[jax.ShapeDtypeStruct((seq, ATTN_WIDTH), F32)]
        + [jax.ShapeDtypeStruct((N_DEV, *b.shape), b.dtype) for b in shards],
        scratch_shapes=_Gather.semaphores(ns),
        compiler_params=_params(),
    )(sinks, q, kv, kv, *shards)
    return out[0], out[1:]


def _pool_counts(first_row, rows, window):
    t = first_row + lax.broadcasted_iota(jnp.int32, (rows, 1), 0)
    return jnp.minimum(t + 1, window).astype(F32)


def _fwd_out0(gug, o, pool_w, pool_scale, w_out, g_post, x, ts):
    seq = x.shape[0]
    hb = ts // POOL_HALO

    def body(gug_ref, halo_ref, o_ref, pw_ref, ps_ref, w_ref, g_ref, x_ref, mix_ref, pooled_ref, y_ref, x1_ref, ubuf):
        i = pl.program_id(0)
        ga = gug_ref[:, :ATTN_WIDTH].astype(F32)
        u = gug_ref[:, ATTN_WIDTH:ATTN_WIDTH + POOL_WIDTH].astype(F32)
        gb = gug_ref[:, ATTN_WIDTH + POOL_WIDTH:].astype(F32)
        mix_ref[:, :ATTN_WIDTH] = (o_ref[...] * (ga * _sigmoid(ga))).astype(BF16)
        ubuf[:POOL_HALO, :] = jnp.where(i > 0, halo_ref[...].astype(F32), 0.0)
        ubuf[POOL_HALO:, :] = u
        silu_gb = gb * _sigmoid(gb)
        for g, window in enumerate(POOL_WINDOWS):
            lanes = slice(POOL_GC * g, POOL_GC * (g + 1))
            acc = ubuf[pl.ds(POOL_HALO, ts), lanes]
            for k in range(1, window):
                acc = acc + ubuf[pl.ds(POOL_HALO - k, ts), lanes]
            pooled = (acc / _pool_counts(i * ts, ts, window) - u[:, lanes]).astype(BF16)
            pooled_ref[:, lanes] = pooled
            ypool = _dot(pooled, pw_ref[g].astype(BF16)) * ps_ref[:, lanes]
            mix_ref[:, ATTN_WIDTH + POOL_GC * g:ATTN_WIDTH + POOL_GC * (g + 1)] = (ypool * silu_gb[:, lanes]).astype(BF16)
        y = _dot(mix_ref[...], w_ref[...])
        y_ref[...] = y
        yn, _ = _rms_fwd(y, g_ref[...])
        x1_ref[...] = x_ref[...] + yn

    return pl.pallas_call(
        body, name="fwd_out0", grid=(seq // ts,),
        in_specs=[_tile(ts, 3 * POOL_WIDTH),
                  pl.BlockSpec((POOL_HALO, POOL_WIDTH), lambda i: (jnp.maximum(i * hb - 1, 0), 1)),
                  _tile(ts, ATTN_WIDTH), _full((4, POOL_GC, POOL_GC)), _full((1, POOL_WIDTH)),
                  _full((D_MODEL, D_MODEL)), _full((1, D_MODEL)), _tile(ts, D_MODEL)],
        out_specs=[_tile(ts, D_MODEL), _tile(ts, POOL_WIDTH), _tile(ts, D_MODEL), _tile(ts, D_MODEL)],
        out_shape=[jax.ShapeDtypeStruct((seq, D_MODEL), BF16), jax.ShapeDtypeStruct((seq, POOL_WIDTH), BF16),
                   jax.ShapeDtypeStruct((seq, D_MODEL), F32), jax.ShapeDtypeStruct((seq, D_MODEL), F32)],
        scratch_shapes=[pltpu.VMEM((ts + POOL_HALO, POOL_WIDTH), F32)],
        compiler_params=_params(),
    )(gug, gug, o, pool_w, pool_scale, w_out, g_post, x)


def _fwd_in1(x1, g, w, ts):
    seq = x1.shape[0]

    def body(x_ref, g_ref, w_ref, h_ref, proj_ref, glu_ref):
        h, _ = _rms_fwd(x_ref[...], g_ref[...])
        h = h.astype(BF16)
        h_ref[...] = h
        proj = _dot_nt(h, w_ref[...])
        proj_ref[...] = proj.astype(BF16)
        glu_ref[...] = proj[:, :D_MODEL] * _sigmoid(proj[:, D_MODEL:2 * D_MODEL])

    return pl.pallas_call(
        body, name="fwd_in1", grid=(seq // ts,),
        in_specs=[_tile(ts, D_MODEL), _full((1, D_MODEL)), _full((ODD_IN, D_MODEL))],
        out_specs=[_tile(ts, D_MODEL), _tile(ts, ODD_IN), _tile(ts, D_MODEL)],
        out_shape=[jax.ShapeDtypeStruct((seq, D_MODEL), BF16), jax.ShapeDtypeStruct((seq, ODD_IN), BF16),
                   jax.ShapeDtypeStruct((seq, D_MODEL), F32)],
        compiler_params=_params(),
    )(x1, g, w)


ACC_LOSS, ACC_POST, ACC_LN_G, ACC_LN_B, ACC_DW_B = range(5)
DDW_ROWS = 88
CONV_FIRST = CONV_HALO - CONV_K + 1


def _fwd_tap(offset):
    return offset - CONV_FIRST if CONV_FIRST <= offset <= CONV_HALO else None


def _bwd_tap(offset):
    return CONV_K - 1 - offset if offset < CONV_K else None


def _conv_taps(w_ref, buf_ref, ts, lanes, tap_of_offset):
    out = None
    for b in range(SUBLANES):
        rows = ts if b == 0 else ts + SUBLANES
        part = None
        for a in range(CONV_HALO // SUBLANES + 1):
            k = tap_of_offset(SUBLANES * a + b)
            if k is None:
                continue
            term = w_ref[k:k + 1, lanes] * buf_ref[pl.ds(SUBLANES * a, rows), lanes]
            part = term if part is None else part + term
        if part is None:
            continue
        if b:
            part = part[b:b + ts, :]
        out = part if out is None else out + part
    return out


def _fwd_out1(glu, proj, dw_w, dw_b, ln_g, ln_b, w_out, g_post, x1, target, ts):
    seq = x1.shape[0]
    hb = ts // CONV_HALO

    def body(glu_ref, halo_ref, gate_ref, dww_ref, dwb_ref, lng_ref, lnb_ref, w_ref, g_ref, x1_ref, t_ref,
             ymix_ref, dy_ref, dx2_ref, dcf_ref, dgate_ref, acc_ref, gbuf):
        i = pl.program_id(0)

        @pl.when(i == 0)
        def _():
            acc_ref[...] = jnp.zeros_like(acc_ref)

        gbuf[:CONV_HALO, :] = jnp.where(i > 0, halo_ref[...], 0.0)
        gbuf[CONV_HALO:, :] = glu_ref[...]
        for lb in range(D_MODEL // LANES):
            lanes = slice(LANES * lb, LANES * (lb + 1))
            dcf_ref[:, lanes] = _conv_taps(dww_ref, gbuf, ts, lanes, _fwd_tap)
        cf = dcf_ref[...] + dwb_ref[...]
        mu = jnp.mean(cf, axis=-1, keepdims=True)
        cen = cf - mu
        rs = lax.rsqrt(jnp.mean(cen * cen, axis=-1, keepdims=True) + EPS)
        xhat = cen * rs
        cn = xhat * lng_ref[...] + lnb_ref[...]
        gate = gate_ref[...].astype(F32)
        sg = _sigmoid(gate)
        sc = _sigmoid(cn)
        silu_gate = gate * sg
        silu_cn = cn * sc
        ymix = (silu_cn * silu_gate).astype(BF16)
        ymix_ref[...] = ymix
        y = _dot(ymix, w_ref[...])
        yn, r = _rms_fwd(y, g_ref[...])
        err = (x1_ref[...] + yn) - t_ref[...]
        acc_ref[ACC_LOSS] += _rows8(err * err)
        dx2 = err * (1.0 / D_MODEL)
        dx2_ref[...] = dx2
        dy, dpost = _rms_bwd(y, r, g_ref[...], dx2)
        acc_ref[ACC_POST] += _rows8(dpost)
        dy = dy.astype(BF16)
        dy_ref[...] = dy
        dymix = _dot_nt(dy, w_ref[...])
        dgate_ref[...] = (dymix * silu_cn * (sg * (1.0 + gate * (1.0 - sg)))).astype(BF16)
        dcn = dymix * silu_gate * (sc * (1.0 + cn * (1.0 - sc)))
        acc_ref[ACC_LN_G] += _rows8(dcn * xhat)
        acc_ref[ACC_LN_B] += _rows8(dcn)
        dxhat = dcn * lng_ref[...]
        dcf = rs * (dxhat - jnp.mean(dxhat, axis=-1, keepdims=True)
                    - xhat * jnp.mean(dxhat * xhat, axis=-1, keepdims=True))
        acc_ref[ACC_DW_B] += _rows8(dcf)
        dcf_ref[...] = dcf

    return pl.pallas_call(
        body, name="fwd_out1", grid=(seq // ts,),
        in_specs=[_tile(ts, D_MODEL),
                  pl.BlockSpec((CONV_HALO, D_MODEL), lambda i: (jnp.maximum(i * hb - 1, 0), 0)),
                  _tile(ts, D_MODEL, 2), _full((CONV_HALO, D_MODEL)), _full((1, D_MODEL)), _full((1, D_MODEL)),
                  _full((1, D_MODEL)), _full((D_MODEL, D_MODEL)), _full((1, D_MODEL)), _tile(ts, D_MODEL),
                  _tile(ts, D_MODEL)],
        out_specs=[_tile(ts, D_MODEL), _tile(ts, D_MODEL), _tile(ts, D_MODEL), _tile(ts, D_MODEL), _tile(ts, D_MODEL),
                   _full((5, SUBLANES, D_MODEL))],
        out_shape=[jax.ShapeDtypeStruct((seq, D_MODEL), BF16), jax.ShapeDtypeStruct((seq, D_MODEL), BF16),
                   jax.ShapeDtypeStruct((seq, D_MODEL), F32), jax.ShapeDtypeStruct((seq, D_MODEL), F32),
                   jax.ShapeDtypeStruct((seq, D_MODEL), BF16), jax.ShapeDtypeStruct((5, SUBLANES, D_MODEL), F32)],
        scratch_shapes=[pltpu.VMEM((ts + CONV_HALO, D_MODEL), F32)],
        compiler_params=_params(),
    )(glu, glu, proj, dw_w, dw_b, ln_g, ln_b, w_out, g_post, x1, target)


def _bwd_in1(dcf, glu, proj, dgate, dw_w, w_in, x1, g_pre, dx2, ts):
    seq = x1.shape[0]
    hb = ts // CONV_HALO
    last = seq // CONV_HALO - 1
    nt = seq // ts

    def body(dcf_ref, dnext_ref, glu_ref, gprev_ref, ab_ref, dgate_ref, dww_ref, w_ref, x_ref, g_ref, dx2_ref,
             dproj_ref, dx1_ref, ddw_ref, dpre_ref, dbuf, gbuf, zbuf, sbuf):
        i = pl.program_id(0)

        @pl.when(i == 0)
        def _():
            ddw_ref[...] = jnp.zeros_like(ddw_ref)
            dpre_ref[...] = jnp.zeros_like(dpre_ref)

        dcf = dcf_ref[...]
        dbuf[:ts, :] = dcf
        dbuf[ts:, :] = jnp.where(i < nt - 1, dnext_ref[...], 0.0)
        gbuf[:CONV_HALO, :] = jnp.where(i > 0, gprev_ref[...], 0.0)
        gbuf[CONV_HALO:, :] = glu_ref[...]
        zbuf[:SUBLANES, :] = jnp.zeros((SUBLANES, D_MODEL), F32)
        zbuf[pl.ds(SUBLANES, ts), :] = dcf
        zbuf[pl.ds(SUBLANES + ts, SUBLANES), :] = jnp.zeros((SUBLANES, D_MODEL), F32)
        dh = _dot(dgate_ref[...], w_ref[2 * D_MODEL:, :])
        for lb in range(D_MODEL // LANES):
            lanes = slice(LANES * lb, LANES * (lb + 1))
            gate_lanes = slice(D_MODEL + LANES * lb, D_MODEL + LANES * (lb + 1))
            dglu = _conv_taps(dww_ref, dbuf, ts, lanes, _bwd_tap)
            a = ab_ref[:, lanes].astype(F32)
            sb = _sigmoid(ab_ref[:, gate_lanes].astype(F32))
            dproj_ref[:, lanes] = (dglu * sb).astype(BF16)
            dproj_ref[:, gate_lanes] = (dglu * a * sb * (1.0 - sb)).astype(BF16)
            for b in range(SUBLANES):
                rows = ts if b == 0 else ts + SUBLANES
                sbuf[b, pl.ds(0, rows), :] = zbuf[pl.ds(SUBLANES - b, rows), lanes]
                taps = [(a8, _fwd_tap(SUBLANES * a8 + b)) for a8 in range(CONV_HALO // SUBLANES + 1)]
                taps = [(a8, k) for a8, k in taps if k is not None]
                sums = [jnp.zeros((SUBLANES, LANES), F32) for _ in taps]
                for r0 in range(0, rows, DDW_ROWS):
                    rc = min(DDW_ROWS, rows - r0)
                    shifted = sbuf[b, pl.ds(r0, rc), :]
                    for n, (a8, k) in enumerate(taps):
                        sums[n] = sums[n] + _rows8(shifted * gbuf[pl.ds(SUBLANES * a8 + r0, rc), lanes])
                for n, (a8, k) in enumerate(taps):
                    ddw_ref[k, :, lanes] += sums[n]
            if lb % 2:
                for c0 in (LANES * (lb - 1), D_MODEL + LANES * (lb - 1)):
                    dh = dh + _dot(dproj_ref[:, c0:c0 + 2 * LANES], w_ref[c0:c0 + 2 * LANES, :])
        dproj_ref[:, 2 * D_MODEL:] = dgate_ref[...]
        x = x_ref[...]
        r = lax.rsqrt(jnp.mean(x * x, axis=-1, keepdims=True) + EPS)
        dx, dpre = _rms_bwd(x, r, g_ref[...], dh)
        dx1_ref[...] = dx2_ref[...] + dx
        dpre_ref[...] += _rows8(dpre)

    return pl.pallas_call(
        body, name="bwd_in1", grid=(nt,),
        in_specs=[_tile(ts, D_MODEL),
                  pl.BlockSpec((CONV_HALO, D_MODEL), lambda i: (jnp.minimum((i + 1) * hb, last), 0)),
                  _tile(ts, D_MODEL),
                  pl.BlockSpec((CONV_HALO, D_MODEL), lambda i: (jnp.maximum(i * hb - 1, 0), 0)),
                  _tile(ts, 2 * D_MODEL), _tile(ts, D_MODEL), _full((CONV_HALO, D_MODEL)),
                  _full((ODD_IN, D_MODEL)), _tile(ts, D_MODEL), _full((1, D_MODEL)), _tile(ts, D_MODEL)],
        out_specs=[_tile(ts, ODD_IN), _tile(ts, D_MODEL), _full((CONV_HALO, SUBLANES, D_MODEL)), _full((SUBLANES, D_MODEL))],
        out_shape=[jax.ShapeDtypeStruct((seq, ODD_IN), BF16), jax.ShapeDtypeStruct((seq, D_MODEL), F32),
                   jax.ShapeDtypeStruct((CONV_HALO, SUBLANES, D_MODEL), F32), jax.ShapeDtypeStruct((SUBLANES, D_MODEL), F32)],
        scratch_shapes=[pltpu.VMEM((ts + CONV_HALO, D_MODEL), F32), pltpu.VMEM((ts + CONV_HALO, D_MODEL), F32),
                        pltpu.VMEM((ts + 2 * SUBLANES, D_MODEL), F32), pltpu.VMEM((SUBLANES, ts + SUBLANES, LANES), F32)],
        compiler_params=_params(),
    )(dcf, dcf, glu, glu, proj, dgate, dw_w, w_in, x1, g_pre, dx2)


def _bwd_out0(dx1, y0, g_post, w_out, gug, o, pooled, pool_w, pool_scale, mix, ts):
    seq = dx1.shape[0]

    def body(dx1_ref, y_ref, g_ref, w_ref, gug_ref, o_ref, pooled_ref, pw_ref, ps_ref, mix_ref,
             do_ref, dgg_ref, dpooled_ref, dpost_ref, dscale_ref, dpw_ref, gw_ref, gacc):
        i = pl.program_id(0)

        @pl.when(i == 0)
        def _():
            dpost_ref[...] = jnp.zeros_like(dpost_ref)
            dscale_ref[...] = jnp.zeros_like(dscale_ref)
            dpw_ref[...] = jnp.zeros_like(dpw_ref)

        y = y_ref[...]
        r = lax.rsqrt(jnp.mean(y * y, axis=-1, keepdims=True) + EPS)
        dy, dpost = _rms_bwd(y, r, g_ref[...], dx1_ref[...])
        dpost_ref[...] += _rows8(dpost)
        dy = dy.astype(BF16)
        _accumulate_tn(gacc, mix_ref[...], dy, i == 0)
        dmix = _dot_nt(dy, w_ref[...])
        dya = dmix[:, :ATTN_WIDTH]
        dyb = dmix[:, ATTN_WIDTH:]
        ga = gug_ref[:, :ATTN_WIDTH].astype(F32)
        gb = gug_ref[:, ATTN_WIDTH + POOL_WIDTH:].astype(F32)
        sga = _sigmoid(ga)
        sgb = _sigmoid(gb)
        do_ref[...] = (dya * (ga * sga)).astype(BF16)
        dgg_ref[:, :ATTN_WIDTH] = (dya * o_ref[...] * (sga * (1.0 + ga * (1.0 - sga)))).astype(BF16)
        dypool = dyb * (gb * sgb)
        dsilu_gb = sgb * (1.0 + gb * (1.0 - sgb))
        for g in range(len(POOL_WINDOWS)):
            lanes = slice(POOL_GC * g, POOL_GC * (g + 1))
            pooled = pooled_ref[:, lanes]
            wg = pw_ref[g].astype(BF16)
            pw = _dot(pooled, wg)
            scale = ps_ref[:, lanes]
            dgg_ref[:, ATTN_WIDTH + POOL_GC * g:ATTN_WIDTH + POOL_GC * (g + 1)] = (
                dyb[:, lanes] * (pw * scale) * dsilu_gb[:, lanes]).astype(BF16)
            dscale_ref[:, lanes] += _rows8(dypool[:, lanes] * pw)
            dpw = (dypool[:, lanes] * scale).astype(BF16)
            dpooled_ref[:, lanes] = _dot_nt(dpw, wg)
            dpw_ref[g] += _dot_tn(pooled, dpw)

        @pl.when(i == seq // ts - 1)
        def _():
            gw_ref[...] = gacc[...].astype(BF16)

    return pl.pallas_call(
        body, name="bwd_out0", grid=(seq // ts,),
        in_specs=[_tile(ts, D_MODEL), _tile(ts, D_MODEL), _full((1, D_MODEL)), _resident((D_MODEL, D_MODEL)),
                  _tile(ts, 3 * POOL_WIDTH), _tile(ts, ATTN_WIDTH), _tile(ts, POOL_WIDTH),
                  _full((4, POOL_GC, POOL_GC)), _full((1, POOL_WIDTH)), _tile(ts, D_MODEL)],
        out_specs=[_tile(ts, ATTN_WIDTH), _tile(ts, ATTN_WIDTH + POOL_WIDTH), _tile(ts, POOL_WIDTH),
                   _full((SUBLANES, D_MODEL)), _full((SUBLANES, POOL_WIDTH)), _full((4, POOL_GC, POOL_GC)),
                   _full((D_MODEL, D_MODEL))],
        out_shape=[jax.ShapeDtypeStruct((seq, ATTN_WIDTH), BF16),
                   jax.ShapeDtypeStruct((seq, ATTN_WIDTH + POOL_WIDTH), BF16), jax.ShapeDtypeStruct((seq, POOL_WIDTH), F32),
                   jax.ShapeDtypeStruct((SUBLANES, D_MODEL), F32), jax.ShapeDtypeStruct((SUBLANES, POOL_WIDTH), F32),
                   jax.ShapeDtypeStruct((4, POOL_GC, POOL_GC), F32), jax.ShapeDtypeStruct((D_MODEL, D_MODEL), BF16)],
        scratch_shapes=[pltpu.VMEM((D_MODEL, D_MODEL), F32)],
        compiler_params=_params(),
    )(dx1, y0, g_post, w_out, gug, o, pooled, pool_w, pool_scale, mix)


def _attn_bwd(q, kv, do, sinks, parts, early):
    seq = q.shape[0]
    nb = seq // BLOCK

    def qblock(j):
        return jnp.minimum(j, nb - 1)

    n = len(parts)

    def body(*refs):
        sink_ref, q_ref, kvc_ref, kvp_ref, do_ref = refs[:5]
        early_ref = refs[5 + n]
        dq_ref, dkv_ref, dsink_ref = refs[6 + n:9 + n]
        all_ref = refs[9 + 2 * n]
        carry, dkv_acc = refs[10 + 2 * n:12 + 2 * n]
        scatter = _Scatter(refs[5:5 + n], refs[9 + n:9 + 2 * n], *refs[12 + 2 * n:14 + 2 * n])
        gather = lambda: _Gather([early_ref], [all_ref], *refs[14 + 2 * n:])
        j = pl.program_id(0)

        @pl.when(j == 0)
        def _():
            gather().begin()
            scatter.begin()
            dsink_ref[...] = jnp.zeros_like(dsink_ref)
            carry[...] = jnp.zeros_like(carry)

        @pl.when(j < nb)
        def _():
            valid, distf = _attn_mask(GROUP_ROWS, j == 0)
            qb = q_ref[...]
            dob = do_ref[...]
            kk = jnp.concatenate([kvp_ref[...], kvc_ref[...]], axis=0)
            lane = lax.broadcasted_iota(jnp.int32, (BLOCK, LANES), 1)
            dsink = jnp.zeros((BLOCK, LANES), F32)
            for kvh in range(Q_HEADS // GROUP):
                kh = kk[:, HEAD_DIM * kvh:HEAD_DIM * (kvh + 1)]
                vh = kk[:, KV_WIDTH + HEAD_DIM * kvh:KV_WIDTH + HEAD_DIM * (kvh + 1)]
                qg = _group_rows(qb, kvh)
                dog = _group_rows(dob, kvh)
                sink, slope = _group_columns(sink_ref, kvh)
                p, psink = _attn_probs(qg, kh, sink, slope, valid, distf)
                dp = _dot_nt(dog, vh)
                delta = jnp.sum(p * dp, axis=-1, keepdims=True)
                ds = (p * (dp - delta) * (HEAD_DIM ** -0.5)).astype(BF16)
                dsink_rows = -psink * delta
                dqg = _dot(ds, kh).astype(BF16)
                for i in range(GROUP):
                    h = GROUP * kvh + i
                    dsink = dsink + jnp.where(lane == h, dsink_rows[BLOCK * i:BLOCK * (i + 1), :], 0.0)
                    dq_ref[:, HEAD_DIM * h:HEAD_DIM * (h + 1)] = dqg[BLOCK * i:BLOCK * (i + 1), :]
                dkv_acc[:, HEAD_DIM * kvh:HEAD_DIM * (kvh + 1)] = _dot_tn(ds, qg)
                dkv_acc[:, KV_WIDTH + HEAD_DIM * kvh:KV_WIDTH + HEAD_DIM * (kvh + 1)] = _dot_tn(p.astype(BF16), dog)
            dsink_ref[...] += dsink

            @pl.when(j > 0)
            def _():
                dkv_ref[...] = (carry[...] + dkv_acc[:BLOCK, :]).astype(BF16)

            carry[...] = dkv_acc[BLOCK:, :]

        @pl.when(j == nb)
        def _():
            dkv_ref[...] = carry[...].astype(BF16)
            gather().relay()
            scatter.finish()
            gather().finish()

    hbm = pl.BlockSpec(memory_space=pl.ANY)
    out = pl.pallas_call(
        body, name="attn_bwd", grid=(nb + 1,),
        in_specs=[pl.BlockSpec(memory_space=pltpu.SMEM),
                  pl.BlockSpec((BLOCK, ATTN_WIDTH), lambda j: (qblock(j), 0)),
                  pl.BlockSpec((BLOCK, 2 * KV_WIDTH), lambda j: (qblock(j), 0)),
                  pl.BlockSpec((BLOCK, 2 * KV_WIDTH), lambda j: (jnp.maximum(qblock(j) - 1, 0), 0)),
                  pl.BlockSpec((BLOCK, ATTN_WIDTH), lambda j: (qblock(j), 0))] + [hbm] * (n + 1),
        out_specs=[pl.BlockSpec((BLOCK, ATTN_WIDTH), lambda j: (qblock(j), 0)),
                   pl.BlockSpec((BLOCK, 2 * KV_WIDTH), lambda j: (jnp.maximum(j - 1, 0), 0)),
                   _full((BLOCK, LANES))] + [hbm] * (n + 1),
        out_shape=[jax.ShapeDtypeStruct((seq, ATTN_WIDTH), BF16), jax.ShapeDtypeStruct((seq, 2 * KV_WIDTH), BF16),
                   jax.ShapeDtypeStruct((BLOCK, LANES), F32)]
        + [jax.ShapeDtypeStruct((N_DEV - 1, *p.shape[1:]), p.dtype) for p in parts]
        + [jax.ShapeDtypeStruct((N_DEV, *early.shape), early.dtype)],
        scratch_shapes=[pltpu.VMEM((BLOCK, 2 * KV_WIDTH), F32), pltpu.VMEM((2 * BLOCK, 2 * KV_WIDTH), F32)]
        + _Scatter.semaphores(n) + _Gather.semaphores(1),
        compiler_params=_params(),
    )(sinks, q, kv, kv, do, *parts, early)
    return out[:3], out[3:3 + n], out[3 + n]


def _bwd_in0(dpooled, dq, dkv, dgg, w_in, x, g_pre, dx1, ts):
    seq = x.shape[0]
    hb = ts // POOL_HALO
    last = seq // POOL_HALO - 1
    nt = seq // ts

    def body(dp_ref, dnext_ref, dq_ref, dkv_ref, dgg_ref, w_ref, x_ref, g_ref, dx1_ref,
             dproj_ref, gx_ref, dpre_ref, dbuf):
        i = pl.program_id(0)

        @pl.when(i == 0)
        def _():
            dpre_ref[...] = jnp.zeros_like(dpre_ref)

        dpool = dp_ref[...]
        dnext = jnp.where(i < nt - 1, dnext_ref[...], 0.0)
        u0 = ATTN_WIDTH + 2 * KV_WIDTH + ATTN_WIDTH
        for g, window in enumerate(POOL_WINDOWS):
            lanes = slice(POOL_GC * g, POOL_GC * (g + 1))
            dbuf[:ts, lanes] = dpool[:, lanes] / _pool_counts(i * ts, ts, window)
            dbuf[ts:, lanes] = dnext[:, lanes] / _pool_counts((i + 1) * ts, POOL_HALO, window)
        for g, window in enumerate(POOL_WINDOWS):
            lanes = slice(POOL_GC * g, POOL_GC * (g + 1))
            acc = dbuf[pl.ds(0, ts), lanes]
            for k in range(1, window):
                acc = acc + dbuf[pl.ds(k, ts), lanes]
            dproj_ref[:, u0 + POOL_GC * g:u0 + POOL_GC * (g + 1)] = (acc - dpool[:, lanes]).astype(BF16)
        dproj_ref[:, :ATTN_WIDTH] = dq_ref[...]
        dproj_ref[:, ATTN_WIDTH:ATTN_WIDTH + 2 * KV_WIDTH] = dkv_ref[...]
        dproj_ref[:, ATTN_WIDTH + 2 * KV_WIDTH:u0] = dgg_ref[:, :ATTN_WIDTH]
        dproj_ref[:, u0 + POOL_WIDTH:] = dgg_ref[:, ATTN_WIDTH:]
        dh = _dot(dproj_ref[...], w_ref[...])
        x = x_ref[...]
        r = lax.rsqrt(jnp.mean(x * x, axis=-1, keepdims=True) + EPS)
        dx, dpre = _rms_bwd(x, r, g_ref[...], dh)
        gx_ref[...] = dx1_ref[...] + dx
        dpre_ref[...] += _rows8(dpre)

    return pl.pallas_call(
        body, name="bwd_in0", grid=(nt,),
        in_specs=[_tile(ts, POOL_WIDTH),
                  pl.BlockSpec((POOL_HALO, POOL_WIDTH), lambda i: (jnp.minimum((i + 1) * hb, last), 0)),
                  _tile(ts, ATTN_WIDTH), _tile(ts, 2 * KV_WIDTH), _tile(ts, ATTN_WIDTH + POOL_WIDTH),
                  _full((EVEN_IN, D_MODEL)), _tile(ts, D_MODEL), _full((1, D_MODEL)), _tile(ts, D_MODEL)],
        out_specs=[_tile(ts, EVEN_IN), _tile(ts, D_MODEL), _full((SUBLANES, D_MODEL))],
        out_shape=[jax.ShapeDtypeStruct((seq, EVEN_IN), BF16), jax.ShapeDtypeStruct((seq, D_MODEL), F32),
                   jax.ShapeDtypeStruct((SUBLANES, D_MODEL), F32)],
        scratch_shapes=[pltpu.VMEM((ts + POOL_HALO, POOL_WIDTH), F32)],
        compiler_params=_params(),
    )(dpooled, dpooled, dq, dkv, dgg, w_in, x, g_pre, dx1)


def _matmul_tn(a, b, name, ts, tm):
    seq, m = a.shape
    n = b.shape[1]
    steps = seq // ts

    def body(a_ref, b_ref, o_ref, acc):
        s = pl.program_id(1)

        @pl.when(s == 0)
        def _():
            acc[...] = jnp.zeros_like(acc)

        acc[...] += _dot_tn(a_ref[...], b_ref[...])

        @pl.when(s == steps - 1)
        def _():
            o_ref[...] = acc[...].astype(BF16)

    return pl.pallas_call(
        body, name=name, grid=(m // tm, steps),
        in_specs=[pl.BlockSpec((ts, tm), lambda j, s: (s, j)), pl.BlockSpec((ts, n), lambda j, s: (s, 0))],
        out_specs=pl.BlockSpec((tm, n), lambda j, s: (j, 0)),
        out_shape=jax.ShapeDtypeStruct((m, n), BF16),
        scratch_shapes=[pltpu.VMEM((tm, n), F32)],
        compiler_params=pltpu.CompilerParams(dimension_semantics=("arbitrary", "arbitrary"), vmem_limit_bytes=VMEM_LIMIT),
    )(a, b)


def _adamw_math(w, g, m, v):
    m = ADAM_B1 * m + (1.0 - ADAM_B1) * g
    v = ADAM_B2 * v + (1.0 - ADAM_B2) * (g * g)
    m_hat = m / (1.0 - ADAM_B1 ** ADAM_STEP)
    v_hat = v / (1.0 - ADAM_B2 ** ADAM_STEP)
    delta = -ADAM_LR * (m_hat / (jnp.sqrt(v_hat) + ADAM_EPS) + ADAM_WD * w)
    return delta, m, v


def _adamw(ws, gs, ms, vs, name):
    n = len(ws)

    def body(*refs):
        ins, outs = refs[:4 * n], refs[4 * n:]
        for k in range(n):
            delta, m, v = _adamw_math(ins[k][...], ins[n + k][...], ins[2 * n + k][...], ins[3 * n + k][...])
            outs[k][...] = delta
            outs[n + k][...] = m
            outs[2 * n + k][...] = v

    shapes = [jax.ShapeDtypeStruct(w.shape, F32) for w in ws]
    out = pl.pallas_call(body, name=name, out_shape=shapes * 3,
                         compiler_params=pltpu.CompilerParams(vmem_limit_bytes=VMEM_LIMIT))(*ws, *gs, *ms, *vs)
    return out[:n], out[n:2 * n], out[2 * n:]


TS_MATMUL, TS_CONV_FWD, TS_CONV_BWD, TS_GRAD = 512, 512, 256, 1024


def kernel(x, pre_norm, post_norm, a_w_in, a_sinks, b_pool_w, b_pool_scale, ab_w_out, c_w_in, c_dw_w, c_dw_b, c_ln_g, c_ln_b, c_w_out, loss_target, m_pre_norm, m_post_norm, m_a_w_in, m_a_sinks, m_b_pool_w, m_b_pool_scale, m_ab_w_out, m_c_w_in, m_c_dw_w, m_c_dw_b, m_c_ln_g, m_c_ln_b, m_c_w_out, v_pre_norm, v_post_norm, v_a_w_in, v_a_sinks, v_b_pool_w, v_b_pool_scale, v_ab_w_out, v_c_w_in, v_c_dw_w, v_c_dw_b, v_c_ln_g, v_c_ln_b, v_c_w_out):
    seq = x.shape[1]
    ts_big, ts_grad = min(TS_MATMUL, seq), min(TS_GRAD, seq)
    x2d = x[0]
    target = loss_target[0]
    ch = c_dw_b.shape[1]

    whole = lambda g: g.reshape(-1, D_MODEL)
    vec_rows = 40
    vecs = jnp.concatenate([c_dw_w[0, :, 0, :], c_dw_b, c_ln_g, c_ln_b, jnp.zeros((vec_rows - CONV_K - 3, ch), F32)], axis=0)
    w_in0t, vg = _all_gather([a_w_in[0].T.astype(BF16), vecs], "gather_first")
    w_in0t = whole(w_in0t)
    vg = vg.transpose(1, 0, 2).reshape(vec_rows, D_MODEL)
    dw_w = vg[:CONV_HALO]
    dw_b, ln_g, ln_b = vg[CONV_K:CONV_K + 1], vg[CONV_K + 1:CONV_K + 2], vg[CONV_K + 2:CONV_K + 3]

    sinks = a_sinks[0]
    (h0, q, kv, gug), (w_out0,) = _fwd_in0(x2d, pre_norm[0:1], w_in0t, [ab_w_out[0].astype(BF16)], ts_big)
    o, (w_in1t, w_out1) = _attn_fwd(q, kv, sinks, [c_w_in[0].T.astype(BF16), c_w_out[0].astype(BF16)])
    w_out0, w_in1t, w_out1 = whole(w_out0), whole(w_in1t), whole(w_out1)
    mix0, pooled, y0, x1 = _fwd_out0(gug, o, b_pool_w[0], b_pool_scale, w_out0, post_norm[0:1], x2d, ts_big)
    h1, proj1, glu = _fwd_in1(x1, pre_norm[1:2], w_in1t, ts_big)
    ymix1, dy1, dx2, dcf, dgate, acc1 = _fwd_out1(glu, proj1, dw_w, dw_b, ln_g, ln_b, w_out1, post_norm[1:2], x1, target, min(TS_CONV_FWD, seq))

    dproj1, dx1, ddw_w, dpre1 = _bwd_in1(dcf, glu, proj1, dgate, dw_w, w_in1t, x1, pre_norm[1:2], dx2, min(TS_CONV_BWD, seq))
    g_in1t = _matmul_tn(dproj1, h1, "grad_w_in1", ts_grad, 1024)
    g_out1 = _matmul_tn(ymix1, dy1, "grad_w_out1", ts_grad, 1024)
    do, dgg, dpooled, dpost0, dscale, dpool_w, g_out0 = _bwd_out0(dx1, y0, post_norm[0:1], w_out0, gug, o, pooled, b_pool_w[0], b_pool_scale, mix0, ts_big)
    slabs = lambda g: g.reshape(N_DEV, -1, D_MODEL)
    me = 4 * lax.axis_index("x") + 2 * lax.axis_index("y") + lax.axis_index("c")
    early = [slabs(g_in1t), slabs(g_out1), slabs(g_out0)]
    row = lambda a: jnp.sum(a, axis=0, keepdims=True)
    lanes8 = lambda a: row(a).reshape(-1, LANES)
    loss_row = jnp.pad(jnp.sum(acc1[ACC_LOSS]).reshape(1, 1), ((0, 0), (0, LANES - 1)))
    done = jnp.concatenate([lanes8(dpre1), lanes8(dpost0), lanes8(acc1[ACC_POST]), lanes8(dscale),
                            loss_row, jnp.zeros((3, LANES), F32), dpool_w.reshape(4 * POOL_GC, LANES)], axis=0)
    (dq, dkv, dsink), arrived, done_all = _attn_bwd(q, kv, do, sinks, early, done)
    g_in1t, g_c_w_out, g_ab_w_out = [_sum_slabs(lax.dynamic_index_in_dim(p, me, keepdims=False), r, name)
                                     for p, r, name in zip(early, arrived, ("sum_w_in1", "sum_w_out1", "sum_w_out0"))]
    dproj0, grad_x, dpre0 = _bwd_in0(dpooled, dq, dkv, dgg, w_in0t, x2d, pre_norm[0:1], dx1, ts_big)
    g_in0t = _matmul_tn(dproj0, h0, "grad_w_in0", ts_grad, 768)

    vec_g = jnp.concatenate([jnp.sum(ddw_w[:CONV_K], axis=1), row(acc1[ACC_DW_B]), row(acc1[ACC_LN_G]), row(acc1[ACC_LN_B]),
                             jnp.zeros((vec_rows - CONV_K - 3, D_MODEL), F32)], axis=0)
    last = jnp.concatenate([lanes8(dpre0), row(dsink), jnp.zeros((SUBLANES - 1, LANES), F32)], axis=0)
    g_in0t, vec_g, last, rep = _final_reduce(slabs(g_in0t), vec_g.reshape(vec_rows, N_DEV, ch).transpose(1, 0, 2),
                                             last, done_all, "final_reduce")
    g_a_w_in, g_c_w_in = g_in0t.T, g_in1t.T
    g_dw_w, g_dw_b, g_ln_g, g_ln_b = vec_g[:CONV_K], vec_g[CONV_K:CONV_K + 1], vec_g[CONV_K + 1:CONV_K + 2], vec_g[CONV_K + 2:CONV_K + 3]
    g_pre = jnp.concatenate([last[:8], rep[:8]], axis=0).reshape(2, D_MODEL)
    g_sinks = last[8:9, :Q_HEADS]
    g_post = rep[8:24].reshape(2, D_MODEL)
    g_scale = rep[24:28].reshape(1, POOL_WIDTH)
    loss = (0.5 / D_MODEL) * rep[28, 0]
    g_pool_w = rep[32:]

    grads = [g_pre, g_post, g_a_w_in, g_sinks, g_pool_w, g_scale, g_ab_w_out, g_c_w_in, g_dw_w, g_dw_b, g_ln_g, g_ln_b, g_c_w_out]
    weights = [pre_norm, post_norm, a_w_in, a_sinks, b_pool_w, b_pool_scale, ab_w_out, c_w_in, c_dw_w, c_dw_b, c_ln_g, c_ln_b, c_w_out]
    m_in = [m_pre_norm, m_post_norm, m_a_w_in, m_a_sinks, m_b_pool_w, m_b_pool_scale, m_ab_w_out, m_c_w_in, m_c_dw_w, m_c_dw_b, m_c_ln_g, m_c_ln_b, m_c_w_out]
    v_in = [v_pre_norm, v_post_norm, v_a_w_in, v_a_sinks, v_b_pool_w, v_b_pool_scale, v_ab_w_out, v_c_w_in, v_c_dw_w, v_c_dw_b, v_c_ln_g, v_c_ln_b, v_c_w_out]
    flat = lambda arrs: [a.reshape(g.shape) for a, g in zip(arrs, grads)]
    big = (2, 6, 7, 12)
    small = tuple(k for k in range(len(grads)) if k not in big)
    pick = lambda arrs, idx: [arrs[k] for k in idx]
    deltas, new_m, new_v = [None] * 13, [None] * 13, [None] * 13
    for idx, name in ((big, "adamw_matrices"), (small, "adamw_vectors")):
        d, m, v = _adamw(pick(flat(weights), idx), pick(grads, idx), pick(flat(m_in), idx), pick(flat(v_in), idx), name)
        for k, dk, mk, vk in zip(idx, d, m, v):
            deltas[k], new_m[k], new_v[k] = dk, mk, vk
    shaped = lambda arrs: [a.reshape(w.shape) for a, w in zip(arrs, weights)]
    return (loss, grad_x[None], *shaped(grads), *shaped(deltas), *shaped(new_m), *shaped(new_v))
```

```python
import jax
import jax.numpy as jnp
from jax import lax
from jax.experimental import pallas as pl
from jax.experimental.pallas import tpu as pltpu

F32 = jnp.float32
BF16 = jnp.bfloat16
MESH = pl.DeviceIdType.MESH
N_DEV = 8

D_MODEL = 1024
HEAD_DIM = 64
Q_HEADS = 8
GROUP = 4
ATTN_WIDTH = 512
KV_WIDTH = 128
BLOCK = 128
POOL_WIDTH = 512
POOL_WINDOWS = (2, 4, 8, 16)
POOL_GC = 128
POOL_HALO = 16
EVEN_IN = 2304
CONV_K = 31
CONV_HALO = 32
ODD_IN = 3072
EPS = 1e-6
NEG = -1e30
SLOPES = tuple(2.0 ** (-8.0 * (h + 1) / Q_HEADS) for h in range(Q_HEADS))

ADAM_LR = 0.001
ADAM_B1 = 0.9
ADAM_B2 = 0.999
ADAM_EPS = 1e-08
ADAM_WD = 0.01
ADAM_STEP = 10

SUBLANES = 8
LANES = 128
VMEM_LIMIT = 56 * 1024 * 1024

NT = (((1,), (1,)), ((), ()))
TN = (((0,), (0,)), ((), ()))


def _params(**kw):
    return pltpu.CompilerParams(dimension_semantics=("arbitrary",), vmem_limit_bytes=VMEM_LIMIT, **kw)


def _dot(a, b):
    return jnp.dot(a, b, preferred_element_type=F32)


def _dot_nt(a, b):
    return lax.dot_general(a, b, NT, preferred_element_type=F32)


def _dot_tn(a, b):
    return lax.dot_general(a, b, TN, preferred_element_type=F32)


def _sigmoid(v):
    return 1.0 / (1.0 + jnp.exp(-v))


def _rows8(v):
    r, c = v.shape
    return jnp.sum(v.reshape(r // SUBLANES, SUBLANES, c), axis=0)


def _rms_fwd(v, g):
    r = lax.rsqrt(jnp.mean(v * v, axis=-1, keepdims=True) + EPS)
    return v * r * g, r


def _rms_bwd(v, r, g, dout):
    gd = dout * g
    dv = r * gd - v * (r * r * r) * jnp.mean(v * gd, axis=-1, keepdims=True)
    return dv, dout * (v * r)


def _full(shape):
    return pl.BlockSpec(shape, lambda i: (0,) * len(shape))


def _resident(shape):
    return pl.BlockSpec(shape, lambda i: (0,) * len(shape), pipeline_mode=pl.Buffered(1))


def _accumulate_tn(acc, a, b, first_step):
    @pl.when(first_step)
    def _():
        acc[...] = jnp.zeros_like(acc)

    acc[...] += _dot_tn(a, b)


def _tile(ts, cols, col_block=0):
    return pl.BlockSpec((ts, cols), lambda i: (i, col_block))


def _position():
    return lax.axis_index("x"), lax.axis_index("y"), lax.axis_index("c")


class _Gather:
    def __init__(self, x_refs, out_refs, send_sems, recv_sems, local_sems):
        self.x_refs, self.out_refs = x_refs, out_refs
        self.send_sems, self.recv_sems, self.local_sems = send_sems, recv_sems, local_sems
        self.n = len(x_refs)
        x, y, c = _position()
        self.c = c
        self.me, self.sibling = (x, y, c), (x, y, 1 - c)
        self.chips = [(1 - x, y), (x, 1 - y), (1 - x, 1 - y)]

    def copy(self, k, j, owner, to, src=None):
        slab = self.out_refs[k].at[4 * owner[0] + 2 * owner[1] + owner[2]]
        return pltpu.make_async_remote_copy(
            src_ref=slab if src is None else src, dst_ref=slab, send_sem=self.send_sems.at[7 * k + j],
            recv_sem=self.recv_sems.at[7 * k + j], device_id=to, device_id_type=MESH)

    def mine(self, k):
        return pltpu.make_async_copy(self.x_refs[k], self.out_refs[k].at[4 * self.me[0] + 2 * self.me[1] + self.c],
                                     self.local_sems.at[k])

    def first(self, k):
        out = [self.copy(k, 0, self.me, self.sibling, src=self.x_refs[k])]
        return out + [self.copy(k, 1 + j, self.me, (*chip, self.c), src=self.x_refs[k]) for j, chip in enumerate(self.chips)]

    def begin(self):
        for k in range(self.n):
            self.mine(k).start()
            for cp in self.first(k):
                cp.start()

    def passed(self, k):
        return [self.copy(k, 4 + j, (*chip, self.c), self.sibling) for j, chip in enumerate(self.chips)]

    def relay(self):
        for k in range(self.n):
            for j, chip in enumerate(self.chips):
                self.copy(k, 1 + j, (*chip, self.c), self.me).wait_recv()
                self.passed(k)[j].start()

    def finish(self):
        for k in range(self.n):
            self.copy(k, 0, self.sibling, self.me).wait_recv()
            for j, chip in enumerate(self.chips):
                self.copy(k, 4 + j, (*chip, 1 - self.c), self.me).wait_recv()
        for k in range(self.n):
            for cp in self.first(k) + self.passed(k):
                cp.wait_send()
            self.mine(k).wait()

    @staticmethod
    def semaphores(n):
        return [pltpu.SemaphoreType.DMA((7 * n,)), pltpu.SemaphoreType.DMA((7 * n,)), pltpu.SemaphoreType.DMA((n,))]


def RELAY_AT(steps):
    return steps * 3 // 4


def _all_gather(blocks, name):
    n = len(blocks)

    def body(*refs):
        gather = _Gather(refs[:n], refs[n:2 * n], *refs[2 * n:])
        gather.begin()
        gather.relay()
        gather.finish()

    return pl.pallas_call(
        body, name=name,
        out_shape=[jax.ShapeDtypeStruct((N_DEV, *b.shape), b.dtype) for b in blocks],
        in_specs=[pl.BlockSpec(memory_space=pltpu.VMEM)] * n,
        out_specs=[pl.BlockSpec(memory_space=pltpu.VMEM)] * n,
        scratch_shapes=_Gather.semaphores(n),
        compiler_params=pltpu.CompilerParams(vmem_limit_bytes=VMEM_LIMIT),
    )(*blocks)


class _Scatter:
    def __init__(self, part_refs, recv_refs, send_sems, recv_sems):
        self.part_refs, self.recv_refs, self.send_sems, self.recv_sems = part_refs, recv_refs, send_sems, recv_sems
        self.n = len(part_refs)

    def copies(self):
        x, y, c = _position()
        me = 4 * x + 2 * y + c
        out = []
        for k in range(self.n):
            for j in range(N_DEV - 1):
                d = j + 1
                out.append(pltpu.make_async_remote_copy(
                    src_ref=self.part_refs[k].at[me ^ d], dst_ref=self.recv_refs[k].at[j],
                    send_sem=self.send_sems.at[7 * k + j], recv_sem=self.recv_sems.at[7 * k + j],
                    device_id=(x ^ (d >> 2), y ^ ((d >> 1) & 1), c ^ (d & 1)), device_id_type=MESH))
        return out

    def begin(self):
        for cp in self.copies():
            cp.start()

    def finish(self):
        for cp in self.copies():
            cp.wait_recv()
        for cp in self.copies():
            cp.wait_send()

    @staticmethod
    def semaphores(n):
        return [pltpu.SemaphoreType.DMA((7 * n,)), pltpu.SemaphoreType.DMA((7 * n,))]


def _sum_slabs(own, recv, name):
    n = len(own)

    def body(*refs):
        for k in range(n):
            acc = refs[k][...].astype(F32)
            for j in range(N_DEV - 1):
                acc = acc + refs[n + k][j].astype(F32)
            refs[2 * n + k][...] = acc

    return pl.pallas_call(body, name=name, out_shape=[jax.ShapeDtypeStruct(o.shape, F32) for o in own])(*own, *recv)


N_CHIPS = 4


def _final_reduce(parts, vec_parts, rep, early_all, name):
    _, rows, cols = parts.shape
    vrows, prows = vec_parts.shape[1], rep.shape[0]
    peers = N_CHIPS - 1

    def body(parts_ref, vec_ref, rep_ref, early_ref, out_ref, vec_out, rep_out, early_out, recv_a, own_a, mid, recv_b, vec_recv, rep_all,
             a_send, a_recv, a_local, b_send, b_recv, v_send, v_recv, r_send, r_recv):
        x, y, c = _position()
        chip = 2 * x + y
        me = 2 * chip + c
        everyone = [(d, (x ^ (d >> 2), y ^ ((d >> 1) & 1), c ^ (d & 1))) for d in range(1, N_DEV)]

        stage1, loads = [], []
        for j in range(N_CHIPS):
            stage1.append(pltpu.make_async_remote_copy(
                src_ref=parts_ref.at[2 * j + 1 - c], dst_ref=recv_a.at[j], send_sem=a_send.at[j], recv_sem=a_recv.at[j],
                device_id=(x, y, 1 - c), device_id_type=MESH))
            loads.append(pltpu.make_async_copy(parts_ref.at[2 * j + c], own_a.at[j], a_local.at[j]))
            stage1[-1].start()
            loads[-1].start()
        small = []
        for d, to in everyone:
            small.append(pltpu.make_async_remote_copy(
                src_ref=vec_ref.at[me ^ d], dst_ref=vec_recv.at[d - 1], send_sem=v_send.at[d - 1], recv_sem=v_recv.at[d - 1],
                device_id=to, device_id_type=MESH))
            small.append(pltpu.make_async_remote_copy(
                src_ref=rep_ref, dst_ref=rep_all.at[me], send_sem=r_send.at[d - 1], recv_sem=r_recv.at[d - 1],
                device_id=to, device_id_type=MESH))
        for cp in small:
            cp.start()
        rep_all[me] = rep_ref[...]

        for j in range(N_CHIPS):
            loads[j].wait()
            stage1[j].wait_recv()
            mid[j] = (own_a[j].astype(F32) + recv_a[j].astype(F32)).astype(BF16)
        stage2 = []
        for d in range(1, N_CHIPS):
            stage2.append(pltpu.make_async_remote_copy(
                src_ref=mid.at[chip ^ d], dst_ref=recv_b.at[d - 1], send_sem=b_send.at[d - 1], recv_sem=b_recv.at[d - 1],
                device_id=(x ^ (d >> 1), y ^ (d & 1), c), device_id_type=MESH))
            stage2[-1].start()

        for cp in small:
            cp.wait_recv()
        vec_sum = vec_ref[me]
        for j in range(N_DEV - 1):
            vec_sum = vec_sum + vec_recv[j]
        vec_out[...] = vec_sum
        rep_sum = rep_all[0]
        for d in range(1, N_DEV):
            rep_sum = rep_sum + rep_all[d]
        rep_out[...] = rep_sum
        early_sum = early_ref[0]
        for d in range(1, N_DEV):
            early_sum = early_sum + early_ref[d]
        early_out[...] = early_sum

        acc = mid[chip].astype(F32)
        for d in range(peers):
            stage2[d].wait_recv()
            acc = acc + recv_b[d].astype(F32)
        out_ref[...] = acc
        for cp in stage1 + stage2 + small:
            cp.wait_send()

    vmem = pl.BlockSpec(memory_space=pltpu.VMEM)
    dma = pltpu.SemaphoreType.DMA
    return pl.pallas_call(
        body, name=name,
        out_shape=[jax.ShapeDtypeStruct((rows, cols), F32), jax.ShapeDtypeStruct((vrows, LANES), F32),
                   jax.ShapeDtypeStruct((prows, LANES), F32), jax.ShapeDtypeStruct(early_all.shape[1:], F32)],
        in_specs=[pl.BlockSpec(memory_space=pl.ANY), vmem, vmem, vmem],
        out_specs=[vmem, vmem, vmem, vmem],
        scratch_shapes=[pltpu.VMEM((N_CHIPS, rows, cols), BF16), pltpu.VMEM((N_CHIPS, rows, cols), BF16),
                        pltpu.VMEM((N_CHIPS, rows, cols), BF16), pltpu.VMEM((peers, rows, cols), BF16),
                        pltpu.VMEM((N_DEV - 1, vrows, LANES), F32), pltpu.VMEM((N_DEV, prows, LANES), F32),
                        dma((N_CHIPS,)), dma((N_CHIPS,)), dma((N_CHIPS,)), dma((peers,)), dma((peers,)),
                        dma((N_DEV - 1,)), dma((N_DEV - 1,)), dma((N_DEV - 1,)), dma((N_DEV - 1,))],
        compiler_params=pltpu.CompilerParams(vmem_limit_bytes=VMEM_LIMIT),
    )(parts, vec_parts, rep, early_all)


def _fwd_in0(x, g, w, shards, ts):
    seq = x.shape[0]
    n = len(shards)
    steps = seq // ts

    def body(*refs):
        x_ref, g_ref, w_ref = refs[:3]
        h_ref, q_ref, kv_ref, gug_ref = refs[3 + n:7 + n]
        gather = lambda: _Gather(refs[3:3 + n], refs[7 + n:7 + 2 * n], *refs[7 + 2 * n:])
        i = pl.program_id(0)

        @pl.when(i == 0)
        def _():
            gather().begin()

        h, _ = _rms_fwd(x_ref[...], g_ref[...])
        h = h.astype(BF16)
        h_ref[...] = h
        proj = _dot_nt(h, w_ref[...])
        q_ref[...] = proj[:, :ATTN_WIDTH].astype(BF16)
        kv_ref[...] = proj[:, ATTN_WIDTH:ATTN_WIDTH + 2 * KV_WIDTH].astype(BF16)
        gug_ref[...] = proj[:, ATTN_WIDTH + 2 * KV_WIDTH:].astype(BF16)

        @pl.when(i == RELAY_AT(steps))
        def _():
            gather().relay()

        @pl.when(i == steps - 1)
        def _():
            gather().finish()

    hbm = pl.BlockSpec(memory_space=pl.ANY)
    out = pl.pallas_call(
        body, name="fwd_in0", grid=(steps,),
        in_specs=[_tile(ts, D_MODEL), _full((1, D_MODEL)), _full((EVEN_IN, D_MODEL))] + [hbm] * n,
        out_specs=[_tile(ts, D_MODEL), _tile(ts, ATTN_WIDTH), _tile(ts, 2 * KV_WIDTH), _tile(ts, 3 * POOL_WIDTH)] + [hbm] * n,
        out_shape=[jax.ShapeDtypeStruct((seq, D_MODEL), BF16), jax.ShapeDtypeStruct((seq, ATTN_WIDTH), BF16),
                   jax.ShapeDtypeStruct((seq, 2 * KV_WIDTH), BF16), jax.ShapeDtypeStruct((seq, 3 * POOL_WIDTH), BF16)]
        + [jax.ShapeDtypeStruct((N_DEV, *b.shape), b.dtype) for b in shards],
        scratch_shapes=_Gather.semaphores(n),
        compiler_params=_params(),
    )(x, g, w, *shards)
    return out[:4], out[4:]


GROUP_ROWS = GROUP * BLOCK


def _attn_mask(rows, first_block):
    row = lax.broadcasted_iota(jnp.int32, (rows, 2 * BLOCK), 0) & (BLOCK - 1)
    col = lax.broadcasted_iota(jnp.int32, (rows, 2 * BLOCK), 1)
    dist = row + BLOCK - col
    valid = (dist >= 0) & (dist < BLOCK) & ((col >= BLOCK) | jnp.logical_not(first_block))
    return valid, dist.astype(F32)


def _group_rows(block, kvh):
    return jnp.concatenate([block[:, HEAD_DIM * h:HEAD_DIM * (h + 1)] for h in range(GROUP * kvh, GROUP * (kvh + 1))], axis=0)


def _group_columns(sink_ref, kvh):
    head = lax.broadcasted_iota(jnp.int32, (GROUP_ROWS, 1), 0) // BLOCK
    sink = jnp.zeros((GROUP_ROWS, 1), F32)
    slope = jnp.zeros((GROUP_ROWS, 1), F32)
    for i in range(GROUP):
        sink = jnp.where(head == i, sink_ref[GROUP * kvh + i], sink)
        slope = jnp.where(head == i, SLOPES[GROUP * kvh + i], slope)
    return sink, slope


def _attn_probs(qh, kh, sink, slope, valid, distf):
    s = _dot_nt(qh, kh) * (HEAD_DIM ** -0.5)
    s = jnp.where(valid, s - slope * distf, NEG)
    mx = jnp.maximum(jnp.max(s, axis=-1, keepdims=True), sink)
    e = jnp.exp(s - mx)
    es = jnp.exp(sink - mx)
    den = jnp.sum(e, axis=-1, keepdims=True) + es
    return e / den, es / den


def _attn_fwd(q, kv, sinks, shards):
    seq = q.shape[0]
    nb = seq // BLOCK
    ns = len(shards)

    def body(*refs):
        sink_ref, q_ref, kvc_ref, kvp_ref = refs[:4]
        o_ref = refs[4 + ns]
        gather = lambda: _Gather(refs[4:4 + ns], refs[5 + ns:5 + 2 * ns], *refs[5 + 2 * ns:])
        n = pl.program_id(0)

        @pl.when(n == 0)
        def _():
            gather().begin()

        valid, distf = _attn_mask(BLOCK, n == 0)
        qb = q_ref[...]
        kk = jnp.concatenate([kvp_ref[...], kvc_ref[...]], axis=0)
        for h in range(Q_HEADS):
            kvh = h // GROUP
            qh = qb[:, HEAD_DIM * h:HEAD_DIM * (h + 1)]
            kh = kk[:, HEAD_DIM * kvh:HEAD_DIM * (kvh + 1)]
            vh = kk[:, KV_WIDTH + HEAD_DIM * kvh:KV_WIDTH + HEAD_DIM * (kvh + 1)]
            p, _ = _attn_probs(qh, kh, sink_ref[h], SLOPES[h], valid, distf)
            o_ref[:, HEAD_DIM * h:HEAD_DIM * (h + 1)] = _dot(p.astype(BF16), vh)

        @pl.when(n == RELAY_AT(nb))
        def _():
            gather().relay()

        @pl.when(n == nb - 1)
        def _():
            gather().finish()

    hbm = pl.BlockSpec(memory_space=pl.ANY)
    out = pl.pallas_call(
        body, name="attn_fwd", grid=(nb,),
        in_specs=[pl.BlockSpec(memory_space=pltpu.SMEM),
                  pl.BlockSpec((BLOCK, ATTN_WIDTH), lambda n: (n, 0)),
                  pl.BlockSpec((BLOCK, 2 * KV_WIDTH), lambda n: (n, 0)),
                  pl.BlockSpec((BLOCK, 2 * KV_WIDTH), lambda n: (jnp.maximum(n - 1, 0), 0))] + [hbm] * ns,
        out_specs=[pl.BlockSpec((BLOCK, ATTN_WIDTH), lambda n: (n, 0))] + [hbm] * ns,
        out_shape=[jax.ShapeDtypeStruct((seq, ATTN_WIDTH), F32)]
        + [jax.ShapeDtypeStruct((N_DEV, *b.shape), b.dtype) for b in shards],
        scratch_shapes=_Gather.semaphores(ns),
        compiler_params=_params(),
    )(sinks, q, kv, kv, *shards)
    return out[0], out[1:]


def _pool_counts(first_row, rows, window):
    t = first_row + lax.broadcasted_iota(jnp.int32, (rows, 1), 0)
    return jnp.minimum(t + 1, window).astype(F32)


def _fwd_out0(gug, o, pool_w, pool_scale, w_out, g_post, x, ts):
    seq = x.shape[0]
    hb = ts // POOL_HALO

    def body(gug_ref, halo_ref, o_ref, pw_ref, ps_ref, w_ref, g_ref, x_ref, mix_ref, pooled_ref, y_ref, x1_ref, ubuf):
        i = pl.program_id(0)
        ga = gug_ref[:, :ATTN_WIDTH].astype(F32)
        u = gug_ref[:, ATTN_WIDTH:ATTN_WIDTH + POOL_WIDTH].astype(F32)
        gb = gug_ref[:, ATTN_WIDTH + POOL_WIDTH:].astype(F32)
        mix_ref[:, :ATTN_WIDTH] = (o_ref[...] * (ga * _sigmoid(ga))).astype(BF16)
        ubuf[:POOL_HALO, :] = jnp.where(i > 0, halo_ref[...].astype(F32), 0.0)
        ubuf[POOL_HALO:, :] = u
        silu_gb = gb * _sigmoid(gb)
        for g, window in enumerate(POOL_WINDOWS):
            lanes = slice(POOL_GC * g, POOL_GC * (g + 1))
            acc = ubuf[pl.ds(POOL_HALO, ts), lanes]
            for k in range(1, window):
                acc = acc + ubuf[pl.ds(POOL_HALO - k, ts), lanes]
            pooled = (acc / _pool_counts(i * ts, ts, window) - u[:, lanes]).astype(BF16)
            pooled_ref[:, lanes] = pooled
            ypool = _dot(pooled, pw_ref[g].astype(BF16)) * ps_ref[:, lanes]
            mix_ref[:, ATTN_WIDTH + POOL_GC * g:ATTN_WIDTH + POOL_GC * (g + 1)] = (ypool * silu_gb[:, lanes]).astype(BF16)
        y = _dot(mix_ref[...], w_ref[...])
        y_ref[...] = y
        yn, _ = _rms_fwd(y, g_ref[...])
        x1_ref[...] = x_ref[...] + yn

    return pl.pallas_call(
        body, name="fwd_out0", grid=(seq // ts,),
        in_specs=[_tile(ts, 3 * POOL_WIDTH),
                  pl.BlockSpec((POOL_HALO, POOL_WIDTH), lambda i: (jnp.maximum(i * hb - 1, 0), 1)),
                  _tile(ts, ATTN_WIDTH), _full((4, POOL_GC, POOL_GC)), _full((1, POOL_WIDTH)),
                  _full((D_MODEL, D_MODEL)), _full((1, D_MODEL)), _tile(ts, D_MODEL)],
        out_specs=[_tile(ts, D_MODEL), _tile(ts, POOL_WIDTH), _tile(ts, D_MODEL), _tile(ts, D_MODEL)],
        out_shape=[jax.ShapeDtypeStruct((seq, D_MODEL), BF16), jax.ShapeDtypeStruct((seq, POOL_WIDTH), BF16),
                   jax.ShapeDtypeStruct((seq, D_MODEL), F32), jax.ShapeDtypeStruct((seq, D_MODEL), F32)],
        scratch_shapes=[pltpu.VMEM((ts + POOL_HALO, POOL_WIDTH), F32)],
        compiler_params=_params(),
    )(gug, gug, o, pool_w, pool_scale, w_out, g_post, x)


def _fwd_in1(x1, g, w, ts):
    seq = x1.shape[0]

    def body(x_ref, g_ref, w_ref, h_ref, proj_ref, glu_ref):
        h, _ = _rms_fwd(x_ref[...], g_ref[...])
        h = h.astype(BF16)
        h_ref[...] = h
        proj = _dot_nt(h, w_ref[...])
        proj_ref[...] = proj.astype(BF16)
        glu_ref[...] = proj[:, :D_MODEL] * _sigmoid(proj[:, D_MODEL:2 * D_MODEL])

    return pl.pallas_call(
        body, name="fwd_in1", grid=(seq // ts,),
        in_specs=[_tile(ts, D_MODEL), _full((1, D_MODEL)), _full((ODD_IN, D_MODEL))],
        out_specs=[_tile(ts, D_MODEL), _tile(ts, ODD_IN), _tile(ts, D_MODEL)],
        out_shape=[jax.ShapeDtypeStruct((seq, D_MODEL), BF16), jax.ShapeDtypeStruct((seq, ODD_IN), BF16),
                   jax.ShapeDtypeStruct((seq, D_MODEL), F32)],
        compiler_params=_params(),
    )(x1, g, w)


ACC_LOSS, ACC_POST, ACC_LN_G, ACC_LN_B, ACC_DW_B = range(5)
DDW_ROWS = 88
CONV_FIRST = CONV_HALO - CONV_K + 1


def _fwd_tap(offset):
    return offset - CONV_FIRST if CONV_FIRST <= offset <= CONV_HALO else None


def _bwd_tap(offset):
    return CONV_K - 1 - offset if offset < CONV_K else None


def _conv_taps(w_ref, buf_ref, ts, lanes, tap_of_offset):
    out = None
    for b in range(SUBLANES):
        rows = ts if b == 0 else ts + SUBLANES
        part = None
        for a in range(CONV_HALO // SUBLANES + 1):
            k = tap_of_offset(SUBLANES * a + b)
            if k is None:
                continue
            term = w_ref[k:k + 1, lanes] * buf_ref[pl.ds(SUBLANES * a, rows), lanes]
            part = term if part is None else part + term
        if part is None:
            continue
        if b:
            part = part[b:b + ts, :]
        out = part if out is None else out + part
    return out


def _fwd_out1(glu, proj, dw_w, dw_b, ln_g, ln_b, w_out, g_post, x1, target, ts):
    seq = x1.shape[0]
    hb = ts // CONV_HALO

    def body(glu_ref, halo_ref, gate_ref, dww_ref, dwb_ref, lng_ref, lnb_ref, w_ref, g_ref, x1_ref, t_ref,
             ymix_ref, dy_ref, dx2_ref, dcf_ref, dgate_ref, acc_ref, gbuf):
        i = pl.program_id(0)

        @pl.when(i == 0)
        def _():
            acc_ref[...] = jnp.zeros_like(acc_ref)

        gbuf[:CONV_HALO, :] = jnp.where(i > 0, halo_ref[...], 0.0)
        gbuf[CONV_HALO:, :] = glu_ref[...]
        for lb in range(D_MODEL // LANES):
            lanes = slice(LANES * lb, LANES * (lb + 1))
            dcf_ref[:, lanes] = _conv_taps(dww_ref, gbuf, ts, lanes, _fwd_tap)
        cf = dcf_ref[...] + dwb_ref[...]
        mu = jnp.mean(cf, axis=-1, keepdims=True)
        cen = cf - mu
        rs = lax.rsqrt(jnp.mean(cen * cen, axis=-1, keepdims=True) + EPS)
        xhat = cen * rs
        cn = xhat * lng_ref[...] + lnb_ref[...]
        gate = gate_ref[...].astype(F32)
        sg = _sigmoid(gate)
        sc = _sigmoid(cn)
        silu_gate = gate * sg
        silu_cn = cn * sc
        ymix = (silu_cn * silu_gate).astype(BF16)
        ymix_ref[...] = ymix
        y = _dot(ymix, w_ref[...])
        yn, r = _rms_fwd(y, g_ref[...])
        err = (x1_ref[...] + yn) - t_ref[...]
        acc_ref[ACC_LOSS] += _rows8(err * err)
        dx2 = err * (1.0 / D_MODEL)
        dx2_ref[...] = dx2
        dy, dpost = _rms_bwd(y, r, g_ref[...], dx2)
        acc_ref[ACC_POST] += _rows8(dpost)
        dy = dy.astype(BF16)
        dy_ref[...] = dy
        dymix = _dot_nt(dy, w_ref[...])
        dgate_ref[...] = (dymix * silu_cn * (sg * (1.0 + gate * (1.0 - sg)))).astype(BF16)
        dcn = dymix * silu_gate * (sc * (1.0 + cn * (1.0 - sc)))
        acc_ref[ACC_LN_G] += _rows8(dcn * xhat)
        acc_ref[ACC_LN_B] += _rows8(dcn)
        dxhat = dcn * lng_ref[...]
        dcf = rs * (dxhat - jnp.mean(dxhat, axis=-1, keepdims=True)
                    - xhat * jnp.mean(dxhat * xhat, axis=-1, keepdims=True))
        acc_ref[ACC_DW_B] += _rows8(dcf)
        dcf_ref[...] = dcf

    return pl.pallas_call(
        body, name="fwd_out1", grid=(seq // ts,),
        in_specs=[_tile(ts, D_MODEL),
                  pl.BlockSpec((CONV_HALO, D_MODEL), lambda i: (jnp.maximum(i * hb - 1, 0), 0)),
                  _tile(ts, D_MODEL, 2), _full((CONV_HALO, D_MODEL)), _full((1, D_MODEL)), _full((1, D_MODEL)),
                  _full((1, D_MODEL)), _full((D_MODEL, D_MODEL)), _full((1, D_MODEL)), _tile(ts, D_MODEL),
                  _tile(ts, D_MODEL)],
        out_specs=[_tile(ts, D_MODEL), _tile(ts, D_MODEL), _tile(ts, D_MODEL), _tile(ts, D_MODEL), _tile(ts, D_MODEL),
                   _full((5, SUBLANES, D_MODEL))],
        out_shape=[jax.ShapeDtypeStruct((seq, D_MODEL), BF16), jax.ShapeDtypeStruct((seq, D_MODEL), BF16),
                   jax.ShapeDtypeStruct((seq, D_MODEL), F32), jax.ShapeDtypeStruct((seq, D_MODEL), F32),
                   jax.ShapeDtypeStruct((seq, D_MODEL), BF16), jax.ShapeDtypeStruct((5, SUBLANES, D_MODEL), F32)],
        scratch_shapes=[pltpu.VMEM((ts + CONV_HALO, D_MODEL), F32)],
        compiler_params=_params(),
    )(glu, glu, proj, dw_w, dw_b, ln_g, ln_b, w_out, g_post, x1, target)


def _bwd_in1(dcf, glu, proj, dgate, dw_w, w_in, x1, g_pre, dx2, ts):
    seq = x1.shape[0]
    hb = ts // CONV_HALO
    last = seq // CONV_HALO - 1
    nt = seq // ts

    def body(dcf_ref, dnext_ref, glu_ref, gprev_ref, ab_ref, dgate_ref, dww_ref, w_ref, x_ref, g_ref, dx2_ref,
             dproj_ref, dx1_ref, ddw_ref, dpre_ref, dbuf, gbuf, zbuf, sbuf):
        i = pl.program_id(0)

        @pl.when(i == 0)
        def _():
            ddw_ref[...] = jnp.zeros_like(ddw_ref)
            dpre_ref[...] = jnp.zeros_like(dpre_ref)

        dcf = dcf_ref[...]
        dbuf[:ts, :] = dcf
        dbuf[ts:, :] = jnp.where(i < nt - 1, dnext_ref[...], 0.0)
        gbuf[:CONV_HALO, :] = jnp.where(i > 0, gprev_ref[...], 0.0)
        gbuf[CONV_HALO:, :] = glu_ref[...]
        zbuf[:SUBLANES, :] = jnp.zeros((SUBLANES, D_MODEL), F32)
        zbuf[pl.ds(SUBLANES, ts), :] = dcf
        zbuf[pl.ds(SUBLANES + ts, SUBLANES), :] = jnp.zeros((SUBLANES, D_MODEL), F32)
        dh = _dot(dgate_ref[...], w_ref[2 * D_MODEL:, :])
        for lb in range(D_MODEL // LANES):
            lanes = slice(LANES * lb, LANES * (lb + 1))
            gate_lanes = slice(D_MODEL + LANES * lb, D_MODEL + LANES * (lb + 1))
            dglu = _conv_taps(dww_ref, dbuf, ts, lanes, _bwd_tap)
            a = ab_ref[:, lanes].astype(F32)
            sb = _sigmoid(ab_ref[:, gate_lanes].astype(F32))
            dproj_ref[:, lanes] = (dglu * sb).astype(BF16)
            dproj_ref[:, gate_lanes] = (dglu * a * sb * (1.0 - sb)).astype(BF16)
            for b in range(SUBLANES):
                rows = ts if b == 0 else ts + SUBLANES
                sbuf[b, pl.ds(0, rows), :] = zbuf[pl.ds(SUBLANES - b, rows), lanes]
                taps = [(a8, _fwd_tap(SUBLANES * a8 + b)) for a8 in range(CONV_HALO // SUBLANES + 1)]
                taps = [(a8, k) for a8, k in taps if k is not None]
                sums = [jnp.zeros((SUBLANES, LANES), F32) for _ in taps]
                for r0 in range(0, rows, DDW_ROWS):
                    rc = min(DDW_ROWS, rows - r0)
                    shifted = sbuf[b, pl.ds(r0, rc), :]
                    for n, (a8, k) in enumerate(taps):
                        sums[n] = sums[n] + _rows8(shifted * gbuf[pl.ds(SUBLANES * a8 + r0, rc), lanes])
                for n, (a8, k) in enumerate(taps):
                    ddw_ref[k, :, lanes] += sums[n]
            if lb % 2:
                for c0 in (LANES * (lb - 1), D_MODEL + LANES * (lb - 1)):
                    dh = dh + _dot(dproj_ref[:, c0:c0 + 2 * LANES], w_ref[c0:c0 + 2 * LANES, :])
        dproj_ref[:, 2 * D_MODEL:] = dgate_ref[...]
        x = x_ref[...]
        r = lax.rsqrt(jnp.mean(x * x, axis=-1, keepdims=True) + EPS)
        dx, dpre = _rms_bwd(x, r, g_ref[...], dh)
        dx1_ref[...] = dx2_ref[...] + dx
        dpre_ref[...] += _rows8(dpre)

    return pl.pallas_call(
        body, name="bwd_in1", grid=(nt,),
        in_specs=[_tile(ts, D_MODEL),
                  pl.BlockSpec((CONV_HALO, D_MODEL), lambda i: (jnp.minimum((i + 1) * hb, last), 0)),
                  _tile(ts, D_MODEL),
                  pl.BlockSpec((CONV_HALO, D_MODEL), lambda i: (jnp.maximum(i * hb - 1, 0), 0)),
                  _tile(ts, 2 * D_MODEL), _tile(ts, D_MODEL), _full((CONV_HALO, D_MODEL)),
                  _full((ODD_IN, D_MODEL)), _tile(ts, D_MODEL), _full((1, D_MODEL)), _tile(ts, D_MODEL)],
        out_specs=[_tile(ts, ODD_IN), _tile(ts, D_MODEL), _full((CONV_HALO, SUBLANES, D_MODEL)), _full((SUBLANES, D_MODEL))],
        out_shape=[jax.ShapeDtypeStruct((seq, ODD_IN), BF16), jax.ShapeDtypeStruct((seq, D_MODEL), F32),
                   jax.ShapeDtypeStruct((CONV_HALO, SUBLANES, D_MODEL), F32), jax.ShapeDtypeStruct((SUBLANES, D_MODEL), F32)],
        scratch_shapes=[pltpu.VMEM((ts + CONV_HALO, D_MODEL), F32), pltpu.VMEM((ts + CONV_HALO, D_MODEL), F32),
                        pltpu.VMEM((ts + 2 * SUBLANES, D_MODEL), F32), pltpu.VMEM((SUBLANES, ts + SUBLANES, LANES), F32)],
        compiler_params=_params(),
    )(dcf, dcf, glu, glu, proj, dgate, dw_w, w_in, x1, g_pre, dx2)


def _bwd_out0(dx1, y0, g_post, w_out, gug, o, pooled, pool_w, pool_scale, mix, ts):
    seq = dx1.shape[0]

    def body(dx1_ref, y_ref, g_ref, w_ref, gug_ref, o_ref, pooled_ref, pw_ref, ps_ref, mix_ref,
             do_ref, dgg_ref, dpooled_ref, dpost_ref, dscale_ref, dpw_ref, gw_ref, gacc):
        i = pl.program_id(0)

        @pl.when(i == 0)
        def _():
            dpost_ref[...] = jnp.zeros_like(dpost_ref)
            dscale_ref[...] = jnp.zeros_like(dscale_ref)
            dpw_ref[...] = jnp.zeros_like(dpw_ref)

        y = y_ref[...]
        r = lax.rsqrt(jnp.mean(y * y, axis=-1, keepdims=True) + EPS)
        dy, dpost = _rms_bwd(y, r, g_ref[...], dx1_ref[...])
        dpost_ref[...] += _rows8(dpost)
        dy = dy.astype(BF16)
        _accumulate_tn(gacc, mix_ref[...], dy, i == 0)
        dmix = _dot_nt(dy, w_ref[...])
        dya = dmix[:, :ATTN_WIDTH]
        dyb = dmix[:, ATTN_WIDTH:]
        ga = gug_ref[:, :ATTN_WIDTH].astype(F32)
        gb = gug_ref[:, ATTN_WIDTH + POOL_WIDTH:].astype(F32)
        sga = _sigmoid(ga)
        sgb = _sigmoid(gb)
        do_ref[...] = (dya * (ga * sga)).astype(BF16)
        dgg_ref[:, :ATTN_WIDTH] = (dya * o_ref[...] * (sga * (1.0 + ga * (1.0 - sga)))).astype(BF16)
        dypool = dyb * (gb * sgb)
        dsilu_gb = sgb * (1.0 + gb * (1.0 - sgb))
        for g in range(len(POOL_WINDOWS)):
            lanes = slice(POOL_GC * g, POOL_GC * (g + 1))
            pooled = pooled_ref[:, lanes]
            wg = pw_ref[g].astype(BF16)
            pw = _dot(pooled, wg)
            scale = ps_ref[:, lanes]
            dgg_ref[:, ATTN_WIDTH + POOL_GC * g:ATTN_WIDTH + POOL_GC * (g + 1)] = (
                dyb[:, lanes] * (pw * scale) * dsilu_gb[:, lanes]).astype(BF16)
            dscale_ref[:, lanes] += _rows8(dypool[:, lanes] * pw)
            dpw = (dypool[:, lanes] * scale).astype(BF16)
            dpooled_ref[:, lanes] = _dot_nt(dpw, wg)
            dpw_ref[g] += _dot_tn(pooled, dpw)

        @pl.when(i == seq // ts - 1)
        def _():
            gw_ref[...] = gacc[...].astype(BF16)

    return pl.pallas_call(
        body, name="bwd_out0", grid=(seq // ts,),
        in_specs=[_tile(ts, D_MODEL), _tile(ts, D_MODEL), _full((1, D_MODEL)), _resident((D_MODEL, D_MODEL)),
                  _tile(ts, 3 * POOL_WIDTH), _tile(ts, ATTN_WIDTH), _tile(ts, POOL_WIDTH),
                  _full((4, POOL_GC, POOL_GC)), _full((1, POOL_WIDTH)), _tile(ts, D_MODEL)],
        out_specs=[_tile(ts, ATTN_WIDTH), _tile(ts, ATTN_WIDTH + POOL_WIDTH), _tile(ts, POOL_WIDTH),
                   _full((SUBLANES, D_MODEL)), _full((SUBLANES, POOL_WIDTH)), _full((4, POOL_GC, POOL_GC)),
                   _full((D_MODEL, D_MODEL))],
        out_shape=[jax.ShapeDtypeStruct((seq, ATTN_WIDTH), BF16),
                   jax.ShapeDtypeStruct((seq, ATTN_WIDTH + POOL_WIDTH), BF16), jax.ShapeDtypeStruct((seq, POOL_WIDTH), F32),
                   jax.ShapeDtypeStruct((SUBLANES, D_MODEL), F32), jax.ShapeDtypeStruct((SUBLANES, POOL_WIDTH), F32),
                   jax.ShapeDtypeStruct((4, POOL_GC, POOL_GC), F32), jax.ShapeDtypeStruct((D_MODEL, D_MODEL), BF16)],
        scratch_shapes=[pltpu.VMEM((D_MODEL, D_MODEL), F32)],
        compiler_params=_params(),
    )(dx1, y0, g_post, w_out, gug, o, pooled, pool_w, pool_scale, mix)


def _attn_bwd(q, kv, do, sinks, parts, early):
    seq = q.shape[0]
    nb = seq // BLOCK

    def qblock(j):
        return jnp.minimum(j, nb - 1)

    n = len(parts)

    def body(*refs):
        sink_ref, q_ref, kvc_ref, kvp_ref, do_ref = refs[:5]
        early_ref = refs[5 + n]
        dq_ref, dkv_ref, dsink_ref = refs[6 + n:9 + n]
        all_ref = refs[9 + 2 * n]
        carry, dkv_acc = refs[10 + 2 * n:12 + 2 * n]
        scatter = _Scatter(refs[5:5 + n], refs[9 + n:9 + 2 * n], *refs[12 + 2 * n:14 + 2 * n])
        gather = lambda: _Gather([early_ref], [all_ref], *refs[14 + 2 * n:])
        j = pl.program_id(0)

        @pl.when(j == 0)
        def _():
            gather().begin()
            scatter.begin()
            dsink_ref[...] = jnp.zeros_like(dsink_ref)
            carry[...] = jnp.zeros_like(carry)

        @pl.when(j < nb)
        def _():
            valid, distf = _attn_mask(GROUP_ROWS, j == 0)
            qb = q_ref[...]
            dob = do_ref[...]
            kk = jnp.concatenate([kvp_ref[...], kvc_ref[...]], axis=0)
            lane = lax.broadcasted_iota(jnp.int32, (BLOCK, LANES), 1)
            dsink = jnp.zeros((BLOCK, LANES), F32)
            for kvh in range(Q_HEADS // GROUP):
                kh = kk[:, HEAD_DIM * kvh:HEAD_DIM * (kvh + 1)]
                vh = kk[:, KV_WIDTH + HEAD_DIM * kvh:KV_WIDTH + HEAD_DIM * (kvh + 1)]
                qg = _group_rows(qb, kvh)
                dog = _group_rows(dob, kvh)
                sink, slope = _group_columns(sink_ref, kvh)
                p, psink = _attn_probs(qg, kh, sink, slope, valid, distf)
                dp = _dot_nt(dog, vh)
                delta = jnp.sum(p * dp, axis=-1, keepdims=True)
                ds = (p * (dp - delta) * (HEAD_DIM ** -0.5)).astype(BF16)
                dsink_rows = -psink * delta
                dqg = _dot(ds, kh).astype(BF16)
                for i in range(GROUP):
                    h = GROUP * kvh + i
                    dsink = dsink + jnp.where(lane == h, dsink_rows[BLOCK * i:BLOCK * (i + 1), :], 0.0)
                    dq_ref[:, HEAD_DIM * h:HEAD_DIM * (h + 1)] = dqg[BLOCK * i:BLOCK * (i + 1), :]
                dkv_acc[:, HEAD_DIM * kvh:HEAD_DIM * (kvh + 1)] = _dot_tn(ds, qg)
                dkv_acc[:, KV_WIDTH + HEAD_DIM * kvh:KV_WIDTH + HEAD_DIM * (kvh + 1)] = _dot_tn(p.astype(BF16), dog)
            dsink_ref[...] += dsink

            @pl.when(j > 0)
            def _():
                dkv_ref[...] = (carry[...] + dkv_acc[:BLOCK, :]).astype(BF16)

            carry[...] = dkv_acc[BLOCK:, :]

        @pl.when(j == nb)
        def _():
            dkv_ref[...] = carry[...].astype(BF16)
            gather().relay()
            scatter.finish()
            gather().finish()

    hbm = pl.BlockSpec(memory_space=pl.ANY)
    out = pl.pallas_call(
        body, name="attn_bwd", grid=(nb + 1,),
        in_specs=[pl.BlockSpec(memory_space=pltpu.SMEM),
                  pl.BlockSpec((BLOCK, ATTN_WIDTH), lambda j: (qblock(j), 0)),
                  pl.BlockSpec((BLOCK, 2 * KV_WIDTH), lambda j: (qblock(j), 0)),
                  pl.BlockSpec((BLOCK, 2 * KV_WIDTH), lambda j: (jnp.maximum(qblock(j) - 1, 0), 0)),
                  pl.BlockSpec((BLOCK, ATTN_WIDTH), lambda j: (qblock(j), 0))] + [hbm] * (n + 1),
        out_specs=[pl.BlockSpec((BLOCK, ATTN_WIDTH), lambda j: (qblock(j), 0)),
                   pl.BlockSpec((BLOCK, 2 * KV_WIDTH), lambda j: (jnp.maximum(j - 1, 0), 0)),
                   _full((BLOCK, LANES))] + [hbm] * (n + 1),
        out_shape=[jax.ShapeDtypeStruct((seq, ATTN_WIDTH), BF16), jax.ShapeDtypeStruct((seq, 2 * KV_WIDTH), BF16),
                   jax.ShapeDtypeStruct((BLOCK, LANES), F32)]
        + [jax.ShapeDtypeStruct((N_DEV - 1, *p.shape[1:]), p.dtype) for p in parts]
        + [jax.ShapeDtypeStruct((N_DEV, *early.shape), early.dtype)],
        scratch_shapes=[pltpu.VMEM((BLOCK, 2 * KV_WIDTH), F32), pltpu.VMEM((2 * BLOCK, 2 * KV_WIDTH), F32)]
        + _Scatter.semaphores(n) + _Gather.semaphores(1),
        compiler_params=_params(),
    )(sinks, q, kv, kv, do, *parts, early)
    return out[:3], out[3:3 + n], out[3 + n]


def _bwd_in0(dpooled, dq, dkv, dgg, w_in, x, g_pre, dx1, ts):
    seq = x.shape[0]
    hb = ts // POOL_HALO
    last = seq // POOL_HALO - 1
    nt = seq // ts

    def body(dp_ref, dnext_ref, dq_ref, dkv_ref, dgg_ref, w_ref, x_ref, g_ref, dx1_ref,
             dproj_ref, gx_ref, dpre_ref, dbuf):
        i = pl.program_id(0)

        @pl.when(i == 0)
        def _():
            dpre_ref[...] = jnp.zeros_like(dpre_ref)

        dpool = dp_ref[...]
        dnext = jnp.where(i < nt - 1, dnext_ref[...], 0.0)
        u0 = ATTN_WIDTH + 2 * KV_WIDTH + ATTN_WIDTH
        for g, window in enumerate(POOL_WINDOWS):
            lanes = slice(POOL_GC * g, POOL_GC * (g + 1))
            dbuf[:ts, lanes] = dpool[:, lanes] / _pool_counts(i * ts, ts, window)
            dbuf[ts:, lanes] = dnext[:, lanes] / _pool_counts((i + 1) * ts, POOL_HALO, window)
        for g, window in enumerate(POOL_WINDOWS):
            lanes = slice(POOL_GC * g, POOL_GC * (g + 1))
            acc = dbuf[pl.ds(0, ts), lanes]
            for k in range(1, window):
                acc = acc + dbuf[pl.ds(k, ts), lanes]
            dproj_ref[:, u0 + POOL_GC * g:u0 + POOL_GC * (g + 1)] = (acc - dpool[:, lanes]).astype(BF16)
        dproj_ref[:, :ATTN_WIDTH] = dq_ref[...]
        dproj_ref[:, ATTN_WIDTH:ATTN_WIDTH + 2 * KV_WIDTH] = dkv_ref[...]
        dproj_ref[:, ATTN_WIDTH + 2 * KV_WIDTH:u0] = dgg_ref[:, :ATTN_WIDTH]
        dproj_ref[:, u0 + POOL_WIDTH:] = dgg_ref[:, ATTN_WIDTH:]
        dh = _dot(dproj_ref[...], w_ref[...])
        x = x_ref[...]
        r = lax.rsqrt(jnp.mean(x * x, axis=-1, keepdims=True) + EPS)
        dx, dpre = _rms_bwd(x, r, g_ref[...], dh)
        gx_ref[...] = dx1_ref[...] + dx
        dpre_ref[...] += _rows8(dpre)

    return pl.pallas_call(
        body, name="bwd_in0", grid=(nt,),
        in_specs=[_tile(ts, POOL_WIDTH),
                  pl.BlockSpec((POOL_HALO, POOL_WIDTH), lambda i: (jnp.minimum((i + 1) * hb, last), 0)),
                  _tile(ts, ATTN_WIDTH), _tile(ts, 2 * KV_WIDTH), _tile(ts, ATTN_WIDTH + POOL_WIDTH),
                  _full((EVEN_IN, D_MODEL)), _tile(ts, D_MODEL), _full((1, D_MODEL)), _tile(ts, D_MODEL)],
        out_specs=[_tile(ts, EVEN_IN), _tile(ts, D_MODEL), _full((SUBLANES, D_MODEL))],
        out_shape=[jax.ShapeDtypeStruct((seq, EVEN_IN), BF16), jax.ShapeDtypeStruct((seq, D_MODEL), F32),
                   jax.ShapeDtypeStruct((SUBLANES, D_MODEL), F32)],
        scratch_shapes=[pltpu.VMEM((ts + POOL_HALO, POOL_WIDTH), F32)],
        compiler_params=_params(),
    )(dpooled, dpooled, dq, dkv, dgg, w_in, x, g_pre, dx1)


def _matmul_tn(a, b, name, ts, tm):
    seq, m = a.shape
    n = b.shape[1]
    steps = seq // ts

    def body(a_ref, b_ref, o_ref, acc):
        s = pl.program_id(1)

        @pl.when(s == 0)
        def _():
            acc[...] = jnp.zeros_like(acc)

        acc[...] += _dot_tn(a_ref[...], b_ref[...])

        @pl.when(s == steps - 1)
        def _():
            o_ref[...] = acc[...].astype(BF16)

    return pl.pallas_call(
        body, name=name, grid=(m // tm, steps),
        in_specs=[pl.BlockSpec((ts, tm), lambda j, s: (s, j)), pl.BlockSpec((ts, n), lambda j, s: (s, 0))],
        out_specs=pl.BlockSpec((tm, n), lambda j, s: (j, 0)),
        out_shape=jax.ShapeDtypeStruct((m, n), BF16),
        scratch_shapes=[pltpu.VMEM((tm, n), F32)],
        compiler_params=pltpu.CompilerParams(dimension_semantics=("arbitrary", "arbitrary"), vmem_limit_bytes=VMEM_LIMIT),
    )(a, b)


def _adamw_math(w, g, m, v):
    m = ADAM_B1 * m + (1.0 - ADAM_B1) * g
    v = ADAM_B2 * v + (1.0 - ADAM_B2) * (g * g)
    m_hat = m / (1.0 - ADAM_B1 ** ADAM_STEP)
    v_hat = v / (1.0 - ADAM_B2 ** ADAM_STEP)
    delta = -ADAM_LR * (m_hat / (jnp.sqrt(v_hat) + ADAM_EPS) + ADAM_WD * w)
    return delta, m, v


def _adamw(ws, gs, ms, vs, name):
    n = len(ws)

    def body(*refs):
        ins, outs = refs[:4 * n], refs[4 * n:]
        for k in range(n):
            delta, m, v = _adamw_math(ins[k][...], ins[n + k][...], ins[2 * n + k][...], ins[3 * n + k][...])
            outs[k][...] = delta
            outs[n + k][...] = m
            outs[2 * n + k][...] = v

    shapes = [jax.ShapeDtypeStruct(w.shape, F32) for w in ws]
    out = pl.pallas_call(body, name=name, out_shape=shapes * 3,
                         compiler_params=pltpu.CompilerParams(vmem_limit_bytes=VMEM_LIMIT))(*ws, *gs, *ms, *vs)
    return out[:n], out[n:2 * n], out[2 * n:]


TS_MATMUL, TS_IN1, TS_CONV_FWD, TS_CONV_BWD, TS_GRAD = 512, 1024, 512, 256, 1024


def kernel(x, pre_norm, post_norm, a_w_in, a_sinks, b_pool_w, b_pool_scale, ab_w_out, c_w_in, c_dw_w, c_dw_b, c_ln_g, c_ln_b, c_w_out, loss_target, m_pre_norm, m_post_norm, m_a_w_in, m_a_sinks, m_b_pool_w, m_b_pool_scale, m_ab_w_out, m_c_w_in, m_c_dw_w, m_c_dw_b, m_c_ln_g, m_c_ln_b, m_c_w_out, v_pre_norm, v_post_norm, v_a_w_in, v_a_sinks, v_b_pool_w, v_b_pool_scale, v_ab_w_out, v_c_w_in, v_c_dw_w, v_c_dw_b, v_c_ln_g, v_c_ln_b, v_c_w_out):
    seq = x.shape[1]
    ts_big, ts_grad = min(TS_MATMUL, seq), min(TS_GRAD, seq)
    x2d = x[0]
    target = loss_target[0]
    ch = c_dw_b.shape[1]

    whole = lambda g: g.reshape(-1, D_MODEL)
    vec_rows = 40
    vecs = jnp.concatenate([c_dw_w[0, :, 0, :], c_dw_b, c_ln_g, c_ln_b, jnp.zeros((vec_rows - CONV_K - 3, ch), F32)], axis=0)
    w_in0t, vg = _all_gather([a_w_in[0].T.astype(BF16), vecs], "gather_first")
    w_in0t = whole(w_in0t)
    vg = vg.transpose(1, 0, 2).reshape(vec_rows, D_MODEL)
    dw_w = vg[:CONV_HALO]
    dw_b, ln_g, ln_b = vg[CONV_K:CONV_K + 1], vg[CONV_K + 1:CONV_K + 2], vg[CONV_K + 2:CONV_K + 3]

    sinks = a_sinks[0]
    (h0, q, kv, gug), (w_out0,) = _fwd_in0(x2d, pre_norm[0:1], w_in0t, [ab_w_out[0].astype(BF16)], ts_big)
    o, (w_in1t, w_out1) = _attn_fwd(q, kv, sinks, [c_w_in[0].T.astype(BF16), c_w_out[0].astype(BF16)])
    w_out0, w_in1t, w_out1 = whole(w_out0), whole(w_in1t), whole(w_out1)
    mix0, pooled, y0, x1 = _fwd_out0(gug, o, b_pool_w[0], b_pool_scale, w_out0, post_norm[0:1], x2d, ts_big)
    h1, proj1, glu = _fwd_in1(x1, pre_norm[1:2], w_in1t, min(TS_IN1, seq))
    ymix1, dy1, dx2, dcf, dgate, acc1 = _fwd_out1(glu, proj1, dw_w, dw_b, ln_g, ln_b, w_out1, post_norm[1:2], x1, target, min(TS_CONV_FWD, seq))

    dproj1, dx1, ddw_w, dpre1 = _bwd_in1(dcf, glu, proj1, dgate, dw_w, w_in1t, x1, pre_norm[1:2], dx2, min(TS_CONV_BWD, seq))
    g_in1t = _matmul_tn(dproj1, h1, "grad_w_in1", ts_grad, 1024)
    g_out1 = _matmul_tn(ymix1, dy1, "grad_w_out1", ts_grad, 1024)
    do, dgg, dpooled, dpost0, dscale, dpool_w, g_out0 = _bwd_out0(dx1, y0, post_norm[0:1], w_out0, gug, o, pooled, b_pool_w[0], b_pool_scale, mix0, ts_big)
    slabs = lambda g: g.reshape(N_DEV, -1, D_MODEL)
    me = 4 * lax.axis_index("x") + 2 * lax.axis_index("y") + lax.axis_index("c")
    early = [slabs(g_in1t), slabs(g_out1), slabs(g_out0)]
    row = lambda a: jnp.sum(a, axis=0, keepdims=True)
    lanes8 = lambda a: row(a).reshape(-1, LANES)
    loss_row = jnp.pad(jnp.sum(acc1[ACC_LOSS]).reshape(1, 1), ((0, 0), (0, LANES - 1)))
    done = jnp.concatenate([lanes8(dpre1), lanes8(dpost0), lanes8(acc1[ACC_POST]), lanes8(dscale),
                            loss_row, jnp.zeros((3, LANES), F32), dpool_w.reshape(4 * POOL_GC, LANES)], axis=0)
    (dq, dkv, dsink), arrived, done_all = _attn_bwd(q, kv, do, sinks, early, done)
    g_in1t, g_c_w_out, g_ab_w_out = _sum_slabs([lax.dynamic_index_in_dim(p, me, keepdims=False) for p in early],
                                               arrived, "sum_early")
    dproj0, grad_x, dpre0 = _bwd_in0(dpooled, dq, dkv, dgg, w_in0t, x2d, pre_norm[0:1], dx1, ts_big)
    g_in0t = _matmul_tn(dproj0, h0, "grad_w_in0", ts_grad, 768)

    vec_g = jnp.concatenate([jnp.sum(ddw_w[:CONV_K], axis=1), row(acc1[ACC_DW_B]), row(acc1[ACC_LN_G]), row(acc1[ACC_LN_B]),
                             jnp.zeros((vec_rows - CONV_K - 3, D_MODEL), F32)], axis=0)
    last = jnp.concatenate([lanes8(dpre0), row(dsink), jnp.zeros((SUBLANES - 1, LANES), F32)], axis=0)
    g_in0t, vec_g, last, rep = _final_reduce(slabs(g_in0t), vec_g.reshape(vec_rows, N_DEV, ch).transpose(1, 0, 2),
                                             last, done_all, "final_reduce")
    g_a_w_in, g_c_w_in = g_in0t.T, g_in1t.T
    g_dw_w, g_dw_b, g_ln_g, g_ln_b = vec_g[:CONV_K], vec_g[CONV_K:CONV_K + 1], vec_g[CONV_K + 1:CONV_K + 2], vec_g[CONV_K + 2:CONV_K + 3]
    g_pre = jnp.concatenate([last[:8], rep[:8]], axis=0).reshape(2, D_MODEL)
    g_sinks = last[8:9, :Q_HEADS]
    g_post = rep[8:24].reshape(2, D_MODEL)
    g_scale = rep[24:28].reshape(1, POOL_WIDTH)
    loss = (0.5 / D_MODEL) * rep[28, 0]
    g_pool_w = rep[32:]

    grads = [g_pre, g_post, g_a_w_in, g_sinks, g_pool_w, g_scale, g_ab_w_out, g_c_w_in, g_dw_w, g_dw_b, g_ln_g, g_ln_b, g_c_w_out]
    weights = [pre_norm, post_norm, a_w_in, a_sinks, b_pool_w, b_pool_scale, ab_w_out, c_w_in, c_dw_w, c_dw_b, c_ln_g, c_ln_b, c_w_out]
    m_in = [m_pre_norm, m_post_norm, m_a_w_in, m_a_sinks, m_b_pool_w, m_b_pool_scale, m_ab_w_out, m_c_w_in, m_c_dw_w, m_c_dw_b, m_c_ln_g, m_c_ln_b, m_c_w_out]
    v_in = [v_pre_norm, v_post_norm, v_a_w_in, v_a_sinks, v_b_pool_w, v_b_pool_scale, v_ab_w_out, v_c_w_in, v_c_dw_w, v_c_dw_b, v_c_ln_g, v_c_ln_b, v_c_w_out]
    flat = lambda arrs: [a.reshape(g.shape) for a, g in zip(arrs, grads)]
    big = (2, 6, 7, 12)
    small = tuple(k for k in range(len(grads)) if k not in big)
    pick = lambda arrs, idx: [arrs[k] for k in idx]
    deltas, new_m, new_v = [None] * 13, [None] * 13, [None] * 13
    for idx, name in ((big, "adamw_matrices"), (small, "adamw_vectors")):
        d, m, v = _adamw(pick(flat(weights), idx), pick(grads, idx), pick(flat(m_in), idx), pick(flat(v_in), idx), name)
        for k, dk, mk, vk in zip(idx, d, m, v):
            deltas[k], new_m[k], new_v[k] = dk, mk, vk
    shaped = lambda arrs: [a.reshape(w.shape) for a, w in zip(arrs, weights)]
    return (loss, grad_x[None], *shaped(grads), *shaped(deltas), *shaped(new_m), *shaped(new_v))
```

```python
import jax
import jax.numpy as jnp
from jax import lax
from jax.experimental import pallas as pl
from jax.experimental.pallas import tpu as pltpu

F32 = jnp.float32
BF16 = jnp.bfloat16
MESH = pl.DeviceIdType.MESH
N_DEV = 8

D_MODEL = 1024
HEAD_DIM = 64
Q_HEADS = 8
GROUP = 4
ATTN_WIDTH = 512
KV_WIDTH = 128
BLOCK = 128
POOL_WIDTH = 512
POOL_WINDOWS = (2, 4, 8, 16)
POOL_GC = 128
POOL_HALO = 16
EVEN_IN = 2304
CONV_K = 31
CONV_HALO = 32
ODD_IN = 3072
EPS = 1e-6
NEG = -1e30
SLOPES = tuple(2.0 ** (-8.0 * (h + 1) / Q_HEADS) for h in range(Q_HEADS))

ADAM_LR = 0.001
ADAM_B1 = 0.9
ADAM_B2 = 0.999
ADAM_EPS = 1e-08
ADAM_WD = 0.01
ADAM_STEP = 10

SUBLANES = 8
LANES = 128
VMEM_LIMIT = 56 * 1024 * 1024

NT = (((1,), (1,)), ((), ()))
TN = (((0,), (0,)), ((), ()))


def _params(**kw):
    return pltpu.CompilerParams(dimension_semantics=("arbitrary",), vmem_limit_bytes=VMEM_LIMIT, **kw)


def _dot(a, b):
    return jnp.dot(a, b, preferred_element_type=F32)


def _dot_nt(a, b):
    return lax.dot_general(a, b, NT, preferred_element_type=F32)


def _dot_tn(a, b):
    return lax.dot_general(a, b, TN, preferred_element_type=F32)


def _sigmoid(v):
    return 1.0 / (1.0 + jnp.exp(-v))


def _rows8(v):
    r, c = v.shape
    return jnp.sum(v.reshape(r // SUBLANES, SUBLANES, c), axis=0)


def _rms_fwd(v, g):
    r = lax.rsqrt(jnp.mean(v * v, axis=-1, keepdims=True) + EPS)
    return v * r * g, r


def _rms_bwd(v, r, g, dout):
    gd = dout * g
    dv = r * gd - v * (r * r * r) * jnp.mean(v * gd, axis=-1, keepdims=True)
    return dv, dout * (v * r)


def _full(shape):
    return pl.BlockSpec(shape, lambda i: (0,) * len(shape))


def _resident(shape):
    return pl.BlockSpec(shape, lambda i: (0,) * len(shape), pipeline_mode=pl.Buffered(1))


def _accumulate_tn(acc, a, b, first_step):
    @pl.when(first_step)
    def _():
        acc[...] = jnp.zeros_like(acc)

    acc[...] += _dot_tn(a, b)


def _tile(ts, cols, col_block=0):
    return pl.BlockSpec((ts, cols), lambda i: (i, col_block))


def _position():
    return lax.axis_index("x"), lax.axis_index("y"), lax.axis_index("c")


class _Gather:
    def __init__(self, x_refs, out_refs, send_sems, recv_sems, local_sems):
        self.x_refs, self.out_refs = x_refs, out_refs
        self.send_sems, self.recv_sems, self.local_sems = send_sems, recv_sems, local_sems
        self.n = len(x_refs)
        x, y, c = _position()
        self.c = c
        self.me, self.sibling = (x, y, c), (x, y, 1 - c)
        self.chips = [(1 - x, y), (x, 1 - y), (1 - x, 1 - y)]

    def copy(self, k, j, owner, to, src=None):
        slab = self.out_refs[k].at[4 * owner[0] + 2 * owner[1] + owner[2]]
        return pltpu.make_async_remote_copy(
            src_ref=slab if src is None else src, dst_ref=slab, send_sem=self.send_sems.at[7 * k + j],
            recv_sem=self.recv_sems.at[7 * k + j], device_id=to, device_id_type=MESH)

    def mine(self, k):
        return pltpu.make_async_copy(self.x_refs[k], self.out_refs[k].at[4 * self.me[0] + 2 * self.me[1] + self.c],
                                     self.local_sems.at[k])

    def first(self, k):
        out = [self.copy(k, 0, self.me, self.sibling, src=self.x_refs[k])]
        return out + [self.copy(k, 1 + j, self.me, (*chip, self.c), src=self.x_refs[k]) for j, chip in enumerate(self.chips)]

    def begin(self):
        for k in range(self.n):
            self.mine(k).start()
            for cp in self.first(k):
                cp.start()

    def passed(self, k):
        return [self.copy(k, 4 + j, (*chip, self.c), self.sibling) for j, chip in enumerate(self.chips)]

    def relay(self):
        for k in range(self.n):
            for j, chip in enumerate(self.chips):
                self.copy(k, 1 + j, (*chip, self.c), self.me).wait_recv()
                self.passed(k)[j].start()

    def finish(self):
        for k in range(self.n):
            self.copy(k, 0, self.sibling, self.me).wait_recv()
            for j, chip in enumerate(self.chips):
                self.copy(k, 4 + j, (*chip, 1 - self.c), self.me).wait_recv()
        for k in range(self.n):
            for cp in self.first(k) + self.passed(k):
                cp.wait_send()
            self.mine(k).wait()

    @staticmethod
    def semaphores(n):
        return [pltpu.SemaphoreType.DMA((7 * n,)), pltpu.SemaphoreType.DMA((7 * n,)), pltpu.SemaphoreType.DMA((n,))]


def RELAY_AT(steps):
    return steps * 3 // 4


def _all_gather(blocks, name):
    n = len(blocks)

    def body(*refs):
        gather = _Gather(refs[:n], refs[n:2 * n], *refs[2 * n:])
        gather.begin()
        gather.relay()
        gather.finish()

    return pl.pallas_call(
        body, name=name,
        out_shape=[jax.ShapeDtypeStruct((N_DEV, *b.shape), b.dtype) for b in blocks],
        in_specs=[pl.BlockSpec(memory_space=pltpu.VMEM)] * n,
        out_specs=[pl.BlockSpec(memory_space=pltpu.VMEM)] * n,
        scratch_shapes=_Gather.semaphores(n),
        compiler_params=pltpu.CompilerParams(vmem_limit_bytes=VMEM_LIMIT),
    )(*blocks)


class _Scatter:
    def __init__(self, part_refs, recv_refs, send_sems, recv_sems):
        self.part_refs, self.recv_refs, self.send_sems, self.recv_sems = part_refs, recv_refs, send_sems, recv_sems
        self.n = len(part_refs)

    def copies(self):
        x, y, c = _position()
        me = 4 * x + 2 * y + c
        out = []
        for k in range(self.n):
            for j in range(N_DEV - 1):
                d = j + 1
                out.append(pltpu.make_async_remote_copy(
                    src_ref=self.part_refs[k].at[me ^ d], dst_ref=self.recv_refs[k].at[j],
                    send_sem=self.send_sems.at[7 * k + j], recv_sem=self.recv_sems.at[7 * k + j],
                    device_id=(x ^ (d >> 2), y ^ ((d >> 1) & 1), c ^ (d & 1)), device_id_type=MESH))
        return out

    def begin(self):
        for cp in self.copies():
            cp.start()

    def finish(self):
        for cp in self.copies():
            cp.wait_recv()
        for cp in self.copies():
            cp.wait_send()

    @staticmethod
    def semaphores(n):
        return [pltpu.SemaphoreType.DMA((7 * n,)), pltpu.SemaphoreType.DMA((7 * n,))]


def _sum_slabs(own, recv, name):
    n = len(own)

    def body(*refs):
        for k in range(n):
            acc = refs[k][...].astype(F32)
            for j in range(N_DEV - 1):
                acc = acc + refs[n + k][j].astype(F32)
            refs[2 * n + k][...] = acc

    return pl.pallas_call(body, name=name, out_shape=[jax.ShapeDtypeStruct(o.shape, F32) for o in own])(*own, *recv)


N_CHIPS = 4


def _final_reduce(parts, vec_parts, rep, early_all, name):
    _, rows, cols = parts.shape
    vrows, prows = vec_parts.shape[1], rep.shape[0]
    peers = N_CHIPS - 1

    def body(parts_ref, vec_ref, rep_ref, early_ref, out_ref, vec_out, rep_out, early_out, recv_a, own_a, mid, recv_b, vec_recv, rep_all,
             a_send, a_recv, a_local, b_send, b_recv, v_send, v_recv, r_send, r_recv):
        x, y, c = _position()
        chip = 2 * x + y
        me = 2 * chip + c
        everyone = [(d, (x ^ (d >> 2), y ^ ((d >> 1) & 1), c ^ (d & 1))) for d in range(1, N_DEV)]

        stage1, loads = [], []
        for j in range(N_CHIPS):
            stage1.append(pltpu.make_async_remote_copy(
                src_ref=parts_ref.at[2 * j + 1 - c], dst_ref=recv_a.at[j], send_sem=a_send.at[j], recv_sem=a_recv.at[j],
                device_id=(x, y, 1 - c), device_id_type=MESH))
            loads.append(pltpu.make_async_copy(parts_ref.at[2 * j + c], own_a.at[j], a_local.at[j]))
            stage1[-1].start()
            loads[-1].start()
        small = []
        for d, to in everyone:
            small.append(pltpu.make_async_remote_copy(
                src_ref=vec_ref.at[me ^ d], dst_ref=vec_recv.at[d - 1], send_sem=v_send.at[d - 1], recv_sem=v_recv.at[d - 1],
                device_id=to, device_id_type=MESH))
            small.append(pltpu.make_async_remote_copy(
                src_ref=rep_ref, dst_ref=rep_all.at[me], send_sem=r_send.at[d - 1], recv_sem=r_recv.at[d - 1],
                device_id=to, device_id_type=MESH))
        for cp in small:
            cp.start()
        rep_all[me] = rep_ref[...]

        for j in range(N_CHIPS):
            loads[j].wait()
            stage1[j].wait_recv()
            mid[j] = (own_a[j].astype(F32) + recv_a[j].astype(F32)).astype(BF16)
        stage2 = []
        for d in range(1, N_CHIPS):
            stage2.append(pltpu.make_async_remote_copy(
                src_ref=mid.at[chip ^ d], dst_ref=recv_b.at[d - 1], send_sem=b_send.at[d - 1], recv_sem=b_recv.at[d - 1],
                device_id=(x ^ (d >> 1), y ^ (d & 1), c), device_id_type=MESH))
            stage2[-1].start()

        for cp in small:
            cp.wait_recv()
        vec_sum = vec_ref[me]
        for j in range(N_DEV - 1):
            vec_sum = vec_sum + vec_recv[j]
        vec_out[...] = vec_sum
        rep_sum = rep_all[0]
        for d in range(1, N_DEV):
            rep_sum = rep_sum + rep_all[d]
        rep_out[...] = rep_sum
        early_sum = early_ref[0]
        for d in range(1, N_DEV):
            early_sum = early_sum + early_ref[d]
        early_out[...] = early_sum

        acc = mid[chip].astype(F32)
        for d in range(peers):
            stage2[d].wait_recv()
            acc = acc + recv_b[d].astype(F32)
        out_ref[...] = acc
        for cp in stage1 + stage2 + small:
            cp.wait_send()

    vmem = pl.BlockSpec(memory_space=pltpu.VMEM)
    dma = pltpu.SemaphoreType.DMA
    return pl.pallas_call(
        body, name=name,
        out_shape=[jax.ShapeDtypeStruct((rows, cols), F32), jax.ShapeDtypeStruct((vrows, LANES), F32),
                   jax.ShapeDtypeStruct((prows, LANES), F32), jax.ShapeDtypeStruct(early_all.shape[1:], F32)],
        in_specs=[pl.BlockSpec(memory_space=pl.ANY), vmem, vmem, vmem],
        out_specs=[vmem, vmem, vmem, vmem],
        scratch_shapes=[pltpu.VMEM((N_CHIPS, rows, cols), BF16), pltpu.VMEM((N_CHIPS, rows, cols), BF16),
                        pltpu.VMEM((N_CHIPS, rows, cols), BF16), pltpu.VMEM((peers, rows, cols), BF16),
                        pltpu.VMEM((N_DEV - 1, vrows, LANES), F32), pltpu.VMEM((N_DEV, prows, LANES), F32),
                        dma((N_CHIPS,)), dma((N_CHIPS,)), dma((N_CHIPS,)), dma((peers,)), dma((peers,)),
                        dma((N_DEV - 1,)), dma((N_DEV - 1,)), dma((N_DEV - 1,)), dma((N_DEV - 1,))],
        compiler_params=pltpu.CompilerParams(vmem_limit_bytes=VMEM_LIMIT),
    )(parts, vec_parts, rep, early_all)


def _fwd_in0(x, g, w, shards, ts):
    seq = x.shape[0]
    n = len(shards)
    steps = seq // ts

    def body(*refs):
        x_ref, g_ref, w_ref = refs[:3]
        h_ref, q_ref, kv_ref, gug_ref = refs[3 + n:7 + n]
        gather = lambda: _Gather(refs[3:3 + n], refs[7 + n:7 + 2 * n], *refs[7 + 2 * n:])
        i = pl.program_id(0)

        @pl.when(i == 0)
        def _():
            gather().begin()

        h, _ = _rms_fwd(x_ref[...], g_ref[...])
        h = h.astype(BF16)
        h_ref[...] = h
        proj = _dot_nt(h, w_ref[...])
        q_ref[...] = proj[:, :ATTN_WIDTH].astype(BF16)
        kv_ref[...] = proj[:, ATTN_WIDTH:ATTN_WIDTH + 2 * KV_WIDTH].astype(BF16)
        gug_ref[...] = proj[:, ATTN_WIDTH + 2 * KV_WIDTH:].astype(BF16)

        @pl.when(i == RELAY_AT(steps))
        def _():
            gather().relay()

        @pl.when(i == steps - 1)
        def _():
            gather().finish()

    hbm = pl.BlockSpec(memory_space=pl.ANY)
    out = pl.pallas_call(
        body, name="fwd_in0", grid=(steps,),
        in_specs=[_tile(ts, D_MODEL), _full((1, D_MODEL)), _full((EVEN_IN, D_MODEL))] + [hbm] * n,
        out_specs=[_tile(ts, D_MODEL), _tile(ts, ATTN_WIDTH), _tile(ts, 2 * KV_WIDTH), _tile(ts, 3 * POOL_WIDTH)] + [hbm] * n,
        out_shape=[jax.ShapeDtypeStruct((seq, D_MODEL), BF16), jax.ShapeDtypeStruct((seq, ATTN_WIDTH), BF16),
                   jax.ShapeDtypeStruct((seq, 2 * KV_WIDTH), BF16), jax.ShapeDtypeStruct((seq, 3 * POOL_WIDTH), BF16)]
        + [jax.ShapeDtypeStruct((N_DEV, *b.shape), b.dtype) for b in shards],
        scratch_shapes=_Gather.semaphores(n),
        compiler_params=_params(),
    )(x, g, w, *shards)
    return out[:4], out[4:]


GROUP_ROWS = GROUP * BLOCK
FWD_BLOCKS = 2


def _attn_mask(rows, first_block):
    row = lax.broadcasted_iota(jnp.int32, (rows, 2 * BLOCK), 0) & (BLOCK - 1)
    col = lax.broadcasted_iota(jnp.int32, (rows, 2 * BLOCK), 1)
    dist = row + BLOCK - col
    valid = (dist >= 0) & (dist < BLOCK) & ((col >= BLOCK) | jnp.logical_not(first_block))
    return valid, dist.astype(F32)


def _group_rows(block, kvh):
    return jnp.concatenate([block[:, HEAD_DIM * h:HEAD_DIM * (h + 1)] for h in range(GROUP * kvh, GROUP * (kvh + 1))], axis=0)


def _group_columns(sink_ref, kvh):
    head = lax.broadcasted_iota(jnp.int32, (GROUP_ROWS, 1), 0) // BLOCK
    sink = jnp.zeros((GROUP_ROWS, 1), F32)
    slope = jnp.zeros((GROUP_ROWS, 1), F32)
    for i in range(GROUP):
        sink = jnp.where(head == i, sink_ref[GROUP * kvh + i], sink)
        slope = jnp.where(head == i, SLOPES[GROUP * kvh + i], slope)
    return sink, slope


def _attn_probs(qh, kh, sink, slope, valid, distf):
    s = _dot_nt(qh, kh) * (HEAD_DIM ** -0.5)
    s = jnp.where(valid, s - slope * distf, NEG)
    mx = jnp.maximum(jnp.max(s, axis=-1, keepdims=True), sink)
    e = jnp.exp(s - mx)
    es = jnp.exp(sink - mx)
    den = jnp.sum(e, axis=-1, keepdims=True) + es
    return e / den, es / den


def _attn_fwd(q, kv, sinks, shards):
    seq = q.shape[0]
    per = min(FWD_BLOCKS, seq // BLOCK)
    nb = seq // (per * BLOCK)
    ns = len(shards)

    def body(*refs):
        sink_ref, q_ref, kvc_ref, kvp_ref = refs[:4]
        o_ref = refs[4 + ns]
        gather = lambda: _Gather(refs[4:4 + ns], refs[5 + ns:5 + 2 * ns], *refs[5 + 2 * ns:])
        n = pl.program_id(0)

        @pl.when(n == 0)
        def _():
            gather().begin()

        keys = jnp.concatenate([kvp_ref[...], kvc_ref[...]], axis=0)
        for sub in range(per):
            rows = slice(BLOCK * sub, BLOCK * (sub + 1))
            valid, distf = _attn_mask(BLOCK, (n == 0) if sub == 0 else False)
            qb = q_ref[rows, :]
            kk = keys[BLOCK * sub:BLOCK * (sub + 2), :]
            for h in range(Q_HEADS):
                kvh = h // GROUP
                qh = qb[:, HEAD_DIM * h:HEAD_DIM * (h + 1)]
                kh = kk[:, HEAD_DIM * kvh:HEAD_DIM * (kvh + 1)]
                vh = kk[:, KV_WIDTH + HEAD_DIM * kvh:KV_WIDTH + HEAD_DIM * (kvh + 1)]
                p, _ = _attn_probs(qh, kh, sink_ref[h], SLOPES[h], valid, distf)
                o_ref[rows, HEAD_DIM * h:HEAD_DIM * (h + 1)] = _dot(p.astype(BF16), vh)

        @pl.when(n == RELAY_AT(nb))
        def _():
            gather().relay()

        @pl.when(n == nb - 1)
        def _():
            gather().finish()

    hbm = pl.BlockSpec(memory_space=pl.ANY)
    out = pl.pallas_call(
        body, name="attn_fwd", grid=(nb,),
        in_specs=[pl.BlockSpec(memory_space=pltpu.SMEM),
                  pl.BlockSpec((per * BLOCK, ATTN_WIDTH), lambda n: (n, 0)),
                  pl.BlockSpec((per * BLOCK, 2 * KV_WIDTH), lambda n: (n, 0)),
                  pl.BlockSpec((BLOCK, 2 * KV_WIDTH), lambda n: (jnp.maximum(per * n - 1, 0), 0))] + [hbm] * ns,
        out_specs=[pl.BlockSpec((per * BLOCK, ATTN_WIDTH), lambda n: (n, 0))] + [hbm] * ns,
        out_shape=[jax.ShapeDtypeStruct((seq, ATTN_WIDTH), F32)]
        + [jax.ShapeDtypeStruct((N_DEV, *b.shape), b.dtype) for b in shards],
        scratch_shapes=_Gather.semaphores(ns),
        compiler_params=_params(),
    )(sinks, q, kv, kv, *shards)
    return out[0], out[1:]


def _pool_counts(first_row, rows, window):
    t = first_row + lax.broadcasted_iota(jnp.int32, (rows, 1), 0)
    return jnp.minimum(t + 1, window).astype(F32)


def _fwd_out0(gug, o, pool_w, pool_scale, w_out, g_post, x, ts):
    seq = x.shape[0]
    hb = ts // POOL_HALO

    def body(gug_ref, halo_ref, o_ref, pw_ref, ps_ref, w_ref, g_ref, x_ref, mix_ref, pooled_ref, y_ref, x1_ref, ubuf):
        i = pl.program_id(0)
        ga = gug_ref[:, :ATTN_WIDTH].astype(F32)
        u = gug_ref[:, ATTN_WIDTH:ATTN_WIDTH + POOL_WIDTH].astype(F32)
        gb = gug_ref[:, ATTN_WIDTH + POOL_WIDTH:].astype(F32)
        mix_ref[:, :ATTN_WIDTH] = (o_ref[...] * (ga * _sigmoid(ga))).astype(BF16)
        ubuf[:POOL_HALO, :] = jnp.where(i > 0, halo_ref[...].astype(F32), 0.0)
        ubuf[POOL_HALO:, :] = u
        silu_gb = gb * _sigmoid(gb)
        for g, window in enumerate(POOL_WINDOWS):
            lanes = slice(POOL_GC * g, POOL_GC * (g + 1))
            acc = ubuf[pl.ds(POOL_HALO, ts), lanes]
            for k in range(1, window):
                acc = acc + ubuf[pl.ds(POOL_HALO - k, ts), lanes]
            pooled = (acc / _pool_counts(i * ts, ts, window) - u[:, lanes]).astype(BF16)
            pooled_ref[:, lanes] = pooled
            ypool = _dot(pooled, pw_ref[g].astype(BF16)) * ps_ref[:, lanes]
            mix_ref[:, ATTN_WIDTH + POOL_GC * g:ATTN_WIDTH + POOL_GC * (g + 1)] = (ypool * silu_gb[:, lanes]).astype(BF16)
        y = _dot(mix_ref[...], w_ref[...])
        y_ref[...] = y
        yn, _ = _rms_fwd(y, g_ref[...])
        x1_ref[...] = x_ref[...] + yn

    return pl.pallas_call(
        body, name="fwd_out0", grid=(seq // ts,),
        in_specs=[_tile(ts, 3 * POOL_WIDTH),
                  pl.BlockSpec((POOL_HALO, POOL_WIDTH), lambda i: (jnp.maximum(i * hb - 1, 0), 1)),
                  _tile(ts, ATTN_WIDTH), _full((4, POOL_GC, POOL_GC)), _full((1, POOL_WIDTH)),
                  _full((D_MODEL, D_MODEL)), _full((1, D_MODEL)), _tile(ts, D_MODEL)],
        out_specs=[_tile(ts, D_MODEL), _tile(ts, POOL_WIDTH), _tile(ts, D_MODEL), _tile(ts, D_MODEL)],
        out_shape=[jax.ShapeDtypeStruct((seq, D_MODEL), BF16), jax.ShapeDtypeStruct((seq, POOL_WIDTH), BF16),
                   jax.ShapeDtypeStruct((seq, D_MODEL), F32), jax.ShapeDtypeStruct((seq, D_MODEL), F32)],
        scratch_shapes=[pltpu.VMEM((ts + POOL_HALO, POOL_WIDTH), F32)],
        compiler_params=_params(),
    )(gug, gug, o, pool_w, pool_scale, w_out, g_post, x)


def _fwd_in1(x1, g, w, ts):
    seq = x1.shape[0]

    def body(x_ref, g_ref, w_ref, h_ref, proj_ref, glu_ref):
        h, _ = _rms_fwd(x_ref[...], g_ref[...])
        h = h.astype(BF16)
        h_ref[...] = h
        proj = _dot_nt(h, w_ref[...])
        proj_ref[...] = proj.astype(BF16)
        glu_ref[...] = proj[:, :D_MODEL] * _sigmoid(proj[:, D_MODEL:2 * D_MODEL])

    return pl.pallas_call(
        body, name="fwd_in1", grid=(seq // ts,),
        in_specs=[_tile(ts, D_MODEL), _full((1, D_MODEL)), _full((ODD_IN, D_MODEL))],
        out_specs=[_tile(ts, D_MODEL), _tile(ts, ODD_IN), _tile(ts, D_MODEL)],
        out_shape=[jax.ShapeDtypeStruct((seq, D_MODEL), BF16), jax.ShapeDtypeStruct((seq, ODD_IN), BF16),
                   jax.ShapeDtypeStruct((seq, D_MODEL), F32)],
        compiler_params=_params(),
    )(x1, g, w)


ACC_LOSS, ACC_POST, ACC_LN_G, ACC_LN_B, ACC_DW_B = range(5)
DDW_ROWS = 88
CONV_FIRST = CONV_HALO - CONV_K + 1


def _fwd_tap(offset):
    return offset - CONV_FIRST if CONV_FIRST <= offset <= CONV_HALO else None


def _bwd_tap(offset):
    return CONV_K - 1 - offset if offset < CONV_K else None


def _conv_taps(w_ref, buf_ref, ts, lanes, tap_of_offset):
    out = None
    for b in range(SUBLANES):
        rows = ts if b == 0 else ts + SUBLANES
        part = None
        for a in range(CONV_HALO // SUBLANES + 1):
            k = tap_of_offset(SUBLANES * a + b)
            if k is None:
                continue
            term = w_ref[k:k + 1, lanes] * buf_ref[pl.ds(SUBLANES * a, rows), lanes]
            part = term if part is None else part + term
        if part is None:
            continue
        if b:
            part = part[b:b + ts, :]
        out = part if out is None else out + part
    return out


def _fwd_out1(glu, proj, dw_w, dw_b, ln_g, ln_b, w_out, g_post, x1, target, ts):
    seq = x1.shape[0]
    hb = ts // CONV_HALO

    def body(glu_ref, halo_ref, gate_ref, dww_ref, dwb_ref, lng_ref, lnb_ref, w_ref, g_ref, x1_ref, t_ref,
             ymix_ref, dy_ref, dx2_ref, dcf_ref, dgate_ref, acc_ref, gbuf):
        i = pl.program_id(0)

        @pl.when(i == 0)
        def _():
            acc_ref[...] = jnp.zeros_like(acc_ref)

        gbuf[:CONV_HALO, :] = jnp.where(i > 0, halo_ref[...], 0.0)
        gbuf[CONV_HALO:, :] = glu_ref[...]
        for lb in range(D_MODEL // LANES):
            lanes = slice(LANES * lb, LANES * (lb + 1))
            dcf_ref[:, lanes] = _conv_taps(dww_ref, gbuf, ts, lanes, _fwd_tap)
        cf = dcf_ref[...] + dwb_ref[...]
        mu = jnp.mean(cf, axis=-1, keepdims=True)
        cen = cf - mu
        rs = lax.rsqrt(jnp.mean(cen * cen, axis=-1, keepdims=True) + EPS)
        xhat = cen * rs
        cn = xhat * lng_ref[...] + lnb_ref[...]
        gate = gate_ref[...].astype(F32)
        sg = _sigmoid(gate)
        sc = _sigmoid(cn)
        silu_gate = gate * sg
        silu_cn = cn * sc
        ymix = (silu_cn * silu_gate).astype(BF16)
        ymix_ref[...] = ymix
        y = _dot(ymix, w_ref[...])
        yn, r = _rms_fwd(y, g_ref[...])
        err = (x1_ref[...] + yn) - t_ref[...]
        acc_ref[ACC_LOSS] += _rows8(err * err)
        dx2 = err * (1.0 / D_MODEL)
        dx2_ref[...] = dx2
        dy, dpost = _rms_bwd(y, r, g_ref[...], dx2)
        acc_ref[ACC_POST] += _rows8(dpost)
        dy = dy.astype(BF16)
        dy_ref[...] = dy
        dymix = _dot_nt(dy, w_ref[...])
        dgate_ref[...] = (dymix * silu_cn * (sg * (1.0 + gate * (1.0 - sg)))).astype(BF16)
        dcn = dymix * silu_gate * (sc * (1.0 + cn * (1.0 - sc)))
        acc_ref[ACC_LN_G] += _rows8(dcn * xhat)
        acc_ref[ACC_LN_B] += _rows8(dcn)
        dxhat = dcn * lng_ref[...]
        dcf = rs * (dxhat - jnp.mean(dxhat, axis=-1, keepdims=True)
                    - xhat * jnp.mean(dxhat * xhat, axis=-1, keepdims=True))
        acc_ref[ACC_DW_B] += _rows8(dcf)
        dcf_ref[...] = dcf

    return pl.pallas_call(
        body, name="fwd_out1", grid=(seq // ts,),
        in_specs=[_tile(ts, D_MODEL),
                  pl.BlockSpec((CONV_HALO, D_MODEL), lambda i: (jnp.maximum(i * hb - 1, 0), 0)),
                  _tile(ts, D_MODEL, 2), _full((CONV_HALO, D_MODEL)), _full((1, D_MODEL)), _full((1, D_MODEL)),
                  _full((1, D_MODEL)), _full((D_MODEL, D_MODEL)), _full((1, D_MODEL)), _tile(ts, D_MODEL),
                  _tile(ts, D_MODEL)],
        out_specs=[_tile(ts, D_MODEL), _tile(ts, D_MODEL), _tile(ts, D_MODEL), _tile(ts, D_MODEL), _tile(ts, D_MODEL),
                   _full((5, SUBLANES, D_MODEL))],
        out_shape=[jax.ShapeDtypeStruct((seq, D_MODEL), BF16), jax.ShapeDtypeStruct((seq, D_MODEL), BF16),
                   jax.ShapeDtypeStruct((seq, D_MODEL), F32), jax.ShapeDtypeStruct((seq, D_MODEL), F32),
                   jax.ShapeDtypeStruct((seq, D_MODEL), BF16), jax.ShapeDtypeStruct((5, SUBLANES, D_MODEL), F32)],
        scratch_shapes=[pltpu.VMEM((ts + CONV_HALO, D_MODEL), F32)],
        compiler_params=_params(),
    )(glu, glu, proj, dw_w, dw_b, ln_g, ln_b, w_out, g_post, x1, target)


def _bwd_in1(dcf, glu, proj, dgate, dw_w, w_in, x1, g_pre, dx2, ts):
    seq = x1.shape[0]
    hb = ts // CONV_HALO
    last = seq // CONV_HALO - 1
    nt = seq // ts

    def body(dcf_ref, dnext_ref, glu_ref, gprev_ref, ab_ref, dgate_ref, dww_ref, w_ref, x_ref, g_ref, dx2_ref,
             dproj_ref, dx1_ref, ddw_ref, dpre_ref, dbuf, gbuf, zbuf, sbuf):
        i = pl.program_id(0)

        @pl.when(i == 0)
        def _():
            ddw_ref[...] = jnp.zeros_like(ddw_ref)
            dpre_ref[...] = jnp.zeros_like(dpre_ref)

        dcf = dcf_ref[...]
        dbuf[:ts, :] = dcf
        dbuf[ts:, :] = jnp.where(i < nt - 1, dnext_ref[...], 0.0)
        gbuf[:CONV_HALO, :] = jnp.where(i > 0, gprev_ref[...], 0.0)
        gbuf[CONV_HALO:, :] = glu_ref[...]
        zbuf[:SUBLANES, :] = jnp.zeros((SUBLANES, D_MODEL), F32)
        zbuf[pl.ds(SUBLANES, ts), :] = dcf
        zbuf[pl.ds(SUBLANES + ts, SUBLANES), :] = jnp.zeros((SUBLANES, D_MODEL), F32)
        dh = _dot(dgate_ref[...], w_ref[2 * D_MODEL:, :])
        for lb in range(D_MODEL // LANES):
            lanes = slice(LANES * lb, LANES * (lb + 1))
            gate_lanes = slice(D_MODEL + LANES * lb, D_MODEL + LANES * (lb + 1))
            dglu = _conv_taps(dww_ref, dbuf, ts, lanes, _bwd_tap)
            a = ab_ref[:, lanes].astype(F32)
            sb = _sigmoid(ab_ref[:, gate_lanes].astype(F32))
            dproj_ref[:, lanes] = (dglu * sb).astype(BF16)
            dproj_ref[:, gate_lanes] = (dglu * a * sb * (1.0 - sb)).astype(BF16)
            for b in range(SUBLANES):
                rows = ts if b == 0 else ts + SUBLANES
                sbuf[b, pl.ds(0, rows), :] = zbuf[pl.ds(SUBLANES - b, rows), lanes]
                taps = [(a8, _fwd_tap(SUBLANES * a8 + b)) for a8 in range(CONV_HALO // SUBLANES + 1)]
                taps = [(a8, k) for a8, k in taps if k is not None]
                sums = [jnp.zeros((SUBLANES, LANES), F32) for _ in taps]
                for r0 in range(0, rows, DDW_ROWS):
                    rc = min(DDW_ROWS, rows - r0)
                    shifted = sbuf[b, pl.ds(r0, rc), :]
                    for n, (a8, k) in enumerate(taps):
                        sums[n] = sums[n] + _rows8(shifted * gbuf[pl.ds(SUBLANES * a8 + r0, rc), lanes])
                for n, (a8, k) in enumerate(taps):
                    ddw_ref[k, :, lanes] += sums[n]
            if lb % 2:
                for c0 in (LANES * (lb - 1), D_MODEL + LANES * (lb - 1)):
                    dh = dh + _dot(dproj_ref[:, c0:c0 + 2 * LANES], w_ref[c0:c0 + 2 * LANES, :])
        dproj_ref[:, 2 * D_MODEL:] = dgate_ref[...]
        x = x_ref[...]
        r = lax.rsqrt(jnp.mean(x * x, axis=-1, keepdims=True) + EPS)
        dx, dpre = _rms_bwd(x, r, g_ref[...], dh)
        dx1_ref[...] = dx2_ref[...] + dx
        dpre_ref[...] += _rows8(dpre)

    return pl.pallas_call(
        body, name="bwd_in1", grid=(nt,),
        in_specs=[_tile(ts, D_MODEL),
                  pl.BlockSpec((CONV_HALO, D_MODEL), lambda i: (jnp.minimum((i + 1) * hb, last), 0)),
                  _tile(ts, D_MODEL),
                  pl.BlockSpec((CONV_HALO, D_MODEL), lambda i: (jnp.maximum(i * hb - 1, 0), 0)),
                  _tile(ts, 2 * D_MODEL), _tile(ts, D_MODEL), _full((CONV_HALO, D_MODEL)),
                  _full((ODD_IN, D_MODEL)), _tile(ts, D_MODEL), _full((1, D_MODEL)), _tile(ts, D_MODEL)],
        out_specs=[_tile(ts, ODD_IN), _tile(ts, D_MODEL), _full((CONV_HALO, SUBLANES, D_MODEL)), _full((SUBLANES, D_MODEL))],
        out_shape=[jax.ShapeDtypeStruct((seq, ODD_IN), BF16), jax.ShapeDtypeStruct((seq, D_MODEL), F32),
                   jax.ShapeDtypeStruct((CONV_HALO, SUBLANES, D_MODEL), F32), jax.ShapeDtypeStruct((SUBLANES, D_MODEL), F32)],
        scratch_shapes=[pltpu.VMEM((ts + CONV_HALO, D_MODEL), F32), pltpu.VMEM((ts + CONV_HALO, D_MODEL), F32),
                        pltpu.VMEM((ts + 2 * SUBLANES, D_MODEL), F32), pltpu.VMEM((SUBLANES, ts + SUBLANES, LANES), F32)],
        compiler_params=_params(),
    )(dcf, dcf, glu, glu, proj, dgate, dw_w, w_in, x1, g_pre, dx2)


def _bwd_out0(dx1, y0, g_post, w_out, gug, o, pooled, pool_w, pool_scale, mix, ts):
    seq = dx1.shape[0]

    def body(dx1_ref, y_ref, g_ref, w_ref, gug_ref, o_ref, pooled_ref, pw_ref, ps_ref, mix_ref,
             do_ref, dgg_ref, dpooled_ref, dpost_ref, dscale_ref, dpw_ref, gw_ref, gacc):
        i = pl.program_id(0)

        @pl.when(i == 0)
        def _():
            dpost_ref[...] = jnp.zeros_like(dpost_ref)
            dscale_ref[...] = jnp.zeros_like(dscale_ref)
            dpw_ref[...] = jnp.zeros_like(dpw_ref)

        y = y_ref[...]
        r = lax.rsqrt(jnp.mean(y * y, axis=-1, keepdims=True) + EPS)
        dy, dpost = _rms_bwd(y, r, g_ref[...], dx1_ref[...])
        dpost_ref[...] += _rows8(dpost)
        dy = dy.astype(BF16)
        _accumulate_tn(gacc, mix_ref[...], dy, i == 0)
        dmix = _dot_nt(dy, w_ref[...])
        dya = dmix[:, :ATTN_WIDTH]
        dyb = dmix[:, ATTN_WIDTH:]
        ga = gug_ref[:, :ATTN_WIDTH].astype(F32)
        gb = gug_ref[:, ATTN_WIDTH + POOL_WIDTH:].astype(F32)
        sga = _sigmoid(ga)
        sgb = _sigmoid(gb)
        do_ref[...] = (dya * (ga * sga)).astype(BF16)
        dgg_ref[:, :ATTN_WIDTH] = (dya * o_ref[...] * (sga * (1.0 + ga * (1.0 - sga)))).astype(BF16)
        dypool = dyb * (gb * sgb)
        dsilu_gb = sgb * (1.0 + gb * (1.0 - sgb))
        for g in range(len(POOL_WINDOWS)):
            lanes = slice(POOL_GC * g, POOL_GC * (g + 1))
            pooled = pooled_ref[:, lanes]
            wg = pw_ref[g].astype(BF16)
            pw = _dot(pooled, wg)
            scale = ps_ref[:, lanes]
            dgg_ref[:, ATTN_WIDTH + POOL_GC * g:ATTN_WIDTH + POOL_GC * (g + 1)] = (
                dyb[:, lanes] * (pw * scale) * dsilu_gb[:, lanes]).astype(BF16)
            dscale_ref[:, lanes] += _rows8(dypool[:, lanes] * pw)
            dpw = (dypool[:, lanes] * scale).astype(BF16)
            dpooled_ref[:, lanes] = _dot_nt(dpw, wg)
            dpw_ref[g] += _dot_tn(pooled, dpw)

        @pl.when(i == seq // ts - 1)
        def _():
            gw_ref[...] = gacc[...].astype(BF16)

    return pl.pallas_call(
        body, name="bwd_out0", grid=(seq // ts,),
        in_specs=[_tile(ts, D_MODEL), _tile(ts, D_MODEL), _full((1, D_MODEL)), _resident((D_MODEL, D_MODEL)),
                  _tile(ts, 3 * POOL_WIDTH), _tile(ts, ATTN_WIDTH), _tile(ts, POOL_WIDTH),
                  _full((4, POOL_GC, POOL_GC)), _full((1, POOL_WIDTH)), _tile(ts, D_MODEL)],
        out_specs=[_tile(ts, ATTN_WIDTH), _tile(ts, ATTN_WIDTH + POOL_WIDTH), _tile(ts, POOL_WIDTH),
                   _full((SUBLANES, D_MODEL)), _full((SUBLANES, POOL_WIDTH)), _full((4, POOL_GC, POOL_GC)),
                   _full((D_MODEL, D_MODEL))],
        out_shape=[jax.ShapeDtypeStruct((seq, ATTN_WIDTH), BF16),
                   jax.ShapeDtypeStruct((seq, ATTN_WIDTH + POOL_WIDTH), BF16), jax.ShapeDtypeStruct((seq, POOL_WIDTH), F32),
                   jax.ShapeDtypeStruct((SUBLANES, D_MODEL), F32), jax.ShapeDtypeStruct((SUBLANES, POOL_WIDTH), F32),
                   jax.ShapeDtypeStruct((4, POOL_GC, POOL_GC), F32), jax.ShapeDtypeStruct((D_MODEL, D_MODEL), BF16)],
        scratch_shapes=[pltpu.VMEM((D_MODEL, D_MODEL), F32)],
        compiler_params=_params(),
    )(dx1, y0, g_post, w_out, gug, o, pooled, pool_w, pool_scale, mix)


def _attn_bwd(q, kv, do, sinks, parts, early):
    seq = q.shape[0]
    nb = seq // BLOCK

    def qblock(j):
        return jnp.minimum(j, nb - 1)

    n = len(parts)

    def body(*refs):
        sink_ref, q_ref, kvc_ref, kvp_ref, do_ref = refs[:5]
        early_ref = refs[5 + n]
        dq_ref, dkv_ref, dsink_ref = refs[6 + n:9 + n]
        all_ref = refs[9 + 2 * n]
        carry, dkv_acc = refs[10 + 2 * n:12 + 2 * n]
        scatter = _Scatter(refs[5:5 + n], refs[9 + n:9 + 2 * n], *refs[12 + 2 * n:14 + 2 * n])
        gather = lambda: _Gather([early_ref], [all_ref], *refs[14 + 2 * n:])
        j = pl.program_id(0)

        @pl.when(j == 0)
        def _():
            gather().begin()
            scatter.begin()
            dsink_ref[...] = jnp.zeros_like(dsink_ref)
            carry[...] = jnp.zeros_like(carry)

        @pl.when(j < nb)
        def _():
            valid, distf = _attn_mask(GROUP_ROWS, j == 0)
            qb = q_ref[...]
            dob = do_ref[...]
            kk = jnp.concatenate([kvp_ref[...], kvc_ref[...]], axis=0)
            lane = lax.broadcasted_iota(jnp.int32, (BLOCK, LANES), 1)
            dsink = jnp.zeros((BLOCK, LANES), F32)
            for kvh in range(Q_HEADS // GROUP):
                kh = kk[:, HEAD_DIM * kvh:HEAD_DIM * (kvh + 1)]
                vh = kk[:, KV_WIDTH + HEAD_DIM * kvh:KV_WIDTH + HEAD_DIM * (kvh + 1)]
                qg = _group_rows(qb, kvh)
                dog = _group_rows(dob, kvh)
                sink, slope = _group_columns(sink_ref, kvh)
                p, psink = _attn_probs(qg, kh, sink, slope, valid, distf)
                dp = _dot_nt(dog, vh)
                delta = jnp.sum(p * dp, axis=-1, keepdims=True)
                ds = (p * (dp - delta) * (HEAD_DIM ** -0.5)).astype(BF16)
                dsink_rows = -psink * delta
                dqg = _dot(ds, kh).astype(BF16)
                for i in range(GROUP):
                    h = GROUP * kvh + i
                    dsink = dsink + jnp.where(lane == h, dsink_rows[BLOCK * i:BLOCK * (i + 1), :], 0.0)
                    dq_ref[:, HEAD_DIM * h:HEAD_DIM * (h + 1)] = dqg[BLOCK * i:BLOCK * (i + 1), :]
                dkv_acc[:, HEAD_DIM * kvh:HEAD_DIM * (kvh + 1)] = _dot_tn(ds, qg)
                dkv_acc[:, KV_WIDTH + HEAD_DIM * kvh:KV_WIDTH + HEAD_DIM * (kvh + 1)] = _dot_tn(p.astype(BF16), dog)
            dsink_ref[...] += dsink

            @pl.when(j > 0)
            def _():
                dkv_ref[...] = (carry[...] + dkv_acc[:BLOCK, :]).astype(BF16)

            carry[...] = dkv_acc[BLOCK:, :]

        @pl.when(j == nb)
        def _():
            dkv_ref[...] = carry[...].astype(BF16)
            gather().relay()
            scatter.finish()
            gather().finish()

    hbm = pl.BlockSpec(memory_space=pl.ANY)
    out = pl.pallas_call(
        body, name="attn_bwd", grid=(nb + 1,),
        in_specs=[pl.BlockSpec(memory_space=pltpu.SMEM),
                  pl.BlockSpec((BLOCK, ATTN_WIDTH), lambda j: (qblock(j), 0)),
                  pl.BlockSpec((BLOCK, 2 * KV_WIDTH), lambda j: (qblock(j), 0)),
                  pl.BlockSpec((BLOCK, 2 * KV_WIDTH), lambda j: (jnp.maximum(qblock(j) - 1, 0), 0)),
                  pl.BlockSpec((BLOCK, ATTN_WIDTH), lambda j: (qblock(j), 0))] + [hbm] * (n + 1),
        out_specs=[pl.BlockSpec((BLOCK, ATTN_WIDTH), lambda j: (qblock(j), 0)),
                   pl.BlockSpec((BLOCK, 2 * KV_WIDTH), lambda j: (jnp.maximum(j - 1, 0), 0)),
                   _full((BLOCK, LANES))] + [hbm] * (n + 1),
        out_shape=[jax.ShapeDtypeStruct((seq, ATTN_WIDTH), BF16), jax.ShapeDtypeStruct((seq, 2 * KV_WIDTH), BF16),
                   jax.ShapeDtypeStruct((BLOCK, LANES), F32)]
        + [jax.ShapeDtypeStruct((N_DEV - 1, *p.shape[1:]), p.dtype) for p in parts]
        + [jax.ShapeDtypeStruct((N_DEV, *early.shape), early.dtype)],
        scratch_shapes=[pltpu.VMEM((BLOCK, 2 * KV_WIDTH), F32), pltpu.VMEM((2 * BLOCK, 2 * KV_WIDTH), F32)]
        + _Scatter.semaphores(n) + _Gather.semaphores(1),
        compiler_params=_params(),
    )(sinks, q, kv, kv, do, *parts, early)
    return out[:3], out[3:3 + n], out[3 + n]


def _bwd_in0(dpooled, dq, dkv, dgg, w_in, x, g_pre, dx1, ts):
    seq = x.shape[0]
    hb = ts // POOL_HALO
    last = seq // POOL_HALO - 1
    nt = seq // ts

    def body(dp_ref, dnext_ref, dq_ref, dkv_ref, dgg_ref, w_ref, x_ref, g_ref, dx1_ref,
             dproj_ref, gx_ref, dpre_ref, dbuf):
        i = pl.program_id(0)

        @pl.when(i == 0)
        def _():
            dpre_ref[...] = jnp.zeros_like(dpre_ref)

        dpool = dp_ref[...]
        dnext = jnp.where(i < nt - 1, dnext_ref[...], 0.0)
        u0 = ATTN_WIDTH + 2 * KV_WIDTH + ATTN_WIDTH
        for g, window in enumerate(POOL_WINDOWS):
            lanes = slice(POOL_GC * g, POOL_GC * (g + 1))
            dbuf[:ts, lanes] = dpool[:, lanes] / _pool_counts(i * ts, ts, window)
            dbuf[ts:, lanes] = dnext[:, lanes] / _pool_counts((i + 1) * ts, POOL_HALO, window)
        for g, window in enumerate(POOL_WINDOWS):
            lanes = slice(POOL_GC * g, POOL_GC * (g + 1))
            acc = dbuf[pl.ds(0, ts), lanes]
            for k in range(1, window):
                acc = acc + dbuf[pl.ds(k, ts), lanes]
            dproj_ref[:, u0 + POOL_GC * g:u0 + POOL_GC * (g + 1)] = (acc - dpool[:, lanes]).astype(BF16)
        dproj_ref[:, :ATTN_WIDTH] = dq_ref[...]
        dproj_ref[:, ATTN_WIDTH:ATTN_WIDTH + 2 * KV_WIDTH] = dkv_ref[...]
        dproj_ref[:, ATTN_WIDTH + 2 * KV_WIDTH:u0] = dgg_ref[:, :ATTN_WIDTH]
        dproj_ref[:, u0 + POOL_WIDTH:] = dgg_ref[:, ATTN_WIDTH:]
        dh = _dot(dproj_ref[...], w_ref[...])
        x = x_ref[...]
        r = lax.rsqrt(jnp.mean(x * x, axis=-1, keepdims=True) + EPS)
        dx, dpre = _rms_bwd(x, r, g_ref[...], dh)
        gx_ref[...] = dx1_ref[...] + dx
        dpre_ref[...] += _rows8(dpre)

    return pl.pallas_call(
        body, name="bwd_in0", grid=(nt,),
        in_specs=[_tile(ts, POOL_WIDTH),
                  pl.BlockSpec((POOL_HALO, POOL_WIDTH), lambda i: (jnp.minimum((i + 1) * hb, last), 0)),
                  _tile(ts, ATTN_WIDTH), _tile(ts, 2 * KV_WIDTH), _tile(ts, ATTN_WIDTH + POOL_WIDTH),
                  _full((EVEN_IN, D_MODEL)), _tile(ts, D_MODEL), _full((1, D_MODEL)), _tile(ts, D_MODEL)],
        out_specs=[_tile(ts, EVEN_IN), _tile(ts, D_MODEL), _full((SUBLANES, D_MODEL))],
        out_shape=[jax.ShapeDtypeStruct((seq, EVEN_IN), BF16), jax.ShapeDtypeStruct((seq, D_MODEL), F32),
                   jax.ShapeDtypeStruct((SUBLANES, D_MODEL), F32)],
        scratch_shapes=[pltpu.VMEM((ts + POOL_HALO, POOL_WIDTH), F32)],
        compiler_params=_params(),
    )(dpooled, dpooled, dq, dkv, dgg, w_in, x, g_pre, dx1)


def _matmul_tn(a, b, name, ts, tm):
    seq, m = a.shape
    n = b.shape[1]
    steps = seq // ts

    def body(a_ref, b_ref, o_ref, acc):
        s = pl.program_id(1)

        @pl.when(s == 0)
        def _():
            acc[...] = jnp.zeros_like(acc)

        acc[...] += _dot_tn(a_ref[...], b_ref[...])

        @pl.when(s == steps - 1)
        def _():
            o_ref[...] = acc[...].astype(BF16)

    return pl.pallas_call(
        body, name=name, grid=(m // tm, steps),
        in_specs=[pl.BlockSpec((ts, tm), lambda j, s: (s, j)), pl.BlockSpec((ts, n), lambda j, s: (s, 0))],
        out_specs=pl.BlockSpec((tm, n), lambda j, s: (j, 0)),
        out_shape=jax.ShapeDtypeStruct((m, n), BF16),
        scratch_shapes=[pltpu.VMEM((tm, n), F32)],
        compiler_params=pltpu.CompilerParams(dimension_semantics=("arbitrary", "arbitrary"), vmem_limit_bytes=VMEM_LIMIT),
    )(a, b)


def _adamw_math(w, g, m, v):
    m = ADAM_B1 * m + (1.0 - ADAM_B1) * g
    v = ADAM_B2 * v + (1.0 - ADAM_B2) * (g * g)
    m_hat = m / (1.0 - ADAM_B1 ** ADAM_STEP)
    v_hat = v / (1.0 - ADAM_B2 ** ADAM_STEP)
    delta = -ADAM_LR * (m_hat / (jnp.sqrt(v_hat) + ADAM_EPS) + ADAM_WD * w)
    return delta, m, v


def _adamw(ws, gs, ms, vs, name):
    n = len(ws)

    def body(*refs):
        ins, outs = refs[:4 * n], refs[4 * n:]
        for k in range(n):
            delta, m, v = _adamw_math(ins[k][...], ins[n + k][...], ins[2 * n + k][...], ins[3 * n + k][...])
            outs[k][...] = delta
            outs[n + k][...] = m
            outs[2 * n + k][...] = v

    shapes = [jax.ShapeDtypeStruct(w.shape, F32) for w in ws]
    out = pl.pallas_call(body, name=name, out_shape=shapes * 3,
                         compiler_params=pltpu.CompilerParams(vmem_limit_bytes=VMEM_LIMIT))(*ws, *gs, *ms, *vs)
    return out[:n], out[n:2 * n], out[2 * n:]


TS_MATMUL, TS_IN1, TS_CONV_FWD, TS_CONV_BWD, TS_GRAD = 512, 1024, 512, 256, 1024


def kernel(x, pre_norm, post_norm, a_w_in, a_sinks, b_pool_w, b_pool_scale, ab_w_out, c_w_in, c_dw_w, c_dw_b, c_ln_g, c_ln_b, c_w_out, loss_target, m_pre_norm, m_post_norm, m_a_w_in, m_a_sinks, m_b_pool_w, m_b_pool_scale, m_ab_w_out, m_c_w_in, m_c_dw_w, m_c_dw_b, m_c_ln_g, m_c_ln_b, m_c_w_out, v_pre_norm, v_post_norm, v_a_w_in, v_a_sinks, v_b_pool_w, v_b_pool_scale, v_ab_w_out, v_c_w_in, v_c_dw_w, v_c_dw_b, v_c_ln_g, v_c_ln_b, v_c_w_out):
    seq = x.shape[1]
    ts_big, ts_grad = min(TS_MATMUL, seq), min(TS_GRAD, seq)
    x2d = x[0]
    target = loss_target[0]
    ch = c_dw_b.shape[1]

    whole = lambda g: g.reshape(-1, D_MODEL)
    vec_rows = 40
    vecs = jnp.concatenate([c_dw_w[0, :, 0, :], c_dw_b, c_ln_g, c_ln_b, jnp.zeros((vec_rows - CONV_K - 3, ch), F32)], axis=0)
    w_in0t, vg = _all_gather([a_w_in[0].T.astype(BF16), vecs], "gather_first")
    w_in0t = whole(w_in0t)
    vg = vg.transpose(1, 0, 2).reshape(vec_rows, D_MODEL)
    dw_w = vg[:CONV_HALO]
    dw_b, ln_g, ln_b = vg[CONV_K:CONV_K + 1], vg[CONV_K + 1:CONV_K + 2], vg[CONV_K + 2:CONV_K + 3]

    sinks = a_sinks[0]
    (h0, q, kv, gug), (w_out0,) = _fwd_in0(x2d, pre_norm[0:1], w_in0t, [ab_w_out[0].astype(BF16)], ts_big)
    o, (w_in1t, w_out1) = _attn_fwd(q, kv, sinks, [c_w_in[0].T.astype(BF16), c_w_out[0].astype(BF16)])
    w_out0, w_in1t, w_out1 = whole(w_out0), whole(w_in1t), whole(w_out1)
    mix0, pooled, y0, x1 = _fwd_out0(gug, o, b_pool_w[0], b_pool_scale, w_out0, post_norm[0:1], x2d, ts_big)
    h1, proj1, glu = _fwd_in1(x1, pre_norm[1:2], w_in1t, min(TS_IN1, seq))
    ymix1, dy1, dx2, dcf, dgate, acc1 = _fwd_out1(glu, proj1, dw_w, dw_b, ln_g, ln_b, w_out1, post_norm[1:2], x1, target, min(TS_CONV_FWD, seq))

    dproj1, dx1, ddw_w, dpre1 = _bwd_in1(dcf, glu, proj1, dgate, dw_w, w_in1t, x1, pre_norm[1:2], dx2, min(TS_CONV_BWD, seq))
    g_in1t = _matmul_tn(dproj1, h1, "grad_w_in1", ts_grad, 1024)
    g_out1 = _matmul_tn(ymix1, dy1, "grad_w_out1", ts_grad, 1024)
    do, dgg, dpooled, dpost0, dscale, dpool_w, g_out0 = _bwd_out0(dx1, y0, post_norm[0:1], w_out0, gug, o, pooled, b_pool_w[0], b_pool_scale, mix0, ts_big)
    slabs = lambda g: g.reshape(N_DEV, -1, D_MODEL)
    me = 4 * lax.axis_index("x") + 2 * lax.axis_index("y") + lax.axis_index("c")
    early = [slabs(g_in1t), slabs(g_out1), slabs(g_out0)]
    row = lambda a: jnp.sum(a, axis=0, keepdims=True)
    lanes8 = lambda a: row(a).reshape(-1, LANES)
    loss_row = jnp.pad(jnp.sum(acc1[ACC_LOSS]).reshape(1, 1), ((0, 0), (0, LANES - 1)))
    done = jnp.concatenate([lanes8(dpre1), lanes8(dpost0), lanes8(acc1[ACC_POST]), lanes8(dscale),
                            loss_row, jnp.zeros((3, LANES), F32), dpool_w.reshape(4 * POOL_GC, LANES)], axis=0)
    (dq, dkv, dsink), arrived, done_all = _attn_bwd(q, kv, do, sinks, early, done)
    g_in1t, g_c_w_out, g_ab_w_out = _sum_slabs([lax.dynamic_index_in_dim(p, me, keepdims=False) for p in early],
                                               arrived, "sum_early")
    dproj0, grad_x, dpre0 = _bwd_in0(dpooled, dq, dkv, dgg, w_in0t, x2d, pre_norm[0:1], dx1, ts_big)
    g_in0t = _matmul_tn(dproj0, h0, "grad_w_in0", ts_grad, 768)

    vec_g = jnp.concatenate([jnp.sum(ddw_w[:CONV_K], axis=1), row(acc1[ACC_DW_B]), row(acc1[ACC_LN_G]), row(acc1[ACC_LN_B]),
                             jnp.zeros((vec_rows - CONV_K - 3, D_MODEL), F32)], axis=0)
    last = jnp.concatenate([lanes8(dpre0), row(dsink), jnp.zeros((SUBLANES - 1, LANES), F32)], axis=0)
    g_in0t, vec_g, last, rep = _final_reduce(slabs(g_in0t), vec_g.reshape(vec_rows, N_DEV, ch).transpose(1, 0, 2),
                                             last, done_all, "final_reduce")
    g_a_w_in, g_c_w_in = g_in0t.T, g_in1t.T
    g_dw_w, g_dw_b, g_ln_g, g_ln_b = vec_g[:CONV_K], vec_g[CONV_K:CONV_K + 1], vec_g[CONV_K + 1:CONV_K + 2], vec_g[CONV_K + 2:CONV_K + 3]
    g_pre = jnp.concatenate([last[:8], rep[:8]], axis=0).reshape(2, D_MODEL)
    g_sinks = last[8:9, :Q_HEADS]
    g_post = rep[8:24].reshape(2, D_MODEL)
    g_scale = rep[24:28].reshape(1, POOL_WIDTH)
    loss = (0.5 / D_MODEL) * rep[28, 0]
    g_pool_w = rep[32:]

    grads = [g_pre, g_post, g_a_w_in, g_sinks, g_pool_w, g_scale, g_ab_w_out, g_c_w_in, g_dw_w, g_dw_b, g_ln_g, g_ln_b, g_c_w_out]
    weights = [pre_norm, post_norm, a_w_in, a_sinks, b_pool_w, b_pool_scale, ab_w_out, c_w_in, c_dw_w, c_dw_b, c_ln_g, c_ln_b, c_w_out]
    m_in = [m_pre_norm, m_post_norm, m_a_w_in, m_a_sinks, m_b_pool_w, m_b_pool_scale, m_ab_w_out, m_c_w_in, m_c_dw_w, m_c_dw_b, m_c_ln_g, m_c_ln_b, m_c_w_out]
    v_in = [v_pre_norm, v_post_norm, v_a_w_in, v_a_sinks, v_b_pool_w, v_b_pool_scale, v_ab_w_out, v_c_w_in, v_c_dw_w, v_c_dw_b, v_c_ln_g, v_c_ln_b, v_c_w_out]
    flat = lambda arrs: [a.reshape(g.shape) for a, g in zip(arrs, grads)]
    big = (2, 6, 7, 12)
    small = tuple(k for k in range(len(grads)) if k not in big)
    pick = lambda arrs, idx: [arrs[k] for k in idx]
    deltas, new_m, new_v = [None] * 13, [None] * 13, [None] * 13
    for idx, name in ((big, "adamw_matrices"), (small, "adamw_vectors")):
        d, m, v = _adamw(pick(flat(weights), idx), pick(grads, idx), pick(flat(m_in), idx), pick(flat(v_in), idx), name)
        for k, dk, mk, vk in zip(idx, d, m, v):
            deltas[k], new_m[k], new_v[k] = dk, mk, vk
    shaped = lambda arrs: [a.reshape(w.shape) for a, w in zip(arrs, weights)]
    return (loss, grad_x[None], *shaped(grads), *shaped(deltas), *shaped(new_m), *shaped(new_v))
```

```python
import jax
import jax.numpy as jnp
from jax import lax
from jax.experimental import pallas as pl
from jax.experimental.pallas import tpu as pltpu

F32 = jnp.float32
BF16 = jnp.bfloat16
MESH = pl.DeviceIdType.MESH
N_DEV = 8

D_MODEL = 1024
HEAD_DIM = 64
Q_HEADS = 8
GROUP = 4
ATTN_WIDTH = 512
KV_WIDTH = 128
BLOCK = 128
POOL_WIDTH = 512
POOL_WINDOWS = (2, 4, 8, 16)
POOL_GC = 128
POOL_HALO = 16
EVEN_IN = 2304
CONV_K = 31
CONV_HALO = 32
ODD_IN = 3072
EPS = 1e-6
NEG = -1e30
SLOPES = tuple(2.0 ** (-8.0 * (h + 1) / Q_HEADS) for h in range(Q_HEADS))

ADAM_LR = 0.001
ADAM_B1 = 0.9
ADAM_B2 = 0.999
ADAM_EPS = 1e-08
ADAM_WD = 0.01
ADAM_STEP = 10

SUBLANES = 8
LANES = 128
VMEM_LIMIT = 56 * 1024 * 1024

NT = (((1,), (1,)), ((), ()))
TN = (((0,), (0,)), ((), ()))


def _params(**kw):
    return pltpu.CompilerParams(dimension_semantics=("arbitrary",), vmem_limit_bytes=VMEM_LIMIT, **kw)


def _dot(a, b):
    return jnp.dot(a, b, preferred_element_type=F32)


def _dot_nt(a, b):
    return lax.dot_general(a, b, NT, preferred_element_type=F32)


def _dot_tn(a, b):
    return lax.dot_general(a, b, TN, preferred_element_type=F32)


def _sigmoid(v):
    return 1.0 / (1.0 + jnp.exp(-v))


def _rows8(v):
    r, c = v.shape
    return jnp.sum(v.reshape(r // SUBLANES, SUBLANES, c), axis=0)


def _rms_fwd(v, g):
    r = lax.rsqrt(jnp.mean(v * v, axis=-1, keepdims=True) + EPS)
    return v * r * g, r


def _rms_bwd(v, r, g, dout):
    gd = dout * g
    dv = r * gd - v * (r * r * r) * jnp.mean(v * gd, axis=-1, keepdims=True)
    return dv, dout * (v * r)


def _full(shape):
    return pl.BlockSpec(shape, lambda i: (0,) * len(shape))


def _resident(shape):
    return pl.BlockSpec(shape, lambda i: (0,) * len(shape), pipeline_mode=pl.Buffered(1))


def _accumulate_tn(acc, a, b, first_step):
    @pl.when(first_step)
    def _():
        acc[...] = jnp.zeros_like(acc)

    acc[...] += _dot_tn(a, b)


def _tile(ts, cols, col_block=0):
    return pl.BlockSpec((ts, cols), lambda i: (i, col_block))


def _position():
    return lax.axis_index("x"), lax.axis_index("y"), lax.axis_index("c")


class _Gather:
    def __init__(self, x_refs, out_refs, send_sems, recv_sems, local_sems):
        self.x_refs, self.out_refs = x_refs, out_refs
        self.send_sems, self.recv_sems, self.local_sems = send_sems, recv_sems, local_sems
        self.n = len(x_refs)
        x, y, c = _position()
        self.c = c
        self.me, self.sibling = (x, y, c), (x, y, 1 - c)
        self.chips = [(1 - x, y), (x, 1 - y), (1 - x, 1 - y)]

    def copy(self, k, j, owner, to, src=None):
        slab = self.out_refs[k].at[4 * owner[0] + 2 * owner[1] + owner[2]]
        return pltpu.make_async_remote_copy(
            src_ref=slab if src is None else src, dst_ref=slab, send_sem=self.send_sems.at[7 * k + j],
            recv_sem=self.recv_sems.at[7 * k + j], device_id=to, device_id_type=MESH)

    def mine(self, k):
        return pltpu.make_async_copy(self.x_refs[k], self.out_refs[k].at[4 * self.me[0] + 2 * self.me[1] + self.c],
                                     self.local_sems.at[k])

    def first(self, k):
        out = [self.copy(k, 0, self.me, self.sibling, src=self.x_refs[k])]
        return out + [self.copy(k, 1 + j, self.me, (*chip, self.c), src=self.x_refs[k]) for j, chip in enumerate(self.chips)]

    def begin(self):
        for k in range(self.n):
            self.mine(k).start()
            for cp in self.first(k):
                cp.start()

    def passed(self, k):
        return [self.copy(k, 4 + j, (*chip, self.c), self.sibling) for j, chip in enumerate(self.chips)]

    def relay(self):
        for k in range(self.n):
            for j, chip in enumerate(self.chips):
                self.copy(k, 1 + j, (*chip, self.c), self.me).wait_recv()
                self.passed(k)[j].start()

    def finish(self):
        for k in range(self.n):
            self.copy(k, 0, self.sibling, self.me).wait_recv()
            for j, chip in enumerate(self.chips):
                self.copy(k, 4 + j, (*chip, 1 - self.c), self.me).wait_recv()
        for k in range(self.n):
            for cp in self.first(k) + self.passed(k):
                cp.wait_send()
            self.mine(k).wait()

    @staticmethod
    def semaphores(n):
        return [pltpu.SemaphoreType.DMA((7 * n,)), pltpu.SemaphoreType.DMA((7 * n,)), pltpu.SemaphoreType.DMA((n,))]


def RELAY_AT(steps):
    return steps * 3 // 4


def _all_gather(blocks, name):
    n = len(blocks)

    def body(*refs):
        gather = _Gather(refs[:n], refs[n:2 * n], *refs[2 * n:])
        gather.begin()
        gather.relay()
        gather.finish()

    return pl.pallas_call(
        body, name=name,
        out_shape=[jax.ShapeDtypeStruct((N_DEV, *b.shape), b.dtype) for b in blocks],
        in_specs=[pl.BlockSpec(memory_space=pltpu.VMEM)] * n,
        out_specs=[pl.BlockSpec(memory_space=pltpu.VMEM)] * n,
        scratch_shapes=_Gather.semaphores(n),
        compiler_params=pltpu.CompilerParams(vmem_limit_bytes=VMEM_LIMIT),
    )(*blocks)


class _Scatter:
    def __init__(self, part_refs, recv_refs, send_sems, recv_sems):
        self.part_refs, self.recv_refs, self.send_sems, self.recv_sems = part_refs, recv_refs, send_sems, recv_sems
        self.n = len(part_refs)

    def copies(self):
        x, y, c = _position()
        me = 4 * x + 2 * y + c
        out = []
        for k in range(self.n):
            for j in range(N_DEV - 1):
                d = j + 1
                out.append(pltpu.make_async_remote_copy(
                    src_ref=self.part_refs[k].at[me ^ d], dst_ref=self.recv_refs[k].at[j],
                    send_sem=self.send_sems.at[7 * k + j], recv_sem=self.recv_sems.at[7 * k + j],
                    device_id=(x ^ (d >> 2), y ^ ((d >> 1) & 1), c ^ (d & 1)), device_id_type=MESH))
        return out

    def begin(self):
        for cp in self.copies():
            cp.start()

    def finish(self):
        for cp in self.copies():
            cp.wait_recv()
        for cp in self.copies():
            cp.wait_send()

    @staticmethod
    def semaphores(n):
        return [pltpu.SemaphoreType.DMA((7 * n,)), pltpu.SemaphoreType.DMA((7 * n,))]


def _sum_slabs(own, recv, name):
    n = len(own)

    def body(*refs):
        for k in range(n):
            acc = refs[k][...].astype(F32)
            for j in range(N_DEV - 1):
                acc = acc + refs[n + k][j].astype(F32)
            refs[2 * n + k][...] = acc

    return pl.pallas_call(body, name=name, out_shape=[jax.ShapeDtypeStruct(o.shape, F32) for o in own])(*own, *recv)


N_CHIPS = 4


def _final_reduce(parts, vec_parts, rep, early_all, name):
    _, rows, cols = parts.shape
    vrows, prows = vec_parts.shape[1], rep.shape[0]
    peers = N_CHIPS - 1

    def body(parts_ref, vec_ref, rep_ref, early_ref, out_ref, vec_out, rep_out, early_out, recv_a, own_a, mid, recv_b, vec_recv, rep_all,
             a_send, a_recv, a_local, b_send, b_recv, v_send, v_recv, r_send, r_recv):
        x, y, c = _position()
        chip = 2 * x + y
        me = 2 * chip + c
        everyone = [(d, (x ^ (d >> 2), y ^ ((d >> 1) & 1), c ^ (d & 1))) for d in range(1, N_DEV)]

        stage1, loads = [], []
        for j in range(N_CHIPS):
            stage1.append(pltpu.make_async_remote_copy(
                src_ref=parts_ref.at[2 * j + 1 - c], dst_ref=recv_a.at[j], send_sem=a_send.at[j], recv_sem=a_recv.at[j],
                device_id=(x, y, 1 - c), device_id_type=MESH))
            loads.append(pltpu.make_async_copy(parts_ref.at[2 * j + c], own_a.at[j], a_local.at[j]))
            stage1[-1].start()
            loads[-1].start()
        small = []
        for d, to in everyone:
            small.append(pltpu.make_async_remote_copy(
                src_ref=vec_ref.at[me ^ d], dst_ref=vec_recv.at[d - 1], send_sem=v_send.at[d - 1], recv_sem=v_recv.at[d - 1],
                device_id=to, device_id_type=MESH))
            small.append(pltpu.make_async_remote_copy(
                src_ref=rep_ref, dst_ref=rep_all.at[me], send_sem=r_send.at[d - 1], recv_sem=r_recv.at[d - 1],
                device_id=to, device_id_type=MESH))
        for cp in small:
            cp.start()
        rep_all[me] = rep_ref[...]

        for j in range(N_CHIPS):
            loads[j].wait()
            stage1[j].wait_recv()
            mid[j] = (own_a[j].astype(F32) + recv_a[j].astype(F32)).astype(BF16)
        stage2 = []
        for d in range(1, N_CHIPS):
            stage2.append(pltpu.make_async_remote_copy(
                src_ref=mid.at[chip ^ d], dst_ref=recv_b.at[d - 1], send_sem=b_send.at[d - 1], recv_sem=b_recv.at[d - 1],
                device_id=(x ^ (d >> 1), y ^ (d & 1), c), device_id_type=MESH))
            stage2[-1].start()

        for cp in small:
            cp.wait_recv()
        vec_sum = vec_ref[me]
        for j in range(N_DEV - 1):
            vec_sum = vec_sum + vec_recv[j]
        vec_out[...] = vec_sum
        rep_sum = rep_all[0]
        for d in range(1, N_DEV):
            rep_sum = rep_sum + rep_all[d]
        rep_out[...] = rep_sum
        early_sum = early_ref[0]
        for d in range(1, N_DEV):
            early_sum = early_sum + early_ref[d]
        early_out[...] = early_sum

        acc = mid[chip].astype(F32)
        for d in range(peers):
            stage2[d].wait_recv()
            acc = acc + recv_b[d].astype(F32)
        out_ref[...] = acc
        for cp in stage1 + stage2 + small:
            cp.wait_send()

    vmem = pl.BlockSpec(memory_space=pltpu.VMEM)
    dma = pltpu.SemaphoreType.DMA
    return pl.pallas_call(
        body, name=name,
        out_shape=[jax.ShapeDtypeStruct((rows, cols), F32), jax.ShapeDtypeStruct((vrows, LANES), F32),
                   jax.ShapeDtypeStruct((prows, LANES), F32), jax.ShapeDtypeStruct(early_all.shape[1:], F32)],
        in_specs=[pl.BlockSpec(memory_space=pl.ANY), vmem, vmem, vmem],
        out_specs=[vmem, vmem, vmem, vmem],
        scratch_shapes=[pltpu.VMEM((N_CHIPS, rows, cols), BF16), pltpu.VMEM((N_CHIPS, rows, cols), BF16),
                        pltpu.VMEM((N_CHIPS, rows, cols), BF16), pltpu.VMEM((peers, rows, cols), BF16),
                        pltpu.VMEM((N_DEV - 1, vrows, LANES), F32), pltpu.VMEM((N_DEV, prows, LANES), F32),
                        dma((N_CHIPS,)), dma((N_CHIPS,)), dma((N_CHIPS,)), dma((peers,)), dma((peers,)),
                        dma((N_DEV - 1,)), dma((N_DEV - 1,)), dma((N_DEV - 1,)), dma((N_DEV - 1,))],
        compiler_params=pltpu.CompilerParams(vmem_limit_bytes=VMEM_LIMIT),
    )(parts, vec_parts, rep, early_all)


def _fwd_in0(x, g, w, shards, ts):
    seq = x.shape[0]
    n = len(shards)
    steps = seq // ts

    def body(*refs):
        x_ref, g_ref, w_ref = refs[:3]
        h_ref, q_ref, kv_ref, gug_ref = refs[3 + n:7 + n]
        gather = lambda: _Gather(refs[3:3 + n], refs[7 + n:7 + 2 * n], *refs[7 + 2 * n:])
        i = pl.program_id(0)

        @pl.when(i == 0)
        def _():
            gather().begin()

        h, _ = _rms_fwd(x_ref[...], g_ref[...])
        h = h.astype(BF16)
        h_ref[...] = h
        proj = _dot_nt(h, w_ref[...])
        q_ref[...] = proj[:, :ATTN_WIDTH].astype(BF16)
        kv_ref[...] = proj[:, ATTN_WIDTH:ATTN_WIDTH + 2 * KV_WIDTH].astype(BF16)
        gug_ref[...] = proj[:, ATTN_WIDTH + 2 * KV_WIDTH:].astype(BF16)

        @pl.when(i == RELAY_AT(steps))
        def _():
            gather().relay()

        @pl.when(i == steps - 1)
        def _():
            gather().finish()

    hbm = pl.BlockSpec(memory_space=pl.ANY)
    out = pl.pallas_call(
        body, name="fwd_in0", grid=(steps,),
        in_specs=[_tile(ts, D_MODEL), _full((1, D_MODEL)), _full((EVEN_IN, D_MODEL))] + [hbm] * n,
        out_specs=[_tile(ts, D_MODEL), _tile(ts, ATTN_WIDTH), _tile(ts, 2 * KV_WIDTH), _tile(ts, 3 * POOL_WIDTH)] + [hbm] * n,
        out_shape=[jax.ShapeDtypeStruct((seq, D_MODEL), BF16), jax.ShapeDtypeStruct((seq, ATTN_WIDTH), BF16),
                   jax.ShapeDtypeStruct((seq, 2 * KV_WIDTH), BF16), jax.ShapeDtypeStruct((seq, 3 * POOL_WIDTH), BF16)]
        + [jax.ShapeDtypeStruct((N_DEV, *b.shape), b.dtype) for b in shards],
        scratch_shapes=_Gather.semaphores(n),
        compiler_params=_params(),
    )(x, g, w, *shards)
    return out[:4], out[4:]


GROUP_ROWS = GROUP * BLOCK
FWD_BLOCKS = 4


def _attn_mask(rows, first_block):
    row = lax.broadcasted_iota(jnp.int32, (rows, 2 * BLOCK), 0) & (BLOCK - 1)
    col = lax.broadcasted_iota(jnp.int32, (rows, 2 * BLOCK), 1)
    dist = row + BLOCK - col
    valid = (dist >= 0) & (dist < BLOCK) & ((col >= BLOCK) | jnp.logical_not(first_block))
    return valid, dist.astype(F32)


def _group_rows(block, kvh):
    return jnp.concatenate([block[:, HEAD_DIM * h:HEAD_DIM * (h + 1)] for h in range(GROUP * kvh, GROUP * (kvh + 1))], axis=0)


def _group_columns(sink_ref, kvh):
    head = lax.broadcasted_iota(jnp.int32, (GROUP_ROWS, 1), 0) // BLOCK
    sink = jnp.zeros((GROUP_ROWS, 1), F32)
    slope = jnp.zeros((GROUP_ROWS, 1), F32)
    for i in range(GROUP):
        sink = jnp.where(head == i, sink_ref[GROUP * kvh + i], sink)
        slope = jnp.where(head == i, SLOPES[GROUP * kvh + i], slope)
    return sink, slope


def _attn_probs(qh, kh, sink, slope, valid, distf):
    s = _dot_nt(qh, kh) * (HEAD_DIM ** -0.5)
    s = jnp.where(valid, s - slope * distf, NEG)
    mx = jnp.maximum(jnp.max(s, axis=-1, keepdims=True), sink)
    e = jnp.exp(s - mx)
    es = jnp.exp(sink - mx)
    den = jnp.sum(e, axis=-1, keepdims=True) + es
    return e / den, es / den


def _attn_fwd(q, kv, sinks, shards):
    seq = q.shape[0]
    per = min(FWD_BLOCKS, seq // BLOCK)
    nb = seq // (per * BLOCK)
    ns = len(shards)

    def body(*refs):
        sink_ref, q_ref, kvc_ref, kvp_ref = refs[:4]
        o_ref = refs[4 + ns]
        gather = lambda: _Gather(refs[4:4 + ns], refs[5 + ns:5 + 2 * ns], *refs[5 + 2 * ns:])
        n = pl.program_id(0)

        @pl.when(n == 0)
        def _():
            gather().begin()

        keys = jnp.concatenate([kvp_ref[...], kvc_ref[...]], axis=0)
        for sub in range(per):
            rows = slice(BLOCK * sub, BLOCK * (sub + 1))
            valid, distf = _attn_mask(BLOCK, (n == 0) if sub == 0 else False)
            qb = q_ref[rows, :]
            kk = keys[BLOCK * sub:BLOCK * (sub + 2), :]
            for h in range(Q_HEADS):
                kvh = h // GROUP
                qh = qb[:, HEAD_DIM * h:HEAD_DIM * (h + 1)]
                kh = kk[:, HEAD_DIM * kvh:HEAD_DIM * (kvh + 1)]
                vh = kk[:, KV_WIDTH + HEAD_DIM * kvh:KV_WIDTH + HEAD_DIM * (kvh + 1)]
                p, _ = _attn_probs(qh, kh, sink_ref[h], SLOPES[h], valid, distf)
                o_ref[rows, HEAD_DIM * h:HEAD_DIM * (h + 1)] = _dot(p.astype(BF16), vh)

        @pl.when(n == RELAY_AT(nb))
        def _():
            gather().relay()

        @pl.when(n == nb - 1)
        def _():
            gather().finish()

    hbm = pl.BlockSpec(memory_space=pl.ANY)
    out = pl.pallas_call(
        body, name="attn_fwd", grid=(nb,),
        in_specs=[pl.BlockSpec(memory_space=pltpu.SMEM),
                  pl.BlockSpec((per * BLOCK, ATTN_WIDTH), lambda n: (n, 0)),
                  pl.BlockSpec((per * BLOCK, 2 * KV_WIDTH), lambda n: (n, 0)),
                  pl.BlockSpec((BLOCK, 2 * KV_WIDTH), lambda n: (jnp.maximum(per * n - 1, 0), 0))] + [hbm] * ns,
        out_specs=[pl.BlockSpec((per * BLOCK, ATTN_WIDTH), lambda n: (n, 0))] + [hbm] * ns,
        out_shape=[jax.ShapeDtypeStruct((seq, ATTN_WIDTH), F32)]
        + [jax.ShapeDtypeStruct((N_DEV, *b.shape), b.dtype) for b in shards],
        scratch_shapes=_Gather.semaphores(ns),
        compiler_params=_params(),
    )(sinks, q, kv, kv, *shards)
    return out[0], out[1:]


def _pool_counts(first_row, rows, window):
    t = first_row + lax.broadcasted_iota(jnp.int32, (rows, 1), 0)
    return jnp.minimum(t + 1, window).astype(F32)


def _fwd_out0(gug, o, pool_w, pool_scale, w_out, g_post, x, ts):
    seq = x.shape[0]
    hb = ts // POOL_HALO

    def body(gug_ref, halo_ref, o_ref, pw_ref, ps_ref, w_ref, g_ref, x_ref, mix_ref, pooled_ref, y_ref, x1_ref, ubuf):
        i = pl.program_id(0)
        ga = gug_ref[:, :ATTN_WIDTH].astype(F32)
        u = gug_ref[:, ATTN_WIDTH:ATTN_WIDTH + POOL_WIDTH].astype(F32)
        gb = gug_ref[:, ATTN_WIDTH + POOL_WIDTH:].astype(F32)
        mix_ref[:, :ATTN_WIDTH] = (o_ref[...] * (ga * _sigmoid(ga))).astype(BF16)
        ubuf[:POOL_HALO, :] = jnp.where(i > 0, halo_ref[...].astype(F32), 0.0)
        ubuf[POOL_HALO:, :] = u
        silu_gb = gb * _sigmoid(gb)
        for g, window in enumerate(POOL_WINDOWS):
            lanes = slice(POOL_GC * g, POOL_GC * (g + 1))
            acc = ubuf[pl.ds(POOL_HALO, ts), lanes]
            for k in range(1, window):
                acc = acc + ubuf[pl.ds(POOL_HALO - k, ts), lanes]
            pooled = (acc / _pool_counts(i * ts, ts, window) - u[:, lanes]).astype(BF16)
            pooled_ref[:, lanes] = pooled
            ypool = _dot(pooled, pw_ref[g].astype(BF16)) * ps_ref[:, lanes]
            mix_ref[:, ATTN_WIDTH + POOL_GC * g:ATTN_WIDTH + POOL_GC * (g + 1)] = (ypool * silu_gb[:, lanes]).astype(BF16)
        y = _dot(mix_ref[...], w_ref[...])
        y_ref[...] = y
        yn, _ = _rms_fwd(y, g_ref[...])
        x1_ref[...] = x_ref[...] + yn

    return pl.pallas_call(
        body, name="fwd_out0", grid=(seq // ts,),
        in_specs=[_tile(ts, 3 * POOL_WIDTH),
                  pl.BlockSpec((POOL_HALO, POOL_WIDTH), lambda i: (jnp.maximum(i * hb - 1, 0), 1)),
                  _tile(ts, ATTN_WIDTH), _full((4, POOL_GC, POOL_GC)), _full((1, POOL_WIDTH)),
                  _full((D_MODEL, D_MODEL)), _full((1, D_MODEL)), _tile(ts, D_MODEL)],
        out_specs=[_tile(ts, D_MODEL), _tile(ts, POOL_WIDTH), _tile(ts, D_MODEL), _tile(ts, D_MODEL)],
        out_shape=[jax.ShapeDtypeStruct((seq, D_MODEL), BF16), jax.ShapeDtypeStruct((seq, POOL_WIDTH), BF16),
                   jax.ShapeDtypeStruct((seq, D_MODEL), F32), jax.ShapeDtypeStruct((seq, D_MODEL), F32)],
        scratch_shapes=[pltpu.VMEM((ts + POOL_HALO, POOL_WIDTH), F32)],
        compiler_params=_params(),
    )(gug, gug, o, pool_w, pool_scale, w_out, g_post, x)


def _fwd_in1(x1, g, w, ts):
    seq = x1.shape[0]

    def body(x_ref, g_ref, w_ref, h_ref, proj_ref, glu_ref):
        h, _ = _rms_fwd(x_ref[...], g_ref[...])
        h = h.astype(BF16)
        h_ref[...] = h
        proj = _dot_nt(h, w_ref[...])
        proj_ref[...] = proj.astype(BF16)
        glu_ref[...] = proj[:, :D_MODEL] * _sigmoid(proj[:, D_MODEL:2 * D_MODEL])

    return pl.pallas_call(
        body, name="fwd_in1", grid=(seq // ts,),
        in_specs=[_tile(ts, D_MODEL), _full((1, D_MODEL)), _full((ODD_IN, D_MODEL))],
        out_specs=[_tile(ts, D_MODEL), _tile(ts, ODD_IN), _tile(ts, D_MODEL)],
        out_shape=[jax.ShapeDtypeStruct((seq, D_MODEL), BF16), jax.ShapeDtypeStruct((seq, ODD_IN), BF16),
                   jax.ShapeDtypeStruct((seq, D_MODEL), F32)],
        compiler_params=_params(),
    )(x1, g, w)


ACC_LOSS, ACC_POST, ACC_LN_G, ACC_LN_B, ACC_DW_B = range(5)
DDW_ROWS = 88
CONV_FIRST = CONV_HALO - CONV_K + 1


def _fwd_tap(offset):
    return offset - CONV_FIRST if CONV_FIRST <= offset <= CONV_HALO else None


def _bwd_tap(offset):
    return CONV_K - 1 - offset if offset < CONV_K else None


def _conv_taps(w_ref, buf_ref, ts, lanes, tap_of_offset):
    out = None
    for b in range(SUBLANES):
        rows = ts if b == 0 else ts + SUBLANES
        part = None
        for a in range(CONV_HALO // SUBLANES + 1):
            k = tap_of_offset(SUBLANES * a + b)
            if k is None:
                continue
            term = w_ref[k:k + 1, lanes] * buf_ref[pl.ds(SUBLANES * a, rows), lanes]
            part = term if part is None else part + term
        if part is None:
            continue
        if b:
            part = part[b:b + ts, :]
        out = part if out is None else out + part
    return out


def _fwd_out1(glu, proj, dw_w, dw_b, ln_g, ln_b, w_out, g_post, x1, target, ts):
    seq = x1.shape[0]
    hb = ts // CONV_HALO

    def body(glu_ref, halo_ref, gate_ref, dww_ref, dwb_ref, lng_ref, lnb_ref, w_ref, g_ref, x1_ref, t_ref,
             ymix_ref, dy_ref, dx2_ref, dcf_ref, dgate_ref, acc_ref, gbuf):
        i = pl.program_id(0)

        @pl.when(i == 0)
        def _():
            acc_ref[...] = jnp.zeros_like(acc_ref)

        gbuf[:CONV_HALO, :] = jnp.where(i > 0, halo_ref[...], 0.0)
        gbuf[CONV_HALO:, :] = glu_ref[...]
        for lb in range(D_MODEL // LANES):
            lanes = slice(LANES * lb, LANES * (lb + 1))
            dcf_ref[:, lanes] = _conv_taps(dww_ref, gbuf, ts, lanes, _fwd_tap)
        cf = dcf_ref[...] + dwb_ref[...]
        mu = jnp.mean(cf, axis=-1, keepdims=True)
        cen = cf - mu
        rs = lax.rsqrt(jnp.mean(cen * cen, axis=-1, keepdims=True) + EPS)
        xhat = cen * rs
        cn = xhat * lng_ref[...] + lnb_ref[...]
        gate = gate_ref[...].astype(F32)
        sg = _sigmoid(gate)
        sc = _sigmoid(cn)
        silu_gate = gate * sg
        silu_cn = cn * sc
        ymix = (silu_cn * silu_gate).astype(BF16)
        ymix_ref[...] = ymix
        y = _dot(ymix, w_ref[...])
        yn, r = _rms_fwd(y, g_ref[...])
        err = (x1_ref[...] + yn) - t_ref[...]
        acc_ref[ACC_LOSS] += _rows8(err * err)
        dx2 = err * (1.0 / D_MODEL)
        dx2_ref[...] = dx2
        dy, dpost = _rms_bwd(y, r, g_ref[...], dx2)
        acc_ref[ACC_POST] += _rows8(dpost)
        dy = dy.astype(BF16)
        dy_ref[...] = dy
        dymix = _dot_nt(dy, w_ref[...])
        dgate_ref[...] = (dymix * silu_cn * (sg * (1.0 + gate * (1.0 - sg)))).astype(BF16)
        dcn = dymix * silu_gate * (sc * (1.0 + cn * (1.0 - sc)))
        acc_ref[ACC_LN_G] += _rows8(dcn * xhat)
        acc_ref[ACC_LN_B] += _rows8(dcn)
        dxhat = dcn * lng_ref[...]
        dcf = rs * (dxhat - jnp.mean(dxhat, axis=-1, keepdims=True)
                    - xhat * jnp.mean(dxhat * xhat, axis=-1, keepdims=True))
        acc_ref[ACC_DW_B] += _rows8(dcf)
        dcf_ref[...] = dcf

    return pl.pallas_call(
        body, name="fwd_out1", grid=(seq // ts,),
        in_specs=[_tile(ts, D_MODEL),
                  pl.BlockSpec((CONV_HALO, D_MODEL), lambda i: (jnp.maximum(i * hb - 1, 0), 0)),
                  _tile(ts, D_MODEL, 2), _full((CONV_HALO, D_MODEL)), _full((1, D_MODEL)), _full((1, D_MODEL)),
                  _full((1, D_MODEL)), _full((D_MODEL, D_MODEL)), _full((1, D_MODEL)), _tile(ts, D_MODEL),
                  _tile(ts, D_MODEL)],
        out_specs=[_tile(ts, D_MODEL), _tile(ts, D_MODEL), _tile(ts, D_MODEL), _tile(ts, D_MODEL), _tile(ts, D_MODEL),
                   _full((5, SUBLANES, D_MODEL))],
        out_shape=[jax.ShapeDtypeStruct((seq, D_MODEL), BF16), jax.ShapeDtypeStruct((seq, D_MODEL), BF16),
                   jax.ShapeDtypeStruct((seq, D_MODEL), F32), jax.ShapeDtypeStruct((seq, D_MODEL), F32),
                   jax.ShapeDtypeStruct((seq, D_MODEL), BF16), jax.ShapeDtypeStruct((5, SUBLANES, D_MODEL), F32)],
        scratch_shapes=[pltpu.VMEM((ts + CONV_HALO, D_MODEL), F32)],
        compiler_params=_params(),
    )(glu, glu, proj, dw_w, dw_b, ln_g, ln_b, w_out, g_post, x1, target)


def _bwd_in1(dcf, glu, proj, dgate, dw_w, w_in, x1, g_pre, dx2, ts):
    seq = x1.shape[0]
    hb = ts // CONV_HALO
    last = seq // CONV_HALO - 1
    nt = seq // ts

    def body(dcf_ref, dnext_ref, glu_ref, gprev_ref, ab_ref, dgate_ref, dww_ref, w_ref, x_ref, g_ref, dx2_ref,
             dproj_ref, dx1_ref, ddw_ref, dpre_ref, dbuf, gbuf, zbuf, sbuf):
        i = pl.program_id(0)

        @pl.when(i == 0)
        def _():
            ddw_ref[...] = jnp.zeros_like(ddw_ref)
            dpre_ref[...] = jnp.zeros_like(dpre_ref)

        dcf = dcf_ref[...]
        dbuf[:ts, :] = dcf
        dbuf[ts:, :] = jnp.where(i < nt - 1, dnext_ref[...], 0.0)
        gbuf[:CONV_HALO, :] = jnp.where(i > 0, gprev_ref[...], 0.0)
        gbuf[CONV_HALO:, :] = glu_ref[...]
        zbuf[:SUBLANES, :] = jnp.zeros((SUBLANES, D_MODEL), F32)
        zbuf[pl.ds(SUBLANES, ts), :] = dcf
        zbuf[pl.ds(SUBLANES + ts, SUBLANES), :] = jnp.zeros((SUBLANES, D_MODEL), F32)
        dh = _dot(dgate_ref[...], w_ref[2 * D_MODEL:, :])
        for lb in range(D_MODEL // LANES):
            lanes = slice(LANES * lb, LANES * (lb + 1))
            gate_lanes = slice(D_MODEL + LANES * lb, D_MODEL + LANES * (lb + 1))
            dglu = _conv_taps(dww_ref, dbuf, ts, lanes, _bwd_tap)
            a = ab_ref[:, lanes].astype(F32)
            sb = _sigmoid(ab_ref[:, gate_lanes].astype(F32))
            dproj_ref[:, lanes] = (dglu * sb).astype(BF16)
            dproj_ref[:, gate_lanes] = (dglu * a * sb * (1.0 - sb)).astype(BF16)
            for b in range(SUBLANES):
                rows = ts if b == 0 else ts + SUBLANES
                sbuf[b, pl.ds(0, rows), :] = zbuf[pl.ds(SUBLANES - b, rows), lanes]
                taps = [(a8, _fwd_tap(SUBLANES * a8 + b)) for a8 in range(CONV_HALO // SUBLANES + 1)]
                taps = [(a8, k) for a8, k in taps if k is not None]
                sums = [jnp.zeros((SUBLANES, LANES), F32) for _ in taps]
                for r0 in range(0, rows, DDW_ROWS):
                    rc = min(DDW_ROWS, rows - r0)
                    shifted = sbuf[b, pl.ds(r0, rc), :]
                    for n, (a8, k) in enumerate(taps):
                        sums[n] = sums[n] + _rows8(shifted * gbuf[pl.ds(SUBLANES * a8 + r0, rc), lanes])
                for n, (a8, k) in enumerate(taps):
                    ddw_ref[k, :, lanes] += sums[n]
            if lb % 2:
                for c0 in (LANES * (lb - 1), D_MODEL + LANES * (lb - 1)):
                    dh = dh + _dot(dproj_ref[:, c0:c0 + 2 * LANES], w_ref[c0:c0 + 2 * LANES, :])
        dproj_ref[:, 2 * D_MODEL:] = dgate_ref[...]
        x = x_ref[...]
        r = lax.rsqrt(jnp.mean(x * x, axis=-1, keepdims=True) + EPS)
        dx, dpre = _rms_bwd(x, r, g_ref[...], dh)
        dx1_ref[...] = dx2_ref[...] + dx
        dpre_ref[...] += _rows8(dpre)

    return pl.pallas_call(
        body, name="bwd_in1", grid=(nt,),
        in_specs=[_tile(ts, D_MODEL),
                  pl.BlockSpec((CONV_HALO, D_MODEL), lambda i: (jnp.minimum((i + 1) * hb, last), 0)),
                  _tile(ts, D_MODEL),
                  pl.BlockSpec((CONV_HALO, D_MODEL), lambda i: (jnp.maximum(i * hb - 1, 0), 0)),
                  _tile(ts, 2 * D_MODEL), _tile(ts, D_MODEL), _full((CONV_HALO, D_MODEL)),
                  _full((ODD_IN, D_MODEL)), _tile(ts, D_MODEL), _full((1, D_MODEL)), _tile(ts, D_MODEL)],
        out_specs=[_tile(ts, ODD_IN), _tile(ts, D_MODEL), _full((CONV_HALO, SUBLANES, D_MODEL)), _full((SUBLANES, D_MODEL))],
        out_shape=[jax.ShapeDtypeStruct((seq, ODD_IN), BF16), jax.ShapeDtypeStruct((seq, D_MODEL), F32),
                   jax.ShapeDtypeStruct((CONV_HALO, SUBLANES, D_MODEL), F32), jax.ShapeDtypeStruct((SUBLANES, D_MODEL), F32)],
        scratch_shapes=[pltpu.VMEM((ts + CONV_HALO, D_MODEL), F32), pltpu.VMEM((ts + CONV_HALO, D_MODEL), F32),
                        pltpu.VMEM((ts + 2 * SUBLANES, D_MODEL), F32), pltpu.VMEM((SUBLANES, ts + SUBLANES, LANES), F32)],
        compiler_params=_params(),
    )(dcf, dcf, glu, glu, proj, dgate, dw_w, w_in, x1, g_pre, dx2)


def _bwd_out0(dx1, y0, g_post, w_out, gug, o, pooled, pool_w, pool_scale, mix, ts):
    seq = dx1.shape[0]

    def body(dx1_ref, y_ref, g_ref, w_ref, gug_ref, o_ref, pooled_ref, pw_ref, ps_ref, mix_ref,
             do_ref, dgg_ref, dpooled_ref, dpost_ref, dscale_ref, dpw_ref, gw_ref, gacc):
        i = pl.program_id(0)

        @pl.when(i == 0)
        def _():
            dpost_ref[...] = jnp.zeros_like(dpost_ref)
            dscale_ref[...] = jnp.zeros_like(dscale_ref)
            dpw_ref[...] = jnp.zeros_like(dpw_ref)

        y = y_ref[...]
        r = lax.rsqrt(jnp.mean(y * y, axis=-1, keepdims=True) + EPS)
        dy, dpost = _rms_bwd(y, r, g_ref[...], dx1_ref[...])
        dpost_ref[...] += _rows8(dpost)
        dy = dy.astype(BF16)
        _accumulate_tn(gacc, mix_ref[...], dy, i == 0)
        dmix = _dot_nt(dy, w_ref[...])
        dya = dmix[:, :ATTN_WIDTH]
        dyb = dmix[:, ATTN_WIDTH:]
        ga = gug_ref[:, :ATTN_WIDTH].astype(F32)
        gb = gug_ref[:, ATTN_WIDTH + POOL_WIDTH:].astype(F32)
        sga = _sigmoid(ga)
        sgb = _sigmoid(gb)
        do_ref[...] = (dya * (ga * sga)).astype(BF16)
        dgg_ref[:, :ATTN_WIDTH] = (dya * o_ref[...] * (sga * (1.0 + ga * (1.0 - sga)))).astype(BF16)
        dypool = dyb * (gb * sgb)
        dsilu_gb = sgb * (1.0 + gb * (1.0 - sgb))
        for g in range(len(POOL_WINDOWS)):
            lanes = slice(POOL_GC * g, POOL_GC * (g + 1))
            pooled = pooled_ref[:, lanes]
            wg = pw_ref[g].astype(BF16)
            pw = _dot(pooled, wg)
            scale = ps_ref[:, lanes]
            dgg_ref[:, ATTN_WIDTH + POOL_GC * g:ATTN_WIDTH + POOL_GC * (g + 1)] = (
                dyb[:, lanes] * (pw * scale) * dsilu_gb[:, lanes]).astype(BF16)
            dscale_ref[:, lanes] += _rows8(dypool[:, lanes] * pw)
            dpw = (dypool[:, lanes] * scale).astype(BF16)
            dpooled_ref[:, lanes] = _dot_nt(dpw, wg)
            dpw_ref[g] += _dot_tn(pooled, dpw)

        @pl.when(i == seq // ts - 1)
        def _():
            gw_ref[...] = gacc[...].astype(BF16)

    return pl.pallas_call(
        body, name="bwd_out0", grid=(seq // ts,),
        in_specs=[_tile(ts, D_MODEL), _tile(ts, D_MODEL), _full((1, D_MODEL)), _resident((D_MODEL, D_MODEL)),
                  _tile(ts, 3 * POOL_WIDTH), _tile(ts, ATTN_WIDTH), _tile(ts, POOL_WIDTH),
                  _full((4, POOL_GC, POOL_GC)), _full((1, POOL_WIDTH)), _tile(ts, D_MODEL)],
        out_specs=[_tile(ts, ATTN_WIDTH), _tile(ts, ATTN_WIDTH + POOL_WIDTH), _tile(ts, POOL_WIDTH),
                   _full((SUBLANES, D_MODEL)), _full((SUBLANES, POOL_WIDTH)), _full((4, POOL_GC, POOL_GC)),
                   _full((D_MODEL, D_MODEL))],
        out_shape=[jax.ShapeDtypeStruct((seq, ATTN_WIDTH), BF16),
                   jax.ShapeDtypeStruct((seq, ATTN_WIDTH + POOL_WIDTH), BF16), jax.ShapeDtypeStruct((seq, POOL_WIDTH), F32),
                   jax.ShapeDtypeStruct((SUBLANES, D_MODEL), F32), jax.ShapeDtypeStruct((SUBLANES, POOL_WIDTH), F32),
                   jax.ShapeDtypeStruct((4, POOL_GC, POOL_GC), F32), jax.ShapeDtypeStruct((D_MODEL, D_MODEL), BF16)],
        scratch_shapes=[pltpu.VMEM((D_MODEL, D_MODEL), F32)],
        compiler_params=_params(),
    )(dx1, y0, g_post, w_out, gug, o, pooled, pool_w, pool_scale, mix)


def _attn_bwd(q, kv, do, sinks, parts, early):
    seq = q.shape[0]
    nb = seq // BLOCK

    def qblock(j):
        return jnp.minimum(j, nb - 1)

    n = len(parts)

    def body(*refs):
        sink_ref, q_ref, kvc_ref, kvp_ref, do_ref = refs[:5]
        early_ref = refs[5 + n]
        dq_ref, dkv_ref, dsink_ref = refs[6 + n:9 + n]
        all_ref = refs[9 + 2 * n]
        carry, dkv_acc = refs[10 + 2 * n:12 + 2 * n]
        scatter = _Scatter(refs[5:5 + n], refs[9 + n:9 + 2 * n], *refs[12 + 2 * n:14 + 2 * n])
        gather = lambda: _Gather([early_ref], [all_ref], *refs[14 + 2 * n:])
        j = pl.program_id(0)

        @pl.when(j == 0)
        def _():
            gather().begin()
            scatter.begin()
            dsink_ref[...] = jnp.zeros_like(dsink_ref)
            carry[...] = jnp.zeros_like(carry)

        @pl.when(j < nb)
        def _():
            valid, distf = _attn_mask(GROUP_ROWS, j == 0)
            qb = q_ref[...]
            dob = do_ref[...]
            kk = jnp.concatenate([kvp_ref[...], kvc_ref[...]], axis=0)
            lane = lax.broadcasted_iota(jnp.int32, (BLOCK, LANES), 1)
            dsink = jnp.zeros((BLOCK, LANES), F32)
            for kvh in range(Q_HEADS // GROUP):
                kh = kk[:, HEAD_DIM * kvh:HEAD_DIM * (kvh + 1)]
                vh = kk[:, KV_WIDTH + HEAD_DIM * kvh:KV_WIDTH + HEAD_DIM * (kvh + 1)]
                qg = _group_rows(qb, kvh)
                dog = _group_rows(dob, kvh)
                sink, slope = _group_columns(sink_ref, kvh)
                p, psink = _attn_probs(qg, kh, sink, slope, valid, distf)
                dp = _dot_nt(dog, vh)
                delta = jnp.sum(p * dp, axis=-1, keepdims=True)
                ds = (p * (dp - delta) * (HEAD_DIM ** -0.5)).astype(BF16)
                dsink_rows = -psink * delta
                dqg = _dot(ds, kh).astype(BF16)
                for i in range(GROUP):
                    h = GROUP * kvh + i
                    dsink = dsink + jnp.where(lane == h, dsink_rows[BLOCK * i:BLOCK * (i + 1), :], 0.0)
                    dq_ref[:, HEAD_DIM * h:HEAD_DIM * (h + 1)] = dqg[BLOCK * i:BLOCK * (i + 1), :]
                dkv_acc[:, HEAD_DIM * kvh:HEAD_DIM * (kvh + 1)] = _dot_tn(ds, qg)
                dkv_acc[:, KV_WIDTH + HEAD_DIM * kvh:KV_WIDTH + HEAD_DIM * (kvh + 1)] = _dot_tn(p.astype(BF16), dog)
            dsink_ref[...] += dsink

            @pl.when(j > 0)
            def _():
                dkv_ref[...] = (carry[...] + dkv_acc[:BLOCK, :]).astype(BF16)

            carry[...] = dkv_acc[BLOCK:, :]

        @pl.when(j == nb)
        def _():
            dkv_ref[...] = carry[...].astype(BF16)
            gather().relay()
            scatter.finish()
            gather().finish()

    hbm = pl.BlockSpec(memory_space=pl.ANY)
    out = pl.pallas_call(
        body, name="attn_bwd", grid=(nb + 1,),
        in_specs=[pl.BlockSpec(memory_space=pltpu.SMEM),
                  pl.BlockSpec((BLOCK, ATTN_WIDTH), lambda j: (qblock(j), 0)),
                  pl.BlockSpec((BLOCK, 2 * KV_WIDTH), lambda j: (qblock(j), 0)),
                  pl.BlockSpec((BLOCK, 2 * KV_WIDTH), lambda j: (jnp.maximum(qblock(j) - 1, 0), 0)),
                  pl.BlockSpec((BLOCK, ATTN_WIDTH), lambda j: (qblock(j), 0))] + [hbm] * (n + 1),
        out_specs=[pl.BlockSpec((BLOCK, ATTN_WIDTH), lambda j: (qblock(j), 0)),
                   pl.BlockSpec((BLOCK, 2 * KV_WIDTH), lambda j: (jnp.maximum(j - 1, 0), 0)),
                   _full((BLOCK, LANES))] + [hbm] * (n + 1),
        out_shape=[jax.ShapeDtypeStruct((seq, ATTN_WIDTH), BF16), jax.ShapeDtypeStruct((seq, 2 * KV_WIDTH), BF16),
                   jax.ShapeDtypeStruct((BLOCK, LANES), F32)]
        + [jax.ShapeDtypeStruct((N_DEV - 1, *p.shape[1:]), p.dtype) for p in parts]
        + [jax.ShapeDtypeStruct((N_DEV, *early.shape), early.dtype)],
        scratch_shapes=[pltpu.VMEM((BLOCK, 2 * KV_WIDTH), F32), pltpu.VMEM((2 * BLOCK, 2 * KV_WIDTH), F32)]
        + _Scatter.semaphores(n) + _Gather.semaphores(1),
        compiler_params=_params(),
    )(sinks, q, kv, kv, do, *parts, early)
    return out[:3], out[3:3 + n], out[3 + n]


def _bwd_in0(dpooled, dq, dkv, dgg, w_in, x, g_pre, dx1, ts):
    seq = x.shape[0]
    hb = ts // POOL_HALO
    last = seq // POOL_HALO - 1
    nt = seq // ts

    def body(dp_ref, dnext_ref, dq_ref, dkv_ref, dgg_ref, w_ref, x_ref, g_ref, dx1_ref,
             dproj_ref, gx_ref, dpre_ref, dbuf):
        i = pl.program_id(0)

        @pl.when(i == 0)
        def _():
            dpre_ref[...] = jnp.zeros_like(dpre_ref)

        dpool = dp_ref[...]
        dnext = jnp.where(i < nt - 1, dnext_ref[...], 0.0)
        u0 = ATTN_WIDTH + 2 * KV_WIDTH + ATTN_WIDTH
        for g, window in enumerate(POOL_WINDOWS):
            lanes = slice(POOL_GC * g, POOL_GC * (g + 1))
            dbuf[:ts, lanes] = dpool[:, lanes] / _pool_counts(i * ts, ts, window)
            dbuf[ts:, lanes] = dnext[:, lanes] / _pool_counts((i + 1) * ts, POOL_HALO, window)
        for g, window in enumerate(POOL_WINDOWS):
            lanes = slice(POOL_GC * g, POOL_GC * (g + 1))
            acc = dbuf[pl.ds(0, ts), lanes]
            for k in range(1, window):
                acc = acc + dbuf[pl.ds(k, ts), lanes]
            dproj_ref[:, u0 + POOL_GC * g:u0 + POOL_GC * (g + 1)] = (acc - dpool[:, lanes]).astype(BF16)
        dproj_ref[:, :ATTN_WIDTH] = dq_ref[...]
        dproj_ref[:, ATTN_WIDTH:ATTN_WIDTH + 2 * KV_WIDTH] = dkv_ref[...]
        dproj_ref[:, ATTN_WIDTH + 2 * KV_WIDTH:u0] = dgg_ref[:, :ATTN_WIDTH]
        dproj_ref[:, u0 + POOL_WIDTH:] = dgg_ref[:, ATTN_WIDTH:]
        dh = _dot(dproj_ref[...], w_ref[...])
        x = x_ref[...]
        r = lax.rsqrt(jnp.mean(x * x, axis=-1, keepdims=True) + EPS)
        dx, dpre = _rms_bwd(x, r, g_ref[...], dh)
        gx_ref[...] = dx1_ref[...] + dx
        dpre_ref[...] += _rows8(dpre)

    return pl.pallas_call(
        body, name="bwd_in0", grid=(nt,),
        in_specs=[_tile(ts, POOL_WIDTH),
                  pl.BlockSpec((POOL_HALO, POOL_WIDTH), lambda i: (jnp.minimum((i + 1) * hb, last), 0)),
                  _tile(ts, ATTN_WIDTH), _tile(ts, 2 * KV_WIDTH), _tile(ts, ATTN_WIDTH + POOL_WIDTH),
                  _full((EVEN_IN, D_MODEL)), _tile(ts, D_MODEL), _full((1, D_MODEL)), _tile(ts, D_MODEL)],
        out_specs=[_tile(ts, EVEN_IN), _tile(ts, D_MODEL), _full((SUBLANES, D_MODEL))],
        out_shape=[jax.ShapeDtypeStruct((seq, EVEN_IN), BF16), jax.ShapeDtypeStruct((seq, D_MODEL), F32),
                   jax.ShapeDtypeStruct((SUBLANES, D_MODEL), F32)],
        scratch_shapes=[pltpu.VMEM((ts + POOL_HALO, POOL_WIDTH), F32)],
        compiler_params=_params(),
    )(dpooled, dpooled, dq, dkv, dgg, w_in, x, g_pre, dx1)


def _matmul_tn(a, b, name, ts, tm):
    seq, m = a.shape
    n = b.shape[1]
    steps = seq // ts

    def body(a_ref, b_ref, o_ref, acc):
        s = pl.program_id(1)

        @pl.when(s == 0)
        def _():
            acc[...] = jnp.zeros_like(acc)

        acc[...] += _dot_tn(a_ref[...], b_ref[...])

        @pl.when(s == steps - 1)
        def _():
            o_ref[...] = acc[...].astype(BF16)

    return pl.pallas_call(
        body, name=name, grid=(m // tm, steps),
        in_specs=[pl.BlockSpec((ts, tm), lambda j, s: (s, j)), pl.BlockSpec((ts, n), lambda j, s: (s, 0))],
        out_specs=pl.BlockSpec((tm, n), lambda j, s: (j, 0)),
        out_shape=jax.ShapeDtypeStruct((m, n), BF16),
        scratch_shapes=[pltpu.VMEM((tm, n), F32)],
        compiler_params=pltpu.CompilerParams(dimension_semantics=("arbitrary", "arbitrary"), vmem_limit_bytes=VMEM_LIMIT),
    )(a, b)


def _adamw_math(w, g, m, v):
    m = ADAM_B1 * m + (1.0 - ADAM_B1) * g
    v = ADAM_B2 * v + (1.0 - ADAM_B2) * (g * g)
    m_hat = m / (1.0 - ADAM_B1 ** ADAM_STEP)
    v_hat = v / (1.0 - ADAM_B2 ** ADAM_STEP)
    delta = -ADAM_LR * (m_hat / (jnp.sqrt(v_hat) + ADAM_EPS) + ADAM_WD * w)
    return delta, m, v


def _adamw(ws, gs, ms, vs, name):
    n = len(ws)

    def body(*refs):
        ins, outs = refs[:4 * n], refs[4 * n:]
        for k in range(n):
            delta, m, v = _adamw_math(ins[k][...], ins[n + k][...], ins[2 * n + k][...], ins[3 * n + k][...])
            outs[k][...] = delta
            outs[n + k][...] = m
            outs[2 * n + k][...] = v

    shapes = [jax.ShapeDtypeStruct(w.shape, F32) for w in ws]
    out = pl.pallas_call(body, name=name, out_shape=shapes * 3,
                         compiler_params=pltpu.CompilerParams(vmem_limit_bytes=VMEM_LIMIT))(*ws, *gs, *ms, *vs)
    return out[:n], out[n:2 * n], out[2 * n:]


TS_MATMUL, TS_IN1, TS_CONV_FWD, TS_CONV_BWD, TS_GRAD = 512, 1024, 512, 256, 1024


def kernel(x, pre_norm, post_norm, a_w_in, a_sinks, b_pool_w, b_pool_scale, ab_w_out, c_w_in, c_dw_w, c_dw_b, c_ln_g, c_ln_b, c_w_out, loss_target, m_pre_norm, m_post_norm, m_a_w_in, m_a_sinks, m_b_pool_w, m_b_pool_scale, m_ab_w_out, m_c_w_in, m_c_dw_w, m_c_dw_b, m_c_ln_g, m_c_ln_b, m_c_w_out, v_pre_norm, v_post_norm, v_a_w_in, v_a_sinks, v_b_pool_w, v_b_pool_scale, v_ab_w_out, v_c_w_in, v_c_dw_w, v_c_dw_b, v_c_ln_g, v_c_ln_b, v_c_w_out):
    seq = x.shape[1]
    ts_big, ts_grad = min(TS_MATMUL, seq), min(TS_GRAD, seq)
    x2d = x[0]
    target = loss_target[0]
    ch = c_dw_b.shape[1]

    whole = lambda g: g.reshape(-1, D_MODEL)
    vec_rows = 40
    vecs = jnp.concatenate([c_dw_w[0, :, 0, :], c_dw_b, c_ln_g, c_ln_b, jnp.zeros((vec_rows - CONV_K - 3, ch), F32)], axis=0)
    w_in0t, vg = _all_gather([a_w_in[0].T.astype(BF16), vecs], "gather_first")
    w_in0t = whole(w_in0t)
    vg = vg.transpose(1, 0, 2).reshape(vec_rows, D_MODEL)
    dw_w = vg[:CONV_HALO]
    dw_b, ln_g, ln_b = vg[CONV_K:CONV_K + 1], vg[CONV_K + 1:CONV_K + 2], vg[CONV_K + 2:CONV_K + 3]

    sinks = a_sinks[0]
    (h0, q, kv, gug), (w_out0,) = _fwd_in0(x2d, pre_norm[0:1], w_in0t, [ab_w_out[0].astype(BF16)], ts_big)
    o, (w_in1t, w_out1) = _attn_fwd(q, kv, sinks, [c_w_in[0].T.astype(BF16), c_w_out[0].astype(BF16)])
    w_out0, w_in1t, w_out1 = whole(w_out0), whole(w_in1t), whole(w_out1)
    mix0, pooled, y0, x1 = _fwd_out0(gug, o, b_pool_w[0], b_pool_scale, w_out0, post_norm[0:1], x2d, ts_big)
    h1, proj1, glu = _fwd_in1(x1, pre_norm[1:2], w_in1t, min(TS_IN1, seq))
    ymix1, dy1, dx2, dcf, dgate, acc1 = _fwd_out1(glu, proj1, dw_w, dw_b, ln_g, ln_b, w_out1, post_norm[1:2], x1, target, min(TS_CONV_FWD, seq))

    dproj1, dx1, ddw_w, dpre1 = _bwd_in1(dcf, glu, proj1, dgate, dw_w, w_in1t, x1, pre_norm[1:2], dx2, min(TS_CONV_BWD, seq))
    g_in1t = _matmul_tn(dproj1, h1, "grad_w_in1", ts_grad, 1024)
    g_out1 = _matmul_tn(ymix1, dy1, "grad_w_out1", ts_grad, 1024)
    do, dgg, dpooled, dpost0, dscale, dpool_w, g_out0 = _bwd_out0(dx1, y0, post_norm[0:1], w_out0, gug, o, pooled, b_pool_w[0], b_pool_scale, mix0, ts_big)
    slabs = lambda g: g.reshape(N_DEV, -1, D_MODEL)
    me = 4 * lax.axis_index("x") + 2 * lax.axis_index("y") + lax.axis_index("c")
    early = [slabs(g_in1t), slabs(g_out1), slabs(g_out0)]
    row = lambda a: jnp.sum(a, axis=0, keepdims=True)
    lanes8 = lambda a: row(a).reshape(-1, LANES)
    loss_row = jnp.pad(jnp.sum(acc1[ACC_LOSS]).reshape(1, 1), ((0, 0), (0, LANES - 1)))
    done = jnp.concatenate([lanes8(dpre1), lanes8(dpost0), lanes8(acc1[ACC_POST]), lanes8(dscale),
                            loss_row, jnp.zeros((3, LANES), F32), dpool_w.reshape(4 * POOL_GC, LANES)], axis=0)
    (dq, dkv, dsink), arrived, done_all = _attn_bwd(q, kv, do, sinks, early, done)
    g_in1t, g_c_w_out, g_ab_w_out = _sum_slabs([lax.dynamic_index_in_dim(p, me, keepdims=False) for p in early],
                                               arrived, "sum_early")
    dproj0, grad_x, dpre0 = _bwd_in0(dpooled, dq, dkv, dgg, w_in0t, x2d, pre_norm[0:1], dx1, ts_big)
    g_in0t = _matmul_tn(dproj0, h0, "grad_w_in0", ts_grad, 768)

    vec_g = jnp.concatenate([jnp.sum(ddw_w[:CONV_K], axis=1), row(acc1[ACC_DW_B]), row(acc1[ACC_LN_G]), row(acc1[ACC_LN_B]),
                             jnp.zeros((vec_rows - CONV_K - 3, D_MODEL), F32)], axis=0)
    last = jnp.concatenate([lanes8(dpre0), row(dsink), jnp.zeros((SUBLANES - 1, LANES), F32)], axis=0)
    g_in0t, vec_g, last, rep = _final_reduce(slabs(g_in0t), vec_g.reshape(vec_rows, N_DEV, ch).transpose(1, 0, 2),
                                             last, done_all, "final_reduce")
    g_a_w_in, g_c_w_in = g_in0t.T, g_in1t.T
    g_dw_w, g_dw_b, g_ln_g, g_ln_b = vec_g[:CONV_K], vec_g[CONV_K:CONV_K + 1], vec_g[CONV_K + 1:CONV_K + 2], vec_g[CONV_K + 2:CONV_K + 3]
    g_pre = jnp.concatenate([last[:8], rep[:8]], axis=0).reshape(2, D_MODEL)
    g_sinks = last[8:9, :Q_HEADS]
    g_post = rep[8:24].reshape(2, D_MODEL)
    g_scale = rep[24:28].reshape(1, POOL_WIDTH)
    loss = (0.5 / D_MODEL) * rep[28, 0]
    g_pool_w = rep[32:]

    grads = [g_pre, g_post, g_a_w_in, g_sinks, g_pool_w, g_scale, g_ab_w_out, g_c_w_in, g_dw_w, g_dw_b, g_ln_g, g_ln_b, g_c_w_out]
    weights = [pre_norm, post_norm, a_w_in, a_sinks, b_pool_w, b_pool_scale, ab_w_out, c_w_in, c_dw_w, c_dw_b, c_ln_g, c_ln_b, c_w_out]
    m_in = [m_pre_norm, m_post_norm, m_a_w_in, m_a_sinks, m_b_pool_w, m_b_pool_scale, m_ab_w_out, m_c_w_in, m_c_dw_w, m_c_dw_b, m_c_ln_g, m_c_ln_b, m_c_w_out]
    v_in = [v_pre_norm, v_post_norm, v_a_w_in, v_a_sinks, v_b_pool_w, v_b_pool_scale, v_ab_w_out, v_c_w_in, v_c_dw_w, v_c_dw_b, v_c_ln_g, v_c_ln_b, v_c_w_out]
    flat = lambda arrs: [a.reshape(g.shape) for a, g in zip(arrs, grads)]
    big = (2, 6, 7, 12)
    small = tuple(k for k in range(len(grads)) if k not in big)
    pick = lambda arrs, idx: [arrs[k] for k in idx]
    deltas, new_m, new_v = [None] * 13, [None] * 13, [None] * 13
    for idx, name in ((big, "adamw_matrices"), (small, "adamw_vectors")):
        d, m, v = _adamw(pick(flat(weights), idx), pick(grads, idx), pick(flat(m_in), idx), pick(flat(v_in), idx), name)
        for k, dk, mk, vk in zip(idx, d, m, v):
            deltas[k], new_m[k], new_v[k] = dk, mk, vk
    shaped = lambda arrs: [a.reshape(w.shape) for a, w in zip(arrs, weights)]
    return (loss, grad_x[None], *shaped(grads), *shaped(deltas), *shaped(new_m), *shaped(new_v))
```

```python
import jax
import jax.numpy as jnp
from jax import lax
from jax.experimental import pallas as pl
from jax.experimental.pallas import tpu as pltpu

F32 = jnp.float32
BF16 = jnp.bfloat16
MESH = pl.DeviceIdType.MESH
N_DEV = 8

D_MODEL = 1024
HEAD_DIM = 64
Q_HEADS = 8
GROUP = 4
ATTN_WIDTH = 512
KV_WIDTH = 128
BLOCK = 128
POOL_WIDTH = 512
POOL_WINDOWS = (2, 4, 8, 16)
POOL_GC = 128
POOL_HALO = 16
EVEN_IN = 2304
CONV_K = 31
CONV_HALO = 32
ODD_IN = 3072
EPS = 1e-6
NEG = -1e30
SLOPES = tuple(2.0 ** (-8.0 * (h + 1) / Q_HEADS) for h in range(Q_HEADS))

ADAM_LR = 0.001
ADAM_B1 = 0.9
ADAM_B2 = 0.999
ADAM_EPS = 1e-08
ADAM_WD = 0.01
ADAM_STEP = 10

SUBLANES = 8
LANES = 128
VMEM_LIMIT = 56 * 1024 * 1024

NT = (((1,), (1,)), ((), ()))
TN = (((0,), (0,)), ((), ()))


def _params(**kw):
    return pltpu.CompilerParams(dimension_semantics=("arbitrary",), vmem_limit_bytes=VMEM_LIMIT, **kw)


def _dot(a, b):
    return jnp.dot(a, b, preferred_element_type=F32)


def _dot_nt(a, b):
    return lax.dot_general(a, b, NT, preferred_element_type=F32)


def _dot_tn(a, b):
    return lax.dot_general(a, b, TN, preferred_element_type=F32)


def _sigmoid(v):
    return 1.0 / (1.0 + jnp.exp(-v))


def _rows8(v):
    r, c = v.shape
    return jnp.sum(v.reshape(r // SUBLANES, SUBLANES, c), axis=0)


def _rms_fwd(v, g):
    r = lax.rsqrt(jnp.mean(v * v, axis=-1, keepdims=True) + EPS)
    return v * r * g, r


def _rms_bwd(v, r, g, dout):
    gd = dout * g
    dv = r * gd - v * (r * r * r) * jnp.mean(v * gd, axis=-1, keepdims=True)
    return dv, dout * (v * r)


def _full(shape):
    return pl.BlockSpec(shape, lambda i: (0,) * len(shape))


def _resident(shape):
    return pl.BlockSpec(shape, lambda i: (0,) * len(shape), pipeline_mode=pl.Buffered(1))


def _accumulate_tn(acc, a, b, first_step):
    @pl.when(first_step)
    def _():
        acc[...] = jnp.zeros_like(acc)

    acc[...] += _dot_tn(a, b)


def _tile(ts, cols, col_block=0):
    return pl.BlockSpec((ts, cols), lambda i: (i, col_block))


def _position():
    return lax.axis_index("x"), lax.axis_index("y"), lax.axis_index("c")


class _Gather:
    def __init__(self, x_refs, out_refs, send_sems, recv_sems, local_sems):
        self.x_refs, self.out_refs = x_refs, out_refs
        self.send_sems, self.recv_sems, self.local_sems = send_sems, recv_sems, local_sems
        self.n = len(x_refs)
        x, y, c = _position()
        self.c = c
        self.me, self.sibling = (x, y, c), (x, y, 1 - c)
        self.chips = [(1 - x, y), (x, 1 - y), (1 - x, 1 - y)]

    def copy(self, k, j, owner, to, src=None):
        slab = self.out_refs[k].at[4 * owner[0] + 2 * owner[1] + owner[2]]
        return pltpu.make_async_remote_copy(
            src_ref=slab if src is None else src, dst_ref=slab, send_sem=self.send_sems.at[7 * k + j],
            recv_sem=self.recv_sems.at[7 * k + j], device_id=to, device_id_type=MESH)

    def mine(self, k):
        return pltpu.make_async_copy(self.x_refs[k], self.out_refs[k].at[4 * self.me[0] + 2 * self.me[1] + self.c],
                                     self.local_sems.at[k])

    def first(self, k):
        out = [self.copy(k, 0, self.me, self.sibling, src=self.x_refs[k])]
        return out + [self.copy(k, 1 + j, self.me, (*chip, self.c), src=self.x_refs[k]) for j, chip in enumerate(self.chips)]

    def begin(self):
        for k in range(self.n):
            self.mine(k).start()
            for cp in self.first(k):
                cp.start()

    def passed(self, k):
        return [self.copy(k, 4 + j, (*chip, self.c), self.sibling) for j, chip in enumerate(self.chips)]

    def relay(self):
        for k in range(self.n):
            for j, chip in enumerate(self.chips):
                self.copy(k, 1 + j, (*chip, self.c), self.me).wait_recv()
                self.passed(k)[j].start()

    def finish(self):
        for k in range(self.n):
            self.copy(k, 0, self.sibling, self.me).wait_recv()
            for j, chip in enumerate(self.chips):
                self.copy(k, 4 + j, (*chip, 1 - self.c), self.me).wait_recv()
        for k in range(self.n):
            for cp in self.first(k) + self.passed(k):
                cp.wait_send()
            self.mine(k).wait()

    @staticmethod
    def semaphores(n):
        return [pltpu.SemaphoreType.DMA((7 * n,)), pltpu.SemaphoreType.DMA((7 * n,)), pltpu.SemaphoreType.DMA((n,))]


def RELAY_AT(steps):
    return steps * 3 // 4


class _Scatter:
    def __init__(self, part_refs, recv_refs, send_sems, recv_sems):
        self.part_refs, self.recv_refs, self.send_sems, self.recv_sems = part_refs, recv_refs, send_sems, recv_sems
        self.n = len(part_refs)

    def copies(self):
        x, y, c = _position()
        me = 4 * x + 2 * y + c
        out = []
        for k in range(self.n):
            for j in range(N_DEV - 1):
                d = j + 1
                out.append(pltpu.make_async_remote_copy(
                    src_ref=self.part_refs[k].at[me ^ d], dst_ref=self.recv_refs[k].at[j],
                    send_sem=self.send_sems.at[7 * k + j], recv_sem=self.recv_sems.at[7 * k + j],
                    device_id=(x ^ (d >> 2), y ^ ((d >> 1) & 1), c ^ (d & 1)), device_id_type=MESH))
        return out

    def begin(self):
        for cp in self.copies():
            cp.start()

    def finish(self):
        for cp in self.copies():
            cp.wait_recv()
        for cp in self.copies():
            cp.wait_send()

    @staticmethod
    def semaphores(n):
        return [pltpu.SemaphoreType.DMA((7 * n,)), pltpu.SemaphoreType.DMA((7 * n,))]


def _sum_slabs(own, recv, name):
    n = len(own)

    def body(*refs):
        for k in range(n):
            acc = refs[k][...].astype(F32)
            for j in range(N_DEV - 1):
                acc = acc + refs[n + k][j].astype(F32)
            refs[2 * n + k][...] = acc

    return pl.pallas_call(body, name=name, out_shape=[jax.ShapeDtypeStruct(o.shape, F32) for o in own])(*own, *recv)


N_CHIPS = 4


def _final_reduce(parts, vec_parts, rep, early_all, name):
    _, rows, cols = parts.shape
    vrows, prows = vec_parts.shape[1], rep.shape[0]
    peers = N_CHIPS - 1

    def body(parts_ref, vec_ref, rep_ref, early_ref, out_ref, vec_out, rep_out, early_out, recv_a, own_a, mid, recv_b, vec_recv, rep_all,
             a_send, a_recv, a_local, b_send, b_recv, v_send, v_recv, r_send, r_recv):
        x, y, c = _position()
        chip = 2 * x + y
        me = 2 * chip + c
        everyone = [(d, (x ^ (d >> 2), y ^ ((d >> 1) & 1), c ^ (d & 1))) for d in range(1, N_DEV)]

        stage1, loads = [], []
        for j in range(N_CHIPS):
            stage1.append(pltpu.make_async_remote_copy(
                src_ref=parts_ref.at[2 * j + 1 - c], dst_ref=recv_a.at[j], send_sem=a_send.at[j], recv_sem=a_recv.at[j],
                device_id=(x, y, 1 - c), device_id_type=MESH))
            loads.append(pltpu.make_async_copy(parts_ref.at[2 * j + c], own_a.at[j], a_local.at[j]))
            stage1[-1].start()
            loads[-1].start()
        small = []
        for d, to in everyone:
            small.append(pltpu.make_async_remote_copy(
                src_ref=vec_ref.at[me ^ d], dst_ref=vec_recv.at[d - 1], send_sem=v_send.at[d - 1], recv_sem=v_recv.at[d - 1],
                device_id=to, device_id_type=MESH))
            small.append(pltpu.make_async_remote_copy(
                src_ref=rep_ref, dst_ref=rep_all.at[me], send_sem=r_send.at[d - 1], recv_sem=r_recv.at[d - 1],
                device_id=to, device_id_type=MESH))
        for cp in small:
            cp.start()
        rep_all[me] = rep_ref[...]

        for j in range(N_CHIPS):
            loads[j].wait()
            stage1[j].wait_recv()
            mid[j] = (own_a[j].astype(F32) + recv_a[j].astype(F32)).astype(BF16)
        stage2 = []
        for d in range(1, N_CHIPS):
            stage2.append(pltpu.make_async_remote_copy(
                src_ref=mid.at[chip ^ d], dst_ref=recv_b.at[d - 1], send_sem=b_send.at[d - 1], recv_sem=b_recv.at[d - 1],
                device_id=(x ^ (d >> 1), y ^ (d & 1), c), device_id_type=MESH))
            stage2[-1].start()

        for cp in small:
            cp.wait_recv()
        vec_sum = vec_ref[me]
        for j in range(N_DEV - 1):
            vec_sum = vec_sum + vec_recv[j]
        vec_out[...] = vec_sum
        rep_sum = rep_all[0]
        for d in range(1, N_DEV):
            rep_sum = rep_sum + rep_all[d]
        rep_out[...] = rep_sum
        early_sum = early_ref[0]
        for d in range(1, N_DEV):
            early_sum = early_sum + early_ref[d]
        early_out[...] = early_sum

        acc = mid[chip].astype(F32)
        for d in range(peers):
            stage2[d].wait_recv()
            acc = acc + recv_b[d].astype(F32)
        out_ref[...] = acc
        for cp in stage1 + stage2 + small:
            cp.wait_send()

    vmem = pl.BlockSpec(memory_space=pltpu.VMEM)
    dma = pltpu.SemaphoreType.DMA
    return pl.pallas_call(
        body, name=name,
        out_shape=[jax.ShapeDtypeStruct((rows, cols), F32), jax.ShapeDtypeStruct((vrows, LANES), F32),
                   jax.ShapeDtypeStruct((prows, LANES), F32), jax.ShapeDtypeStruct(early_all.shape[1:], F32)],
        in_specs=[pl.BlockSpec(memory_space=pl.ANY), vmem, vmem, vmem],
        out_specs=[vmem, vmem, vmem, vmem],
        scratch_shapes=[pltpu.VMEM((N_CHIPS, rows, cols), BF16), pltpu.VMEM((N_CHIPS, rows, cols), BF16),
                        pltpu.VMEM((N_CHIPS, rows, cols), BF16), pltpu.VMEM((peers, rows, cols), BF16),
                        pltpu.VMEM((N_DEV - 1, vrows, LANES), F32), pltpu.VMEM((N_DEV, prows, LANES), F32),
                        dma((N_CHIPS,)), dma((N_CHIPS,)), dma((N_CHIPS,)), dma((peers,)), dma((peers,)),
                        dma((N_DEV - 1,)), dma((N_DEV - 1,)), dma((N_DEV - 1,)), dma((N_DEV - 1,))],
        compiler_params=pltpu.CompilerParams(vmem_limit_bytes=VMEM_LIMIT),
    )(parts, vec_parts, rep, early_all)


def _norm0(x, g, shards, ts):
    seq = x.shape[0]
    n = len(shards)
    steps = seq // ts

    def body(*refs):
        x_ref, g_ref = refs[:2]
        h_ref = refs[2 + n]
        gather = lambda: _Gather(refs[2:2 + n], refs[3 + n:3 + 2 * n], *refs[3 + 2 * n:])
        i = pl.program_id(0)

        @pl.when(i == 0)
        def _():
            gather().begin()

        h, _ = _rms_fwd(x_ref[...], g_ref[...])
        h_ref[...] = h.astype(BF16)

        @pl.when(i == RELAY_AT(steps))
        def _():
            gather().relay()

        @pl.when(i == steps - 1)
        def _():
            gather().finish()

    hbm = pl.BlockSpec(memory_space=pl.ANY)
    out = pl.pallas_call(
        body, name="norm0", grid=(steps,),
        in_specs=[_tile(ts, D_MODEL), _full((1, D_MODEL))] + [hbm] * n,
        out_specs=[_tile(ts, D_MODEL)] + [hbm] * n,
        out_shape=[jax.ShapeDtypeStruct((seq, D_MODEL), BF16)]
        + [jax.ShapeDtypeStruct((N_DEV, *b.shape), b.dtype) for b in shards],
        scratch_shapes=_Gather.semaphores(n),
        compiler_params=_params(),
    )(x, g, *shards)
    return out[0], out[1:]


def _fwd_in0(h, w, shards, ts):
    seq = h.shape[0]
    n = len(shards)
    steps = seq // ts

    def body(*refs):
        h_ref, w_ref = refs[:2]
        q_ref, kv_ref, gug_ref = refs[2 + n:5 + n]
        gather = lambda: _Gather(refs[2:2 + n], refs[5 + n:5 + 2 * n], *refs[5 + 2 * n:])
        i = pl.program_id(0)

        @pl.when(i == 0)
        def _():
            gather().begin()

        proj = _dot_nt(h_ref[...], w_ref[...])
        q_ref[...] = proj[:, :ATTN_WIDTH].astype(BF16)
        kv_ref[...] = proj[:, ATTN_WIDTH:ATTN_WIDTH + 2 * KV_WIDTH].astype(BF16)
        gug_ref[...] = proj[:, ATTN_WIDTH + 2 * KV_WIDTH:].astype(BF16)

        @pl.when(i == RELAY_AT(steps))
        def _():
            gather().relay()

        @pl.when(i == steps - 1)
        def _():
            gather().finish()

    hbm = pl.BlockSpec(memory_space=pl.ANY)
    out = pl.pallas_call(
        body, name="fwd_in0", grid=(steps,),
        in_specs=[_tile(ts, D_MODEL), _full((EVEN_IN, D_MODEL))] + [hbm] * n,
        out_specs=[_tile(ts, ATTN_WIDTH), _tile(ts, 2 * KV_WIDTH), _tile(ts, 3 * POOL_WIDTH)] + [hbm] * n,
        out_shape=[jax.ShapeDtypeStruct((seq, ATTN_WIDTH), BF16),
                   jax.ShapeDtypeStruct((seq, 2 * KV_WIDTH), BF16), jax.ShapeDtypeStruct((seq, 3 * POOL_WIDTH), BF16)]
        + [jax.ShapeDtypeStruct((N_DEV, *b.shape), b.dtype) for b in shards],
        scratch_shapes=_Gather.semaphores(n),
        compiler_params=_params(),
    )(h, w, *shards)
    return out[:3], out[3:]


GROUP_ROWS = GROUP * BLOCK
FWD_BLOCKS = 2


def _attn_mask(rows, first_block):
    row = lax.broadcasted_iota(jnp.int32, (rows, 2 * BLOCK), 0) & (BLOCK - 1)
    col = lax.broadcasted_iota(jnp.int32, (rows, 2 * BLOCK), 1)
    dist = row + BLOCK - col
    valid = (dist >= 0) & (dist < BLOCK) & ((col >= BLOCK) | jnp.logical_not(first_block))
    return valid, dist.astype(F32)


def _group_rows(block, kvh):
    return jnp.concatenate([block[:, HEAD_DIM * h:HEAD_DIM * (h + 1)] for h in range(GROUP * kvh, GROUP * (kvh + 1))], axis=0)


def _group_columns(sink_ref, kvh):
    head = lax.broadcasted_iota(jnp.int32, (GROUP_ROWS, 1), 0) // BLOCK
    sink = jnp.zeros((GROUP_ROWS, 1), F32)
    slope = jnp.zeros((GROUP_ROWS, 1), F32)
    for i in range(GROUP):
        sink = jnp.where(head == i, sink_ref[GROUP * kvh + i], sink)
        slope = jnp.where(head == i, SLOPES[GROUP * kvh + i], slope)
    return sink, slope


def _attn_probs(qh, kh, sink, slope, valid, distf):
    s = _dot_nt(qh, kh) * (HEAD_DIM ** -0.5)
    s = jnp.where(valid, s - slope * distf, NEG)
    mx = jnp.maximum(jnp.max(s, axis=-1, keepdims=True), sink)
    e = jnp.exp(s - mx)
    es = jnp.exp(sink - mx)
    den = jnp.sum(e, axis=-1, keepdims=True) + es
    return e / den, es / den


def _attn_fwd(q, kv, sinks, shards):
    seq = q.shape[0]
    per = min(FWD_BLOCKS, seq // BLOCK)
    nb = seq // (per * BLOCK)
    ns = len(shards)

    def body(*refs):
        sink_ref, q_ref, kvc_ref, kvp_ref = refs[:4]
        o_ref = refs[4 + ns]
        gather = lambda: _Gather(refs[4:4 + ns], refs[5 + ns:5 + 2 * ns], *refs[5 + 2 * ns:])
        n = pl.program_id(0)

        @pl.when(n == 0)
        def _():
            gather().begin()

        keys = jnp.concatenate([kvp_ref[...], kvc_ref[...]], axis=0)
        for sub in range(per):
            rows = slice(BLOCK * sub, BLOCK * (sub + 1))
            valid, distf = _attn_mask(BLOCK, (n == 0) if sub == 0 else False)
            qb = q_ref[rows, :]
            kk = keys[BLOCK * sub:BLOCK * (sub + 2), :]
            for h in range(Q_HEADS):
                kvh = h // GROUP
                qh = qb[:, HEAD_DIM * h:HEAD_DIM * (h + 1)]
                kh = kk[:, HEAD_DIM * kvh:HEAD_DIM * (kvh + 1)]
                vh = kk[:, KV_WIDTH + HEAD_DIM * kvh:KV_WIDTH + HEAD_DIM * (kvh + 1)]
                p, _ = _attn_probs(qh, kh, sink_ref[h], SLOPES[h], valid, distf)
                o_ref[rows, HEAD_DIM * h:HEAD_DIM * (h + 1)] = _dot(p.astype(BF16), vh)

        @pl.when(n == RELAY_AT(nb))
        def _():
            gather().relay()

        @pl.when(n == nb - 1)
        def _():
            gather().finish()

    hbm = pl.BlockSpec(memory_space=pl.ANY)
    out = pl.pallas_call(
        body, name="attn_fwd", grid=(nb,),
        in_specs=[pl.BlockSpec(memory_space=pltpu.SMEM),
                  pl.BlockSpec((per * BLOCK, ATTN_WIDTH), lambda n: (n, 0)),
                  pl.BlockSpec((per * BLOCK, 2 * KV_WIDTH), lambda n: (n, 0)),
                  pl.BlockSpec((BLOCK, 2 * KV_WIDTH), lambda n: (jnp.maximum(per * n - 1, 0), 0))] + [hbm] * ns,
        out_specs=[pl.BlockSpec((per * BLOCK, ATTN_WIDTH), lambda n: (n, 0))] + [hbm] * ns,
        out_shape=[jax.ShapeDtypeStruct((seq, ATTN_WIDTH), F32)]
        + [jax.ShapeDtypeStruct((N_DEV, *b.shape), b.dtype) for b in shards],
        scratch_shapes=_Gather.semaphores(ns),
        compiler_params=_params(),
    )(sinks, q, kv, kv, *shards)
    return out[0], out[1:]


def _pool_counts(first_row, rows, window):
    t = first_row + lax.broadcasted_iota(jnp.int32, (rows, 1), 0)
    return jnp.minimum(t + 1, window).astype(F32)


def _fwd_out0(gug, o, pool_w, pool_scale, w_out, g_post, x, ts):
    seq = x.shape[0]
    hb = ts // POOL_HALO

    def body(gug_ref, halo_ref, o_ref, pw_ref, ps_ref, w_ref, g_ref, x_ref, mix_ref, pooled_ref, y_ref, x1_ref, ubuf):
        i = pl.program_id(0)
        ga = gug_ref[:, :ATTN_WIDTH].astype(F32)
        u = gug_ref[:, ATTN_WIDTH:ATTN_WIDTH + POOL_WIDTH].astype(F32)
        gb = gug_ref[:, ATTN_WIDTH + POOL_WIDTH:].astype(F32)
        mix_ref[:, :ATTN_WIDTH] = (o_ref[...] * (ga * _sigmoid(ga))).astype(BF16)
        ubuf[:POOL_HALO, :] = jnp.where(i > 0, halo_ref[...].astype(F32), 0.0)
        ubuf[POOL_HALO:, :] = u
        silu_gb = gb * _sigmoid(gb)
        for g, window in enumerate(POOL_WINDOWS):
            lanes = slice(POOL_GC * g, POOL_GC * (g + 1))
            acc = ubuf[pl.ds(POOL_HALO, ts), lanes]
            for k in range(1, window):
                acc = acc + ubuf[pl.ds(POOL_HALO - k, ts), lanes]
            pooled = (acc / _pool_counts(i * ts, ts, window) - u[:, lanes]).astype(BF16)
            pooled_ref[:, lanes] = pooled
            ypool = _dot(pooled, pw_ref[g].astype(BF16)) * ps_ref[:, lanes]
            mix_ref[:, ATTN_WIDTH + POOL_GC * g:ATTN_WIDTH + POOL_GC * (g + 1)] = (ypool * silu_gb[:, lanes]).astype(BF16)
        y = _dot(mix_ref[...], w_ref[...])
        y_ref[...] = y
        yn, _ = _rms_fwd(y, g_ref[...])
        x1_ref[...] = x_ref[...] + yn

    return pl.pallas_call(
        body, name="fwd_out0", grid=(seq // ts,),
        in_specs=[_tile(ts, 3 * POOL_WIDTH),
                  pl.BlockSpec((POOL_HALO, POOL_WIDTH), lambda i: (jnp.maximum(i * hb - 1, 0), 1)),
                  _tile(ts, ATTN_WIDTH), _full((4, POOL_GC, POOL_GC)), _full((1, POOL_WIDTH)),
                  _full((D_MODEL, D_MODEL)), _full((1, D_MODEL)), _tile(ts, D_MODEL)],
        out_specs=[_tile(ts, D_MODEL), _tile(ts, POOL_WIDTH), _tile(ts, D_MODEL), _tile(ts, D_MODEL)],
        out_shape=[jax.ShapeDtypeStruct((seq, D_MODEL), BF16), jax.ShapeDtypeStruct((seq, POOL_WIDTH), BF16),
                   jax.ShapeDtypeStruct((seq, D_MODEL), F32), jax.ShapeDtypeStruct((seq, D_MODEL), F32)],
        scratch_shapes=[pltpu.VMEM((ts + POOL_HALO, POOL_WIDTH), F32)],
        compiler_params=_params(),
    )(gug, gug, o, pool_w, pool_scale, w_out, g_post, x)


def _fwd_in1(x1, g, w, ts):
    seq = x1.shape[0]

    def body(x_ref, g_ref, w_ref, h_ref, proj_ref, glu_ref):
        h, _ = _rms_fwd(x_ref[...], g_ref[...])
        h = h.astype(BF16)
        h_ref[...] = h
        proj = _dot_nt(h, w_ref[...])
        proj_ref[...] = proj.astype(BF16)
        glu_ref[...] = proj[:, :D_MODEL] * _sigmoid(proj[:, D_MODEL:2 * D_MODEL])

    return pl.pallas_call(
        body, name="fwd_in1", grid=(seq // ts,),
        in_specs=[_tile(ts, D_MODEL), _full((1, D_MODEL)), _full((ODD_IN, D_MODEL))],
        out_specs=[_tile(ts, D_MODEL), _tile(ts, ODD_IN), _tile(ts, D_MODEL)],
        out_shape=[jax.ShapeDtypeStruct((seq, D_MODEL), BF16), jax.ShapeDtypeStruct((seq, ODD_IN), BF16),
                   jax.ShapeDtypeStruct((seq, D_MODEL), F32)],
        compiler_params=_params(),
    )(x1, g, w)


ACC_LOSS, ACC_POST, ACC_LN_G, ACC_LN_B, ACC_DW_B = range(5)
DDW_ROWS = 88
CONV_FIRST = CONV_HALO - CONV_K + 1


def _fwd_tap(offset):
    return offset - CONV_FIRST if CONV_FIRST <= offset <= CONV_HALO else None


def _bwd_tap(offset):
    return CONV_K - 1 - offset if offset < CONV_K else None


def _conv_taps(w_ref, buf_ref, ts, lanes, tap_of_offset):
    out = None
    for b in range(SUBLANES):
        rows = ts if b == 0 else ts + SUBLANES
        part = None
        for a in range(CONV_HALO // SUBLANES + 1):
            k = tap_of_offset(SUBLANES * a + b)
            if k is None:
                continue
            term = w_ref[k:k + 1, lanes] * buf_ref[pl.ds(SUBLANES * a, rows), lanes]
            part = term if part is None else part + term
        if part is None:
            continue
        if b:
            part = part[b:b + ts, :]
        out = part if out is None else out + part
    return out


def _fwd_out1(glu, proj, dw_w, dw_b, ln_g, ln_b, w_out, g_post, x1, target, ts):
    seq = x1.shape[0]
    hb = ts // CONV_HALO

    def body(glu_ref, halo_ref, gate_ref, dww_ref, dwb_ref, lng_ref, lnb_ref, w_ref, g_ref, x1_ref, t_ref,
             ymix_ref, dy_ref, dx2_ref, dcf_ref, dgate_ref, acc_ref, gbuf):
        i = pl.program_id(0)

        @pl.when(i == 0)
        def _():
            acc_ref[...] = jnp.zeros_like(acc_ref)

        gbuf[:CONV_HALO, :] = jnp.where(i > 0, halo_ref[...], 0.0)
        gbuf[CONV_HALO:, :] = glu_ref[...]
        for lb in range(D_MODEL // LANES):
            lanes = slice(LANES * lb, LANES * (lb + 1))
            dcf_ref[:, lanes] = _conv_taps(dww_ref, gbuf, ts, lanes, _fwd_tap)
        cf = dcf_ref[...] + dwb_ref[...]
        mu = jnp.mean(cf, axis=-1, keepdims=True)
        cen = cf - mu
        rs = lax.rsqrt(jnp.mean(cen * cen, axis=-1, keepdims=True) + EPS)
        xhat = cen * rs
        cn = xhat * lng_ref[...] + lnb_ref[...]
        gate = gate_ref[...].astype(F32)
        sg = _sigmoid(gate)
        sc = _sigmoid(cn)
        silu_gate = gate * sg
        silu_cn = cn * sc
        ymix = (silu_cn * silu_gate).astype(BF16)
        ymix_ref[...] = ymix
        y = _dot(ymix, w_ref[...])
        yn, r = _rms_fwd(y, g_ref[...])
        err = (x1_ref[...] + yn) - t_ref[...]
        acc_ref[ACC_LOSS] += _rows8(err * err)
        dx2 = err * (1.0 / D_MODEL)
        dx2_ref[...] = dx2
        dy, dpost = _rms_bwd(y, r, g_ref[...], dx2)
        acc_ref[ACC_POST] += _rows8(dpost)
        dy = dy.astype(BF16)
        dy_ref[...] = dy
        dymix = _dot_nt(dy, w_ref[...])
        dgate_ref[...] = (dymix * silu_cn * (sg * (1.0 + gate * (1.0 - sg)))).astype(BF16)
        dcn = dymix * silu_gate * (sc * (1.0 + cn * (1.0 - sc)))
        acc_ref[ACC_LN_G] += _rows8(dcn * xhat)
        acc_ref[ACC_LN_B] += _rows8(dcn)
        dxhat = dcn * lng_ref[...]
        dcf = rs * (dxhat - jnp.mean(dxhat, axis=-1, keepdims=True)
                    - xhat * jnp.mean(dxhat * xhat, axis=-1, keepdims=True))
        acc_ref[ACC_DW_B] += _rows8(dcf)
        dcf_ref[...] = dcf

    return pl.pallas_call(
        body, name="fwd_out1", grid=(seq // ts,),
        in_specs=[_tile(ts, D_MODEL),
                  pl.BlockSpec((CONV_HALO, D_MODEL), lambda i: (jnp.maximum(i * hb - 1, 0), 0)),
                  _tile(ts, D_MODEL, 2), _full((CONV_HALO, D_MODEL)), _full((1, D_MODEL)), _full((1, D_MODEL)),
                  _full((1, D_MODEL)), _full((D_MODEL, D_MODEL)), _full((1, D_MODEL)), _tile(ts, D_MODEL),
                  _tile(ts, D_MODEL)],
        out_specs=[_tile(ts, D_MODEL), _tile(ts, D_MODEL), _tile(ts, D_MODEL), _tile(ts, D_MODEL), _tile(ts, D_MODEL),
                   _full((5, SUBLANES, D_MODEL))],
        out_shape=[jax.ShapeDtypeStruct((seq, D_MODEL), BF16), jax.ShapeDtypeStruct((seq, D_MODEL), BF16),
                   jax.ShapeDtypeStruct((seq, D_MODEL), F32), jax.ShapeDtypeStruct((seq, D_MODEL), F32),
                   jax.ShapeDtypeStruct((seq, D_MODEL), BF16), jax.ShapeDtypeStruct((5, SUBLANES, D_MODEL), F32)],
        scratch_shapes=[pltpu.VMEM((ts + CONV_HALO, D_MODEL), F32)],
        compiler_params=_params(),
    )(glu, glu, proj, dw_w, dw_b, ln_g, ln_b, w_out, g_post, x1, target)


def _bwd_in1(dcf, glu, proj, dgate, dw_w, w_in, x1, g_pre, dx2, ts):
    seq = x1.shape[0]
    hb = ts // CONV_HALO
    last = seq // CONV_HALO - 1
    nt = seq // ts

    def body(dcf_ref, dnext_ref, glu_ref, gprev_ref, ab_ref, dgate_ref, dww_ref, w_ref, x_ref, g_ref, dx2_ref,
             dproj_ref, dx1_ref, ddw_ref, dpre_ref, dbuf, gbuf, zbuf, sbuf):
        i = pl.program_id(0)

        @pl.when(i == 0)
        def _():
            ddw_ref[...] = jnp.zeros_like(ddw_ref)
            dpre_ref[...] = jnp.zeros_like(dpre_ref)

        dcf = dcf_ref[...]
        dbuf[:ts, :] = dcf
        dbuf[ts:, :] = jnp.where(i < nt - 1, dnext_ref[...], 0.0)
        gbuf[:CONV_HALO, :] = jnp.where(i > 0, gprev_ref[...], 0.0)
        gbuf[CONV_HALO:, :] = glu_ref[...]
        zbuf[:SUBLANES, :] = jnp.zeros((SUBLANES, D_MODEL), F32)
        zbuf[pl.ds(SUBLANES, ts), :] = dcf
        zbuf[pl.ds(SUBLANES + ts, SUBLANES), :] = jnp.zeros((SUBLANES, D_MODEL), F32)
        dh = _dot(dgate_ref[...], w_ref[2 * D_MODEL:, :])
        for lb in range(D_MODEL // LANES):
            lanes = slice(LANES * lb, LANES * (lb + 1))
            gate_lanes = slice(D_MODEL + LANES * lb, D_MODEL + LANES * (lb + 1))
            dglu = _conv_taps(dww_ref, dbuf, ts, lanes, _bwd_tap)
            a = ab_ref[:, lanes].astype(F32)
            sb = _sigmoid(ab_ref[:, gate_lanes].astype(F32))
            dproj_ref[:, lanes] = (dglu * sb).astype(BF16)
            dproj_ref[:, gate_lanes] = (dglu * a * sb * (1.0 - sb)).astype(BF16)
            for b in range(SUBLANES):
                rows = ts if b == 0 else ts + SUBLANES
                sbuf[b, pl.ds(0, rows), :] = zbuf[pl.ds(SUBLANES - b, rows), lanes]
                taps = [(a8, _fwd_tap(SUBLANES * a8 + b)) for a8 in range(CONV_HALO // SUBLANES + 1)]
                taps = [(a8, k) for a8, k in taps if k is not None]
                sums = [jnp.zeros((SUBLANES, LANES), F32) for _ in taps]
                for r0 in range(0, rows, DDW_ROWS):
                    rc = min(DDW_ROWS, rows - r0)
                    shifted = sbuf[b, pl.ds(r0, rc), :]
                    for n, (a8, k) in enumerate(taps):
                        sums[n] = sums[n] + _rows8(shifted * gbuf[pl.ds(SUBLANES * a8 + r0, rc), lanes])
                for n, (a8, k) in enumerate(taps):
                    ddw_ref[k, :, lanes] += sums[n]
            if lb % 2:
                for c0 in (LANES * (lb - 1), D_MODEL + LANES * (lb - 1)):
                    dh = dh + _dot(dproj_ref[:, c0:c0 + 2 * LANES], w_ref[c0:c0 + 2 * LANES, :])
        dproj_ref[:, 2 * D_MODEL:] = dgate_ref[...]
        x = x_ref[...]
        r = lax.rsqrt(jnp.mean(x * x, axis=-1, keepdims=True) + EPS)
        dx, dpre = _rms_bwd(x, r, g_ref[...], dh)
        dx1_ref[...] = dx2_ref[...] + dx
        dpre_ref[...] += _rows8(dpre)

    return pl.pallas_call(
        body, name="bwd_in1", grid=(nt,),
        in_specs=[_tile(ts, D_MODEL),
                  pl.BlockSpec((CONV_HALO, D_MODEL), lambda i: (jnp.minimum((i + 1) * hb, last), 0)),
                  _tile(ts, D_MODEL),
                  pl.BlockSpec((CONV_HALO, D_MODEL), lambda i: (jnp.maximum(i * hb - 1, 0), 0)),
                  _tile(ts, 2 * D_MODEL), _tile(ts, D_MODEL), _full((CONV_HALO, D_MODEL)),
                  _full((ODD_IN, D_MODEL)), _tile(ts, D_MODEL), _full((1, D_MODEL)), _tile(ts, D_MODEL)],
        out_specs=[_tile(ts, ODD_IN), _tile(ts, D_MODEL), _full((CONV_HALO, SUBLANES, D_MODEL)), _full((SUBLANES, D_MODEL))],
        out_shape=[jax.ShapeDtypeStruct((seq, ODD_IN), BF16), jax.ShapeDtypeStruct((seq, D_MODEL), F32),
                   jax.ShapeDtypeStruct((CONV_HALO, SUBLANES, D_MODEL), F32), jax.ShapeDtypeStruct((SUBLANES, D_MODEL), F32)],
        scratch_shapes=[pltpu.VMEM((ts + CONV_HALO, D_MODEL), F32), pltpu.VMEM((ts + CONV_HALO, D_MODEL), F32),
                        pltpu.VMEM((ts + 2 * SUBLANES, D_MODEL), F32), pltpu.VMEM((SUBLANES, ts + SUBLANES, LANES), F32)],
        compiler_params=_params(),
    )(dcf, dcf, glu, glu, proj, dgate, dw_w, w_in, x1, g_pre, dx2)


def _bwd_out0(dx1, y0, g_post, w_out, gug, o, pooled, pool_w, pool_scale, mix, ts):
    seq = dx1.shape[0]

    def body(dx1_ref, y_ref, g_ref, w_ref, gug_ref, o_ref, pooled_ref, pw_ref, ps_ref, mix_ref,
             do_ref, dgg_ref, dpooled_ref, dpost_ref, dscale_ref, dpw_ref, gw_ref, gacc):
        i = pl.program_id(0)

        @pl.when(i == 0)
        def _():
            dpost_ref[...] = jnp.zeros_like(dpost_ref)
            dscale_ref[...] = jnp.zeros_like(dscale_ref)
            dpw_ref[...] = jnp.zeros_like(dpw_ref)

        y = y_ref[...]
        r = lax.rsqrt(jnp.mean(y * y, axis=-1, keepdims=True) + EPS)
        dy, dpost = _rms_bwd(y, r, g_ref[...], dx1_ref[...])
        dpost_ref[...] += _rows8(dpost)
        dy = dy.astype(BF16)
        _accumulate_tn(gacc, mix_ref[...], dy, i == 0)
        dmix = _dot_nt(dy, w_ref[...])
        dya = dmix[:, :ATTN_WIDTH]
        dyb = dmix[:, ATTN_WIDTH:]
        ga = gug_ref[:, :ATTN_WIDTH].astype(F32)
        gb = gug_ref[:, ATTN_WIDTH + POOL_WIDTH:].astype(F32)
        sga = _sigmoid(ga)
        sgb = _sigmoid(gb)
        do_ref[...] = (dya * (ga * sga)).astype(BF16)
        dgg_ref[:, :ATTN_WIDTH] = (dya * o_ref[...] * (sga * (1.0 + ga * (1.0 - sga)))).astype(BF16)
        dypool = dyb * (gb * sgb)
        dsilu_gb = sgb * (1.0 + gb * (1.0 - sgb))
        for g in range(len(POOL_WINDOWS)):
            lanes = slice(POOL_GC * g, POOL_GC * (g + 1))
            pooled = pooled_ref[:, lanes]
            wg = pw_ref[g].astype(BF16)
            pw = _dot(pooled, wg)
            scale = ps_ref[:, lanes]
            dgg_ref[:, ATTN_WIDTH + POOL_GC * g:ATTN_WIDTH + POOL_GC * (g + 1)] = (
                dyb[:, lanes] * (pw * scale) * dsilu_gb[:, lanes]).astype(BF16)
            dscale_ref[:, lanes] += _rows8(dypool[:, lanes] * pw)
            dpw = (dypool[:, lanes] * scale).astype(BF16)
            dpooled_ref[:, lanes] = _dot_nt(dpw, wg)
            dpw_ref[g] += _dot_tn(pooled, dpw)

        @pl.when(i == seq // ts - 1)
        def _():
            gw_ref[...] = gacc[...].astype(BF16)

    return pl.pallas_call(
        body, name="bwd_out0", grid=(seq // ts,),
        in_specs=[_tile(ts, D_MODEL), _tile(ts, D_MODEL), _full((1, D_MODEL)), _resident((D_MODEL, D_MODEL)),
                  _tile(ts, 3 * POOL_WIDTH), _tile(ts, ATTN_WIDTH), _tile(ts, POOL_WIDTH),
                  _full((4, POOL_GC, POOL_GC)), _full((1, POOL_WIDTH)), _tile(ts, D_MODEL)],
        out_specs=[_tile(ts, ATTN_WIDTH), _tile(ts, ATTN_WIDTH + POOL_WIDTH), _tile(ts, POOL_WIDTH),
                   _full((SUBLANES, D_MODEL)), _full((SUBLANES, POOL_WIDTH)), _full((4, POOL_GC, POOL_GC)),
                   _full((D_MODEL, D_MODEL))],
        out_shape=[jax.ShapeDtypeStruct((seq, ATTN_WIDTH), BF16),
                   jax.ShapeDtypeStruct((seq, ATTN_WIDTH + POOL_WIDTH), BF16), jax.ShapeDtypeStruct((seq, POOL_WIDTH), F32),
                   jax.ShapeDtypeStruct((SUBLANES, D_MODEL), F32), jax.ShapeDtypeStruct((SUBLANES, POOL_WIDTH), F32),
                   jax.ShapeDtypeStruct((4, POOL_GC, POOL_GC), F32), jax.ShapeDtypeStruct((D_MODEL, D_MODEL), BF16)],
        scratch_shapes=[pltpu.VMEM((D_MODEL, D_MODEL), F32)],
        compiler_params=_params(),
    )(dx1, y0, g_post, w_out, gug, o, pooled, pool_w, pool_scale, mix)


def _attn_bwd(q, kv, do, sinks, parts, early):
    seq = q.shape[0]
    nb = seq // BLOCK

    def qblock(j):
        return jnp.minimum(j, nb - 1)

    n = len(parts)

    def body(*refs):
        sink_ref, q_ref, kvc_ref, kvp_ref, do_ref = refs[:5]
        early_ref = refs[5 + n]
        dq_ref, dkv_ref, dsink_ref = refs[6 + n:9 + n]
        all_ref = refs[9 + 2 * n]
        carry, dkv_acc = refs[10 + 2 * n:12 + 2 * n]
        scatter = _Scatter(refs[5:5 + n], refs[9 + n:9 + 2 * n], *refs[12 + 2 * n:14 + 2 * n])
        gather = lambda: _Gather([early_ref], [all_ref], *refs[14 + 2 * n:])
        j = pl.program_id(0)

        @pl.when(j == 0)
        def _():
            gather().begin()
            scatter.begin()
            dsink_ref[...] = jnp.zeros_like(dsink_ref)
            carry[...] = jnp.zeros_like(carry)

        @pl.when(j < nb)
        def _():
            valid, distf = _attn_mask(GROUP_ROWS, j == 0)
            qb = q_ref[...]
            dob = do_ref[...]
            kk = jnp.concatenate([kvp_ref[...], kvc_ref[...]], axis=0)
            lane = lax.broadcasted_iota(jnp.int32, (BLOCK, LANES), 1)
            dsink = jnp.zeros((BLOCK, LANES), F32)
            for kvh in range(Q_HEADS // GROUP):
                kh = kk[:, HEAD_DIM * kvh:HEAD_DIM * (kvh + 1)]
                vh = kk[:, KV_WIDTH + HEAD_DIM * kvh:KV_WIDTH + HEAD_DIM * (kvh + 1)]
                qg = _group_rows(qb, kvh)
                dog = _group_rows(dob, kvh)
                sink, slope = _group_columns(sink_ref, kvh)
                p, psink = _attn_probs(qg, kh, sink, slope, valid, distf)
                dp = _dot_nt(dog, vh)
                delta = jnp.sum(p * dp, axis=-1, keepdims=True)
                ds = (p * (dp - delta) * (HEAD_DIM ** -0.5)).astype(BF16)
                dsink_rows = -psink * delta
                dqg = _dot(ds, kh).astype(BF16)
                for i in range(GROUP):
                    h = GROUP * kvh + i
                    dsink = dsink + jnp.where(lane == h, dsink_rows[BLOCK * i:BLOCK * (i + 1), :], 0.0)
                    dq_ref[:, HEAD_DIM * h:HEAD_DIM * (h + 1)] = dqg[BLOCK * i:BLOCK * (i + 1), :]
                dkv_acc[:, HEAD_DIM * kvh:HEAD_DIM * (kvh + 1)] = _dot_tn(ds, qg)
                dkv_acc[:, KV_WIDTH + HEAD_DIM * kvh:KV_WIDTH + HEAD_DIM * (kvh + 1)] = _dot_tn(p.astype(BF16), dog)
            dsink_ref[...] += dsink

            @pl.when(j > 0)
            def _():
                dkv_ref[...] = (carry[...] + dkv_acc[:BLOCK, :]).astype(BF16)

            carry[...] = dkv_acc[BLOCK:, :]

        @pl.when(j == nb)
        def _():
            dkv_ref[...] = carry[...].astype(BF16)
            gather().relay()
            scatter.finish()
            gather().finish()

    hbm = pl.BlockSpec(memory_space=pl.ANY)
    out = pl.pallas_call(
        body, name="attn_bwd", grid=(nb + 1,),
        in_specs=[pl.BlockSpec(memory_space=pltpu.SMEM),
                  pl.BlockSpec((BLOCK, ATTN_WIDTH), lambda j: (qblock(j), 0)),
                  pl.BlockSpec((BLOCK, 2 * KV_WIDTH), lambda j: (qblock(j), 0)),
                  pl.BlockSpec((BLOCK, 2 * KV_WIDTH), lambda j: (jnp.maximum(qblock(j) - 1, 0), 0)),
                  pl.BlockSpec((BLOCK, ATTN_WIDTH), lambda j: (qblock(j), 0))] + [hbm] * (n + 1),
        out_specs=[pl.BlockSpec((BLOCK, ATTN_WIDTH), lambda j: (qblock(j), 0)),
                   pl.BlockSpec((BLOCK, 2 * KV_WIDTH), lambda j: (jnp.maximum(j - 1, 0), 0)),
                   _full((BLOCK, LANES))] + [hbm] * (n + 1),
        out_shape=[jax.ShapeDtypeStruct((seq, ATTN_WIDTH), BF16), jax.ShapeDtypeStruct((seq, 2 * KV_WIDTH), BF16),
                   jax.ShapeDtypeStruct((BLOCK, LANES), F32)]
        + [jax.ShapeDtypeStruct((N_DEV - 1, *p.shape[1:]), p.dtype) for p in parts]
        + [jax.ShapeDtypeStruct((N_DEV, *early.shape), early.dtype)],
        scratch_shapes=[pltpu.VMEM((BLOCK, 2 * KV_WIDTH), F32), pltpu.VMEM((2 * BLOCK, 2 * KV_WIDTH), F32)]
        + _Scatter.semaphores(n) + _Gather.semaphores(1),
        compiler_params=_params(),
    )(sinks, q, kv, kv, do, *parts, early)
    return out[:3], out[3:3 + n], out[3 + n]


def _bwd_in0(dpooled, dq, dkv, dgg, w_in, x, g_pre, dx1, ts):
    seq = x.shape[0]
    hb = ts // POOL_HALO
    last = seq // POOL_HALO - 1
    nt = seq // ts

    def body(dp_ref, dnext_ref, dq_ref, dkv_ref, dgg_ref, w_ref, x_ref, g_ref, dx1_ref,
             dproj_ref, gx_ref, dpre_ref, dbuf):
        i = pl.program_id(0)

        @pl.when(i == 0)
        def _():
            dpre_ref[...] = jnp.zeros_like(dpre_ref)

        dpool = dp_ref[...]
        dnext = jnp.where(i < nt - 1, dnext_ref[...], 0.0)
        u0 = ATTN_WIDTH + 2 * KV_WIDTH + ATTN_WIDTH
        for g, window in enumerate(POOL_WINDOWS):
            lanes = slice(POOL_GC * g, POOL_GC * (g + 1))
            dbuf[:ts, lanes] = dpool[:, lanes] / _pool_counts(i * ts, ts, window)
            dbuf[ts:, lanes] = dnext[:, lanes] / _pool_counts((i + 1) * ts, POOL_HALO, window)
        for g, window in enumerate(POOL_WINDOWS):
            lanes = slice(POOL_GC * g, POOL_GC * (g + 1))
            acc = dbuf[pl.ds(0, ts), lanes]
            for k in range(1, window):
                acc = acc + dbuf[pl.ds(k, ts), lanes]
            dproj_ref[:, u0 + POOL_GC * g:u0 + POOL_GC * (g + 1)] = (acc - dpool[:, lanes]).astype(BF16)
        dproj_ref[:, :ATTN_WIDTH] = dq_ref[...]
        dproj_ref[:, ATTN_WIDTH:ATTN_WIDTH + 2 * KV_WIDTH] = dkv_ref[...]
        dproj_ref[:, ATTN_WIDTH + 2 * KV_WIDTH:u0] = dgg_ref[:, :ATTN_WIDTH]
        dproj_ref[:, u0 + POOL_WIDTH:] = dgg_ref[:, ATTN_WIDTH:]
        dh = _dot(dproj_ref[...], w_ref[...])
        x = x_ref[...]
        r = lax.rsqrt(jnp.mean(x * x, axis=-1, keepdims=True) + EPS)
        dx, dpre = _rms_bwd(x, r, g_ref[...], dh)
        gx_ref[...] = dx1_ref[...] + dx
        dpre_ref[...] += _rows8(dpre)

    return pl.pallas_call(
        body, name="bwd_in0", grid=(nt,),
        in_specs=[_tile(ts, POOL_WIDTH),
                  pl.BlockSpec((POOL_HALO, POOL_WIDTH), lambda i: (jnp.minimum((i + 1) * hb, last), 0)),
                  _tile(ts, ATTN_WIDTH), _tile(ts, 2 * KV_WIDTH), _tile(ts, ATTN_WIDTH + POOL_WIDTH),
                  _full((EVEN_IN, D_MODEL)), _tile(ts, D_MODEL), _full((1, D_MODEL)), _tile(ts, D_MODEL)],
        out_specs=[_tile(ts, EVEN_IN), _tile(ts, D_MODEL), _full((SUBLANES, D_MODEL))],
        out_shape=[jax.ShapeDtypeStruct((seq, EVEN_IN), BF16), jax.ShapeDtypeStruct((seq, D_MODEL), F32),
                   jax.ShapeDtypeStruct((SUBLANES, D_MODEL), F32)],
        scratch_shapes=[pltpu.VMEM((ts + POOL_HALO, POOL_WIDTH), F32)],
        compiler_params=_params(),
    )(dpooled, dpooled, dq, dkv, dgg, w_in, x, g_pre, dx1)


def _matmul_tn(a, b, name, ts, tm):
    seq, m = a.shape
    n = b.shape[1]
    steps = seq // ts

    def body(a_ref, b_ref, o_ref, acc):
        s = pl.program_id(1)

        @pl.when(s == 0)
        def _():
            acc[...] = jnp.zeros_like(acc)

        acc[...] += _dot_tn(a_ref[...], b_ref[...])

        @pl.when(s == steps - 1)
        def _():
            o_ref[...] = acc[...].astype(BF16)

    return pl.pallas_call(
        body, name=name, grid=(m // tm, steps),
        in_specs=[pl.BlockSpec((ts, tm), lambda j, s: (s, j)), pl.BlockSpec((ts, n), lambda j, s: (s, 0))],
        out_specs=pl.BlockSpec((tm, n), lambda j, s: (j, 0)),
        out_shape=jax.ShapeDtypeStruct((m, n), BF16),
        scratch_shapes=[pltpu.VMEM((tm, n), F32)],
        compiler_params=pltpu.CompilerParams(dimension_semantics=("arbitrary", "arbitrary"), vmem_limit_bytes=VMEM_LIMIT),
    )(a, b)


def _adamw_math(w, g, m, v):
    m = ADAM_B1 * m + (1.0 - ADAM_B1) * g
    v = ADAM_B2 * v + (1.0 - ADAM_B2) * (g * g)
    m_hat = m / (1.0 - ADAM_B1 ** ADAM_STEP)
    v_hat = v / (1.0 - ADAM_B2 ** ADAM_STEP)
    delta = -ADAM_LR * (m_hat / (jnp.sqrt(v_hat) + ADAM_EPS) + ADAM_WD * w)
    return delta, m, v


def _adamw(ws, gs, ms, vs, name):
    n = len(ws)

    def body(*refs):
        ins, outs = refs[:4 * n], refs[4 * n:]
        for k in range(n):
            delta, m, v = _adamw_math(ins[k][...], ins[n + k][...], ins[2 * n + k][...], ins[3 * n + k][...])
            outs[k][...] = delta
            outs[n + k][...] = m
            outs[2 * n + k][...] = v

    shapes = [jax.ShapeDtypeStruct(w.shape, F32) for w in ws]
    out = pl.pallas_call(body, name=name, out_shape=shapes * 3,
                         compiler_params=pltpu.CompilerParams(vmem_limit_bytes=VMEM_LIMIT))(*ws, *gs, *ms, *vs)
    return out[:n], out[n:2 * n], out[2 * n:]


TS_MATMUL, TS_IN1, TS_CONV_FWD, TS_CONV_BWD, TS_GRAD = 512, 1024, 512, 256, 1024


def kernel(x, pre_norm, post_norm, a_w_in, a_sinks, b_pool_w, b_pool_scale, ab_w_out, c_w_in, c_dw_w, c_dw_b, c_ln_g, c_ln_b, c_w_out, loss_target, m_pre_norm, m_post_norm, m_a_w_in, m_a_sinks, m_b_pool_w, m_b_pool_scale, m_ab_w_out, m_c_w_in, m_c_dw_w, m_c_dw_b, m_c_ln_g, m_c_ln_b, m_c_w_out, v_pre_norm, v_post_norm, v_a_w_in, v_a_sinks, v_b_pool_w, v_b_pool_scale, v_ab_w_out, v_c_w_in, v_c_dw_w, v_c_dw_b, v_c_ln_g, v_c_ln_b, v_c_w_out):
    seq = x.shape[1]
    ts_big, ts_grad = min(TS_MATMUL, seq), min(TS_GRAD, seq)
    x2d = x[0]
    target = loss_target[0]
    ch = c_dw_b.shape[1]

    whole = lambda g: g.reshape(-1, D_MODEL)
    vec_rows = 40
    vecs = jnp.concatenate([c_dw_w[0, :, 0, :], c_dw_b, c_ln_g, c_ln_b, jnp.zeros((vec_rows - CONV_K - 3, ch), F32)], axis=0)
    h0, (w_in0t, vg) = _norm0(x2d, pre_norm[0:1], [a_w_in[0].T.astype(BF16), vecs], ts_big)
    w_in0t = whole(w_in0t)
    vg = vg.transpose(1, 0, 2).reshape(vec_rows, D_MODEL)
    dw_w = vg[:CONV_HALO]
    dw_b, ln_g, ln_b = vg[CONV_K:CONV_K + 1], vg[CONV_K + 1:CONV_K + 2], vg[CONV_K + 2:CONV_K + 3]

    sinks = a_sinks[0]
    (q, kv, gug), (w_out0,) = _fwd_in0(h0, w_in0t, [ab_w_out[0].astype(BF16)], ts_big)
    o, (w_in1t, w_out1) = _attn_fwd(q, kv, sinks, [c_w_in[0].T.astype(BF16), c_w_out[0].astype(BF16)])
    w_out0, w_in1t, w_out1 = whole(w_out0), whole(w_in1t), whole(w_out1)
    mix0, pooled, y0, x1 = _fwd_out0(gug, o, b_pool_w[0], b_pool_scale, w_out0, post_norm[0:1], x2d, ts_big)
    h1, proj1, glu = _fwd_in1(x1, pre_norm[1:2], w_in1t, min(TS_IN1, seq))
    ymix1, dy1, dx2, dcf, dgate, acc1 = _fwd_out1(glu, proj1, dw_w, dw_b, ln_g, ln_b, w_out1, post_norm[1:2], x1, target, min(TS_CONV_FWD, seq))

    dproj1, dx1, ddw_w, dpre1 = _bwd_in1(dcf, glu, proj1, dgate, dw_w, w_in1t, x1, pre_norm[1:2], dx2, min(TS_CONV_BWD, seq))
    g_in1t = _matmul_tn(dproj1, h1, "grad_w_in1", ts_grad, 1024)
    g_out1 = _matmul_tn(ymix1, dy1, "grad_w_out1", ts_grad, 1024)
    do, dgg, dpooled, dpost0, dscale, dpool_w, g_out0 = _bwd_out0(dx1, y0, post_norm[0:1], w_out0, gug, o, pooled, b_pool_w[0], b_pool_scale, mix0, ts_big)
    slabs = lambda g: g.reshape(N_DEV, -1, D_MODEL)
    me = 4 * lax.axis_index("x") + 2 * lax.axis_index("y") + lax.axis_index("c")
    early = [slabs(g_in1t), slabs(g_out1), slabs(g_out0)]
    row = lambda a: jnp.sum(a, axis=0, keepdims=True)
    lanes8 = lambda a: row(a).reshape(-1, LANES)
    loss_row = jnp.pad(jnp.sum(acc1[ACC_LOSS]).reshape(1, 1), ((0, 0), (0, LANES - 1)))
    done = jnp.concatenate([lanes8(dpre1), lanes8(dpost0), lanes8(acc1[ACC_POST]), lanes8(dscale),
                            loss_row, jnp.zeros((3, LANES), F32), dpool_w.reshape(4 * POOL_GC, LANES)], axis=0)
    (dq, dkv, dsink), arrived, done_all = _attn_bwd(q, kv, do, sinks, early, done)
    g_in1t, g_c_w_out, g_ab_w_out = _sum_slabs([lax.dynamic_index_in_dim(p, me, keepdims=False) for p in early],
                                               arrived, "sum_early")
    dproj0, grad_x, dpre0 = _bwd_in0(dpooled, dq, dkv, dgg, w_in0t, x2d, pre_norm[0:1], dx1, ts_big)
    g_in0t = _matmul_tn(dproj0, h0, "grad_w_in0", ts_grad, 768)

    vec_g = jnp.concatenate([jnp.sum(ddw_w[:CONV_K], axis=1), row(acc1[ACC_DW_B]), row(acc1[ACC_LN_G]), row(acc1[ACC_LN_B]),
                             jnp.zeros((vec_rows - CONV_K - 3, D_MODEL), F32)], axis=0)
    last = jnp.concatenate([lanes8(dpre0), row(dsink), jnp.zeros((SUBLANES - 1, LANES), F32)], axis=0)
    g_in0t, vec_g, last, rep = _final_reduce(slabs(g_in0t), vec_g.reshape(vec_rows, N_DEV, ch).transpose(1, 0, 2),
                                             last, done_all, "final_reduce")
    g_a_w_in, g_c_w_in = g_in0t.T, g_in1t.T
    g_dw_w, g_dw_b, g_ln_g, g_ln_b = vec_g[:CONV_K], vec_g[CONV_K:CONV_K + 1], vec_g[CONV_K + 1:CONV_K + 2], vec_g[CONV_K + 2:CONV_K + 3]
    g_pre = jnp.concatenate([last[:8], rep[:8]], axis=0).reshape(2, D_MODEL)
    g_sinks = last[8:9, :Q_HEADS]
    g_post = rep[8:24].reshape(2, D_MODEL)
    g_scale = rep[24:28].reshape(1, POOL_WIDTH)
    loss = (0.5 / D_MODEL) * rep[28, 0]
    g_pool_w = rep[32:]

    grads = [g_pre, g_post, g_a_w_in, g_sinks, g_pool_w, g_scale, g_ab_w_out, g_c_w_in, g_dw_w, g_dw_b, g_ln_g, g_ln_b, g_c_w_out]
    weights = [pre_norm, post_norm, a_w_in, a_sinks, b_pool_w, b_pool_scale, ab_w_out, c_w_in, c_dw_w, c_dw_b, c_ln_g, c_ln_b, c_w_out]
    m_in = [m_pre_norm, m_post_norm, m_a_w_in, m_a_sinks, m_b_pool_w, m_b_pool_scale, m_ab_w_out, m_c_w_in, m_c_dw_w, m_c_dw_b, m_c_ln_g, m_c_ln_b, m_c_w_out]
    v_in = [v_pre_norm, v_post_norm, v_a_w_in, v_a_sinks, v_b_pool_w, v_b_pool_scale, v_ab_w_out, v_c_w_in, v_c_dw_w, v_c_dw_b, v_c_ln_g, v_c_ln_b, v_c_w_out]
    flat = lambda arrs: [a.reshape(g.shape) for a, g in zip(arrs, grads)]
    big = (2, 6, 7, 12)
    small = tuple(k for k in range(len(grads)) if k not in big)
    pick = lambda arrs, idx: [arrs[k] for k in idx]
    deltas, new_m, new_v = [None] * 13, [None] * 13, [None] * 13
    for idx, name in ((big, "adamw_matrices"), (small, "adamw_vectors")):
        d, m, v = _adamw(pick(flat(weights), idx), pick(grads, idx), pick(flat(m_in), idx), pick(flat(v_in), idx), name)
        for k, dk, mk, vk in zip(idx, d, m, v):
            deltas[k], new_m[k], new_v[k] = dk, mk, vk
    shaped = lambda arrs: [a.reshape(w.shape) for a, w in zip(arrs, weights)]
    return (loss, grad_x[None], *shaped(grads), *shaped(deltas), *shaped(new_m), *shaped(new_v))
```
